```python
import jax, jax.numpy as jnp
from jax import lax
import numpy as np

D_MODEL = 1024
BATCH = 32
SEQ = 256
DEPTH = 1
DEC_BATCH = 8
DEC_SEQ = 1024
PAST_LEN = 256

GRID_W = 64
N_HEADS = 8
N_KV_HEADS = 2
GROUP = N_HEADS // N_KV_HEADS
HEAD_DIM = 64
WINDOW = 128
BLOCK = 128
F_GROUPS = 4
F_GROUP_DIM = 128
F_WIDTH = F_GROUPS * F_GROUP_DIM
ATTN_WIDTH = N_HEADS * HEAD_DIM
KV_WIDTH = N_KV_HEADS * HEAD_DIM
N_BRANCHES = 2
IN_WIDTH = F_WIDTH + ATTN_WIDTH + 2 * KV_WIDTH + N_BRANCHES * D_MODEL
D_FF = ((8 * D_MODEL + 3 * 256 - 1) // (3 * 256)) * 256
ROPE_THETA = 10000.0
EPS = 1e-6
NEG_INF = -1e30

kernel_name = "hybrid_fnet_swa_diffusion_step"


def rms_norm(x, g):
    xf = x.astype(jnp.float32)
    y = xf * lax.rsqrt(jnp.mean(xf * xf, axis=-1, keepdims=True) + EPS)
    return (y * g.astype(jnp.float32)).astype(x.dtype)


def adaln(cvec, w_ada, b_ada):
    mod = jax.nn.silu(cvec) @ w_ada + b_ada
    return jnp.split(mod, 6, axis=-1)


def mixer_project(h, w_in, g_q, g_k):
    b, n, _ = h.shape
    z = h @ w_in
    zf, zq, zk, zv, zg = jnp.split(
        z, [F_WIDTH, F_WIDTH + ATTN_WIDTH, F_WIDTH + ATTN_WIDTH + KV_WIDTH,
            F_WIDTH + ATTN_WIDTH + 2 * KV_WIDTH], axis=-1)
    q = rms_norm(zq.reshape(b, n, N_HEADS, HEAD_DIM), g_q)
    k = rms_norm(zk.reshape(b, n, N_KV_HEADS, HEAD_DIM), g_k)
    v = zv.reshape(b, n, N_KV_HEADS, HEAD_DIM)
    return zf, q, k, v, zg


def fourier_mix(zf, w_f):
    b, n, _ = zf.shape
    zgrp = zf.reshape(b, n, F_GROUPS, F_GROUP_DIM).astype(jnp.float32)
    mixed = jnp.real(jnp.fft.fft2(zgrp, axes=(1, 3), norm="ortho"))
    return mixed.reshape(b, n, F_WIDTH).astype(zf.dtype) @ w_f


def axial_rope(x):
    n = x.shape[1]
    rows = n // GRID_W
    row = jnp.repeat(jnp.arange(rows, dtype=jnp.float32), GRID_W)
    col = jnp.tile(jnp.arange(GRID_W, dtype=jnp.float32), rows)
    axis_dim = HEAD_DIM // 2
    inv_freq = ROPE_THETA ** (-jnp.arange(0, axis_dim, 2, dtype=jnp.float32) / axis_dim)
    xf = x.astype(jnp.float32)

    def rotate(xa, pos):
        ang = pos[:, None] * inv_freq[None, :]
        cos = jnp.cos(ang)[None, :, None, :]
        sin = jnp.sin(ang)[None, :, None, :]
        x1, x2 = jnp.split(xa, 2, axis=-1)
        return jnp.concatenate([x1 * cos - x2 * sin, x1 * sin + x2 * cos], axis=-1)

    out = jnp.concatenate([rotate(xf[..., :axis_dim], row), rotate(xf[..., axis_dim:], col)], axis=-1)
    return out.astype(x.dtype)


def sink_softmax(sc, sink):
    s = sink[:, :, None]
    m = jnp.maximum(jnp.max(sc, axis=-1), s)
    p = jnp.exp(sc - m[..., None])
    return p / (jnp.sum(p, axis=-1) + jnp.exp(s - m))[..., None]


def context_attention(q, k, v, sink):
    b, s = q.shape[:2]
    scale = HEAD_DIM ** -0.5
    qb = jnp.moveaxis(q.reshape(b, s // BLOCK, BLOCK, N_KV_HEADS, GROUP, HEAD_DIM), 1, 0)

    def one_block(qblk):
        sc = jnp.einsum("bqkgd,bjkd->bkgqj", qblk, k).astype(jnp.float32) * scale
        p = sink_softmax(sc, sink).astype(v.dtype)
        return jnp.einsum("bkgqj,bjkd->bqkgd", p, v)

    out = lax.map(one_block, qb)
    return jnp.moveaxis(out, 0, 1).reshape(b, s, ATTN_WIDTH)


def latent_attention(q, k, v, ck, cv, sink):
    b, n = q.shape[:2]
    nb = n // BLOCK
    scale = HEAD_DIM ** -0.5
    qb = q.reshape(b, nb, BLOCK, N_KV_HEADS, GROUP, HEAD_DIM)
    pad = ((0, 0), (BLOCK, BLOCK), (0, 0), (0, 0))
    kb = jnp.pad(k, pad).reshape(b, nb + 2, BLOCK, N_KV_HEADS, HEAD_DIM)
    vb = jnp.pad(v, pad).reshape(b, nb + 2, BLOCK, N_KV_HEADS, HEAD_DIM)
    kwin = jnp.concatenate([kb[:, :-2], kb[:, 1:-1], kb[:, 2:]], axis=2)
    vwin = jnp.concatenate([vb[:, :-2], vb[:, 1:-1], vb[:, 2:]], axis=2)
    q_pos = jnp.arange(n).reshape(nb, BLOCK)
    k_pos = (jnp.arange(nb)[:, None] - 1) * BLOCK + jnp.arange(3 * BLOCK)[None, :]
    mask = ((jnp.abs(q_pos[:, :, None] - k_pos[:, None, :]) <= WINDOW)
            & (k_pos[:, None, :] >= 0) & (k_pos[:, None, :] < n))
    s_loc = jnp.einsum("bnqkgd,bnjkd->bnkgqj", qb, kwin).astype(jnp.float32) * scale
    s_loc = jnp.where(mask[None, :, None, None, :, :], s_loc, NEG_INF)
    s_ctx = jnp.einsum("bnqkgd,bjkd->bnkgqj", qb, ck).astype(jnp.float32) * scale
    p = sink_softmax(jnp.concatenate([s_loc, s_ctx], axis=-1), sink).astype(v.dtype)
    out = (jnp.einsum("bnkgqj,bnjkd->bnqkgd", p[..., :3 * BLOCK], vwin)
           + jnp.einsum("bnkgqj,bjkd->bnqkgd", p[..., 3 * BLOCK:], cv))
    return out.reshape(b, n, ATTN_WIDTH)


def merge_branches(yf, ya, zg, w_out):
    gf, ga = jnp.split(jax.nn.sigmoid(zg), N_BRANCHES, axis=-1)
    return (gf * yf + ga * ya) @ w_out


def swiglu(h, w_up, w_down):
    a, u = jnp.split(h @ w_up, 2, axis=-1)
    return (jax.nn.silu(a) * u) @ w_down


def trunk_layer(x, cvec, attend, w_ada, b_ada, g_norm1, g_norm2, w_in, g_q, g_k,
                w_f, w_ao, w_out, w_up, w_down):
    sh1, sc1, gt1, sh2, sc2, gt2 = adaln(cvec, w_ada, b_ada)
    h = rms_norm(x, g_norm1) * (1.0 + sc1) + sh1
    zf, q, k, v, zg = mixer_project(h, w_in, g_q, g_k)
    yf = fourier_mix(zf, w_f)
    ya = attend(q, k, v) @ w_ao
    x = x + gt1 * merge_branches(yf, ya, zg, w_out)
    h = rms_norm(x, g_norm2) * (1.0 + sc2) + sh2
    x = x + gt2 * swiglu(h, w_up, w_down)
    return x, k, v


def setup_inputs(seed: int = 0) -> dict:
    key = jax.random.key(seed)
    ks = jax.random.split(key, 20)

    def nrm(k, shape, s):
        return jax.random.normal(k, shape, jnp.float32) * s

    cache_shape = (DEC_BATCH, DEPTH, PAST_LEN, N_KV_HEADS, HEAD_DIM)
    return {
        "x_prompt": nrm(ks[0], (BATCH, SEQ, D_MODEL), 1.0),
        "x_sample": nrm(ks[1], (DEC_BATCH, DEC_SEQ, D_MODEL), 1.0),
        "cache_k": nrm(ks[2], cache_shape, 1.0),
        "cache_v": nrm(ks[3], cache_shape, 1.0),
        "c": nrm(ks[4], (DEC_BATCH, D_MODEL), 1.0),
        "c_ctx": nrm(ks[5], (D_MODEL,), 1.0),
        "w_ada": nrm(ks[6], (DEPTH, D_MODEL, 6 * D_MODEL), 0.5 * D_MODEL ** -0.5),
        "b_ada": nrm(ks[7], (DEPTH, 6 * D_MODEL), 0.01),
        "g_norm1": 1.0 + nrm(ks[8], (DEPTH, D_MODEL), 0.02),
        "g_norm2": 1.0 + nrm(ks[9], (DEPTH, D_MODEL), 0.02),
        "w_in": nrm(ks[10], (DEPTH, D_MODEL, IN_WIDTH), D_MODEL ** -0.5),
        "g_q": 1.0 + nrm(ks[11], (DEPTH, HEAD_DIM), 0.02),
        "g_k": 1.0 + nrm(ks[12], (DEPTH, HEAD_DIM), 0.02),
        "sinks": nrm(ks[13], (DEPTH, N_HEADS), 1.0),
        "w_f": nrm(ks[14], (DEPTH, F_WIDTH, D_MODEL), F_WIDTH ** -0.5),
        "w_ao": nrm(ks[15], (DEPTH, ATTN_WIDTH, D_MODEL), ATTN_WIDTH ** -0.5),
        "w_out": nrm(ks[16], (DEPTH, D_MODEL, D_MODEL), D_MODEL ** -0.5),
        "w_up": nrm(ks[17], (DEPTH, D_MODEL, 2 * D_FF), D_MODEL ** -0.5),
        "w_down": nrm(ks[18], (DEPTH, D_FF, D_MODEL), D_FF ** -0.5),
    }


def reference(x_prompt, x_sample, cache_k, cache_v, c, c_ctx, w_ada, b_ada, g_norm1, g_norm2,
              w_in, g_q, g_k, sinks, w_f, w_ao, w_out, w_up, w_down):
    xp = x_prompt
    xs = x_sample
    new_k = []
    new_v = []
    for l in range(DEPTH):
        sink = sinks[l].reshape(N_KV_HEADS, GROUP).astype(jnp.float32)
        params = (w_ada[l], b_ada[l], g_norm1[l], g_norm2[l], w_in[l], g_q[l], g_k[l],
                  w_f[l], w_ao[l], w_out[l], w_up[l], w_down[l])

        def attend_ctx(q, k, v, sink=sink):
            return context_attention(q, k, v, sink)

        xp, k_ctx, v_ctx = trunk_layer(xp, c_ctx[None, None, :], attend_ctx, *params)
        new_k.append(k_ctx)
        new_v.append(v_ctx)

        ck = cache_k[:, l]
        cv = cache_v[:, l]

        def attend_lat(q, k, v, sink=sink, ck=ck, cv=cv):
            return latent_attention(axial_rope(q), axial_rope(k), v, ck, cv, sink)

        xs, _, _ = trunk_layer(xs, c[:, None, :], attend_lat, *params)

    new_cache_k = jnp.stack(new_k, axis=1)
    new_cache_v = jnp.stack(new_v, axis=1)
    return (xp, xs, new_cache_k, new_cache_v)
```

```python
import functools

import numpy as np
import jax
import jax.numpy as jnp
from jax import lax
from jax.experimental import pallas as pl
from jax.experimental.pallas import tpu as pltpu

D_MODEL = 1024
GRID_W = 64
N_HEADS = 8
N_KV_HEADS = 2
GROUP = N_HEADS // N_KV_HEADS
HEAD_DIM = 64
WINDOW = 128
F_GROUPS = 4
F_GROUP_DIM = 128
F_WIDTH = F_GROUPS * F_GROUP_DIM
ATTN_WIDTH = N_HEADS * HEAD_DIM
KV_WIDTH = N_KV_HEADS * HEAD_DIM
D_FF = 2816
ROPE_THETA = 10000.0
EPS = 1e-6
NEG_INF = -1e30

LANES = 128
MXU_DIM = 256
VMEM_BYTES_V7X = 64 * 1024 * 1024

_Q0 = F_WIDTH
_K0 = _Q0 + ATTN_WIDTH
_V0 = _K0 + KV_WIDTH
_G0 = _V0 + KV_WIDTH
IN_WIDTH = _G0 + 2 * D_MODEL

MOD_ROWS = 16
CTX_MOD_ROW = 8

TOKEN_TILE = 512
FF_CHUNK = MXU_DIM
Q_TILE = 256
CTX_SEQS_PER_STEP = 4

F32 = jnp.float32
BF16 = jnp.bfloat16


def _vmem_limit(nbytes):
    return int(min(nbytes, VMEM_BYTES_V7X - 4 * 1024 * 1024))


def _dot(a, b):
    return jnp.dot(a, b, preferred_element_type=F32)


def _dot_nt(a, b):
    return lax.dot_general(a, b, (((1,), (1,)), ((), ())), preferred_element_type=F32)


def _sigmoid(x):
    return 1.0 / (1.0 + jnp.exp(-x))


def _resident(shape):
    zeros = (0,) * len(shape)
    return pl.BlockSpec(shape, lambda *_: zeros, pipeline_mode=pl.Buffered(1))


def _ada_kernel(c_ref, w_ref, b_ref, o_ref):
    c = c_ref[...]
    s = c * _sigmoid(c)
    o_ref[...] = jnp.dot(s, w_ref[...], preferred_element_type=F32,
                         precision=lax.Precision.HIGHEST) + b_ref[...]


def _ada(cvec, w_ada, b_ada):
    n = w_ada.shape[1]
    bn = D_MODEL
    return pl.pallas_call(
        _ada_kernel,
        grid=(n // bn,),
        in_specs=[pl.BlockSpec((MOD_ROWS, D_MODEL), lambda j: (0, 0)),
                  pl.BlockSpec((D_MODEL, bn), lambda j: (0, j)),
                  pl.BlockSpec((1, bn), lambda j: (0, j))],
        out_specs=pl.BlockSpec((MOD_ROWS, bn), lambda j: (0, j)),
        out_shape=jax.ShapeDtypeStruct((MOD_ROWS, n), F32),
        name="ada",
    )(cvec, w_ada, b_ada)


def _head_norm(z, g):
    lo = lax.broadcasted_iota(jnp.int32, z.shape, 1) < HEAD_DIM
    s = z * z
    s_lo = jnp.sum(jnp.where(lo, s, 0.0), axis=-1, keepdims=True)
    s_hi = jnp.sum(jnp.where(lo, 0.0, s), axis=-1, keepdims=True)
    ms = jnp.where(lo, s_lo, s_hi) * (1.0 / HEAD_DIM)
    return z * lax.rsqrt(ms + EPS) * g


def _rope(y, cos, sin):
    lane = lax.broadcasted_iota(jnp.int32, y.shape, 1)
    first = (lane % 32) < 16
    partner = jnp.where(first, pltpu.roll(y, LANES - 16, 1), pltpu.roll(y, 16, 1))
    return y * cos + partner * sin


def _dup_heads(y):
    lo = lax.broadcasted_iota(jnp.int32, y.shape, 1) < HEAD_DIM
    sw = pltpu.roll(y, HEAD_DIM, 1)
    return jnp.concatenate([jnp.where(lo, y, sw), jnp.where(lo, sw, y)], axis=-1)


def _proj_kernel(*refs, rope, cache):
    x_ref, mod_ref, g1_ref, w_ref, gq_ref, gk_ref = refs[:6]
    refs = refs[6:]
    if rope:
        cos_ref, sin_ref = refs[:2]
        refs = refs[2:]
    zf_ref, q_ref, kd_ref, vd_ref, gate_ref = refs[:5]
    if cache:
        kc_ref, vc_ref = refs[5:7]

    x = x_ref[...]
    mod = mod_ref[0]
    sh1, sc1 = mod[0:1], mod[1:2]
    h = x * lax.rsqrt(jnp.mean(x * x, axis=-1, keepdims=True) + EPS) * g1_ref[...]
    hb = (h * (1.0 + sc1) + sh1).astype(BF16)

    zf_ref[...] = _dot(hb, w_ref[:, 0:_Q0]).astype(BF16)

    zq = _dot(hb, w_ref[:, _Q0:_K0])
    for j in range(ATTN_WIDTH // LANES):
        y = _head_norm(zq[:, j * LANES:(j + 1) * LANES], gq_ref[...])
        if rope:
            y = _rope(y, cos_ref[...], sin_ref[...])
        q_ref[:, j * LANES:(j + 1) * LANES] = (y * (HEAD_DIM ** -0.5)).astype(BF16)

    zkv = _dot(hb, w_ref[:, _K0:_G0])
    k = _head_norm(zkv[:, 0:KV_WIDTH], gk_ref[...])
    v = zkv[:, KV_WIDTH:2 * KV_WIDTH]
    if cache:
        kc_ref[...] = k
        vc_ref[...] = v
    if rope:
        k = _rope(k, cos_ref[...], sin_ref[...])
    kd_ref[...] = _dup_heads(k).astype(BF16)
    vd_ref[...] = _dup_heads(v).astype(BF16)

    gc = 512
    for j in range(2 * D_MODEL // gc):
        zg = _dot(hb, w_ref[:, _G0 + j * gc:_G0 + (j + 1) * gc])
        gate_ref[:, j * gc:(j + 1) * gc] = _sigmoid(zg).astype(BF16)


def _proj(x, mod3, g1, w_in_b, gq2, gk2, rope_tabs, mod_row_of_tile, cache):
    t = x.shape[0]
    tm = TOKEN_TILE
    rope = rope_tabs is not None
    row = lambda i: (i, 0)
    in_specs = [pl.BlockSpec((tm, D_MODEL), row),
                pl.BlockSpec((1, 6, D_MODEL), lambda i: (mod_row_of_tile(i), 0, 0)),
                _resident((1, D_MODEL)),
                _resident((D_MODEL, IN_WIDTH)),
                _resident((1, LANES)),
                _resident((1, LANES))]
    args = [x, mod3, g1, w_in_b, gq2, gk2]
    if rope:
        tiles_per_seq = rope_tabs[0].shape[0] // tm
        tab = pl.BlockSpec((tm, LANES), lambda i: (i % tiles_per_seq, 0))
        in_specs += [tab, tab]
        args += list(rope_tabs)
    out_specs = [pl.BlockSpec((tm, F_WIDTH), row),
                 pl.BlockSpec((tm, ATTN_WIDTH), row),
                 pl.BlockSpec((tm, 2 * KV_WIDTH), row),
                 pl.BlockSpec((tm, 2 * KV_WIDTH), row),
                 pl.BlockSpec((tm, 2 * D_MODEL), row)]
    out_shape = [jax.ShapeDtypeStruct((t, F_WIDTH), BF16),
                 jax.ShapeDtypeStruct((t, ATTN_WIDTH), BF16),
                 jax.ShapeDtypeStruct((t, 2 * KV_WIDTH), BF16),
                 jax.ShapeDtypeStruct((t, 2 * KV_WIDTH), BF16),
                 jax.ShapeDtypeStruct((t, 2 * D_MODEL), BF16)]
    if cache:
        out_specs += [pl.BlockSpec((tm, KV_WIDTH), row)] * 2
        out_shape += [jax.ShapeDtypeStruct((t, KV_WIDTH), F32)] * 2
    return pl.pallas_call(
        functools.partial(_proj_kernel, rope=rope, cache=cache),
        grid=(t // tm,),
        in_specs=in_specs,
        out_specs=out_specs,
        out_shape=out_shape,
        compiler_params=pltpu.CompilerParams(
            dimension_semantics=("arbitrary",), vmem_limit_bytes=_vmem_limit(48 * 2 ** 20)),
        name="proj_lat" if rope else "proj_ctx",
    )(*args)


def _dft_tables(n):
    def cs(m):
        idx = np.arange(m, dtype=np.int64)
        ang = 2.0 * np.pi * ((idx[:, None] * idx[None, :]) % m).astype(np.float64) / m
        return np.cos(ang) / np.sqrt(m), np.sin(ang) / np.sqrt(m)
    cd, sd = cs(F_GROUP_DIM)
    cn, sn = cs(n)
    csd = np.concatenate([cd, sd], axis=1).astype(np.float32)
    csn = np.concatenate([cn, -sn], axis=1).astype(np.float32)
    return jnp.asarray(csd).astype(BF16), jnp.asarray(csn).astype(BF16)


def _fourier_kernel(zf_ref, csd_ref, csn_ref, o_ref, ab_ref, *, seqs, n):
    for s in range(seqs):
        for g in range(F_GROUPS):
            cols = slice(g * F_GROUP_DIM, (g + 1) * F_GROUP_DIM)
            ab = _dot(zf_ref[s, :, cols], csd_ref[...])
            ab_ref[0:n, cols] = ab[:, 0:F_GROUP_DIM].astype(BF16)
            ab_ref[n:2 * n, cols] = ab[:, F_GROUP_DIM:].astype(BF16)
        o_ref[s] = _dot(csn_ref[...], ab_ref[...]).astype(BF16)


def _fourier(zf3, seqs):
    b, n, _ = zf3.shape
    csd, csn = _dft_tables(n)
    blk = pl.BlockSpec((seqs, n, F_WIDTH), lambda i: (i, 0, 0))
    return pl.pallas_call(
        functools.partial(_fourier_kernel, seqs=seqs, n=n),
        grid=(b // seqs,),
        in_specs=[blk, _resident(csd.shape), _resident(csn.shape)],
        out_specs=blk,
        out_shape=jax.ShapeDtypeStruct(zf3.shape, BF16),
        scratch_shapes=[pltpu.VMEM((2 * n, F_WIDTH), BF16)],
        compiler_params=pltpu.CompilerParams(
            dimension_semantics=("arbitrary",), vmem_limit_bytes=_vmem_limit(32 * 2 ** 20)),
        name=f"fourier_{n}",
    )(zf3, csd, csn)


def _split_lo_hi(xd):
    lo = lax.broadcasted_iota(jnp.int32, xd.shape, 1) < HEAD_DIM
    zero = jnp.zeros_like(xd)
    return jnp.where(lo, xd, zero), jnp.where(lo, zero, xd)


def _pair_attention(q2, sources, sink_e, sink_o):
    s_e = [_dot_nt(q2, k_lo) for k_lo, _, _, _, _ in sources]
    s_o = [_dot_nt(q2, k_hi) for _, k_hi, _, _, _ in sources]
    for i, (_, _, _, _, mask) in enumerate(sources):
        if mask is not None:
            s_e[i] = jnp.where(mask, s_e[i], NEG_INF)
            s_o[i] = jnp.where(mask, s_o[i], NEG_INF)

    def half(scores, sink, values):
        m = sink
        for s in scores:
            m = jnp.maximum(m, jnp.max(s, axis=-1, keepdims=True))
        l = jnp.exp(sink - m)
        o = None
        for s, v in zip(scores, values):
            p = jnp.exp(s - m)
            l = l + jnp.sum(p, axis=-1, keepdims=True)
            pv = _dot(p.astype(BF16), v)
            o = pv if o is None else o + pv
        return o, 1.0 / l

    o_e, inv_e = half(s_e, sink_e, [src[2] for src in sources])
    o_o, inv_o = half(s_o, sink_o, [src[3] for src in sources])
    lo = lax.broadcasted_iota(jnp.int32, o_e.shape, 1) < HEAD_DIM
    return (o_e + o_o) * jnp.where(lo, inv_e, inv_o)


def _attn_ctx_kernel(sink_ref, q_ref, kd_ref, vd_ref, o_ref, *, seqs):
    for s in range(seqs):
        for g in range(N_KV_HEADS):
            cols = slice(g * LANES, (g + 1) * LANES)
            k_lo, k_hi = _split_lo_hi(kd_ref[s, :, cols])
            v_lo, v_hi = _split_lo_hi(vd_ref[s, :, cols])
            for jj in range(GROUP // 2):
                j = g * (GROUP // 2) + jj
                qc = slice(j * LANES, (j + 1) * LANES)
                o = _pair_attention(q_ref[s, :, qc], [(k_lo, k_hi, v_lo, v_hi, None)],
                                    sink_ref[2 * j], sink_ref[2 * j + 1])
                o_ref[s, :, qc] = o.astype(BF16)


def _attn_ctx(sinks, q3, kd3, vd3):
    b, n, _ = q3.shape
    seqs = CTX_SEQS_PER_STEP
    blk = lambda w: pl.BlockSpec((seqs, n, w), lambda i: (i, 0, 0))
    return pl.pallas_call(
        functools.partial(_attn_ctx_kernel, seqs=seqs),
        grid=(b // seqs,),
        in_specs=[pl.BlockSpec(memory_space=pltpu.SMEM),
                  blk(ATTN_WIDTH), blk(2 * KV_WIDTH), blk(2 * KV_WIDTH)],
        out_specs=blk(ATTN_WIDTH),
        out_shape=jax.ShapeDtypeStruct(q3.shape, BF16),
        compiler_params=pltpu.CompilerParams(
            dimension_semantics=("arbitrary",), vmem_limit_bytes=_vmem_limit(32 * 2 ** 20)),
        name="attn_ctx",
    )(sinks, q3, kd3, vd3)


def _attn_lat_kernel(sink_ref, q_ref, kd_ref, vd_ref, ck_ref, cv_ref, o_ref, *, n, tq):
    t = pl.program_id(1)
    span = tq + 2 * WINDOW
    start = jnp.clip(t * tq - WINDOW, 0, n - span)
    start = pl.multiple_of(start, WINDOW)
    q_pos = t * tq + lax.broadcasted_iota(jnp.int32, (tq, span), 0)
    k_pos = start + lax.broadcasted_iota(jnp.int32, (tq, span), 1)
    band = jnp.abs(q_pos - k_pos) <= WINDOW

    ck = _dup_heads(ck_ref[0]).astype(BF16)
    cv = _dup_heads(cv_ref[0]).astype(BF16)
    for g in range(N_KV_HEADS):
        cols = slice(g * LANES, (g + 1) * LANES)
        k_lo, k_hi = _split_lo_hi(kd_ref[0, pl.ds(start, span), cols])
        v_lo, v_hi = _split_lo_hi(vd_ref[0, pl.ds(start, span), cols])
        ck_lo, ck_hi = _split_lo_hi(ck[:, cols])
        cv_lo, cv_hi = _split_lo_hi(cv[:, cols])
        sources = [(k_lo, k_hi, v_lo, v_hi, band), (ck_lo, ck_hi, cv_lo, cv_hi, None)]
        for jj in range(GROUP // 2):
            j = g * (GROUP // 2) + jj
            qc = slice(j * LANES, (j + 1) * LANES)
            o = _pair_attention(q_ref[0, :, qc], sources, sink_ref[2 * j], sink_ref[2 * j + 1])
            o_ref[0, :, qc] = o.astype(BF16)


def _attn_lat(sinks, q3, kd3, vd3, ck3, cv3):
    b, n, _ = q3.shape
    past = ck3.shape[1]
    tq = Q_TILE
    qblk = pl.BlockSpec((1, tq, ATTN_WIDTH), lambda i, t: (i, t, 0))
    seq = lambda rows, w: pl.BlockSpec((1, rows, w), lambda i, t: (i, 0, 0))
    return pl.pallas_call(
        functools.partial(_attn_lat_kernel, n=n, tq=tq),
        grid=(b, n // tq),
        in_specs=[pl.BlockSpec(memory_space=pltpu.SMEM), qblk,
                  seq(n, 2 * KV_WIDTH), seq(n, 2 * KV_WIDTH),
                  seq(past, KV_WIDTH), seq(past, KV_WIDTH)],
        out_specs=qblk,
        out_shape=jax.ShapeDtypeStruct(q3.shape, BF16),
        compiler_params=pltpu.CompilerParams(
            dimension_semantics=("arbitrary", "arbitrary"),
            vmem_limit_bytes=_vmem_limit(32 * 2 ** 20)),
        name="attn_lat",
    )(sinks, q3, kd3, vd3, ck3, cv3)


def _tail_kernel(x_ref, mod_ref, mix_ref, att_ref, gate_ref, g2_ref,
                 wf_ref, wao_ref, wout_ref, wup_ref, wdn_ref, o_ref):
    mod = mod_ref[0]
    gt1, sh2, sc2, gt2 = mod[2:3], mod[3:4], mod[4:5], mod[5:6]

    yf = _dot(mix_ref[...], wf_ref[...])
    ya = _dot(att_ref[...], wao_ref[...])
    merged = (gate_ref[:, 0:D_MODEL].astype(F32) * yf
              + gate_ref[:, D_MODEL:2 * D_MODEL].astype(F32) * ya)
    x1 = x_ref[...] + gt1 * _dot(merged.astype(BF16), wout_ref[...])

    h = x1 * lax.rsqrt(jnp.mean(x1 * x1, axis=-1, keepdims=True) + EPS) * g2_ref[...]
    hb = (h * (1.0 + sc2) + sh2).astype(BF16)
    acc = None
    for c in range(D_FF // FF_CHUNK):
        a = _dot(hb, wup_ref[:, c * FF_CHUNK:(c + 1) * FF_CHUNK])
        u = _dot(hb, wup_ref[:, D_FF + c * FF_CHUNK:D_FF + (c + 1) * FF_CHUNK])
        act = (a * _sigmoid(a) * u).astype(BF16)
        d = _dot(act, wdn_ref[c * FF_CHUNK:(c + 1) * FF_CHUNK, :])
        acc = d if acc is None else acc + d
    o_ref[...] = x1 + gt2 * acc


def _tail(x, mod3, mix, att, gate, g2, wf, wao, wout, wup, wdn, mod_row_of_tile):
    t = x.shape[0]
    tm = TOKEN_TILE
    row = lambda i: (i, 0)
    return pl.pallas_call(
        _tail_kernel,
        grid=(t // tm,),
        in_specs=[pl.BlockSpec((tm, D_MODEL), row),
                  pl.BlockSpec((1, 6, D_MODEL), lambda i: (mod_row_of_tile(i), 0, 0)),
                  pl.BlockSpec((tm, F_WIDTH), row),
                  pl.BlockSpec((tm, ATTN_WIDTH), row),
                  pl.BlockSpec((tm, 2 * D_MODEL), row),
                  _resident((1, D_MODEL)),
                  _resident(wf.shape), _resident(wao.shape), _resident(wout.shape),
                  _resident(wup.shape), _resident(wdn.shape)],
        out_specs=pl.BlockSpec((tm, D_MODEL), row),
        out_shape=jax.ShapeDtypeStruct((t, D_MODEL), F32),
        compiler_params=pltpu.CompilerParams(
            dimension_semantics=("arbitrary",), vmem_limit_bytes=_vmem_limit(56 * 2 ** 20)),
        name="tail",
    )(x, mod3, mix, att, gate, g2, wf, wao, wout, wup, wdn)


def _rope_tables(n):
    rows = n // GRID_W
    row = jnp.repeat(jnp.arange(rows, dtype=F32), GRID_W)
    col = jnp.tile(jnp.arange(GRID_W, dtype=F32), rows)
    axis_dim = HEAD_DIM // 2
    inv_freq = ROPE_THETA ** (-jnp.arange(0, axis_dim, 2, dtype=F32) / axis_dim)

    def axis_tabs(pos):
        ang = pos[:, None] * inv_freq[None, :]
        cos, sin = jnp.cos(ang), jnp.sin(ang)
        return jnp.concatenate([cos, cos], axis=-1), jnp.concatenate([-sin, sin], axis=-1)

    cr, sr = axis_tabs(row)
    cc, sc = axis_tabs(col)
    cos = jnp.concatenate([cr, cc], axis=-1)
    sin = jnp.concatenate([sr, sc], axis=-1)
    return jnp.tile(cos, (1, LANES // HEAD_DIM)), jnp.tile(sin, (1, LANES // HEAD_DIM))


def _layer(xp, xs, ck, cv, c, c_ctx, w_ada, b_ada, g_norm1, g_norm2, w_in, g_q, g_k, sink,
           w_f, w_ao, w_out, w_up, w_down):
    bp, sp, _ = xp.shape
    bs, ss, _ = xs.shape
    past = ck.shape[1]

    cvec = jnp.concatenate(
        [c, c_ctx[None, :], jnp.zeros((MOD_ROWS - bs - 1, D_MODEL), F32)], axis=0)
    mod3 = _ada(cvec, w_ada, b_ada[None, :]).reshape(MOD_ROWS, 6, D_MODEL)

    w_in_b = w_in.astype(BF16)
    wf, wao, wout = w_f.astype(BF16), w_ao.astype(BF16), w_out.astype(BF16)
    wup, wdn = w_up.astype(BF16), w_down.astype(BF16)
    g1, g2 = g_norm1[None, :], g_norm2[None, :]
    gq2 = jnp.tile(g_q, LANES // HEAD_DIM)[None, :]
    gk2 = jnp.tile(g_k, LANES // HEAD_DIM)[None, :]

    ctx_row = lambda i: CTX_MOD_ROW
    lat_row = lambda i: i // (ss // TOKEN_TILE)

    xp2 = xp.reshape(bp * sp, D_MODEL)
    zf, q, kd, vd, gate, k_new, v_new = _proj(xp2, mod3, g1, w_in_b, gq2, gk2, None, ctx_row, True)
    mix = _fourier(zf.reshape(bp, sp, F_WIDTH), CTX_SEQS_PER_STEP)
    att = _attn_ctx(sink, q.reshape(bp, sp, ATTN_WIDTH), kd.reshape(bp, sp, 2 * KV_WIDTH),
                    vd.reshape(bp, sp, 2 * KV_WIDTH))
    yp = _tail(xp2, mod3, mix.reshape(bp * sp, F_WIDTH), att.reshape(bp * sp, ATTN_WIDTH), gate,
               g2, wf, wao, wout, wup, wdn, ctx_row)

    xs2 = xs.reshape(bs * ss, D_MODEL)
    zf, q, kd, vd, gate = _proj(xs2, mod3, g1, w_in_b, gq2, gk2, _rope_tables(ss), lat_row, False)
    mix = _fourier(zf.reshape(bs, ss, F_WIDTH), 1)
    att = _attn_lat(sink, q.reshape(bs, ss, ATTN_WIDTH), kd.reshape(bs, ss, 2 * KV_WIDTH),
                    vd.reshape(bs, ss, 2 * KV_WIDTH), ck.reshape(bs, past, KV_WIDTH),
                    cv.reshape(bs, past, KV_WIDTH))
    ys = _tail(xs2, mod3, mix.reshape(bs * ss, F_WIDTH), att.reshape(bs * ss, ATTN_WIDTH), gate,
               g2, wf, wao, wout, wup, wdn, lat_row)

    return (yp.reshape(xp.shape), ys.reshape(xs.shape),
            k_new.reshape(bp, sp, N_KV_HEADS, HEAD_DIM), v_new.reshape(bp, sp, N_KV_HEADS, HEAD_DIM))


def kernel(x_prompt, x_sample, cache_k, cache_v, c, c_ctx, w_ada, b_ada, g_norm1, g_norm2,
           w_in, g_q, g_k, sinks, w_f, w_ao, w_out, w_up, w_down):
    depth = w_in.shape[0]
    xp, xs = x_prompt, x_sample
    new_k, new_v = [], []
    for l in range(depth):
        xp, xs, k_ctx, v_ctx = _layer(
            xp, xs, cache_k[:, l], cache_v[:, l], c, c_ctx, w_ada[l], b_ada[l], g_norm1[l],
            g_norm2[l], w_in[l], g_q[l], g_k[l], sinks[l], w_f[l], w_ao[l], w_out[l], w_up[l],
            w_down[l])
        new_k.append(k_ctx)
        new_v.append(v_ctx)
    return (xp, xs, jnp.stack(new_k, axis=1), jnp.stack(new_v, axis=1))
```

```python
import functools

import numpy as np
import jax
import jax.numpy as jnp
from jax import lax
from jax.experimental import pallas as pl
from jax.experimental.pallas import tpu as pltpu

D_MODEL = 1024
GRID_W = 64
N_HEADS = 8
N_KV_HEADS = 2
GROUP = N_HEADS // N_KV_HEADS
HEAD_DIM = 64
WINDOW = 128
F_GROUPS = 4
F_GROUP_DIM = 128
F_WIDTH = F_GROUPS * F_GROUP_DIM
ATTN_WIDTH = N_HEADS * HEAD_DIM
KV_WIDTH = N_KV_HEADS * HEAD_DIM
ROPE_THETA = 10000.0
EPS = 1e-6
NEG_INF = -1e30

LANES = 128
MXU_DIM = 256
VMEM_BYTES_V7X = 64 * 1024 * 1024

_Q0 = F_WIDTH
_K0 = _Q0 + ATTN_WIDTH
_V0 = _K0 + KV_WIDTH
_G0 = _V0 + KV_WIDTH
IN_WIDTH = _G0 + 2 * D_MODEL

MOD_ROWS = 16
CTX_MOD_ROW = 8

TOKEN_TILE = 512
FF_CHUNK = MXU_DIM
Q_TILE = 256
CTX_SEQS_PER_STEP = 4

F32 = jnp.float32
BF16 = jnp.bfloat16


def _vmem_limit(nbytes):
    return int(min(nbytes, VMEM_BYTES_V7X - 4 * 1024 * 1024))


def _dot(a, b):
    return jnp.dot(a, b, preferred_element_type=F32)


def _dot_nt(a, b):
    return lax.dot_general(a, b, (((1,), (1,)), ((), ())), preferred_element_type=F32)


def _sigmoid(x):
    return 1.0 / (1.0 + jnp.exp(-x))


def _resident(shape):
    zeros = (0,) * len(shape)
    return pl.BlockSpec(shape, lambda *_: zeros, pipeline_mode=pl.Buffered(1))


def _hbm(x):
    return pltpu.with_memory_space_constraint(x, pltpu.HBM)


def _ada_kernel(c_ref, w_ref, b_ref, o_ref):
    c = c_ref[...]
    s = c * _sigmoid(c)
    o_ref[...] = jnp.dot(s, w_ref[...], preferred_element_type=F32,
                         precision=lax.Precision.HIGHEST) + b_ref[...]


def _ada(cvec, w_ada, b_ada):
    n = w_ada.shape[1]
    bn = D_MODEL
    return pl.pallas_call(
        _ada_kernel,
        grid=(n // bn,),
        in_specs=[pl.BlockSpec((MOD_ROWS, D_MODEL), lambda j: (0, 0)),
                  pl.BlockSpec((D_MODEL, bn), lambda j: (0, j)),
                  pl.BlockSpec((1, bn), lambda j: (0, j))],
        out_specs=pl.BlockSpec((MOD_ROWS, bn), lambda j: (0, j)),
        out_shape=jax.ShapeDtypeStruct((MOD_ROWS, n), F32),
        name="ada",
    )(cvec, w_ada, b_ada)


def _head_norm(z, g):
    lo = lax.broadcasted_iota(jnp.int32, z.shape, 1) < HEAD_DIM
    s = z * z
    s_lo = jnp.sum(jnp.where(lo, s, 0.0), axis=-1, keepdims=True)
    s_hi = jnp.sum(jnp.where(lo, 0.0, s), axis=-1, keepdims=True)
    ms = jnp.where(lo, s_lo, s_hi) * (1.0 / HEAD_DIM)
    return z * lax.rsqrt(ms + EPS) * g


def _rope(y, cos, sin):
    lane = lax.broadcasted_iota(jnp.int32, y.shape, 1)
    first = (lane % 32) < 16
    partner = jnp.where(first, pltpu.roll(y, LANES - 16, 1), pltpu.roll(y, 16, 1))
    return y * cos + partner * sin


def _dup_heads(y):
    lo = lax.broadcasted_iota(jnp.int32, y.shape, 1) < HEAD_DIM
    sw = pltpu.roll(y, HEAD_DIM, 1)
    return jnp.concatenate([jnp.where(lo, y, sw), jnp.where(lo, sw, y)], axis=-1)


def _proj_kernel(*refs, rope, cache_seq, n_cast):
    x_ref, mod_ref, g1_ref, w_ref, gq_ref, gk_ref = refs[:6]
    refs = refs[6:]
    if rope:
        cos_ref, sin_ref = refs[:2]
        refs = refs[2:]
    cast_in, refs = refs[:n_cast], refs[n_cast:]
    zf_ref, q_ref, kd_ref, vd_ref, gate_ref = refs[:5]
    refs = refs[5:]
    if cache_seq:
        kc_ref, vc_ref = refs[:2]
        refs = refs[2:]
    for src_ref, dst_ref in zip(cast_in, refs):
        dst_ref[...] = src_ref[...].astype(BF16)

    x = x_ref[...]
    mod = mod_ref[0]
    sh1, sc1 = mod[0:1], mod[1:2]
    h = x * lax.rsqrt(jnp.mean(x * x, axis=-1, keepdims=True) + EPS) * g1_ref[...]
    hb = (h * (1.0 + sc1) + sh1).astype(BF16)

    zf_ref[...] = _dot(hb, w_ref[:, 0:_Q0]).astype(BF16)

    zq = _dot(hb, w_ref[:, _Q0:_K0])
    for j in range(ATTN_WIDTH // LANES):
        y = _head_norm(zq[:, j * LANES:(j + 1) * LANES], gq_ref[...])
        if rope:
            y = _rope(y, cos_ref[...], sin_ref[...])
        q_ref[:, j * LANES:(j + 1) * LANES] = (y * (HEAD_DIM ** -0.5)).astype(BF16)

    zkv = _dot(hb, w_ref[:, _K0:_G0])
    k = _head_norm(zkv[:, 0:KV_WIDTH], gk_ref[...])
    v = zkv[:, KV_WIDTH:2 * KV_WIDTH]
    if cache_seq:
        for s in range(x.shape[0] // cache_seq):
            rows = slice(s * cache_seq, (s + 1) * cache_seq)
            kc_ref[s] = k[rows, :].T
            vc_ref[s] = v[rows, :].T
    if rope:
        k = _rope(k, cos_ref[...], sin_ref[...])
    kd_ref[...] = _dup_heads(k).astype(BF16)
    vd_ref[...] = _dup_heads(v).astype(BF16)

    gc = 512
    for j in range(2 * D_MODEL // gc):
        zg = _dot(hb, w_ref[:, _G0 + j * gc:_G0 + (j + 1) * gc])
        gate_ref[:, j * gc:(j + 1) * gc] = _sigmoid(zg).astype(BF16)


def _proj(x, mod3, g1, w_in_b, gq2, gk2, rope_tabs, mod_row_of_tile, cache_seq, cast_weights):
    t = x.shape[0]
    tm = TOKEN_TILE
    steps = t // tm
    rope = rope_tabs is not None
    row = lambda i: (i, 0)
    in_specs = [pl.BlockSpec((tm, D_MODEL), row),
                pl.BlockSpec((1, 6, D_MODEL), lambda i: (mod_row_of_tile(i), 0, 0)),
                _resident((1, D_MODEL)),
                _resident((D_MODEL, IN_WIDTH)),
                _resident((1, LANES)),
                _resident((1, LANES))]
    args = [x, mod3, g1, w_in_b, gq2, gk2]
    if rope:
        tiles_per_seq = rope_tabs[0].shape[0] // tm
        tab = pl.BlockSpec((tm, LANES), lambda i: (i % tiles_per_seq, 0))
        in_specs += [tab, tab]
        args += list(rope_tabs)
    slabs = [pl.BlockSpec((w.shape[0] // steps, w.shape[1]), row) for w in cast_weights]
    in_specs += slabs
    args += list(cast_weights)
    out_specs = [pl.BlockSpec((tm, F_WIDTH), row),
                 pl.BlockSpec((tm, ATTN_WIDTH), row),
                 pl.BlockSpec((tm, 2 * KV_WIDTH), row),
                 pl.BlockSpec((tm, 2 * KV_WIDTH), row),
                 pl.BlockSpec((tm, 2 * D_MODEL), row)]
    out_shape = [pltpu.HBM((t, F_WIDTH), BF16),
                 pltpu.HBM((t, ATTN_WIDTH), BF16),
                 pltpu.HBM((t, 2 * KV_WIDTH), BF16),
                 pltpu.HBM((t, 2 * KV_WIDTH), BF16),
                 pltpu.HBM((t, 2 * D_MODEL), BF16)]
    if cache_seq:
        seqs = tm // cache_seq
        out_specs += [pl.BlockSpec((seqs, KV_WIDTH, cache_seq), lambda i: (i, 0, 0))] * 2
        out_shape += [pltpu.HBM((t // cache_seq, KV_WIDTH, cache_seq), F32)] * 2
    out_specs += slabs
    out_shape += [pltpu.HBM(w.shape, BF16) for w in cast_weights]
    return pl.pallas_call(
        functools.partial(_proj_kernel, rope=rope, cache_seq=cache_seq, n_cast=len(cast_weights)),
        grid=(steps,),
        in_specs=in_specs,
        out_specs=out_specs,
        out_shape=out_shape,
        compiler_params=pltpu.CompilerParams(
            dimension_semantics=("arbitrary",), vmem_limit_bytes=_vmem_limit(48 * 2 ** 20)),
        name="proj_lat" if rope else "proj_ctx",
    )(*[_hbm(a) for a in args])


def _dft_tables(n):
    def cs(m):
        idx = np.arange(m, dtype=np.int64)
        ang = 2.0 * np.pi * ((idx[:, None] * idx[None, :]) % m).astype(np.float64) / m
        return np.cos(ang) / np.sqrt(m), np.sin(ang) / np.sqrt(m)
    cd, sd = cs(F_GROUP_DIM)
    cn, sn = cs(n)
    csd = np.concatenate([cd, sd], axis=1).astype(np.float32)
    csn = np.concatenate([cn, -sn], axis=1).astype(np.float32)
    return jnp.asarray(csd).astype(BF16), jnp.asarray(csn).astype(BF16)


def _fourier_kernel(zf_ref, csd_ref, csn_ref, o_ref, ab_ref, *, seqs, n):
    for s in range(seqs):
        for g in range(F_GROUPS):
            cols = slice(g * F_GROUP_DIM, (g + 1) * F_GROUP_DIM)
            ab = _dot(zf_ref[s, :, cols], csd_ref[...])
            ab_ref[0:n, cols] = ab[:, 0:F_GROUP_DIM].astype(BF16)
            ab_ref[n:2 * n, cols] = ab[:, F_GROUP_DIM:].astype(BF16)
        o_ref[s] = _dot(csn_ref[...], ab_ref[...]).astype(BF16)


def _fourier(zf3, seqs):
    b, n, _ = zf3.shape
    csd, csn = _dft_tables(n)
    blk = pl.BlockSpec((seqs, n, F_WIDTH), lambda i: (i, 0, 0))
    return pl.pallas_call(
        functools.partial(_fourier_kernel, seqs=seqs, n=n),
        grid=(b // seqs,),
        in_specs=[blk, _resident(csd.shape), _resident(csn.shape)],
        out_specs=blk,
        out_shape=pltpu.HBM(zf3.shape, BF16),
        scratch_shapes=[pltpu.VMEM((2 * n, F_WIDTH), BF16)],
        compiler_params=pltpu.CompilerParams(
            dimension_semantics=("arbitrary",), vmem_limit_bytes=_vmem_limit(32 * 2 ** 20)),
        name=f"fourier_{n}",
    )(_hbm(zf3), csd, csn)


def _split_lo_hi(xd):
    lo = lax.broadcasted_iota(jnp.int32, xd.shape, 1) < HEAD_DIM
    zero = jnp.zeros_like(xd)
    return jnp.where(lo, xd, zero), jnp.where(lo, zero, xd)


def _pair_attention(q2, sources, sink_e, sink_o):
    s_e = [_dot_nt(q2, k_lo) for k_lo, _, _, _, _ in sources]
    s_o = [_dot_nt(q2, k_hi) for _, k_hi, _, _, _ in sources]
    for i, (_, _, _, _, mask) in enumerate(sources):
        if mask is not None:
            s_e[i] = jnp.where(mask, s_e[i], NEG_INF)
            s_o[i] = jnp.where(mask, s_o[i], NEG_INF)

    def half(scores, sink, values):
        m = sink
        for s in scores:
            m = jnp.maximum(m, jnp.max(s, axis=-1, keepdims=True))
        l = jnp.exp(sink - m)
        o = None
        for s, v in zip(scores, values):
            p = jnp.exp(s - m)
            l = l + jnp.sum(p, axis=-1, keepdims=True)
            pv = _dot(p.astype(BF16), v)
            o = pv if o is None else o + pv
        return o, 1.0 / l

    o_e, inv_e = half(s_e, sink_e, [src[2] for src in sources])
    o_o, inv_o = half(s_o, sink_o, [src[3] for src in sources])
    lo = lax.broadcasted_iota(jnp.int32, o_e.shape, 1) < HEAD_DIM
    return (o_e + o_o) * jnp.where(lo, inv_e, inv_o)


def _attn_ctx_kernel(sink_ref, q_ref, kd_ref, vd_ref, o_ref, *, seqs):
    for s in range(seqs):
        for g in range(N_KV_HEADS):
            cols = slice(g * LANES, (g + 1) * LANES)
            k_lo, k_hi = _split_lo_hi(kd_ref[s, :, cols])
            v_lo, v_hi = _split_lo_hi(vd_ref[s, :, cols])
            for jj in range(GROUP // 2):
                j = g * (GROUP // 2) + jj
                qc = slice(j * LANES, (j + 1) * LANES)
                o = _pair_attention(q_ref[s, :, qc], [(k_lo, k_hi, v_lo, v_hi, None)],
                                    sink_ref[2 * j], sink_ref[2 * j + 1])
                o_ref[s, :, qc] = o.astype(BF16)


def _attn_ctx(sinks, q3, kd3, vd3):
    b, n, _ = q3.shape
    seqs = CTX_SEQS_PER_STEP
    blk = lambda w: pl.BlockSpec((seqs, n, w), lambda i: (i, 0, 0))
    return pl.pallas_call(
        functools.partial(_attn_ctx_kernel, seqs=seqs),
        grid=(b // seqs,),
        in_specs=[pl.BlockSpec(memory_space=pltpu.SMEM),
                  blk(ATTN_WIDTH), blk(2 * KV_WIDTH), blk(2 * KV_WIDTH)],
        out_specs=blk(ATTN_WIDTH),
        out_shape=pltpu.HBM(q3.shape, BF16),
        compiler_params=pltpu.CompilerParams(
            dimension_semantics=("arbitrary",), vmem_limit_bytes=_vmem_limit(32 * 2 ** 20)),
        name="attn_ctx",
    )(sinks, _hbm(q3), _hbm(kd3), _hbm(vd3))


def _attn_lat_kernel(sink_ref, q_ref, kd_ref, vd_ref, ck_ref, cv_ref, o_ref, *, n, tq):
    t = pl.program_id(1)
    span = tq + 2 * WINDOW
    start = jnp.clip(t * tq - WINDOW, 0, n - span)
    start = pl.multiple_of(start, WINDOW)
    q_pos = t * tq + lax.broadcasted_iota(jnp.int32, (tq, span), 0)
    k_pos = start + lax.broadcasted_iota(jnp.int32, (tq, span), 1)
    band = jnp.abs(q_pos - k_pos) <= WINDOW

    ck = _dup_heads(ck_ref[0]).astype(BF16)
    cv = _dup_heads(cv_ref[0]).astype(BF16)
    for g in range(N_KV_HEADS):
        cols = slice(g * LANES, (g + 1) * LANES)
        k_lo, k_hi = _split_lo_hi(kd_ref[0, pl.ds(start, span), cols])
        v_lo, v_hi = _split_lo_hi(vd_ref[0, pl.ds(start, span), cols])
        ck_lo, ck_hi = _split_lo_hi(ck[:, cols])
        cv_lo, cv_hi = _split_lo_hi(cv[:, cols])
        sources = [(k_lo, k_hi, v_lo, v_hi, band), (ck_lo, ck_hi, cv_lo, cv_hi, None)]
        for jj in range(GROUP // 2):
            j = g * (GROUP // 2) + jj
            qc = slice(j * LANES, (j + 1) * LANES)
            o = _pair_attention(q_ref[0, :, qc], sources, sink_ref[2 * j], sink_ref[2 * j + 1])
            o_ref[0, :, qc] = o.astype(BF16)


def _attn_lat(sinks, q3, kd3, vd3, ck3, cv3):
    b, n, _ = q3.shape
    past = ck3.shape[1]
    tq = Q_TILE
    qblk = pl.BlockSpec((1, tq, ATTN_WIDTH), lambda i, t: (i, t, 0))
    seq = lambda rows, w: pl.BlockSpec((1, rows, w), lambda i, t: (i, 0, 0))
    return pl.pallas_call(
        functools.partial(_attn_lat_kernel, n=n, tq=tq),
        grid=(b, n // tq),
        in_specs=[pl.BlockSpec(memory_space=pltpu.SMEM), qblk,
                  seq(n, 2 * KV_WIDTH), seq(n, 2 * KV_WIDTH),
                  seq(past, KV_WIDTH), seq(past, KV_WIDTH)],
        out_specs=qblk,
        out_shape=pltpu.HBM(q3.shape, BF16),
        compiler_params=pltpu.CompilerParams(
            dimension_semantics=("arbitrary", "arbitrary"),
            vmem_limit_bytes=_vmem_limit(32 * 2 ** 20)),
        name="attn_lat",
    )(sinks, _hbm(q3), _hbm(kd3), _hbm(vd3), _hbm(ck3), _hbm(cv3))


def _tail_kernel(x_ref, mod_ref, mix_ref, att_ref, gate_ref, g2_ref,
                 wf_ref, wao_ref, wout_ref, wup_ref, wdn_ref, o_ref):
    mod = mod_ref[0]
    gt1, sh2, sc2, gt2 = mod[2:3], mod[3:4], mod[4:5], mod[5:6]

    yf = _dot(mix_ref[...], wf_ref[...])
    ya = _dot(att_ref[...], wao_ref[...])
    merged = (gate_ref[:, 0:D_MODEL].astype(F32) * yf
              + gate_ref[:, D_MODEL:2 * D_MODEL].astype(F32) * ya)
    x1 = x_ref[...] + gt1 * _dot(merged.astype(BF16), wout_ref[...])

    h = x1 * lax.rsqrt(jnp.mean(x1 * x1, axis=-1, keepdims=True) + EPS) * g2_ref[...]
    hb = (h * (1.0 + sc2) + sh2).astype(BF16)
    d_ff = wdn_ref.shape[0]
    acc = None
    for c in range(d_ff // FF_CHUNK):
        a = _dot(hb, wup_ref[:, c * FF_CHUNK:(c + 1) * FF_CHUNK])
        u = _dot(hb, wup_ref[:, d_ff + c * FF_CHUNK:d_ff + (c + 1) * FF_CHUNK])
        act = (a * _sigmoid(a) * u).astype(BF16)
        d = _dot(act, wdn_ref[c * FF_CHUNK:(c + 1) * FF_CHUNK, :])
        acc = d if acc is None else acc + d
    o_ref[...] = x1 + gt2 * acc


def _tail(x, mod3, mix, att, gate, g2, wf, wao, wout, wup, wdn, mod_row_of_tile):
    t = x.shape[0]
    tm = TOKEN_TILE
    row = lambda i: (i, 0)
    return pl.pallas_call(
        _tail_kernel,
        grid=(t // tm,),
        in_specs=[pl.BlockSpec((tm, D_MODEL), row),
                  pl.BlockSpec((1, 6, D_MODEL), lambda i: (mod_row_of_tile(i), 0, 0)),
                  pl.BlockSpec((tm, F_WIDTH), row),
                  pl.BlockSpec((tm, ATTN_WIDTH), row),
                  pl.BlockSpec((tm, 2 * D_MODEL), row),
                  _resident((1, D_MODEL)),
                  _resident(wf.shape), _resident(wao.shape), _resident(wout.shape),
                  _resident(wup.shape), _resident(wdn.shape)],
        out_specs=pl.BlockSpec((tm, D_MODEL), row),
        out_shape=pltpu.HBM((t, D_MODEL), F32),
        compiler_params=pltpu.CompilerParams(
            dimension_semantics=("arbitrary",), vmem_limit_bytes=_vmem_limit(56 * 2 ** 20)),
        name="tail",
    )(*[_hbm(a) for a in (x, mod3, mix, att, gate, g2, wf, wao, wout, wup, wdn)])


def _rope_tables(n):
    rows = n // GRID_W
    row = jnp.repeat(jnp.arange(rows, dtype=F32), GRID_W)
    col = jnp.tile(jnp.arange(GRID_W, dtype=F32), rows)
    axis_dim = HEAD_DIM // 2
    inv_freq = ROPE_THETA ** (-jnp.arange(0, axis_dim, 2, dtype=F32) / axis_dim)

    def axis_tabs(pos):
        ang = pos[:, None] * inv_freq[None, :]
        cos, sin = jnp.cos(ang), jnp.sin(ang)
        return jnp.concatenate([cos, cos], axis=-1), jnp.concatenate([-sin, sin], axis=-1)

    cr, sr = axis_tabs(row)
    cc, sc = axis_tabs(col)
    cos = jnp.concatenate([cr, cc], axis=-1)
    sin = jnp.concatenate([sr, sc], axis=-1)
    return jnp.tile(cos, (1, LANES // HEAD_DIM)), jnp.tile(sin, (1, LANES // HEAD_DIM))


def _layer(xp, xs, ck, cv, c, c_ctx, w_ada, b_ada, g_norm1, g_norm2, w_in, g_q, g_k, sink,
           w_f, w_ao, w_out, w_up, w_down):
    bp, sp, _ = xp.shape
    bs, ss, _ = xs.shape
    past = ck.shape[1]

    cvec = jnp.concatenate(
        [c, c_ctx[None, :], jnp.zeros((MOD_ROWS - bs - 1, D_MODEL), F32)], axis=0)
    mod3 = _ada(cvec, w_ada, b_ada[None, :]).reshape(MOD_ROWS, 6, D_MODEL)

    w_in_b = w_in.astype(BF16)
    g1, g2 = g_norm1[None, :], g_norm2[None, :]
    gq2 = jnp.tile(g_q, LANES // HEAD_DIM)[None, :]
    gk2 = jnp.tile(g_k, LANES // HEAD_DIM)[None, :]

    ctx_row = lambda i: CTX_MOD_ROW
    lat_row = lambda i: i // (ss // TOKEN_TILE)

    xp2 = xp.reshape(bp * sp, D_MODEL)
    zf, q, kd, vd, gate, k_new, v_new, wf, wao, wout, wup, wdn = _proj(
        xp2, mod3, g1, w_in_b, gq2, gk2, None, ctx_row, sp, [w_f, w_ao, w_out, w_up, w_down])
    mix = _fourier(zf.reshape(bp, sp, F_WIDTH), CTX_SEQS_PER_STEP)
    att = _attn_ctx(sink, q.reshape(bp, sp, ATTN_WIDTH), kd.reshape(bp, sp, 2 * KV_WIDTH),
                    vd.reshape(bp, sp, 2 * KV_WIDTH))
    yp = _tail(xp2, mod3, mix.reshape(bp * sp, F_WIDTH), att.reshape(bp * sp, ATTN_WIDTH), gate,
               g2, wf, wao, wout, wup, wdn, ctx_row)

    xs2 = xs.reshape(bs * ss, D_MODEL)
    zf, q, kd, vd, gate = _proj(xs2, mod3, g1, w_in_b, gq2, gk2, _rope_tables(ss), lat_row, 0, [])
    mix = _fourier(zf.reshape(bs, ss, F_WIDTH), 1)
    att = _attn_lat(sink, q.reshape(bs, ss, ATTN_WIDTH), kd.reshape(bs, ss, 2 * KV_WIDTH),
                    vd.reshape(bs, ss, 2 * KV_WIDTH), ck.reshape(bs, past, KV_WIDTH),
                    cv.reshape(bs, past, KV_WIDTH))
    ys = _tail(xs2, mod3, mix.reshape(bs * ss, F_WIDTH), att.reshape(bs * ss, ATTN_WIDTH), gate,
               g2, wf, wao, wout, wup, wdn, lat_row)

    def cache_layout(t):
        return t.reshape(bp, N_KV_HEADS, HEAD_DIM, sp).transpose(0, 3, 1, 2)

    return yp.reshape(xp.shape), ys.reshape(xs.shape), cache_layout(k_new), cache_layout(v_new)


def kernel(x_prompt, x_sample, cache_k, cache_v, c, c_ctx, w_ada, b_ada, g_norm1, g_norm2,
           w_in, g_q, g_k, sinks, w_f, w_ao, w_out, w_up, w_down):
    depth = w_in.shape[0]
    xp, xs = x_prompt, x_sample
    new_k, new_v = [], []
    for l in range(depth):
        xp, xs, k_ctx, v_ctx = _layer(
            xp, xs, cache_k[:, l], cache_v[:, l], c, c_ctx, w_ada[l], b_ada[l], g_norm1[l],
            g_norm2[l], w_in[l], g_q[l], g_k[l], sinks[l], w_f[l], w_ao[l], w_out[l], w_up[l],
            w_down[l])
        new_k.append(k_ctx)
        new_v.append(v_ctx)
    return (xp, xs, jnp.stack(new_k, axis=1), jnp.stack(new_v, axis=1))
```

```python
import functools

import numpy as np
import jax
import jax.numpy as jnp
from jax import lax
from jax.experimental import pallas as pl
from jax.experimental.pallas import tpu as pltpu

D_MODEL = 1024
GRID_W = 64
N_HEADS = 8
N_KV_HEADS = 2
GROUP = N_HEADS // N_KV_HEADS
HEAD_DIM = 64
WINDOW = 128
F_GROUPS = 4
F_GROUP_DIM = 128
F_WIDTH = F_GROUPS * F_GROUP_DIM
ATTN_WIDTH = N_HEADS * HEAD_DIM
KV_WIDTH = N_KV_HEADS * HEAD_DIM
ROPE_THETA = 10000.0
EPS = 1e-6
NEG_INF = -1e30

LANES = 128
MXU_DIM = 256
VMEM_BYTES_V7X = 64 * 1024 * 1024

_Q0 = F_WIDTH
_K0 = _Q0 + ATTN_WIDTH
_V0 = _K0 + KV_WIDTH
_G0 = _V0 + KV_WIDTH
IN_WIDTH = _G0 + 2 * D_MODEL

MOD_ROWS = 16
CTX_MOD_ROW = 8

TOKEN_TILE = 512
FF_CHUNK = MXU_DIM
Q_TILE = 256
CTX_SEQS_PER_STEP = 4
CTX_SOFTMAX_ROWS = 64
LAT_SOFTMAX_ROWS = 32

F32 = jnp.float32
BF16 = jnp.bfloat16


def _vmem_limit(nbytes):
    return int(min(nbytes, VMEM_BYTES_V7X - 4 * 1024 * 1024))


def _dot(a, b):
    return jnp.dot(a, b, preferred_element_type=F32)


def _dot_nt(a, b):
    return lax.dot_general(a, b, (((1,), (1,)), ((), ())), preferred_element_type=F32)


def _sigmoid(x):
    return 1.0 / (1.0 + jnp.exp(-x))


def _resident(shape):
    zeros = (0,) * len(shape)
    return pl.BlockSpec(shape, lambda *_: zeros, pipeline_mode=pl.Buffered(1))


def _hbm(x):
    return pltpu.with_memory_space_constraint(x, pltpu.HBM)


def _ada_kernel(c_ref, w_ref, b_ref, o_ref):
    c = c_ref[...]
    s = c * _sigmoid(c)
    o_ref[...] = jnp.dot(s, w_ref[...], preferred_element_type=F32,
                         precision=lax.Precision.HIGHEST) + b_ref[...]


def _ada(cvec, w_ada, b_ada):
    n = w_ada.shape[1]
    bn = D_MODEL
    return pl.pallas_call(
        _ada_kernel,
        grid=(n // bn,),
        in_specs=[pl.BlockSpec((MOD_ROWS, D_MODEL), lambda j: (0, 0)),
                  pl.BlockSpec((D_MODEL, bn), lambda j: (0, j)),
                  pl.BlockSpec((1, bn), lambda j: (0, j))],
        out_specs=pl.BlockSpec((MOD_ROWS, bn), lambda j: (0, j)),
        out_shape=jax.ShapeDtypeStruct((MOD_ROWS, n), F32),
        name="ada",
    )(cvec, w_ada, b_ada)


def _head_norm(z, g):
    lo = lax.broadcasted_iota(jnp.int32, z.shape, 1) < HEAD_DIM
    s = z * z
    s_lo = jnp.sum(jnp.where(lo, s, 0.0), axis=-1, keepdims=True)
    s_hi = jnp.sum(jnp.where(lo, 0.0, s), axis=-1, keepdims=True)
    ms = jnp.where(lo, s_lo, s_hi) * (1.0 / HEAD_DIM)
    return z * lax.rsqrt(ms + EPS) * g


def _rope(y, cos, sin):
    lane = lax.broadcasted_iota(jnp.int32, y.shape, 1)
    first = (lane % 32) < 16
    partner = jnp.where(first, pltpu.roll(y, LANES - 16, 1), pltpu.roll(y, 16, 1))
    return y * cos + partner * sin


def _dup_heads(y):
    lo = lax.broadcasted_iota(jnp.int32, y.shape, 1) < HEAD_DIM
    sw = pltpu.roll(y, HEAD_DIM, 1)
    return jnp.concatenate([jnp.where(lo, y, sw), jnp.where(lo, sw, y)], axis=-1)


def _proj_kernel(*refs, rope, cache_seq, n_cast):
    x_ref, mod_ref, g1_ref, w_ref, gq_ref, gk_ref = refs[:6]
    refs = refs[6:]
    if rope:
        cos_ref, sin_ref = refs[:2]
        refs = refs[2:]
    cast_in, refs = refs[:n_cast], refs[n_cast:]
    zf_ref, q_ref, kd_ref, vd_ref, gate_ref = refs[:5]
    refs = refs[5:]
    if cache_seq:
        kc_ref, vc_ref = refs[:2]
        refs = refs[2:]
    for src_ref, dst_ref in zip(cast_in, refs):
        dst_ref[...] = src_ref[...].astype(BF16)

    x = x_ref[...]
    mod = mod_ref[0]
    sh1, sc1 = mod[0:1], mod[1:2]
    h = x * lax.rsqrt(jnp.mean(x * x, axis=-1, keepdims=True) + EPS) * g1_ref[...]
    hb = (h * (1.0 + sc1) + sh1).astype(BF16)

    zf_ref[...] = _dot(hb, w_ref[:, 0:_Q0]).astype(BF16)

    zq = _dot(hb, w_ref[:, _Q0:_K0])
    for j in range(ATTN_WIDTH // LANES):
        y = _head_norm(zq[:, j * LANES:(j + 1) * LANES], gq_ref[...])
        if rope:
            y = _rope(y, cos_ref[...], sin_ref[...])
        q_ref[:, j * LANES:(j + 1) * LANES] = (y * (HEAD_DIM ** -0.5)).astype(BF16)

    zkv = _dot(hb, w_ref[:, _K0:_G0])
    k = _head_norm(zkv[:, 0:KV_WIDTH], gk_ref[...])
    v = zkv[:, KV_WIDTH:2 * KV_WIDTH]
    if cache_seq:
        for s in range(x.shape[0] // cache_seq):
            rows = slice(s * cache_seq, (s + 1) * cache_seq)
            kc_ref[s] = k[rows, :].T
            vc_ref[s] = v[rows, :].T
    if rope:
        k = _rope(k, cos_ref[...], sin_ref[...])
    kd_ref[...] = _dup_heads(k).astype(BF16)
    vd_ref[...] = _dup_heads(v).astype(BF16)

    gc = 512
    for j in range(2 * D_MODEL // gc):
        zg = _dot(hb, w_ref[:, _G0 + j * gc:_G0 + (j + 1) * gc])
        gate_ref[:, j * gc:(j + 1) * gc] = _sigmoid(zg).astype(BF16)


def _proj(x, mod3, g1, w_in_b, gq2, gk2, rope_tabs, mod_row_of_tile, cache_seq, cast_weights):
    t = x.shape[0]
    tm = TOKEN_TILE
    steps = t // tm
    rope = rope_tabs is not None
    row = lambda i: (i, 0)
    in_specs = [pl.BlockSpec((tm, D_MODEL), row),
                pl.BlockSpec((1, 6, D_MODEL), lambda i: (mod_row_of_tile(i), 0, 0)),
                _resident((1, D_MODEL)),
                _resident((D_MODEL, IN_WIDTH)),
                _resident((1, LANES)),
                _resident((1, LANES))]
    args = [x, mod3, g1, w_in_b, gq2, gk2]
    if rope:
        tiles_per_seq = rope_tabs[0].shape[0] // tm
        tab = pl.BlockSpec((tm, LANES), lambda i: (i % tiles_per_seq, 0))
        in_specs += [tab, tab]
        args += list(rope_tabs)
    slabs = [pl.BlockSpec((w.shape[0] // steps, w.shape[1]), row) for w in cast_weights]
    in_specs += slabs
    args += list(cast_weights)
    out_specs = [pl.BlockSpec((tm, F_WIDTH), row),
                 pl.BlockSpec((tm, ATTN_WIDTH), row),
                 pl.BlockSpec((tm, 2 * KV_WIDTH), row),
                 pl.BlockSpec((tm, 2 * KV_WIDTH), row),
                 pl.BlockSpec((tm, 2 * D_MODEL), row)]
    out_shape = [pltpu.HBM((t, F_WIDTH), BF16),
                 pltpu.HBM((t, ATTN_WIDTH), BF16),
                 pltpu.HBM((t, 2 * KV_WIDTH), BF16),
                 pltpu.HBM((t, 2 * KV_WIDTH), BF16),
                 pltpu.HBM((t, 2 * D_MODEL), BF16)]
    if cache_seq:
        seqs = tm // cache_seq
        out_specs += [pl.BlockSpec((seqs, KV_WIDTH, cache_seq), lambda i: (i, 0, 0))] * 2
        out_shape += [pltpu.HBM((t // cache_seq, KV_WIDTH, cache_seq), F32)] * 2
    out_specs += slabs
    out_shape += [pltpu.HBM(w.shape, BF16) for w in cast_weights]
    return pl.pallas_call(
        functools.partial(_proj_kernel, rope=rope, cache_seq=cache_seq, n_cast=len(cast_weights)),
        grid=(steps,),
        in_specs=in_specs,
        out_specs=out_specs,
        out_shape=out_shape,
        compiler_params=pltpu.CompilerParams(
            dimension_semantics=("arbitrary",), vmem_limit_bytes=_vmem_limit(48 * 2 ** 20)),
        name="proj_lat" if rope else "proj_ctx",
    )(*[_hbm(a) for a in args])


def _dft_tables(n):
    def cs(m):
        idx = np.arange(m, dtype=np.int64)
        ang = 2.0 * np.pi * ((idx[:, None] * idx[None, :]) % m).astype(np.float64) / m
        return np.cos(ang) / np.sqrt(m), np.sin(ang) / np.sqrt(m)
    cd, sd = cs(F_GROUP_DIM)
    cn, sn = cs(n)
    csd = np.concatenate([cd, sd], axis=1).astype(np.float32)
    csn = np.concatenate([cn, -sn], axis=1).astype(np.float32)
    return jnp.asarray(csd).astype(BF16), jnp.asarray(csn).astype(BF16)


def _fourier_kernel(zf_ref, csd_ref, csn_ref, o_ref, ab_ref, *, seqs, n):
    for s in range(seqs):
        for g in range(F_GROUPS):
            cols = slice(g * F_GROUP_DIM, (g + 1) * F_GROUP_DIM)
            ab = _dot(zf_ref[s, :, cols], csd_ref[...])
            ab_ref[0:n, cols] = ab[:, 0:F_GROUP_DIM].astype(BF16)
            ab_ref[n:2 * n, cols] = ab[:, F_GROUP_DIM:].astype(BF16)
        o_ref[s] = _dot(csn_ref[...], ab_ref[...]).astype(BF16)


def _fourier(zf3, seqs):
    b, n, _ = zf3.shape
    csd, csn = _dft_tables(n)
    blk = pl.BlockSpec((seqs, n, F_WIDTH), lambda i: (i, 0, 0))
    return pl.pallas_call(
        functools.partial(_fourier_kernel, seqs=seqs, n=n),
        grid=(b // seqs,),
        in_specs=[blk, _resident(csd.shape), _resident(csn.shape)],
        out_specs=blk,
        out_shape=pltpu.HBM(zf3.shape, BF16),
        scratch_shapes=[pltpu.VMEM((2 * n, F_WIDTH), BF16)],
        compiler_params=pltpu.CompilerParams(
            dimension_semantics=("arbitrary",), vmem_limit_bytes=_vmem_limit(32 * 2 ** 20)),
        name=f"fourier_{n}",
    )(_hbm(zf3), csd, csn)


def _split_lo_hi(xd):
    lo = lax.broadcasted_iota(jnp.int32, xd.shape, 1) < HEAD_DIM
    zero = jnp.zeros_like(xd)
    return jnp.where(lo, xd, zero), jnp.where(lo, zero, xd)


def _group_attention(q_stack, sources, sinks, band, s_ref, p_ref, inv_ref, rows_per_pair, chunk):
    rows = q_stack.shape[0]
    bounds = [0]
    for k_lo, k_hi, _, _ in sources:
        c0, c1 = bounds[-1], bounds[-1] + k_lo.shape[0]
        s_ref[0, :, c0:c1] = _dot_nt(q_stack, k_lo)
        s_ref[1, :, c0:c1] = _dot_nt(q_stack, k_hi)
        bounds.append(c1)

    for half in range(2):
        for r0 in range(0, rows, chunk):
            rs = slice(r0, r0 + chunk)
            sink = sinks[2 * (r0 // rows_per_pair) + half]
            parts = [s_ref[half, rs, bounds[i]:bounds[i + 1]] for i in range(len(sources))]
            if band is not None:
                parts[0] = jnp.where(band(r0 % rows_per_pair, chunk), parts[0], NEG_INF)
            m = sink
            for s in parts:
                m = jnp.maximum(m, jnp.max(s, axis=-1, keepdims=True))
            l = jnp.exp(sink - m)
            for i, s in enumerate(parts):
                p = jnp.exp(s - m)
                l = l + jnp.sum(p, axis=-1, keepdims=True)
                p_ref[half, rs, bounds[i]:bounds[i + 1]] = p.astype(BF16)
            inv_ref[half, rs, :] = jnp.broadcast_to(1.0 / l, (chunk, LANES))

    o_e = o_o = None
    for i, (_, _, v_lo, v_hi) in enumerate(sources):
        pe = _dot(p_ref[0, :, bounds[i]:bounds[i + 1]], v_lo)
        po = _dot(p_ref[1, :, bounds[i]:bounds[i + 1]], v_hi)
        o_e = pe if o_e is None else o_e + pe
        o_o = po if o_o is None else o_o + po
    lo = lax.broadcasted_iota(jnp.int32, (rows, LANES), 1) < HEAD_DIM
    return (o_e + o_o) * jnp.where(lo, inv_ref[0], inv_ref[1])


def _attn_ctx_kernel(sink_ref, q_ref, kd_ref, vd_ref, o_ref, s_ref, p_ref, inv_ref, *, seqs):
    n = q_ref.shape[1]
    unit = 0
    for s in range(seqs):
        for g in range(N_KV_HEADS):
            cols = slice(g * LANES, (g + 1) * LANES)
            k_lo, k_hi = _split_lo_hi(kd_ref[s, :, cols])
            v_lo, v_hi = _split_lo_hi(vd_ref[s, :, cols])
            pa, pb = 2 * g, 2 * g + 1
            qa, qb = slice(pa * LANES, (pa + 1) * LANES), slice(pb * LANES, (pb + 1) * LANES)
            q_stack = jnp.concatenate([q_ref[s, :, qa], q_ref[s, :, qb]], axis=0)
            sinks = [sink_ref[GROUP * g + h] for h in range(GROUP)]
            buf = unit % 2
            o = _group_attention(q_stack, [(k_lo, k_hi, v_lo, v_hi)], sinks, None,
                                 s_ref.at[buf], p_ref.at[buf], inv_ref.at[buf], n, CTX_SOFTMAX_ROWS)
            o_ref[s, :, qa] = o[0:n].astype(BF16)
            o_ref[s, :, qb] = o[n:2 * n].astype(BF16)
            unit += 1


def _attn_ctx(sinks, q3, kd3, vd3):
    b, n, _ = q3.shape
    seqs = CTX_SEQS_PER_STEP
    blk = lambda w: pl.BlockSpec((seqs, n, w), lambda i: (i, 0, 0))
    return pl.pallas_call(
        functools.partial(_attn_ctx_kernel, seqs=seqs),
        grid=(b // seqs,),
        in_specs=[pl.BlockSpec(memory_space=pltpu.SMEM),
                  blk(ATTN_WIDTH), blk(2 * KV_WIDTH), blk(2 * KV_WIDTH)],
        out_specs=blk(ATTN_WIDTH),
        out_shape=pltpu.HBM(q3.shape, BF16),
        scratch_shapes=[pltpu.VMEM((2, 2, 2 * n, n), F32),
                        pltpu.VMEM((2, 2, 2 * n, n), BF16),
                        pltpu.VMEM((2, 2, 2 * n, LANES), F32)],
        compiler_params=pltpu.CompilerParams(
            dimension_semantics=("arbitrary",), vmem_limit_bytes=_vmem_limit(32 * 2 ** 20)),
        name="attn_ctx",
    )(sinks, _hbm(q3), _hbm(kd3), _hbm(vd3))


def _attn_lat_kernel(sink_ref, q_ref, kd_ref, vd_ref, ck_ref, cv_ref, o_ref,
                     s_ref, p_ref, inv_ref, dist_ref, *, n, tq):
    t = pl.program_id(1)
    qb = WINDOW
    span = qb + 2 * WINDOW

    ck = _dup_heads(ck_ref[0]).astype(BF16)
    cv = _dup_heads(cv_ref[0]).astype(BF16)
    unit = 0
    for rr in range(tq // qb):
        q0 = t * tq + rr * qb
        start = pl.multiple_of(jnp.clip(q0 - WINDOW, 0, n - span), WINDOW)
        dist_ref[rr] = jnp.abs((q0 - start) + lax.broadcasted_iota(jnp.int32, (qb, span), 0)
                               - lax.broadcasted_iota(jnp.int32, (qb, span), 1))
        band = lambda r0, rows, rr=rr: dist_ref[rr, r0:r0 + rows, :] <= WINDOW
        rows = slice(rr * qb, (rr + 1) * qb)
        for g in range(N_KV_HEADS):
            cols = slice(g * LANES, (g + 1) * LANES)
            k_lo, k_hi = _split_lo_hi(kd_ref[0, pl.ds(start, span), cols])
            v_lo, v_hi = _split_lo_hi(vd_ref[0, pl.ds(start, span), cols])
            ck_lo, ck_hi = _split_lo_hi(ck[:, cols])
            cv_lo, cv_hi = _split_lo_hi(cv[:, cols])
            pa, pb = 2 * g, 2 * g + 1
            qa, qbc = slice(pa * LANES, (pa + 1) * LANES), slice(pb * LANES, (pb + 1) * LANES)
            q_stack = jnp.concatenate([q_ref[0, rows, qa], q_ref[0, rows, qbc]], axis=0)
            sinks = [sink_ref[GROUP * g + h] for h in range(GROUP)]
            buf = unit % 2
            o = _group_attention(q_stack, [(k_lo, k_hi, v_lo, v_hi), (ck_lo, ck_hi, cv_lo, cv_hi)],
                                 sinks, band, s_ref.at[buf], p_ref.at[buf], inv_ref.at[buf],
                                 qb, LAT_SOFTMAX_ROWS)
            o_ref[0, rows, qa] = o[0:qb].astype(BF16)
            o_ref[0, rows, qbc] = o[qb:2 * qb].astype(BF16)
            unit += 1


def _attn_lat(sinks, q3, kd3, vd3, ck3, cv3):
    b, n, _ = q3.shape
    past = ck3.shape[1]
    tq = Q_TILE
    keys = 3 * WINDOW + past
    qblk = pl.BlockSpec((1, tq, ATTN_WIDTH), lambda i, t: (i, t, 0))
    seq = lambda rows, w: pl.BlockSpec((1, rows, w), lambda i, t: (i, 0, 0))
    return pl.pallas_call(
        functools.partial(_attn_lat_kernel, n=n, tq=tq),
        grid=(b, n // tq),
        in_specs=[pl.BlockSpec(memory_space=pltpu.SMEM), qblk,
                  seq(n, 2 * KV_WIDTH), seq(n, 2 * KV_WIDTH),
                  seq(past, KV_WIDTH), seq(past, KV_WIDTH)],
        out_specs=qblk,
        out_shape=pltpu.HBM(q3.shape, BF16),
        scratch_shapes=[pltpu.VMEM((2, 2, 2 * WINDOW, keys), F32),
                        pltpu.VMEM((2, 2, 2 * WINDOW, keys), BF16),
                        pltpu.VMEM((2, 2, 2 * WINDOW, LANES), F32),
                        pltpu.VMEM((tq // WINDOW, WINDOW, 3 * WINDOW), jnp.int32)],
        compiler_params=pltpu.CompilerParams(
            dimension_semantics=("arbitrary", "arbitrary"),
            vmem_limit_bytes=_vmem_limit(32 * 2 ** 20)),
        name="attn_lat",
    )(sinks, _hbm(q3), _hbm(kd3), _hbm(vd3), _hbm(ck3), _hbm(cv3))


def _tail_kernel(x_ref, mod_ref, mix_ref, att_ref, gate_ref, g2_ref,
                 wf_ref, wao_ref, wout_ref, wup_ref, wdn_ref, o_ref):
    mod = mod_ref[0]
    gt1, sh2, sc2, gt2 = mod[2:3], mod[3:4], mod[4:5], mod[5:6]

    yf = _dot(mix_ref[...], wf_ref[...])
    ya = _dot(att_ref[...], wao_ref[...])
    merged = (gate_ref[:, 0:D_MODEL].astype(F32) * yf
              + gate_ref[:, D_MODEL:2 * D_MODEL].astype(F32) * ya)
    x1 = x_ref[...] + gt1 * _dot(merged.astype(BF16), wout_ref[...])

    h = x1 * lax.rsqrt(jnp.mean(x1 * x1, axis=-1, keepdims=True) + EPS) * g2_ref[...]
    hb = (h * (1.0 + sc2) + sh2).astype(BF16)
    d_ff = wdn_ref.shape[0]
    acc = None
    for c in range(d_ff // FF_CHUNK):
        a = _dot(hb, wup_ref[:, c * FF_CHUNK:(c + 1) * FF_CHUNK])
        u = _dot(hb, wup_ref[:, d_ff + c * FF_CHUNK:d_ff + (c + 1) * FF_CHUNK])
        act = (a * _sigmoid(a) * u).astype(BF16)
        d = _dot(act, wdn_ref[c * FF_CHUNK:(c + 1) * FF_CHUNK, :])
        acc = d if acc is None else acc + d
    o_ref[...] = x1 + gt2 * acc


def _tail(x, mod3, mix, att, gate, g2, wf, wao, wout, wup, wdn, mod_row_of_tile):
    t = x.shape[0]
    tm = TOKEN_TILE
    row = lambda i: (i, 0)
    return pl.pallas_call(
        _tail_kernel,
        grid=(t // tm,),
        in_specs=[pl.BlockSpec((tm, D_MODEL), row),
                  pl.BlockSpec((1, 6, D_MODEL), lambda i: (mod_row_of_tile(i), 0, 0)),
                  pl.BlockSpec((tm, F_WIDTH), row),
                  pl.BlockSpec((tm, ATTN_WIDTH), row),
                  pl.BlockSpec((tm, 2 * D_MODEL), row),
                  _resident((1, D_MODEL)),
                  _resident(wf.shape), _resident(wao.shape), _resident(wout.shape),
                  _resident(wup.shape), _resident(wdn.shape)],
        out_specs=pl.BlockSpec((tm, D_MODEL), row),
        out_shape=pltpu.HBM((t, D_MODEL), F32),
        compiler_params=pltpu.CompilerParams(
            dimension_semantics=("arbitrary",), vmem_limit_bytes=_vmem_limit(56 * 2 ** 20)),
        name="tail",
    )(*[_hbm(a) for a in (x, mod3, mix, att, gate, g2, wf, wao, wout, wup, wdn)])


def _rope_tables(n):
    rows = n // GRID_W
    row = jnp.repeat(jnp.arange(rows, dtype=F32), GRID_W)
    col = jnp.tile(jnp.arange(GRID_W, dtype=F32), rows)
    axis_dim = HEAD_DIM // 2
    inv_freq = ROPE_THETA ** (-jnp.arange(0, axis_dim, 2, dtype=F32) / axis_dim)

    def axis_tabs(pos):
        ang = pos[:, None] * inv_freq[None, :]
        cos, sin = jnp.cos(ang), jnp.sin(ang)
        return jnp.concatenate([cos, cos], axis=-1), jnp.concatenate([-sin, sin], axis=-1)

    cr, sr = axis_tabs(row)
    cc, sc = axis_tabs(col)
    cos = jnp.concatenate([cr, cc], axis=-1)
    sin = jnp.concatenate([sr, sc], axis=-1)
    return jnp.tile(cos, (1, LANES // HEAD_DIM)), jnp.tile(sin, (1, LANES // HEAD_DIM))


def _layer(xp, xs, ck, cv, c, c_ctx, w_ada, b_ada, g_norm1, g_norm2, w_in, g_q, g_k, sink,
           w_f, w_ao, w_out, w_up, w_down):
    bp, sp, _ = xp.shape
    bs, ss, _ = xs.shape
    past = ck.shape[1]

    cvec = jnp.concatenate(
        [c, c_ctx[None, :], jnp.zeros((MOD_ROWS - bs - 1, D_MODEL), F32)], axis=0)
    mod3 = _ada(cvec, w_ada, b_ada[None, :]).reshape(MOD_ROWS, 6, D_MODEL)

    w_in_b = w_in.astype(BF16)
    g1, g2 = g_norm1[None, :], g_norm2[None, :]
    gq2 = jnp.tile(g_q, LANES // HEAD_DIM)[None, :]
    gk2 = jnp.tile(g_k, LANES // HEAD_DIM)[None, :]

    ctx_row = lambda i: CTX_MOD_ROW
    lat_row = lambda i: i // (ss // TOKEN_TILE)

    xp2 = xp.reshape(bp * sp, D_MODEL)
    zf, q, kd, vd, gate, k_new, v_new, wf, wao, wout, wup, wdn = _proj(
        xp2, mod3, g1, w_in_b, gq2, gk2, None, ctx_row, sp, [w_f, w_ao, w_out, w_up, w_down])
    mix = _fourier(zf.reshape(bp, sp, F_WIDTH), CTX_SEQS_PER_STEP)
    att = _attn_ctx(sink, q.reshape(bp, sp, ATTN_WIDTH), kd.reshape(bp, sp, 2 * KV_WIDTH),
                    vd.reshape(bp, sp, 2 * KV_WIDTH))
    yp = _tail(xp2, mod3, mix.reshape(bp * sp, F_WIDTH), att.reshape(bp * sp, ATTN_WIDTH), gate,
               g2, wf, wao, wout, wup, wdn, ctx_row)

    xs2 = xs.reshape(bs * ss, D_MODEL)
    zf, q, kd, vd, gate = _proj(xs2, mod3, g1, w_in_b, gq2, gk2, _rope_tables(ss), lat_row, 0, [])
    mix = _fourier(zf.reshape(bs, ss, F_WIDTH), 1)
    att = _attn_lat(sink, q.reshape(bs, ss, ATTN_WIDTH), kd.reshape(bs, ss, 2 * KV_WIDTH),
                    vd.reshape(bs, ss, 2 * KV_WIDTH), ck.reshape(bs, past, KV_WIDTH),
                    cv.reshape(bs, past, KV_WIDTH))
    ys = _tail(xs2, mod3, mix.reshape(bs * ss, F_WIDTH), att.reshape(bs * ss, ATTN_WIDTH), gate,
               g2, wf, wao, wout, wup, wdn, lat_row)

    def cache_layout(t):
        return t.reshape(bp, N_KV_HEADS, HEAD_DIM, sp).transpose(0, 3, 1, 2)

    return yp.reshape(xp.shape), ys.reshape(xs.shape), cache_layout(k_new), cache_layout(v_new)


def kernel(x_prompt, x_sample, cache_k, cache_v, c, c_ctx, w_ada, b_ada, g_norm1, g_norm2,
           w_in, g_q, g_k, sinks, w_f, w_ao, w_out, w_up, w_down):
    depth = w_in.shape[0]
    xp, xs = x_prompt, x_sample
    new_k, new_v = [], []
    for l in range(depth):
        xp, xs, k_ctx, v_ctx = _layer(
            xp, xs, cache_k[:, l], cache_v[:, l], c, c_ctx, w_ada[l], b_ada[l], g_norm1[l],
            g_norm2[l], w_in[l], g_q[l], g_k[l], sinks[l], w_f[l], w_ao[l], w_out[l], w_up[l],
            w_down[l])
        new_k.append(k_ctx)
        new_v.append(v_ctx)
    return (xp, xs, jnp.stack(new_k, axis=1), jnp.stack(new_v, axis=1))
```

```python
import functools

import numpy as np
import jax
import jax.numpy as jnp
from jax import lax
from jax.experimental import pallas as pl
from jax.experimental.pallas import tpu as pltpu

D_MODEL = 1024
GRID_W = 64
N_HEADS = 8
N_KV_HEADS = 2
GROUP = N_HEADS // N_KV_HEADS
HEAD_DIM = 64
WINDOW = 128
F_GROUPS = 4
F_GROUP_DIM = 128
F_WIDTH = F_GROUPS * F_GROUP_DIM
ATTN_WIDTH = N_HEADS * HEAD_DIM
KV_WIDTH = N_KV_HEADS * HEAD_DIM
ROPE_THETA = 10000.0
EPS = 1e-6
NEG_INF = -1e30
LOG2_E = 1.4426950408889634

LANES = 128
MXU_DIM = 256
VMEM_BYTES_V7X = 64 * 1024 * 1024

_Q0 = F_WIDTH
_K0 = _Q0 + ATTN_WIDTH
_V0 = _K0 + KV_WIDTH
_G0 = _V0 + KV_WIDTH
IN_WIDTH = _G0 + 2 * D_MODEL

MOD_ROWS = 16
CTX_MOD_ROW = 8

TOKEN_TILE = 512
FF_CHUNK = MXU_DIM
Q_TILE = 256
CTX_SEQS_PER_STEP = 4
CTX_SOFTMAX_ROWS = 64
LAT_SOFTMAX_ROWS = 32

F32 = jnp.float32
BF16 = jnp.bfloat16


def _vmem_limit(nbytes):
    return int(min(nbytes, VMEM_BYTES_V7X - 4 * 1024 * 1024))


def _dot(a, b):
    return jnp.dot(a, b, preferred_element_type=F32)


def _dot_nt(a, b):
    return lax.dot_general(a, b, (((1,), (1,)), ((), ())), preferred_element_type=F32)


def _sigmoid(x):
    return 1.0 / (1.0 + jnp.exp(-x))


def _resident(shape):
    zeros = (0,) * len(shape)
    return pl.BlockSpec(shape, lambda *_: zeros, pipeline_mode=pl.Buffered(1))


def _hbm(x):
    return pltpu.with_memory_space_constraint(x, pltpu.HBM)


def _ada_kernel(c_ref, w_ref, b_ref, o_ref):
    c = c_ref[...]
    s = c * _sigmoid(c)
    o_ref[...] = _dot(s.astype(BF16), w_ref[...].astype(BF16)) + b_ref[...]


def _ada(cvec, w_ada, b_ada):
    n = w_ada.shape[1]
    bn = D_MODEL
    return pl.pallas_call(
        _ada_kernel,
        grid=(n // bn,),
        in_specs=[pl.BlockSpec((MOD_ROWS, D_MODEL), lambda j: (0, 0)),
                  pl.BlockSpec((D_MODEL, bn), lambda j: (0, j)),
                  pl.BlockSpec((1, bn), lambda j: (0, j))],
        out_specs=pl.BlockSpec((MOD_ROWS, bn), lambda j: (0, j)),
        out_shape=jax.ShapeDtypeStruct((MOD_ROWS, n), F32),
        name="ada",
    )(cvec, w_ada, b_ada)


def _head_norm(z, g):
    lo = lax.broadcasted_iota(jnp.int32, z.shape, 1) < HEAD_DIM
    s = z * z
    s_lo = jnp.sum(jnp.where(lo, s, 0.0), axis=-1, keepdims=True)
    s_hi = jnp.sum(jnp.where(lo, 0.0, s), axis=-1, keepdims=True)
    ms = jnp.where(lo, s_lo, s_hi) * (1.0 / HEAD_DIM)
    return z * lax.rsqrt(ms + EPS) * g


def _rope(y, cos, sin):
    lane = lax.broadcasted_iota(jnp.int32, y.shape, 1)
    first = (lane % 32) < 16
    partner = jnp.where(first, pltpu.roll(y, LANES - 16, 1), pltpu.roll(y, 16, 1))
    return y * cos + partner * sin


def _dup_heads(y):
    lo = lax.broadcasted_iota(jnp.int32, y.shape, 1) < HEAD_DIM
    sw = pltpu.roll(y, HEAD_DIM, 1)
    return jnp.concatenate([jnp.where(lo, y, sw), jnp.where(lo, sw, y)], axis=-1)


def _proj_kernel(*refs, rope, cache_seq, n_cast):
    x_ref, mod_ref, g1_ref, w_ref, gq_ref, gk_ref = refs[:6]
    refs = refs[6:]
    if rope:
        cos_ref, sin_ref = refs[:2]
        refs = refs[2:]
    cast_in, refs = refs[:n_cast], refs[n_cast:]
    zf_ref, q_ref, kd_ref, vd_ref, gate_ref = refs[:5]
    refs = refs[5:]
    if cache_seq:
        kc_ref, vc_ref = refs[:2]
        refs = refs[2:]
    for src_ref, dst_ref in zip(cast_in, refs):
        dst_ref[...] = src_ref[...].astype(BF16)

    x = x_ref[...]
    mod = mod_ref[0]
    sh1, sc1 = mod[0:1], mod[1:2]
    h = x * lax.rsqrt(jnp.mean(x * x, axis=-1, keepdims=True) + EPS) * g1_ref[...]
    hb = (h * (1.0 + sc1) + sh1).astype(BF16)

    zf_ref[...] = _dot(hb, w_ref[:, 0:_Q0]).astype(BF16)

    zq = _dot(hb, w_ref[:, _Q0:_K0])
    for j in range(ATTN_WIDTH // LANES):
        y = _head_norm(zq[:, j * LANES:(j + 1) * LANES], gq_ref[...])
        if rope:
            y = _rope(y, cos_ref[...], sin_ref[...])
        q_ref[:, j * LANES:(j + 1) * LANES] = (y * (HEAD_DIM ** -0.5 * LOG2_E)).astype(BF16)

    zkv = _dot(hb, w_ref[:, _K0:_G0])
    k = _head_norm(zkv[:, 0:KV_WIDTH], gk_ref[...])
    v = zkv[:, KV_WIDTH:2 * KV_WIDTH]
    if cache_seq:
        for s in range(x.shape[0] // cache_seq):
            rows = slice(s * cache_seq, (s + 1) * cache_seq)
            kc_ref[s] = k[rows, :].T
            vc_ref[s] = v[rows, :].T
    if rope:
        k = _rope(k, cos_ref[...], sin_ref[...])
    kd_ref[...] = _dup_heads(k).astype(BF16)
    vd_ref[...] = _dup_heads(v).astype(BF16)

    gc = 512
    for j in range(2 * D_MODEL // gc):
        zg = _dot(hb, w_ref[:, _G0 + j * gc:_G0 + (j + 1) * gc])
        gate_ref[:, j * gc:(j + 1) * gc] = _sigmoid(zg).astype(BF16)


def _proj(x, mod3, g1, w_in_b, gq2, gk2, rope_tabs, mod_row_of_tile, cache_seq, cast_weights):
    t = x.shape[0]
    tm = TOKEN_TILE
    steps = t // tm
    rope = rope_tabs is not None
    row = lambda i: (i, 0)
    in_specs = [pl.BlockSpec((tm, D_MODEL), row),
                pl.BlockSpec((1, 6, D_MODEL), lambda i: (mod_row_of_tile(i), 0, 0)),
                _resident((1, D_MODEL)),
                _resident((D_MODEL, IN_WIDTH)),
                _resident((1, LANES)),
                _resident((1, LANES))]
    args = [x, mod3, g1, w_in_b, gq2, gk2]
    if rope:
        tiles_per_seq = rope_tabs[0].shape[0] // tm
        tab = pl.BlockSpec((tm, LANES), lambda i: (i % tiles_per_seq, 0))
        in_specs += [tab, tab]
        args += list(rope_tabs)
    slabs = [pl.BlockSpec((w.shape[0] // steps, w.shape[1]), row) for w in cast_weights]
    in_specs += slabs
    args += list(cast_weights)
    out_specs = [pl.BlockSpec((tm, F_WIDTH), row),
                 pl.BlockSpec((tm, ATTN_WIDTH), row),
                 pl.BlockSpec((tm, 2 * KV_WIDTH), row),
                 pl.BlockSpec((tm, 2 * KV_WIDTH), row),
                 pl.BlockSpec((tm, 2 * D_MODEL), row)]
    out_shape = [pltpu.HBM((t, F_WIDTH), BF16),
                 pltpu.HBM((t, ATTN_WIDTH), BF16),
                 pltpu.HBM((t, 2 * KV_WIDTH), BF16),
                 pltpu.HBM((t, 2 * KV_WIDTH), BF16),
                 pltpu.HBM((t, 2 * D_MODEL), BF16)]
    if cache_seq:
        seqs = tm // cache_seq
        out_specs += [pl.BlockSpec((seqs, KV_WIDTH, cache_seq), lambda i: (i, 0, 0))] * 2
        out_shape += [pltpu.HBM((t // cache_seq, KV_WIDTH, cache_seq), F32)] * 2
    out_specs += slabs
    out_shape += [pltpu.HBM(w.shape, BF16) for w in cast_weights]
    return pl.pallas_call(
        functools.partial(_proj_kernel, rope=rope, cache_seq=cache_seq, n_cast=len(cast_weights)),
        grid=(steps,),
        in_specs=in_specs,
        out_specs=out_specs,
        out_shape=out_shape,
        compiler_params=pltpu.CompilerParams(
            dimension_semantics=("arbitrary",), vmem_limit_bytes=_vmem_limit(48 * 2 ** 20)),
        name="proj_lat" if rope else "proj_ctx",
    )(*[_hbm(a) for a in args])


def _dft_tables(n):
    def cs(m):
        idx = np.arange(m, dtype=np.int64)
        ang = 2.0 * np.pi * ((idx[:, None] * idx[None, :]) % m).astype(np.float64) / m
        return np.cos(ang) / np.sqrt(m), np.sin(ang) / np.sqrt(m)
    cd, sd = cs(F_GROUP_DIM)
    cn, sn = cs(n)
    csd = np.concatenate([cd, sd], axis=1).astype(np.float32)
    csn = np.concatenate([cn, -sn], axis=1).astype(np.float32)
    return jnp.asarray(csd).astype(BF16), jnp.asarray(csn).astype(BF16)


def _fourier_kernel(zf_ref, csd_ref, csn_ref, o_ref, ab_ref, *, seqs, n):
    for s in range(seqs):
        for g in range(F_GROUPS):
            cols = slice(g * F_GROUP_DIM, (g + 1) * F_GROUP_DIM)
            ab = _dot(zf_ref[s, :, cols], csd_ref[...])
            ab_ref[0:n, cols] = ab[:, 0:F_GROUP_DIM].astype(BF16)
            ab_ref[n:2 * n, cols] = ab[:, F_GROUP_DIM:].astype(BF16)
        o_ref[s] = _dot(csn_ref[...], ab_ref[...]).astype(BF16)


def _fourier(zf3, seqs):
    b, n, _ = zf3.shape
    csd, csn = _dft_tables(n)
    blk = pl.BlockSpec((seqs, n, F_WIDTH), lambda i: (i, 0, 0))
    return pl.pallas_call(
        functools.partial(_fourier_kernel, seqs=seqs, n=n),
        grid=(b // seqs,),
        in_specs=[blk, _resident(csd.shape), _resident(csn.shape)],
        out_specs=blk,
        out_shape=pltpu.HBM(zf3.shape, BF16),
        scratch_shapes=[pltpu.VMEM((2 * n, F_WIDTH), BF16)],
        compiler_params=pltpu.CompilerParams(
            dimension_semantics=("arbitrary",), vmem_limit_bytes=_vmem_limit(32 * 2 ** 20)),
        name=f"fourier_{n}",
    )(_hbm(zf3), csd, csn)


def _split_lo_hi(xd, fill):
    lo = lax.broadcasted_iota(jnp.int32, xd.shape, 1) < HEAD_DIM
    other = jnp.full_like(xd, fill)
    return jnp.where(lo, xd, other), jnp.where(lo, other, xd)


def _group_scores(q_stack, keys, s_ref):
    c0 = 0
    for kd in keys:
        k_lo, k_hi = _split_lo_hi(kd, 0.0)
        c1 = c0 + kd.shape[0]
        s_ref[0, :, c0:c1] = _dot_nt(q_stack, k_lo)
        s_ref[1, :, c0:c1] = _dot_nt(q_stack, k_hi)
        c0 = c1


def _group_softmax_pv(values, sinks, band, s_ref, p_ref, sk_ref, rows_per_pair, chunk):
    rows = s_ref.shape[1]
    bounds = [0]
    for vd in values:
        bounds.append(bounds[-1] + vd.shape[0])
    sources = values

    for half in range(2):
        for r0 in range(0, rows, chunk):
            rs = slice(r0, r0 + chunk)
            sink = sinks[2 * (r0 // rows_per_pair) + half]
            parts = [s_ref[half, rs, bounds[i]:bounds[i + 1]] for i in range(len(sources))]
            if band is not None:
                parts[0] = jnp.where(band(r0 % rows_per_pair, chunk), parts[0], NEG_INF)
            m = sink
            for s in parts:
                m = jnp.maximum(m, jnp.max(s, axis=-1, keepdims=True))
            for i, s in enumerate(parts):
                p_ref[half, rs, bounds[i]:bounds[i + 1]] = jnp.exp2((s - m).astype(BF16))
            sk_ref[half, rs, :] = jnp.broadcast_to(jnp.exp2(sink - m), (chunk, LANES))

    o_e = o_o = None
    for i, vd in enumerate(values):
        v_lo, v_hi = _split_lo_hi(vd, 1.0)
        pe = _dot(p_ref[0, :, bounds[i]:bounds[i + 1]], v_lo)
        po = _dot(p_ref[1, :, bounds[i]:bounds[i + 1]], v_hi)
        o_e = pe if o_e is None else o_e + pe
        o_o = po if o_o is None else o_o + po
    lo = lax.broadcasted_iota(jnp.int32, (rows, LANES), 1) < HEAD_DIM
    num = jnp.where(lo, o_e, o_o)
    den = pltpu.roll(jnp.where(lo, o_o, o_e), HEAD_DIM, 1) + jnp.where(lo, sk_ref[0], sk_ref[1])
    return num / den


def _run_units(units):
    units[0][0]()
    for k, (_, finish) in enumerate(units):
        if k + 1 < len(units):
            units[k + 1][0]()
        finish()


def _pair_cols(g):
    pa, pb = 2 * g, 2 * g + 1
    return slice(pa * LANES, (pa + 1) * LANES), slice(pb * LANES, (pb + 1) * LANES)


def _attn_ctx_kernel(sink_ref, q_ref, kd_ref, vd_ref, o_ref, s_ref, p_ref, sk_ref, *, seqs):
    n = q_ref.shape[1]
    units = []
    for s in range(seqs):
        for g in range(N_KV_HEADS):
            buf = len(units) % 2
            cols = slice(g * LANES, (g + 1) * LANES)
            qa, qb = _pair_cols(g)

            def scores(s=s, cols=cols, qa=qa, qb=qb, buf=buf):
                q_stack = jnp.concatenate([q_ref[s, :, qa], q_ref[s, :, qb]], axis=0)
                _group_scores(q_stack, [kd_ref[s, :, cols]], s_ref.at[buf])

            def finish(s=s, g=g, cols=cols, qa=qa, qb=qb, buf=buf):
                sinks = [sink_ref[GROUP * g + h] * LOG2_E for h in range(GROUP)]
                o = _group_softmax_pv([vd_ref[s, :, cols]], sinks, None, s_ref.at[buf],
                                      p_ref.at[buf], sk_ref.at[buf], n, CTX_SOFTMAX_ROWS)
                o_ref[s, :, qa] = o[0:n].astype(BF16)
                o_ref[s, :, qb] = o[n:2 * n].astype(BF16)

            units.append((scores, finish))
    _run_units(units)


def _attn_ctx(sinks, q3, kd3, vd3):
    b, n, _ = q3.shape
    seqs = CTX_SEQS_PER_STEP
    blk = lambda w: pl.BlockSpec((seqs, n, w), lambda i: (i, 0, 0))
    return pl.pallas_call(
        functools.partial(_attn_ctx_kernel, seqs=seqs),
        grid=(b // seqs,),
        in_specs=[pl.BlockSpec(memory_space=pltpu.SMEM),
                  blk(ATTN_WIDTH), blk(2 * KV_WIDTH), blk(2 * KV_WIDTH)],
        out_specs=blk(ATTN_WIDTH),
        out_shape=pltpu.HBM(q3.shape, BF16),
        scratch_shapes=[pltpu.VMEM((2, 2, 2 * n, n), F32),
                        pltpu.VMEM((2, 2, 2 * n, n), BF16),
                        pltpu.VMEM((2, 2, 2 * n, LANES), F32)],
        compiler_params=pltpu.CompilerParams(
            dimension_semantics=("arbitrary",), vmem_limit_bytes=_vmem_limit(32 * 2 ** 20)),
        name="attn_ctx",
    )(sinks, _hbm(q3), _hbm(kd3), _hbm(vd3))


def _attn_lat_kernel(sink_ref, q_ref, kd_ref, vd_ref, ck_ref, cv_ref, o_ref,
                     s_ref, p_ref, sk_ref, dist_ref, ckv_ref, *, n, tq):
    t = pl.program_id(1)
    qb = WINDOW
    span = qb + 2 * WINDOW

    ckv_ref[0] = _dup_heads(ck_ref[0]).astype(BF16)
    ckv_ref[1] = _dup_heads(cv_ref[0]).astype(BF16)
    units = []
    for rr in range(tq // qb):
        q0 = t * tq + rr * qb
        start = pl.multiple_of(jnp.clip(q0 - WINDOW, 0, n - span), WINDOW)
        dist_ref[rr] = jnp.abs((q0 - start) + lax.broadcasted_iota(jnp.int32, (qb, span), 0)
                               - lax.broadcasted_iota(jnp.int32, (qb, span), 1))
        band = lambda r0, rows, rr=rr: dist_ref[rr, r0:r0 + rows, :] <= WINDOW
        rows = slice(rr * qb, (rr + 1) * qb)
        for g in range(N_KV_HEADS):
            buf = len(units) % 2
            cols = slice(g * LANES, (g + 1) * LANES)
            qa, qbc = _pair_cols(g)

            def scores(rows=rows, start=start, cols=cols, qa=qa, qbc=qbc, buf=buf):
                q_stack = jnp.concatenate([q_ref[0, rows, qa], q_ref[0, rows, qbc]], axis=0)
                _group_scores(q_stack, [kd_ref[0, pl.ds(start, span), cols], ckv_ref[0, :, cols]],
                              s_ref.at[buf])

            def finish(rows=rows, start=start, g=g, cols=cols, qa=qa, qbc=qbc, buf=buf, band=band):
                sinks = [sink_ref[GROUP * g + h] * LOG2_E for h in range(GROUP)]
                o = _group_softmax_pv([vd_ref[0, pl.ds(start, span), cols], ckv_ref[1, :, cols]],
                                      sinks, band, s_ref.at[buf], p_ref.at[buf], sk_ref.at[buf],
                                      qb, LAT_SOFTMAX_ROWS)
                o_ref[0, rows, qa] = o[0:qb].astype(BF16)
                o_ref[0, rows, qbc] = o[qb:2 * qb].astype(BF16)

            units.append((scores, finish))
    _run_units(units)


def _attn_lat(sinks, q3, kd3, vd3, ck3, cv3):
    b, n, _ = q3.shape
    past = ck3.shape[1]
    tq = Q_TILE
    keys = 3 * WINDOW + past
    qblk = pl.BlockSpec((1, tq, ATTN_WIDTH), lambda i, t: (i, t, 0))
    seq = lambda rows, w: pl.BlockSpec((1, rows, w), lambda i, t: (i, 0, 0))
    return pl.pallas_call(
        functools.partial(_attn_lat_kernel, n=n, tq=tq),
        grid=(b, n // tq),
        in_specs=[pl.BlockSpec(memory_space=pltpu.SMEM), qblk,
                  seq(n, 2 * KV_WIDTH), seq(n, 2 * KV_WIDTH),
                  seq(past, KV_WIDTH), seq(past, KV_WIDTH)],
        out_specs=qblk,
        out_shape=pltpu.HBM(q3.shape, BF16),
        scratch_shapes=[pltpu.VMEM((2, 2, 2 * WINDOW, keys), F32),
                        pltpu.VMEM((2, 2, 2 * WINDOW, keys), BF16),
                        pltpu.VMEM((2, 2, 2 * WINDOW, LANES), F32),
                        pltpu.VMEM((tq // WINDOW, WINDOW, 3 * WINDOW), jnp.int32),
                        pltpu.VMEM((2, past, 2 * KV_WIDTH), BF16)],
        compiler_params=pltpu.CompilerParams(
            dimension_semantics=("arbitrary", "arbitrary"),
            vmem_limit_bytes=_vmem_limit(32 * 2 ** 20)),
        name="attn_lat",
    )(sinks, _hbm(q3), _hbm(kd3), _hbm(vd3), _hbm(ck3), _hbm(cv3))


def _tail_kernel(x_ref, mod_ref, mix_ref, att_ref, gate_ref, g2_ref,
                 wf_ref, wao_ref, wout_ref, wup_ref, wdn_ref, o_ref):
    mod = mod_ref[0]
    gt1, sh2, sc2, gt2 = mod[2:3], mod[3:4], mod[4:5], mod[5:6]

    yf = _dot(mix_ref[...], wf_ref[...])
    ya = _dot(att_ref[...], wao_ref[...])
    merged = (gate_ref[:, 0:D_MODEL].astype(F32) * yf
              + gate_ref[:, D_MODEL:2 * D_MODEL].astype(F32) * ya)
    x1 = x_ref[...] + gt1 * _dot(merged.astype(BF16), wout_ref[...])

    h = x1 * lax.rsqrt(jnp.mean(x1 * x1, axis=-1, keepdims=True) + EPS) * g2_ref[...]
    hb = (h * (1.0 + sc2) + sh2).astype(BF16)
    d_ff = wdn_ref.shape[0]
    acc = None
    for c in range(d_ff // FF_CHUNK):
        a = _dot(hb, wup_ref[:, c * FF_CHUNK:(c + 1) * FF_CHUNK])
        u = _dot(hb, wup_ref[:, d_ff + c * FF_CHUNK:d_ff + (c + 1) * FF_CHUNK])
        act = (a * _sigmoid(a) * u).astype(BF16)
        d = _dot(act, wdn_ref[c * FF_CHUNK:(c + 1) * FF_CHUNK, :])
        acc = d if acc is None else acc + d
    o_ref[...] = x1 + gt2 * acc


def _tail(x, mod3, mix, att, gate, g2, wf, wao, wout, wup, wdn, mod_row_of_tile):
    t = x.shape[0]
    tm = TOKEN_TILE
    row = lambda i: (i, 0)
    return pl.pallas_call(
        _tail_kernel,
        grid=(t // tm,),
        in_specs=[pl.BlockSpec((tm, D_MODEL), row),
                  pl.BlockSpec((1, 6, D_MODEL), lambda i: (mod_row_of_tile(i), 0, 0)),
                  pl.BlockSpec((tm, F_WIDTH), row),
                  pl.BlockSpec((tm, ATTN_WIDTH), row),
                  pl.BlockSpec((tm, 2 * D_MODEL), row),
                  _resident((1, D_MODEL)),
                  _resident(wf.shape), _resident(wao.shape), _resident(wout.shape),
                  _resident(wup.shape), _resident(wdn.shape)],
        out_specs=pl.BlockSpec((tm, D_MODEL), row),
        out_shape=pltpu.HBM((t, D_MODEL), F32),
        compiler_params=pltpu.CompilerParams(
            dimension_semantics=("arbitrary",), vmem_limit_bytes=_vmem_limit(56 * 2 ** 20)),
        name="tail",
    )(*[_hbm(a) for a in (x, mod3, mix, att, gate, g2, wf, wao, wout, wup, wdn)])


def _rope_tables(n):
    rows = n // GRID_W
    row = jnp.repeat(jnp.arange(rows, dtype=F32), GRID_W)
    col = jnp.tile(jnp.arange(GRID_W, dtype=F32), rows)
    axis_dim = HEAD_DIM // 2
    inv_freq = ROPE_THETA ** (-jnp.arange(0, axis_dim, 2, dtype=F32) / axis_dim)

    def axis_tabs(pos):
        ang = pos[:, None] * inv_freq[None, :]
        cos, sin = jnp.cos(ang), jnp.sin(ang)
        return jnp.concatenate([cos, cos], axis=-1), jnp.concatenate([-sin, sin], axis=-1)

    cr, sr = axis_tabs(row)
    cc, sc = axis_tabs(col)
    cos = jnp.concatenate([cr, cc], axis=-1)
    sin = jnp.concatenate([sr, sc], axis=-1)
    return jnp.tile(cos, (1, LANES // HEAD_DIM)), jnp.tile(sin, (1, LANES // HEAD_DIM))


def _layer(xp, xs, ck, cv, c, c_ctx, w_ada, b_ada, g_norm1, g_norm2, w_in, g_q, g_k, sink,
           w_f, w_ao, w_out, w_up, w_down):
    bp, sp, _ = xp.shape
    bs, ss, _ = xs.shape
    past = ck.shape[1]

    cvec = jnp.concatenate(
        [c, c_ctx[None, :], jnp.zeros((MOD_ROWS - bs - 1, D_MODEL), F32)], axis=0)
    mod3 = _ada(cvec, w_ada, b_ada[None, :]).reshape(MOD_ROWS, 6, D_MODEL)

    w_in_b = w_in.astype(BF16)
    g1, g2 = g_norm1[None, :], g_norm2[None, :]
    gq2 = jnp.tile(g_q, LANES // HEAD_DIM)[None, :]
    gk2 = jnp.tile(g_k, LANES // HEAD_DIM)[None, :]

    ctx_row = lambda i: CTX_MOD_ROW
    lat_row = lambda i: i // (ss // TOKEN_TILE)

    xp2 = xp.reshape(bp * sp, D_MODEL)
    zf, q, kd, vd, gate, k_new, v_new, wf, wao, wout, wup, wdn = _proj(
        xp2, mod3, g1, w_in_b, gq2, gk2, None, ctx_row, sp, [w_f, w_ao, w_out, w_up, w_down])
    mix = _fourier(zf.reshape(bp, sp, F_WIDTH), CTX_SEQS_PER_STEP)
    att = _attn_ctx(sink, q.reshape(bp, sp, ATTN_WIDTH), kd.reshape(bp, sp, 2 * KV_WIDTH),
                    vd.reshape(bp, sp, 2 * KV_WIDTH))
    yp = _tail(xp2, mod3, mix.reshape(bp * sp, F_WIDTH), att.reshape(bp * sp, ATTN_WIDTH), gate,
               g2, wf, wao, wout, wup, wdn, ctx_row)

    xs2 = xs.reshape(bs * ss, D_MODEL)
    zf, q, kd, vd, gate = _proj(xs2, mod3, g1, w_in_b, gq2, gk2, _rope_tables(ss), lat_row, 0, [])
    mix = _fourier(zf.reshape(bs, ss, F_WIDTH), 1)
    att = _attn_lat(sink, q.reshape(bs, ss, ATTN_WIDTH), kd.reshape(bs, ss, 2 * KV_WIDTH),
                    vd.reshape(bs, ss, 2 * KV_WIDTH), ck.reshape(bs, past, KV_WIDTH),
                    cv.reshape(bs, past, KV_WIDTH))
    ys = _tail(xs2, mod3, mix.reshape(bs * ss, F_WIDTH), att.reshape(bs * ss, ATTN_WIDTH), gate,
               g2, wf, wao, wout, wup, wdn, lat_row)

    def cache_layout(t):
        return t.reshape(bp, N_KV_HEADS, HEAD_DIM, sp).transpose(0, 3, 1, 2)

    return yp.reshape(xp.shape), ys.reshape(xs.shape), cache_layout(k_new), cache_layout(v_new)


def kernel(x_prompt, x_sample, cache_k, cache_v, c, c_ctx, w_ada, b_ada, g_norm1, g_norm2,
           w_in, g_q, g_k, sinks, w_f, w_ao, w_out, w_up, w_down):
    depth = w_in.shape[0]
    xp, xs = x_prompt, x_sample
    new_k, new_v = [], []
    for l in range(depth):
        xp, xs, k_ctx, v_ctx = _layer(
            xp, xs, cache_k[:, l], cache_v[:, l], c, c_ctx, w_ada[l], b_ada[l], g_norm1[l],
            g_norm2[l], w_in[l], g_q[l], g_k[l], sinks[l], w_f[l], w_ao[l], w_out[l], w_up[l],
            w_down[l])
        new_k.append(k_ctx)
        new_v.append(v_ctx)
    return (xp, xs, jnp.stack(new_k, axis=1), jnp.stack(new_v, axis=1))
```

```python
import functools

import numpy as np
import jax
import jax.numpy as jnp
from jax import lax
from jax.experimental import pallas as pl
from jax.experimental.pallas import tpu as pltpu

D_MODEL = 1024
GRID_W = 64
N_HEADS = 8
N_KV_HEADS = 2
GROUP = N_HEADS // N_KV_HEADS
HEAD_DIM = 64
WINDOW = 128
F_GROUPS = 4
F_GROUP_DIM = 128
F_WIDTH = F_GROUPS * F_GROUP_DIM
ATTN_WIDTH = N_HEADS * HEAD_DIM
KV_WIDTH = N_KV_HEADS * HEAD_DIM
ROPE_THETA = 10000.0
EPS = 1e-6
NEG_INF = -1e30
LOG2_E = 1.4426950408889634

LANES = 128
MXU_DIM = 256
VMEM_BYTES_V7X = 64 * 1024 * 1024

_Q0 = F_WIDTH
_K0 = _Q0 + ATTN_WIDTH
_V0 = _K0 + KV_WIDTH
_G0 = _V0 + KV_WIDTH
IN_WIDTH = _G0 + 2 * D_MODEL

MOD_ROWS = 16
CTX_MOD_ROW = 8

TOKEN_TILE = 512
FF_CHUNK = MXU_DIM
Q_TILE = 512
CTX_SEQS_PER_STEP = 4
CTX_SOFTMAX_ROWS = 64
LAT_SOFTMAX_ROWS = 32

F32 = jnp.float32
BF16 = jnp.bfloat16


def _vmem_limit(nbytes):
    return int(min(nbytes, VMEM_BYTES_V7X - 4 * 1024 * 1024))


def _dot(a, b):
    return jnp.dot(a, b, preferred_element_type=F32)


def _dot_nt(a, b):
    return lax.dot_general(a, b, (((1,), (1,)), ((), ())), preferred_element_type=F32)


def _sigmoid(x):
    return 1.0 / (1.0 + jnp.exp(-x))


def _resident(shape):
    zeros = (0,) * len(shape)
    return pl.BlockSpec(shape, lambda *_: zeros, pipeline_mode=pl.Buffered(1))


def _hbm(x):
    return pltpu.with_memory_space_constraint(x, pltpu.HBM)


def _ada_kernel(c_ref, w_ref, b_ref, o_ref):
    c = c_ref[...]
    s = c * _sigmoid(c)
    o_ref[...] = _dot(s.astype(BF16), w_ref[...].astype(BF16)) + b_ref[...]


def _ada(cvec, w_ada, b_ada):
    n = w_ada.shape[1]
    bn = D_MODEL
    return pl.pallas_call(
        _ada_kernel,
        grid=(n // bn,),
        in_specs=[pl.BlockSpec((MOD_ROWS, D_MODEL), lambda j: (0, 0)),
                  pl.BlockSpec((D_MODEL, bn), lambda j: (0, j)),
                  pl.BlockSpec((1, bn), lambda j: (0, j))],
        out_specs=pl.BlockSpec((MOD_ROWS, bn), lambda j: (0, j)),
        out_shape=jax.ShapeDtypeStruct((MOD_ROWS, n), F32),
        name="ada",
    )(cvec, w_ada, b_ada)


def _head_norm(z, g):
    lo = lax.broadcasted_iota(jnp.int32, z.shape, 1) < HEAD_DIM
    s = z * z
    s_lo = jnp.sum(jnp.where(lo, s, 0.0), axis=-1, keepdims=True)
    s_hi = jnp.sum(jnp.where(lo, 0.0, s), axis=-1, keepdims=True)
    ms = jnp.where(lo, s_lo, s_hi) * (1.0 / HEAD_DIM)
    return z * lax.rsqrt(ms + EPS) * g


def _rope(y, cos, sin):
    lane = lax.broadcasted_iota(jnp.int32, y.shape, 1)
    first = (lane % 32) < 16
    partner = jnp.where(first, pltpu.roll(y, LANES - 16, 1), pltpu.roll(y, 16, 1))
    return y * cos + partner * sin


def _dup_heads(y):
    lo = lax.broadcasted_iota(jnp.int32, y.shape, 1) < HEAD_DIM
    sw = pltpu.roll(y, HEAD_DIM, 1)
    return jnp.concatenate([jnp.where(lo, y, sw), jnp.where(lo, sw, y)], axis=-1)


def _proj_kernel(*refs, rope, cache_seq, n_cast):
    x_ref, mod_ref, g1_ref, w_ref, gq_ref, gk_ref = refs[:6]
    refs = refs[6:]
    if rope:
        cos_ref, sin_ref = refs[:2]
        refs = refs[2:]
    cast_in, refs = refs[:n_cast], refs[n_cast:]
    zf_ref, q_ref, kd_ref, vd_ref, gate_ref = refs[:5]
    refs = refs[5:]
    if cache_seq:
        kc_ref, vc_ref = refs[:2]
        refs = refs[2:]
    for src_ref, dst_ref in zip(cast_in, refs):
        dst_ref[...] = src_ref[...].astype(BF16)

    x = x_ref[...]
    mod = mod_ref[0]
    sh1, sc1 = mod[0:1], mod[1:2]
    h = x * lax.rsqrt(jnp.mean(x * x, axis=-1, keepdims=True) + EPS) * g1_ref[...]
    hb = (h * (1.0 + sc1) + sh1).astype(BF16)

    zq = _dot(hb, w_ref[:, _Q0:_K0])
    for j in range(ATTN_WIDTH // LANES):
        y = _head_norm(zq[:, j * LANES:(j + 1) * LANES], gq_ref[...])
        if rope:
            y = _rope(y, cos_ref[...], sin_ref[...])
        q_ref[:, j * LANES:(j + 1) * LANES] = (y * (HEAD_DIM ** -0.5 * LOG2_E)).astype(BF16)

    zkv = _dot(hb, w_ref[:, _K0:_G0])
    k = _head_norm(zkv[:, 0:KV_WIDTH], gk_ref[...])
    v = zkv[:, KV_WIDTH:2 * KV_WIDTH]
    if cache_seq:
        for s in range(x.shape[0] // cache_seq):
            rows = slice(s * cache_seq, (s + 1) * cache_seq)
            kc_ref[s] = k[rows, :].T
            vc_ref[s] = v[rows, :].T
    if rope:
        k = _rope(k, cos_ref[...], sin_ref[...])
    kd_ref[...] = _dup_heads(k).astype(BF16)
    vd_ref[...] = _dup_heads(v).astype(BF16)

    gc = 512
    for j in range(2 * D_MODEL // gc):
        zg = _dot(hb, w_ref[:, _G0 + j * gc:_G0 + (j + 1) * gc])
        gate_ref[:, j * gc:(j + 1) * gc] = _sigmoid(zg).astype(BF16)

    zf_ref[...] = _dot(hb, w_ref[:, 0:_Q0]).astype(BF16)


def _proj(x, mod3, g1, w_in_b, gq2, gk2, rope_tabs, mod_row_of_tile, cache_seq, cast_weights):
    t = x.shape[0]
    tm = TOKEN_TILE
    steps = t // tm
    rope = rope_tabs is not None
    row = lambda i: (i, 0)
    in_specs = [pl.BlockSpec((tm, D_MODEL), row),
                pl.BlockSpec((1, 6, D_MODEL), lambda i: (mod_row_of_tile(i), 0, 0)),
                _resident((1, D_MODEL)),
                _resident((D_MODEL, IN_WIDTH)),
                _resident((1, LANES)),
                _resident((1, LANES))]
    args = [x, mod3, g1, w_in_b, gq2, gk2]
    if rope:
        tiles_per_seq = rope_tabs[0].shape[0] // tm
        tab = pl.BlockSpec((tm, LANES), lambda i: (i % tiles_per_seq, 0))
        in_specs += [tab, tab]
        args += list(rope_tabs)
    slabs = [pl.BlockSpec((w.shape[0] // steps, w.shape[1]), row) for w in cast_weights]
    in_specs += slabs
    args += list(cast_weights)
    out_specs = [pl.BlockSpec((tm, F_WIDTH), row),
                 pl.BlockSpec((tm, ATTN_WIDTH), row),
                 pl.BlockSpec((tm, 2 * KV_WIDTH), row),
                 pl.BlockSpec((tm, 2 * KV_WIDTH), row),
                 pl.BlockSpec((tm, 2 * D_MODEL), row)]
    out_shape = [pltpu.HBM((t, F_WIDTH), BF16),
                 pltpu.HBM((t, ATTN_WIDTH), BF16),
                 pltpu.HBM((t, 2 * KV_WIDTH), BF16),
                 pltpu.HBM((t, 2 * KV_WIDTH), BF16),
                 pltpu.HBM((t, 2 * D_MODEL), BF16)]
    if cache_seq:
        seqs = tm // cache_seq
        out_specs += [pl.BlockSpec((seqs, KV_WIDTH, cache_seq), lambda i: (i, 0, 0))] * 2
        out_shape += [pltpu.HBM((t // cache_seq, KV_WIDTH, cache_seq), F32)] * 2
    out_specs += slabs
    out_shape += [pltpu.HBM(w.shape, BF16) for w in cast_weights]
    return pl.pallas_call(
        functools.partial(_proj_kernel, rope=rope, cache_seq=cache_seq, n_cast=len(cast_weights)),
        grid=(steps,),
        in_specs=in_specs,
        out_specs=out_specs,
        out_shape=out_shape,
        compiler_params=pltpu.CompilerParams(
            dimension_semantics=("arbitrary",), vmem_limit_bytes=_vmem_limit(48 * 2 ** 20)),
        name="proj_lat" if rope else "proj_ctx",
    )(*[_hbm(a) for a in args])


def _dft_tables(n):
    def cs(m):
        idx = np.arange(m, dtype=np.int64)
        ang = 2.0 * np.pi * ((idx[:, None] * idx[None, :]) % m).astype(np.float64) / m
        return np.cos(ang) / np.sqrt(m), np.sin(ang) / np.sqrt(m)
    cd, sd = cs(F_GROUP_DIM)
    cn, sn = cs(n)
    csd = np.concatenate([cd, sd], axis=1).astype(np.float32)
    csn = np.concatenate([cn, -sn], axis=1).astype(np.float32)
    return jnp.asarray(csd).astype(BF16), jnp.asarray(csn).astype(BF16)


def _fourier_kernel(zf_ref, csd_ref, csn_ref, o_ref, ab_ref, *, seqs, n):
    for s in range(seqs):
        for g in range(F_GROUPS):
            cols = slice(g * F_GROUP_DIM, (g + 1) * F_GROUP_DIM)
            ab = _dot(zf_ref[s, :, cols], csd_ref[...])
            ab_ref[0:n, cols] = ab[:, 0:F_GROUP_DIM].astype(BF16)
            ab_ref[n:2 * n, cols] = ab[:, F_GROUP_DIM:].astype(BF16)
        o_ref[s] = _dot(csn_ref[...], ab_ref[...]).astype(BF16)


def _fourier(zf3, seqs):
    b, n, _ = zf3.shape
    csd, csn = _dft_tables(n)
    blk = pl.BlockSpec((seqs, n, F_WIDTH), lambda i: (i, 0, 0))
    return pl.pallas_call(
        functools.partial(_fourier_kernel, seqs=seqs, n=n),
        grid=(b // seqs,),
        in_specs=[blk, _resident(csd.shape), _resident(csn.shape)],
        out_specs=blk,
        out_shape=pltpu.HBM(zf3.shape, BF16),
        scratch_shapes=[pltpu.VMEM((2 * n, F_WIDTH), BF16)],
        compiler_params=pltpu.CompilerParams(
            dimension_semantics=("arbitrary",), vmem_limit_bytes=_vmem_limit(32 * 2 ** 20)),
        name=f"fourier_{n}",
    )(_hbm(zf3), csd, csn)


def _split_lo_hi(xd, fill):
    lo = lax.broadcasted_iota(jnp.int32, xd.shape, 1) < HEAD_DIM
    other = jnp.full_like(xd, fill)
    return jnp.where(lo, xd, other), jnp.where(lo, other, xd)


def _group_scores(q_stack, keys, s_ref):
    c0 = 0
    for kd in keys:
        k_lo, k_hi = _split_lo_hi(kd, 0.0)
        c1 = c0 + kd.shape[0]
        s_ref[0, :, c0:c1] = _dot_nt(q_stack, k_lo)
        s_ref[1, :, c0:c1] = _dot_nt(q_stack, k_hi)
        c0 = c1


def _group_softmax_pv(values, sinks, band, s_ref, p_ref, sk_ref, rows_per_pair, chunk):
    rows = s_ref.shape[1]
    bounds = [0]
    for vd in values:
        bounds.append(bounds[-1] + vd.shape[0])
    sources = values

    for half in range(2):
        for r0 in range(0, rows, chunk):
            rs = slice(r0, r0 + chunk)
            sink = sinks[2 * (r0 // rows_per_pair) + half]
            parts = [s_ref[half, rs, bounds[i]:bounds[i + 1]] for i in range(len(sources))]
            if band is not None:
                parts[0] = jnp.where(band(r0 % rows_per_pair, chunk), parts[0], NEG_INF)
            m = sink
            for s in parts:
                m = jnp.maximum(m, jnp.max(s, axis=-1, keepdims=True))
            for i, s in enumerate(parts):
                p_ref[half, rs, bounds[i]:bounds[i + 1]] = jnp.exp2((s - m).astype(BF16))
            sk_ref[half, rs, :] = jnp.broadcast_to(jnp.exp2(sink - m), (chunk, LANES))

    o_e = o_o = None
    for i, vd in enumerate(values):
        v_lo, v_hi = _split_lo_hi(vd, 1.0)
        pe = _dot(p_ref[0, :, bounds[i]:bounds[i + 1]], v_lo)
        po = _dot(p_ref[1, :, bounds[i]:bounds[i + 1]], v_hi)
        o_e = pe if o_e is None else o_e + pe
        o_o = po if o_o is None else o_o + po
    lo = lax.broadcasted_iota(jnp.int32, (rows, LANES), 1) < HEAD_DIM
    num = jnp.where(lo, o_e, o_o)
    den = pltpu.roll(jnp.where(lo, o_o, o_e), HEAD_DIM, 1) + jnp.where(lo, sk_ref[0], sk_ref[1])
    return num / den


def _run_units(units):
    units[0][0]()
    for k, (_, finish) in enumerate(units):
        if k + 1 < len(units):
            units[k + 1][0]()
        finish()


def _pair_cols(g):
    pa, pb = 2 * g, 2 * g + 1
    return slice(pa * LANES, (pa + 1) * LANES), slice(pb * LANES, (pb + 1) * LANES)


def _attn_ctx_kernel(sink_ref, q_ref, kd_ref, vd_ref, o_ref, s_ref, p_ref, sk_ref, *, seqs):
    n = q_ref.shape[1]
    units = []
    for s in range(seqs):
        for g in range(N_KV_HEADS):
            buf = len(units) % 2
            cols = slice(g * LANES, (g + 1) * LANES)
            qa, qb = _pair_cols(g)

            def scores(s=s, cols=cols, qa=qa, qb=qb, buf=buf):
                q_stack = jnp.concatenate([q_ref[s, :, qa], q_ref[s, :, qb]], axis=0)
                _group_scores(q_stack, [kd_ref[s, :, cols]], s_ref.at[buf])

            def finish(s=s, g=g, cols=cols, qa=qa, qb=qb, buf=buf):
                sinks = [sink_ref[GROUP * g + h] * LOG2_E for h in range(GROUP)]
                o = _group_softmax_pv([vd_ref[s, :, cols]], sinks, None, s_ref.at[buf],
                                      p_ref.at[buf], sk_ref.at[buf], n, CTX_SOFTMAX_ROWS)
                o_ref[s, :, qa] = o[0:n].astype(BF16)
                o_ref[s, :, qb] = o[n:2 * n].astype(BF16)

            units.append((scores, finish))
    _run_units(units)


def _attn_ctx(sinks, q3, kd3, vd3):
    b, n, _ = q3.shape
    seqs = CTX_SEQS_PER_STEP
    blk = lambda w: pl.BlockSpec((seqs, n, w), lambda i: (i, 0, 0))
    return pl.pallas_call(
        functools.partial(_attn_ctx_kernel, seqs=seqs),
        grid=(b // seqs,),
        in_specs=[pl.BlockSpec(memory_space=pltpu.SMEM),
                  blk(ATTN_WIDTH), blk(2 * KV_WIDTH), blk(2 * KV_WIDTH)],
        out_specs=blk(ATTN_WIDTH),
        out_shape=pltpu.HBM(q3.shape, BF16),
        scratch_shapes=[pltpu.VMEM((2, 2, 2 * n, n), F32),
                        pltpu.VMEM((2, 2, 2 * n, n), BF16),
                        pltpu.VMEM((2, 2, 2 * n, LANES), F32)],
        compiler_params=pltpu.CompilerParams(
            dimension_semantics=("arbitrary",), vmem_limit_bytes=_vmem_limit(32 * 2 ** 20)),
        name="attn_ctx",
    )(sinks, _hbm(q3), _hbm(kd3), _hbm(vd3))


def _attn_lat_kernel(sink_ref, q_ref, kd_ref, vd_ref, ck_ref, cv_ref, o_ref,
                     s_ref, p_ref, sk_ref, dist_ref, ckv_ref, *, n, tq):
    t = pl.program_id(1)
    qb = WINDOW
    span = qb + 2 * WINDOW

    ckv_ref[0] = _dup_heads(ck_ref[0]).astype(BF16)
    ckv_ref[1] = _dup_heads(cv_ref[0]).astype(BF16)
    units = []
    for rr in range(tq // qb):
        q0 = t * tq + rr * qb
        start = pl.multiple_of(jnp.clip(q0 - WINDOW, 0, n - span), WINDOW)
        dist_ref[rr] = jnp.abs((q0 - start) + lax.broadcasted_iota(jnp.int32, (qb, span), 0)
                               - lax.broadcasted_iota(jnp.int32, (qb, span), 1))
        band = lambda r0, rows, rr=rr: dist_ref[rr, r0:r0 + rows, :] <= WINDOW
        rows = slice(rr * qb, (rr + 1) * qb)
        for g in range(N_KV_HEADS):
            buf = len(units) % 2
            cols = slice(g * LANES, (g + 1) * LANES)
            qa, qbc = _pair_cols(g)

            def scores(rows=rows, start=start, cols=cols, qa=qa, qbc=qbc, buf=buf):
                q_stack = jnp.concatenate([q_ref[0, rows, qa], q_ref[0, rows, qbc]], axis=0)
                _group_scores(q_stack, [kd_ref[0, pl.ds(start, span), cols], ckv_ref[0, :, cols]],
                              s_ref.at[buf])

            def finish(rows=rows, start=start, g=g, cols=cols, qa=qa, qbc=qbc, buf=buf, band=band):
                sinks = [sink_ref[GROUP * g + h] * LOG2_E for h in range(GROUP)]
                o = _group_softmax_pv([vd_ref[0, pl.ds(start, span), cols], ckv_ref[1, :, cols]],
                                      sinks, band, s_ref.at[buf], p_ref.at[buf], sk_ref.at[buf],
                                      qb, LAT_SOFTMAX_ROWS)
                o_ref[0, rows, qa] = o[0:qb].astype(BF16)
                o_ref[0, rows, qbc] = o[qb:2 * qb].astype(BF16)

            units.append((scores, finish))
    _run_units(units)


def _attn_lat(sinks, q3, kd3, vd3, ck3, cv3):
    b, n, _ = q3.shape
    past = ck3.shape[1]
    tq = Q_TILE
    keys = 3 * WINDOW + past
    qblk = pl.BlockSpec((1, tq, ATTN_WIDTH), lambda i, t: (i, t, 0))
    seq = lambda rows, w: pl.BlockSpec((1, rows, w), lambda i, t: (i, 0, 0))
    return pl.pallas_call(
        functools.partial(_attn_lat_kernel, n=n, tq=tq),
        grid=(b, n // tq),
        in_specs=[pl.BlockSpec(memory_space=pltpu.SMEM), qblk,
                  seq(n, 2 * KV_WIDTH), seq(n, 2 * KV_WIDTH),
                  seq(past, KV_WIDTH), seq(past, KV_WIDTH)],
        out_specs=qblk,
        out_shape=pltpu.HBM(q3.shape, BF16),
        scratch_shapes=[pltpu.VMEM((2, 2, 2 * WINDOW, keys), F32),
                        pltpu.VMEM((2, 2, 2 * WINDOW, keys), BF16),
                        pltpu.VMEM((2, 2, 2 * WINDOW, LANES), F32),
                        pltpu.VMEM((tq // WINDOW, WINDOW, 3 * WINDOW), jnp.int32),
                        pltpu.VMEM((2, past, 2 * KV_WIDTH), BF16)],
        compiler_params=pltpu.CompilerParams(
            dimension_semantics=("arbitrary", "arbitrary"),
            vmem_limit_bytes=_vmem_limit(32 * 2 ** 20)),
        name="attn_lat",
    )(sinks, _hbm(q3), _hbm(kd3), _hbm(vd3), _hbm(ck3), _hbm(cv3))


def _tail_kernel(x_ref, mod_ref, mix_ref, att_ref, gate_ref, g2_ref,
                 wf_ref, wao_ref, wout_ref, wup_ref, wdn_ref, o_ref):
    mod = mod_ref[0]
    gt1, sh2, sc2, gt2 = mod[2:3], mod[3:4], mod[4:5], mod[5:6]

    yf = _dot(mix_ref[...], wf_ref[...])
    ya = _dot(att_ref[...], wao_ref[...])
    merged = (gate_ref[:, 0:D_MODEL].astype(F32) * yf
              + gate_ref[:, D_MODEL:2 * D_MODEL].astype(F32) * ya)
    x1 = x_ref[...] + gt1 * _dot(merged.astype(BF16), wout_ref[...])

    h = x1 * lax.rsqrt(jnp.mean(x1 * x1, axis=-1, keepdims=True) + EPS) * g2_ref[...]
    hb = (h * (1.0 + sc2) + sh2).astype(BF16)
    d_ff = wdn_ref.shape[0]
    acc = None
    for c in range(d_ff // FF_CHUNK):
        a = _dot(hb, wup_ref[:, c * FF_CHUNK:(c + 1) * FF_CHUNK])
        u = _dot(hb, wup_ref[:, d_ff + c * FF_CHUNK:d_ff + (c + 1) * FF_CHUNK])
        act = (a * _sigmoid(a) * u).astype(BF16)
        d = _dot(act, wdn_ref[c * FF_CHUNK:(c + 1) * FF_CHUNK, :])
        acc = d if acc is None else acc + d
    o_ref[...] = x1 + gt2 * acc


def _tail(x, mod3, mix, att, gate, g2, wf, wao, wout, wup, wdn, mod_row_of_tile):
    t = x.shape[0]
    tm = TOKEN_TILE
    row = lambda i: (i, 0)
    return pl.pallas_call(
        _tail_kernel,
        grid=(t // tm,),
        in_specs=[pl.BlockSpec((tm, D_MODEL), row),
                  pl.BlockSpec((1, 6, D_MODEL), lambda i: (mod_row_of_tile(i), 0, 0)),
                  pl.BlockSpec((tm, F_WIDTH), row),
                  pl.BlockSpec((tm, ATTN_WIDTH), row),
                  pl.BlockSpec((tm, 2 * D_MODEL), row),
                  _resident((1, D_MODEL)),
                  _resident(wf.shape), _resident(wao.shape), _resident(wout.shape),
                  _resident(wup.shape), _resident(wdn.shape)],
        out_specs=pl.BlockSpec((tm, D_MODEL), row),
        out_shape=pltpu.HBM((t, D_MODEL), F32),
        compiler_params=pltpu.CompilerParams(
            dimension_semantics=("arbitrary",), vmem_limit_bytes=_vmem_limit(56 * 2 ** 20)),
        name="tail",
    )(*[_hbm(a) for a in (x, mod3, mix, att, gate, g2, wf, wao, wout, wup, wdn)])


def _rope_tables(n):
    rows = n // GRID_W
    row = jnp.repeat(jnp.arange(rows, dtype=F32), GRID_W)
    col = jnp.tile(jnp.arange(GRID_W, dtype=F32), rows)
    axis_dim = HEAD_DIM // 2
    inv_freq = ROPE_THETA ** (-jnp.arange(0, axis_dim, 2, dtype=F32) / axis_dim)

    def axis_tabs(pos):
        ang = pos[:, None] * inv_freq[None, :]
        cos, sin = jnp.cos(ang), jnp.sin(ang)
        return jnp.concatenate([cos, cos], axis=-1), jnp.concatenate([-sin, sin], axis=-1)

    cr, sr = axis_tabs(row)
    cc, sc = axis_tabs(col)
    cos = jnp.concatenate([cr, cc], axis=-1)
    sin = jnp.concatenate([sr, sc], axis=-1)
    return jnp.tile(cos, (1, LANES // HEAD_DIM)), jnp.tile(sin, (1, LANES // HEAD_DIM))


def _layer(xp, xs, ck, cv, c, c_ctx, w_ada, b_ada, g_norm1, g_norm2, w_in, g_q, g_k, sink,
           w_f, w_ao, w_out, w_up, w_down):
    bp, sp, _ = xp.shape
    bs, ss, _ = xs.shape
    past = ck.shape[1]

    cvec = jnp.concatenate(
        [c, c_ctx[None, :], jnp.zeros((MOD_ROWS - bs - 1, D_MODEL), F32)], axis=0)
    mod3 = _ada(cvec, w_ada, b_ada[None, :]).reshape(MOD_ROWS, 6, D_MODEL)

    w_in_b = w_in.astype(BF16)
    g1, g2 = g_norm1[None, :], g_norm2[None, :]
    gq2 = jnp.tile(g_q, LANES // HEAD_DIM)[None, :]
    gk2 = jnp.tile(g_k, LANES // HEAD_DIM)[None, :]

    ctx_row = lambda i: CTX_MOD_ROW
    lat_row = lambda i: i // (ss // TOKEN_TILE)

    xp2 = xp.reshape(bp * sp, D_MODEL)
    zf, q, kd, vd, gate, k_new, v_new, wf, wao, wout, wup, wdn = _proj(
        xp2, mod3, g1, w_in_b, gq2, gk2, None, ctx_row, sp, [w_f, w_ao, w_out, w_up, w_down])
    mix = _fourier(zf.reshape(bp, sp, F_WIDTH), CTX_SEQS_PER_STEP)
    att = _attn_ctx(sink, q.reshape(bp, sp, ATTN_WIDTH), kd.reshape(bp, sp, 2 * KV_WIDTH),
                    vd.reshape(bp, sp, 2 * KV_WIDTH))
    yp = _tail(xp2, mod3, mix.reshape(bp * sp, F_WIDTH), att.reshape(bp * sp, ATTN_WIDTH), gate,
               g2, wf, wao, wout, wup, wdn, ctx_row)

    xs2 = xs.reshape(bs * ss, D_MODEL)
    zf, q, kd, vd, gate = _proj(xs2, mod3, g1, w_in_b, gq2, gk2, _rope_tables(ss), lat_row, 0, [])
    mix = _fourier(zf.reshape(bs, ss, F_WIDTH), 1)
    att = _attn_lat(sink, q.reshape(bs, ss, ATTN_WIDTH), kd.reshape(bs, ss, 2 * KV_WIDTH),
                    vd.reshape(bs, ss, 2 * KV_WIDTH), ck.reshape(bs, past, KV_WIDTH),
                    cv.reshape(bs, past, KV_WIDTH))
    ys = _tail(xs2, mod3, mix.reshape(bs * ss, F_WIDTH), att.reshape(bs * ss, ATTN_WIDTH), gate,
               g2, wf, wao, wout, wup, wdn, lat_row)

    def cache_layout(t):
        return t.reshape(bp, N_KV_HEADS, HEAD_DIM, sp).transpose(0, 3, 1, 2)

    return yp.reshape(xp.shape), ys.reshape(xs.shape), cache_layout(k_new), cache_layout(v_new)


def kernel(x_prompt, x_sample, cache_k, cache_v, c, c_ctx, w_ada, b_ada, g_norm1, g_norm2,
           w_in, g_q, g_k, sinks, w_f, w_ao, w_out, w_up, w_down):
    depth = w_in.shape[0]
    xp, xs = x_prompt, x_sample
    new_k, new_v = [], []
    for l in range(depth):
        xp, xs, k_ctx, v_ctx = _layer(
            xp, xs, cache_k[:, l], cache_v[:, l], c, c_ctx, w_ada[l], b_ada[l], g_norm1[l],
            g_norm2[l], w_in[l], g_q[l], g_k[l], sinks[l], w_f[l], w_ao[l], w_out[l], w_up[l],
            w_down[l])
        new_k.append(k_ctx)
        new_v.append(v_ctx)
    return (xp, xs, jnp.stack(new_k, axis=1), jnp.stack(new_v, axis=1))
```

```python
import functools

import numpy as np
import jax
import jax.numpy as jnp
from jax import lax
from jax.experimental import pallas as pl
from jax.experimental.pallas import tpu as pltpu

D_MODEL = 1024
GRID_W = 64
N_HEADS = 8
N_KV_HEADS = 2
GROUP = N_HEADS // N_KV_HEADS
HEAD_DIM = 64
WINDOW = 128
F_GROUPS = 4
F_GROUP_DIM = 128
F_WIDTH = F_GROUPS * F_GROUP_DIM
ATTN_WIDTH = N_HEADS * HEAD_DIM
KV_WIDTH = N_KV_HEADS * HEAD_DIM
ROPE_THETA = 10000.0
EPS = 1e-6
NEG_INF = -1e30
LOG2_E = 1.4426950408889634

LANES = 128
MXU_DIM = 256
VMEM_BYTES_V7X = 64 * 1024 * 1024

_Q0 = F_WIDTH
_K0 = _Q0 + ATTN_WIDTH
_V0 = _K0 + KV_WIDTH
_G0 = _V0 + KV_WIDTH
IN_WIDTH = _G0 + 2 * D_MODEL

MOD_ROWS = 16
CTX_MOD_ROW = 8

PROJ_TOKEN_TILE = 1024
TAIL_TOKEN_TILE = 512
FF_CHUNK = MXU_DIM
Q_TILE = 512
CTX_SEQS_PER_STEP = 4
CTX_SOFTMAX_ROWS = 64
LAT_SOFTMAX_ROWS = 32

F32 = jnp.float32
BF16 = jnp.bfloat16


def _vmem_limit(nbytes):
    return int(min(nbytes, VMEM_BYTES_V7X - 4 * 1024 * 1024))


def _dot(a, b):
    return jnp.dot(a, b, preferred_element_type=F32)


def _dot_nt(a, b):
    return lax.dot_general(a, b, (((1,), (1,)), ((), ())), preferred_element_type=F32)


def _sigmoid(x):
    return 1.0 / (1.0 + jnp.exp(-x))


def _resident(shape):
    zeros = (0,) * len(shape)
    return pl.BlockSpec(shape, lambda *_: zeros, pipeline_mode=pl.Buffered(1))


def _hbm(x):
    return pltpu.with_memory_space_constraint(x, pltpu.HBM)


def _run_units(units):
    units[0][0]()
    for k, (_, second) in enumerate(units):
        if k + 1 < len(units):
            units[k + 1][0]()
        second()


def _ada_kernel(c_ref, w_ref, b_ref, o_ref):
    c = c_ref[...]
    s = c * _sigmoid(c)
    o_ref[...] = _dot(s.astype(BF16), w_ref[...].astype(BF16)) + b_ref[...]


def _ada(cvec, w_ada, b_ada):
    n = w_ada.shape[1]
    bn = D_MODEL
    return pl.pallas_call(
        _ada_kernel,
        grid=(n // bn,),
        in_specs=[pl.BlockSpec((MOD_ROWS, D_MODEL), lambda j: (0, 0)),
                  pl.BlockSpec((D_MODEL, bn), lambda j: (0, j)),
                  pl.BlockSpec((1, bn), lambda j: (0, j))],
        out_specs=pl.BlockSpec((MOD_ROWS, bn), lambda j: (0, j)),
        out_shape=jax.ShapeDtypeStruct((MOD_ROWS, n), F32),
        name="ada",
    )(cvec, w_ada, b_ada)


def _head_norm(z, g):
    lo = lax.broadcasted_iota(jnp.int32, z.shape, 1) < HEAD_DIM
    s = z * z
    s_lo = jnp.sum(jnp.where(lo, s, 0.0), axis=-1, keepdims=True)
    s_hi = jnp.sum(jnp.where(lo, 0.0, s), axis=-1, keepdims=True)
    ms = jnp.where(lo, s_lo, s_hi) * (1.0 / HEAD_DIM)
    return z * lax.rsqrt(ms + EPS) * g


def _rope(y, cos, sin):
    lane = lax.broadcasted_iota(jnp.int32, y.shape, 1)
    first = (lane % 32) < 16
    partner = jnp.where(first, pltpu.roll(y, LANES - 16, 1), pltpu.roll(y, 16, 1))
    return y * cos + partner * sin


def _dup_heads(y):
    lo = lax.broadcasted_iota(jnp.int32, y.shape, 1) < HEAD_DIM
    sw = pltpu.roll(y, HEAD_DIM, 1)
    return jnp.concatenate([jnp.where(lo, y, sw), jnp.where(lo, sw, y)], axis=-1)


def _proj_kernel(*refs, rope, cache_seq, n_cast):
    x_ref, mod_ref, g1_ref, w_ref, gq_ref, gk_ref = refs[:6]
    refs = refs[6:]
    if rope:
        cos_ref, sin_ref = refs[:2]
        refs = refs[2:]
    cast_in, refs = refs[:n_cast], refs[n_cast:]
    zf_ref, q_ref, kd_ref, vd_ref, gate_ref = refs[:5]
    refs = refs[5:]
    if cache_seq:
        kc_ref, vc_ref = refs[:2]
        refs = refs[2:]
    for src_ref, dst_ref in zip(cast_in, refs):
        dst_ref[...] = src_ref[...].astype(BF16)

    x = x_ref[...]
    mod = mod_ref[0]
    sh1, sc1 = mod[0:1], mod[1:2]
    h = x * lax.rsqrt(jnp.mean(x * x, axis=-1, keepdims=True) + EPS) * g1_ref[...]
    hb = (h * (1.0 + sc1) + sh1).astype(BF16)

    zq = _dot(hb, w_ref[:, _Q0:_K0])
    for j in range(ATTN_WIDTH // LANES):
        y = _head_norm(zq[:, j * LANES:(j + 1) * LANES], gq_ref[...])
        if rope:
            y = _rope(y, cos_ref[...], sin_ref[...])
        q_ref[:, j * LANES:(j + 1) * LANES] = (y * (HEAD_DIM ** -0.5 * LOG2_E)).astype(BF16)

    zkv = _dot(hb, w_ref[:, _K0:_G0])
    k = _head_norm(zkv[:, 0:KV_WIDTH], gk_ref[...])
    v = zkv[:, KV_WIDTH:2 * KV_WIDTH]
    if cache_seq:
        for s in range(x.shape[0] // cache_seq):
            rows = slice(s * cache_seq, (s + 1) * cache_seq)
            kc_ref[s] = k[rows, :].T
            vc_ref[s] = v[rows, :].T
    if rope:
        k = _rope(k, cos_ref[...], sin_ref[...])
    kd_ref[...] = _dup_heads(k).astype(BF16)
    vd_ref[...] = _dup_heads(v).astype(BF16)

    gc = 512
    for j in range(2 * D_MODEL // gc):
        zg = _dot(hb, w_ref[:, _G0 + j * gc:_G0 + (j + 1) * gc])
        gate_ref[:, j * gc:(j + 1) * gc] = _sigmoid(zg).astype(BF16)

    zf_ref[...] = _dot(hb, w_ref[:, 0:_Q0]).astype(BF16)


def _proj(x, mod3, g1, w_in_b, gq2, gk2, rope_tabs, mod_row_of_tile, cache_seq, cast_weights):
    t = x.shape[0]
    tm = PROJ_TOKEN_TILE
    steps = t // tm
    rope = rope_tabs is not None
    row = lambda i: (i, 0)
    in_specs = [pl.BlockSpec((tm, D_MODEL), row),
                pl.BlockSpec((1, 6, D_MODEL), lambda i: (mod_row_of_tile(i, tm), 0, 0)),
                _resident((1, D_MODEL)),
                _resident((D_MODEL, IN_WIDTH)),
                _resident((1, LANES)),
                _resident((1, LANES))]
    args = [x, mod3, g1, w_in_b, gq2, gk2]
    if rope:
        tiles_per_seq = rope_tabs[0].shape[0] // tm
        tab = pl.BlockSpec((tm, LANES), lambda i: (i % tiles_per_seq, 0))
        in_specs += [tab, tab]
        args += list(rope_tabs)
    slabs = [pl.BlockSpec((w.shape[0] // steps, w.shape[1]), row) for w in cast_weights]
    in_specs += slabs
    args += list(cast_weights)
    out_specs = [pl.BlockSpec((tm, F_WIDTH), row),
                 pl.BlockSpec((tm, ATTN_WIDTH), row),
                 pl.BlockSpec((tm, 2 * KV_WIDTH), row),
                 pl.BlockSpec((tm, 2 * KV_WIDTH), row),
                 pl.BlockSpec((tm, 2 * D_MODEL), row)]
    out_shape = [pltpu.HBM((t, F_WIDTH), BF16),
                 pltpu.HBM((t, ATTN_WIDTH), BF16),
                 pltpu.HBM((t, 2 * KV_WIDTH), BF16),
                 pltpu.HBM((t, 2 * KV_WIDTH), BF16),
                 pltpu.HBM((t, 2 * D_MODEL), BF16)]
    if cache_seq:
        seqs = tm // cache_seq
        out_specs += [pl.BlockSpec((seqs, KV_WIDTH, cache_seq), lambda i: (i, 0, 0))] * 2
        out_shape += [pltpu.HBM((t // cache_seq, KV_WIDTH, cache_seq), F32)] * 2
    out_specs += slabs
    out_shape += [pltpu.HBM(w.shape, BF16) for w in cast_weights]
    return pl.pallas_call(
        functools.partial(_proj_kernel, rope=rope, cache_seq=cache_seq, n_cast=len(cast_weights)),
        grid=(steps,),
        in_specs=in_specs,
        out_specs=out_specs,
        out_shape=out_shape,
        compiler_params=pltpu.CompilerParams(
            dimension_semantics=("arbitrary",), vmem_limit_bytes=_vmem_limit(58 * 2 ** 20)),
        name="proj_lat" if rope else "proj_ctx",
    )(*[_hbm(a) for a in args])


def _dft_tables(n):
    def cs(m):
        idx = np.arange(m, dtype=np.int64)
        ang = 2.0 * np.pi * ((idx[:, None] * idx[None, :]) % m).astype(np.float64) / m
        return np.cos(ang) / np.sqrt(m), np.sin(ang) / np.sqrt(m)
    cd, sd = cs(F_GROUP_DIM)
    cn, sn = cs(n)
    csd = np.concatenate([cd, sd], axis=1).astype(np.float32)
    csn = np.concatenate([cn, -sn], axis=1).astype(np.float32)
    return jnp.asarray(csd).astype(BF16), jnp.asarray(csn).astype(BF16)


def _fourier_kernel(zf_ref, csd_ref, csn_ref, o_ref, ab_ref, *, seqs, n):
    units = []
    for s in range(seqs):
        buf = s % 2

        def channels(s=s, buf=buf):
            for g in range(F_GROUPS):
                cols = slice(g * F_GROUP_DIM, (g + 1) * F_GROUP_DIM)
                ab = _dot(zf_ref[s, :, cols], csd_ref[...])
                ab_ref[buf, 0:n, cols] = ab[:, 0:F_GROUP_DIM].astype(BF16)
                ab_ref[buf, n:2 * n, cols] = ab[:, F_GROUP_DIM:].astype(BF16)

        def positions(s=s, buf=buf):
            o_ref[s] = _dot(csn_ref[...], ab_ref[buf]).astype(BF16)

        units.append((channels, positions))
    _run_units(units)


def _fourier(zf3, seqs):
    b, n, _ = zf3.shape
    csd, csn = _dft_tables(n)
    blk = pl.BlockSpec((seqs, n, F_WIDTH), lambda i: (i, 0, 0))
    return pl.pallas_call(
        functools.partial(_fourier_kernel, seqs=seqs, n=n),
        grid=(b // seqs,),
        in_specs=[blk, _resident(csd.shape), _resident(csn.shape)],
        out_specs=blk,
        out_shape=pltpu.HBM(zf3.shape, BF16),
        scratch_shapes=[pltpu.VMEM((2, 2 * n, F_WIDTH), BF16)],
        compiler_params=pltpu.CompilerParams(
            dimension_semantics=("arbitrary",), vmem_limit_bytes=_vmem_limit(32 * 2 ** 20)),
        name=f"fourier_{n}",
    )(_hbm(zf3), csd, csn)


def _split_lo_hi(xd, fill):
    lo = lax.broadcasted_iota(jnp.int32, xd.shape, 1) < HEAD_DIM
    other = jnp.full_like(xd, fill)
    return jnp.where(lo, xd, other), jnp.where(lo, other, xd)


def _group_scores(q_stack, keys, s_ref):
    c0 = 0
    for kd in keys:
        k_lo, k_hi = _split_lo_hi(kd, 0.0)
        c1 = c0 + kd.shape[0]
        s_ref[0, :, c0:c1] = _dot_nt(q_stack, k_lo)
        s_ref[1, :, c0:c1] = _dot_nt(q_stack, k_hi)
        c0 = c1


def _group_softmax_pv(values, sinks, band, s_ref, p_ref, sk_ref, rows_per_pair, chunk):
    rows = s_ref.shape[1]
    bounds = [0]
    for vd in values:
        bounds.append(bounds[-1] + vd.shape[0])
    sources = values

    for half in range(2):
        for r0 in range(0, rows, chunk):
            rs = slice(r0, r0 + chunk)
            sink = sinks[2 * (r0 // rows_per_pair) + half]
            parts = [s_ref[half, rs, bounds[i]:bounds[i + 1]] for i in range(len(sources))]
            if band is not None:
                parts[0] = jnp.where(band(r0 % rows_per_pair, chunk), parts[0], NEG_INF)
            m = sink
            for s in parts:
                m = jnp.maximum(m, jnp.max(s, axis=-1, keepdims=True))
            for i, s in enumerate(parts):
                p_ref[half, rs, bounds[i]:bounds[i + 1]] = jnp.exp2((s - m).astype(BF16))
            sk_ref[half, rs, :] = jnp.broadcast_to(jnp.exp2(sink - m), (chunk, LANES))

    o_e = o_o = None
    for i, vd in enumerate(values):
        v_lo, v_hi = _split_lo_hi(vd, 1.0)
        pe = _dot(p_ref[0, :, bounds[i]:bounds[i + 1]], v_lo)
        po = _dot(p_ref[1, :, bounds[i]:bounds[i + 1]], v_hi)
        o_e = pe if o_e is None else o_e + pe
        o_o = po if o_o is None else o_o + po
    lo = lax.broadcasted_iota(jnp.int32, (rows, LANES), 1) < HEAD_DIM
    num = jnp.where(lo, o_e, o_o)
    den = pltpu.roll(jnp.where(lo, o_o, o_e), HEAD_DIM, 1) + jnp.where(lo, sk_ref[0], sk_ref[1])
    return num / den


def _pair_cols(g):
    pa, pb = 2 * g, 2 * g + 1
    return slice(pa * LANES, (pa + 1) * LANES), slice(pb * LANES, (pb + 1) * LANES)


def _attn_ctx_kernel(sink_ref, q_ref, kd_ref, vd_ref, o_ref, s_ref, p_ref, sk_ref, *, seqs):
    n = q_ref.shape[1]
    units = []
    for s in range(seqs):
        for g in range(N_KV_HEADS):
            buf = len(units) % 2
            cols = slice(g * LANES, (g + 1) * LANES)
            qa, qb = _pair_cols(g)

            def scores(s=s, cols=cols, qa=qa, qb=qb, buf=buf):
                q_stack = jnp.concatenate([q_ref[s, :, qa], q_ref[s, :, qb]], axis=0)
                _group_scores(q_stack, [kd_ref[s, :, cols]], s_ref.at[buf])

            def finish(s=s, g=g, cols=cols, qa=qa, qb=qb, buf=buf):
                sinks = [sink_ref[GROUP * g + h] * LOG2_E for h in range(GROUP)]
                o = _group_softmax_pv([vd_ref[s, :, cols]], sinks, None, s_ref.at[buf],
                                      p_ref.at[buf], sk_ref.at[buf], n, CTX_SOFTMAX_ROWS)
                o_ref[s, :, qa] = o[0:n].astype(BF16)
                o_ref[s, :, qb] = o[n:2 * n].astype(BF16)

            units.append((scores, finish))
    _run_units(units)


def _attn_ctx(sinks, q3, kd3, vd3):
    b, n, _ = q3.shape
    seqs = CTX_SEQS_PER_STEP
    blk = lambda w: pl.BlockSpec((seqs, n, w), lambda i: (i, 0, 0))
    return pl.pallas_call(
        functools.partial(_attn_ctx_kernel, seqs=seqs),
        grid=(b // seqs,),
        in_specs=[pl.BlockSpec(memory_space=pltpu.SMEM),
                  blk(ATTN_WIDTH), blk(2 * KV_WIDTH), blk(2 * KV_WIDTH)],
        out_specs=blk(ATTN_WIDTH),
        out_shape=pltpu.HBM(q3.shape, BF16),
        scratch_shapes=[pltpu.VMEM((2, 2, 2 * n, n), F32),
                        pltpu.VMEM((2, 2, 2 * n, n), BF16),
                        pltpu.VMEM((2, 2, 2 * n, LANES), F32)],
        compiler_params=pltpu.CompilerParams(
            dimension_semantics=("arbitrary",), vmem_limit_bytes=_vmem_limit(32 * 2 ** 20)),
        name="attn_ctx",
    )(sinks, _hbm(q3), _hbm(kd3), _hbm(vd3))


def _attn_lat_kernel(sink_ref, q_ref, kd_ref, vd_ref, ck_ref, cv_ref, o_ref,
                     s_ref, p_ref, sk_ref, dist_ref, ckv_ref, *, n, tq):
    t = pl.program_id(1)
    qb = WINDOW
    span = qb + 2 * WINDOW

    ckv_ref[0] = _dup_heads(ck_ref[0]).astype(BF16)
    ckv_ref[1] = _dup_heads(cv_ref[0]).astype(BF16)
    units = []
    for rr in range(tq // qb):
        q0 = t * tq + rr * qb
        start = pl.multiple_of(jnp.clip(q0 - WINDOW, 0, n - span), WINDOW)
        dist_ref[rr] = jnp.abs((q0 - start) + lax.broadcasted_iota(jnp.int32, (qb, span), 0)
                               - lax.broadcasted_iota(jnp.int32, (qb, span), 1))
        band = lambda r0, rows, rr=rr: dist_ref[rr, r0:r0 + rows, :] <= WINDOW
        rows = slice(rr * qb, (rr + 1) * qb)
        for g in range(N_KV_HEADS):
            buf = len(units) % 2
            cols = slice(g * LANES, (g + 1) * LANES)
            qa, qbc = _pair_cols(g)

            def scores(rows=rows, start=start, cols=cols, qa=qa, qbc=qbc, buf=buf):
                q_stack = jnp.concatenate([q_ref[0, rows, qa], q_ref[0, rows, qbc]], axis=0)
                _group_scores(q_stack, [kd_ref[0, pl.ds(start, span), cols], ckv_ref[0, :, cols]],
                              s_ref.at[buf])

            def finish(rows=rows, start=start, g=g, cols=cols, qa=qa, qbc=qbc, buf=buf, band=band):
                sinks = [sink_ref[GROUP * g + h] * LOG2_E for h in range(GROUP)]
                o = _group_softmax_pv([vd_ref[0, pl.ds(start, span), cols], ckv_ref[1, :, cols]],
                                      sinks, band, s_ref.at[buf], p_ref.at[buf], sk_ref.at[buf],
                                      qb, LAT_SOFTMAX_ROWS)
                o_ref[0, rows, qa] = o[0:qb].astype(BF16)
                o_ref[0, rows, qbc] = o[qb:2 * qb].astype(BF16)

            units.append((scores, finish))
    _run_units(units)


def _attn_lat(sinks, q3, kd3, vd3, ck3, cv3):
    b, n, _ = q3.shape
    past = ck3.shape[1]
    tq = Q_TILE
    keys = 3 * WINDOW + past
    qblk = pl.BlockSpec((1, tq, ATTN_WIDTH), lambda i, t: (i, t, 0))
    seq = lambda rows, w: pl.BlockSpec((1, rows, w), lambda i, t: (i, 0, 0))
    return pl.pallas_call(
        functools.partial(_attn_lat_kernel, n=n, tq=tq),
        grid=(b, n // tq),
        in_specs=[pl.BlockSpec(memory_space=pltpu.SMEM), qblk,
                  seq(n, 2 * KV_WIDTH), seq(n, 2 * KV_WIDTH),
                  seq(past, KV_WIDTH), seq(past, KV_WIDTH)],
        out_specs=qblk,
        out_shape=pltpu.HBM(q3.shape, BF16),
        scratch_shapes=[pltpu.VMEM((2, 2, 2 * WINDOW, keys), F32),
                        pltpu.VMEM((2, 2, 2 * WINDOW, keys), BF16),
                        pltpu.VMEM((2, 2, 2 * WINDOW, LANES), F32),
                        pltpu.VMEM((tq // WINDOW, WINDOW, 3 * WINDOW), jnp.int32),
                        pltpu.VMEM((2, past, 2 * KV_WIDTH), BF16)],
        compiler_params=pltpu.CompilerParams(
            dimension_semantics=("arbitrary", "arbitrary"),
            vmem_limit_bytes=_vmem_limit(32 * 2 ** 20)),
        name="attn_lat",
    )(sinks, _hbm(q3), _hbm(kd3), _hbm(vd3), _hbm(ck3), _hbm(cv3))


def _tail_kernel(x_ref, mod_ref, mix_ref, att_ref, gate_ref, g2_ref,
                 wf_ref, wao_ref, wout_ref, wup_ref, wdn_ref, o_ref):
    mod = mod_ref[0]
    gt1, sh2, sc2, gt2 = mod[2:3], mod[3:4], mod[4:5], mod[5:6]

    yf = _dot(mix_ref[...], wf_ref[...])
    ya = _dot(att_ref[...], wao_ref[...])
    merged = (gate_ref[:, 0:D_MODEL].astype(F32) * yf
              + gate_ref[:, D_MODEL:2 * D_MODEL].astype(F32) * ya)
    x1 = x_ref[...] + gt1 * _dot(merged.astype(BF16), wout_ref[...])

    h = x1 * lax.rsqrt(jnp.mean(x1 * x1, axis=-1, keepdims=True) + EPS) * g2_ref[...]
    hb = (h * (1.0 + sc2) + sh2).astype(BF16)
    d_ff = wdn_ref.shape[0]
    acc = None
    for c in range(d_ff // FF_CHUNK):
        a = _dot(hb, wup_ref[:, c * FF_CHUNK:(c + 1) * FF_CHUNK])
        u = _dot(hb, wup_ref[:, d_ff + c * FF_CHUNK:d_ff + (c + 1) * FF_CHUNK])
        act = (a * _sigmoid(a) * u).astype(BF16)
        d = _dot(act, wdn_ref[c * FF_CHUNK:(c + 1) * FF_CHUNK, :])
        acc = d if acc is None else acc + d
    o_ref[...] = x1 + gt2 * acc


def _tail(x, mod3, mix, att, gate, g2, wf, wao, wout, wup, wdn, mod_row_of_tile):
    t = x.shape[0]
    tm = TAIL_TOKEN_TILE
    row = lambda i: (i, 0)
    return pl.pallas_call(
        _tail_kernel,
        grid=(t // tm,),
        in_specs=[pl.BlockSpec((tm, D_MODEL), row),
                  pl.BlockSpec((1, 6, D_MODEL), lambda i: (mod_row_of_tile(i, tm), 0, 0)),
                  pl.BlockSpec((tm, F_WIDTH), row),
                  pl.BlockSpec((tm, ATTN_WIDTH), row),
                  pl.BlockSpec((tm, 2 * D_MODEL), row),
                  _resident((1, D_MODEL)),
                  _resident(wf.shape), _resident(wao.shape), _resident(wout.shape),
                  _resident(wup.shape), _resident(wdn.shape)],
        out_specs=pl.BlockSpec((tm, D_MODEL), row),
        out_shape=pltpu.HBM((t, D_MODEL), F32),
        compiler_params=pltpu.CompilerParams(
            dimension_semantics=("arbitrary",), vmem_limit_bytes=_vmem_limit(56 * 2 ** 20)),
        name="tail",
    )(*[_hbm(a) for a in (x, mod3, mix, att, gate, g2, wf, wao, wout, wup, wdn)])


def _rope_tables(n):
    rows = n // GRID_W
    row = jnp.repeat(jnp.arange(rows, dtype=F32), GRID_W)
    col = jnp.tile(jnp.arange(GRID_W, dtype=F32), rows)
    axis_dim = HEAD_DIM // 2
    inv_freq = ROPE_THETA ** (-jnp.arange(0, axis_dim, 2, dtype=F32) / axis_dim)

    def axis_tabs(pos):
        ang = pos[:, None] * inv_freq[None, :]
        cos, sin = jnp.cos(ang), jnp.sin(ang)
        return jnp.concatenate([cos, cos], axis=-1), jnp.concatenate([-sin, sin], axis=-1)

    cr, sr = axis_tabs(row)
    cc, sc = axis_tabs(col)
    cos = jnp.concatenate([cr, cc], axis=-1)
    sin = jnp.concatenate([sr, sc], axis=-1)
    return jnp.tile(cos, (1, LANES // HEAD_DIM)), jnp.tile(sin, (1, LANES // HEAD_DIM))


@jax.jit
def _layer(xp, xs, ck, cv, c, c_ctx, w_ada, b_ada, g_norm1, g_norm2, w_in, g_q, g_k, sink,
           w_f, w_ao, w_out, w_up, w_down):
    bp, sp, _ = xp.shape
    bs, ss, _ = xs.shape
    past = ck.shape[1]

    cvec = jnp.concatenate(
        [c, c_ctx[None, :], jnp.zeros((MOD_ROWS - bs - 1, D_MODEL), F32)], axis=0)
    mod3 = _ada(cvec, w_ada, b_ada[None, :]).reshape(MOD_ROWS, 6, D_MODEL)

    w_in_b = w_in.astype(BF16)
    g1, g2 = g_norm1[None, :], g_norm2[None, :]
    gq2 = jnp.tile(g_q, LANES // HEAD_DIM)[None, :]
    gk2 = jnp.tile(g_k, LANES // HEAD_DIM)[None, :]

    ctx_row = lambda i, tm: CTX_MOD_ROW
    lat_row = lambda i, tm: i // (ss // tm)

    xp2 = xp.reshape(bp * sp, D_MODEL)
    zf, q, kd, vd, gate, k_new, v_new, wf, wao, wout, wup, wdn = _proj(
        xp2, mod3, g1, w_in_b, gq2, gk2, None, ctx_row, sp, [w_f, w_ao, w_out, w_up, w_down])
    mix = _fourier(zf.reshape(bp, sp, F_WIDTH), CTX_SEQS_PER_STEP)
    att = _attn_ctx(sink, q.reshape(bp, sp, ATTN_WIDTH), kd.reshape(bp, sp, 2 * KV_WIDTH),
                    vd.reshape(bp, sp, 2 * KV_WIDTH))
    yp = _tail(xp2, mod3, mix.reshape(bp * sp, F_WIDTH), att.reshape(bp * sp, ATTN_WIDTH), gate,
               g2, wf, wao, wout, wup, wdn, ctx_row)

    xs2 = xs.reshape(bs * ss, D_MODEL)
    zf, q, kd, vd, gate = _proj(xs2, mod3, g1, w_in_b, gq2, gk2, _rope_tables(ss), lat_row, 0, [])
    mix = _fourier(zf.reshape(bs, ss, F_WIDTH), 1)
    att = _attn_lat(sink, q.reshape(bs, ss, ATTN_WIDTH), kd.reshape(bs, ss, 2 * KV_WIDTH),
                    vd.reshape(bs, ss, 2 * KV_WIDTH), ck.reshape(bs, past, KV_WIDTH),
                    cv.reshape(bs, past, KV_WIDTH))
    ys = _tail(xs2, mod3, mix.reshape(bs * ss, F_WIDTH), att.reshape(bs * ss, ATTN_WIDTH), gate,
               g2, wf, wao, wout, wup, wdn, lat_row)

    def cache_layout(t):
        return t.reshape(bp, N_KV_HEADS, HEAD_DIM, sp).transpose(0, 3, 1, 2)

    return yp.reshape(xp.shape), ys.reshape(xs.shape), cache_layout(k_new), cache_layout(v_new)


def kernel(x_prompt, x_sample, cache_k, cache_v, c, c_ctx, w_ada, b_ada, g_norm1, g_norm2,
           w_in, g_q, g_k, sinks, w_f, w_ao, w_out, w_up, w_down):
    depth = w_in.shape[0]
    xp, xs = x_prompt, x_sample
    new_k, new_v = [], []
    for l in range(depth):
        xp, xs, k_ctx, v_ctx = _layer(
            xp, xs, cache_k[:, l], cache_v[:, l], c, c_ctx, w_ada[l], b_ada[l], g_norm1[l],
            g_norm2[l], w_in[l], g_q[l], g_k[l], sinks[l], w_f[l], w_ao[l], w_out[l], w_up[l],
            w_down[l])
        new_k.append(k_ctx)
        new_v.append(v_ctx)
    return (xp, xs, jnp.stack(new_k, axis=1), jnp.stack(new_v, axis=1))
```

```python
import functools

import numpy as np
import jax
import jax.numpy as jnp
from jax import lax
from jax.experimental import pallas as pl
from jax.experimental.pallas import tpu as pltpu

D_MODEL = 1024
GRID_W = 64
N_HEADS = 8
N_KV_HEADS = 2
GROUP = N_HEADS // N_KV_HEADS
HEAD_DIM = 64
WINDOW = 128
F_GROUPS = 4
F_GROUP_DIM = 128
F_WIDTH = F_GROUPS * F_GROUP_DIM
ATTN_WIDTH = N_HEADS * HEAD_DIM
KV_WIDTH = N_KV_HEADS * HEAD_DIM
ROPE_THETA = 10000.0
EPS = 1e-6
NEG_INF = -1e30
LOG2_E = 1.4426950408889634

LANES = 128
MXU_DIM = 256
VMEM_BYTES_V7X = 64 * 1024 * 1024

_Q0 = F_WIDTH
_K0 = _Q0 + ATTN_WIDTH
_V0 = _K0 + KV_WIDTH
_G0 = _V0 + KV_WIDTH
IN_WIDTH = _G0 + 2 * D_MODEL

SUBLANES = 8
N_MOD = 6
MOD_ROWS = 16
CTX_MOD_ROW = 8

PROJ_TOKEN_TILE = 1024
TAIL_TOKEN_TILE = 512
FF_CHUNK = MXU_DIM
Q_TILE = 512
CTX_SEQS_PER_STEP = 4
CTX_SOFTMAX_ROWS = 64
LAT_SOFTMAX_ROWS = 32

F32 = jnp.float32
BF16 = jnp.bfloat16


def _vmem_limit(nbytes):
    return int(min(nbytes, VMEM_BYTES_V7X - 4 * 1024 * 1024))


def _dot(a, b):
    return jnp.dot(a, b, preferred_element_type=F32)


def _dot_nt(a, b):
    return lax.dot_general(a, b, (((1,), (1,)), ((), ())), preferred_element_type=F32)


def _sigmoid(x):
    return 1.0 / (1.0 + jnp.exp(-x))


def _resident(shape):
    zeros = (0,) * len(shape)
    return pl.BlockSpec(shape, lambda *_: zeros, pipeline_mode=pl.Buffered(1))


def _hbm(x):
    return pltpu.with_memory_space_constraint(x, pltpu.HBM)


def _run_units(units):
    units[0][0]()
    for k, (_, second) in enumerate(units):
        if k + 1 < len(units):
            units[k + 1][0]()
        second()


def _ada_kernel(c_ref, w_ref, b_ref, o_ref):
    c = c_ref[...]
    s = c * _sigmoid(c)
    o_ref[0] = _dot(s.astype(BF16), w_ref[...].astype(BF16)) + b_ref[...]


def _ada(cvec, w_ada, b_ada):
    return pl.pallas_call(
        _ada_kernel,
        grid=(N_MOD,),
        in_specs=[pl.BlockSpec((MOD_ROWS, D_MODEL), lambda j: (0, 0)),
                  pl.BlockSpec((D_MODEL, D_MODEL), lambda j: (0, j)),
                  pl.BlockSpec((1, D_MODEL), lambda j: (0, j))],
        out_specs=pl.BlockSpec((1, MOD_ROWS, D_MODEL), lambda j: (j, 0, 0)),
        out_shape=pltpu.HBM((N_MOD, MOD_ROWS, D_MODEL), F32),
        name="ada",
    )(cvec, w_ada, b_ada)


def _mod_spec(mod_row):
    return pl.BlockSpec((N_MOD, SUBLANES, D_MODEL), lambda i: (0, mod_row // SUBLANES, 0))


def _mod_vectors(mod_ref, tiles_per_row):
    if tiles_per_row is None:
        return [mod_ref[j, 0:1, :] for j in range(N_MOD)]
    r = pl.program_id(0) // tiles_per_row
    return [mod_ref[j, pl.ds(r, 1), :] for j in range(N_MOD)]


def _head_norm(z, g):
    lo = lax.broadcasted_iota(jnp.int32, z.shape, 1) < HEAD_DIM
    s = z * z
    s_lo = jnp.sum(jnp.where(lo, s, 0.0), axis=-1, keepdims=True)
    s_hi = jnp.sum(jnp.where(lo, 0.0, s), axis=-1, keepdims=True)
    ms = jnp.where(lo, s_lo, s_hi) * (1.0 / HEAD_DIM)
    return z * lax.rsqrt(ms + EPS) * g


def _rope(y, cos, sin):
    lane = lax.broadcasted_iota(jnp.int32, y.shape, 1)
    first = (lane % 32) < 16
    partner = jnp.where(first, pltpu.roll(y, LANES - 16, 1), pltpu.roll(y, 16, 1))
    return y * cos + partner * sin


def _dup_heads(y):
    lo = lax.broadcasted_iota(jnp.int32, y.shape, 1) < HEAD_DIM
    sw = pltpu.roll(y, HEAD_DIM, 1)
    return jnp.concatenate([jnp.where(lo, y, sw), jnp.where(lo, sw, y)], axis=-1)


def _proj_kernel(*refs, rope, cache_seq, n_cast, tiles_per_mod_row):
    x_ref, mod_ref, g1_ref, w_ref, gq_ref, gk_ref = refs[:6]
    refs = refs[6:]
    if rope:
        cos_ref, sin_ref = refs[:2]
        refs = refs[2:]
    cast_in, refs = refs[:n_cast], refs[n_cast:]
    zf_ref, q_ref, kd_ref, vd_ref, gate_ref = refs[:5]
    refs = refs[5:]
    if cache_seq:
        kc_ref, vc_ref = refs[:2]
        refs = refs[2:]
    for src_ref, dst_ref in zip(cast_in, refs):
        dst_ref[...] = src_ref[...].astype(BF16)

    x = x_ref[...]
    sh1, sc1 = _mod_vectors(mod_ref, tiles_per_mod_row)[0:2]
    h = x * lax.rsqrt(jnp.mean(x * x, axis=-1, keepdims=True) + EPS) * g1_ref[...]
    hb = (h * (1.0 + sc1) + sh1).astype(BF16)

    zq = _dot(hb, w_ref[:, _Q0:_K0])
    for j in range(ATTN_WIDTH // LANES):
        y = _head_norm(zq[:, j * LANES:(j + 1) * LANES], gq_ref[...])
        if rope:
            y = _rope(y, cos_ref[...], sin_ref[...])
        q_ref[:, j * LANES:(j + 1) * LANES] = (y * (HEAD_DIM ** -0.5 * LOG2_E)).astype(BF16)

    zkv = _dot(hb, w_ref[:, _K0:_G0])
    k = _head_norm(zkv[:, 0:KV_WIDTH], gk_ref[...])
    v = zkv[:, KV_WIDTH:2 * KV_WIDTH]
    if cache_seq:
        for s in range(x.shape[0] // cache_seq):
            rows = slice(s * cache_seq, (s + 1) * cache_seq)
            kc_ref[s] = k[rows, :].T
            vc_ref[s] = v[rows, :].T
    if rope:
        k = _rope(k, cos_ref[...], sin_ref[...])
    kd_ref[...] = _dup_heads(k).astype(BF16)
    vd_ref[...] = _dup_heads(v).astype(BF16)

    gc = 512
    for j in range(2 * D_MODEL // gc):
        zg = _dot(hb, w_ref[:, _G0 + j * gc:_G0 + (j + 1) * gc])
        gate_ref[:, j * gc:(j + 1) * gc] = _sigmoid(zg).astype(BF16)

    zf_ref[...] = _dot(hb, w_ref[:, 0:_Q0]).astype(BF16)


def _proj(x, mod3, g1, w_in_b, gq2, gk2, rope_tabs, mod_row, rows_per_mod_row, cache_seq,
          cast_weights):
    t = x.shape[0]
    tm = PROJ_TOKEN_TILE
    steps = t // tm
    rope = rope_tabs is not None
    row = lambda i: (i, 0)
    tiles_per_mod_row = None if rows_per_mod_row is None else rows_per_mod_row // tm
    in_specs = [pl.BlockSpec((tm, D_MODEL), row),
                _mod_spec(mod_row),
                _resident((1, D_MODEL)),
                _resident((D_MODEL, IN_WIDTH)),
                _resident((1, LANES)),
                _resident((1, LANES))]
    args = [x, mod3, g1, w_in_b, gq2, gk2]
    if rope:
        tiles_per_seq = rope_tabs[0].shape[0] // tm
        tab = pl.BlockSpec((tm, LANES), lambda i: (i % tiles_per_seq, 0))
        in_specs += [tab, tab]
        args += list(rope_tabs)
    slabs = [pl.BlockSpec((w.shape[0] // steps, w.shape[1]), row) for w in cast_weights]
    in_specs += slabs
    args += list(cast_weights)
    out_specs = [pl.BlockSpec((tm, F_WIDTH), row),
                 pl.BlockSpec((tm, ATTN_WIDTH), row),
                 pl.BlockSpec((tm, 2 * KV_WIDTH), row),
                 pl.BlockSpec((tm, 2 * KV_WIDTH), row),
                 pl.BlockSpec((tm, 2 * D_MODEL), row)]
    out_shape = [pltpu.HBM((t, F_WIDTH), BF16),
                 pltpu.HBM((t, ATTN_WIDTH), BF16),
                 pltpu.HBM((t, 2 * KV_WIDTH), BF16),
                 pltpu.HBM((t, 2 * KV_WIDTH), BF16),
                 pltpu.HBM((t, 2 * D_MODEL), BF16)]
    if cache_seq:
        seqs = tm // cache_seq
        out_specs += [pl.BlockSpec((seqs, KV_WIDTH, cache_seq), lambda i: (i, 0, 0))] * 2
        out_shape += [pltpu.HBM((t // cache_seq, KV_WIDTH, cache_seq), F32)] * 2
    out_specs += slabs
    out_shape += [pltpu.HBM(w.shape, BF16) for w in cast_weights]
    return pl.pallas_call(
        functools.partial(_proj_kernel, rope=rope, cache_seq=cache_seq, n_cast=len(cast_weights),
                          tiles_per_mod_row=tiles_per_mod_row),
        grid=(steps,),
        in_specs=in_specs,
        out_specs=out_specs,
        out_shape=out_shape,
        compiler_params=pltpu.CompilerParams(
            dimension_semantics=("arbitrary",), vmem_limit_bytes=_vmem_limit(58 * 2 ** 20)),
        name="proj_lat" if rope else "proj_ctx",
    )(*[_hbm(a) for a in args])


def _dft_tables(n):
    def cs(m):
        idx = np.arange(m, dtype=np.int64)
        ang = 2.0 * np.pi * ((idx[:, None] * idx[None, :]) % m).astype(np.float64) / m
        return np.cos(ang) / np.sqrt(m), np.sin(ang) / np.sqrt(m)
    cd, sd = cs(F_GROUP_DIM)
    cn, sn = cs(n)
    csd = np.concatenate([cd, sd], axis=1).astype(np.float32)
    csn = np.concatenate([cn, -sn], axis=1).astype(np.float32)
    return jnp.asarray(csd).astype(BF16), jnp.asarray(csn).astype(BF16)


def _fourier_kernel(zf_ref, csd_ref, csn_ref, o_ref, ab_ref, *, seqs, n):
    units = []
    for s in range(seqs):
        buf = s % 2

        def channels(s=s, buf=buf):
            for g in range(F_GROUPS):
                cols = slice(g * F_GROUP_DIM, (g + 1) * F_GROUP_DIM)
                ab = _dot(zf_ref[s, :, cols], csd_ref[...])
                ab_ref[buf, 0:n, cols] = ab[:, 0:F_GROUP_DIM].astype(BF16)
                ab_ref[buf, n:2 * n, cols] = ab[:, F_GROUP_DIM:].astype(BF16)

        def positions(s=s, buf=buf):
            o_ref[s] = _dot(csn_ref[...], ab_ref[buf]).astype(BF16)

        units.append((channels, positions))
    _run_units(units)


def _fourier(zf3, seqs):
    b, n, _ = zf3.shape
    csd, csn = _dft_tables(n)
    blk = pl.BlockSpec((seqs, n, F_WIDTH), lambda i: (i, 0, 0))
    return pl.pallas_call(
        functools.partial(_fourier_kernel, seqs=seqs, n=n),
        grid=(b // seqs,),
        in_specs=[blk, _resident(csd.shape), _resident(csn.shape)],
        out_specs=blk,
        out_shape=pltpu.HBM(zf3.shape, BF16),
        scratch_shapes=[pltpu.VMEM((2, 2 * n, F_WIDTH), BF16)],
        compiler_params=pltpu.CompilerParams(
            dimension_semantics=("arbitrary",), vmem_limit_bytes=_vmem_limit(32 * 2 ** 20)),
        name=f"fourier_{n}",
    )(_hbm(zf3), csd, csn)


def _split_lo_hi(xd, fill):
    lo = lax.broadcasted_iota(jnp.int32, xd.shape, 1) < HEAD_DIM
    other = jnp.full_like(xd, fill)
    return jnp.where(lo, xd, other), jnp.where(lo, other, xd)


def _group_scores(q_stack, keys, s_ref):
    c0 = 0
    for kd in keys:
        k_lo, k_hi = _split_lo_hi(kd, 0.0)
        c1 = c0 + kd.shape[0]
        s_ref[0, :, c0:c1] = _dot_nt(q_stack, k_lo)
        s_ref[1, :, c0:c1] = _dot_nt(q_stack, k_hi)
        c0 = c1


def _group_softmax_pv(values, sinks, band, s_ref, p_ref, sk_ref, rows_per_pair, chunk):
    rows = s_ref.shape[1]
    bounds = [0]
    for vd in values:
        bounds.append(bounds[-1] + vd.shape[0])
    sources = values

    for half in range(2):
        for r0 in range(0, rows, chunk):
            rs = slice(r0, r0 + chunk)
            sink = sinks[2 * (r0 // rows_per_pair) + half]
            parts = [s_ref[half, rs, bounds[i]:bounds[i + 1]] for i in range(len(sources))]
            if band is not None:
                parts[0] = jnp.where(band(r0 % rows_per_pair, chunk), parts[0], NEG_INF)
            m = sink
            for s in parts:
                m = jnp.maximum(m, jnp.max(s, axis=-1, keepdims=True))
            for i, s in enumerate(parts):
                p_ref[half, rs, bounds[i]:bounds[i + 1]] = jnp.exp2((s - m).astype(BF16))
            sk_ref[half, rs, :] = jnp.broadcast_to(jnp.exp2(sink - m), (chunk, LANES))

    o_e = o_o = None
    for i, vd in enumerate(values):
        v_lo, v_hi = _split_lo_hi(vd, 1.0)
        pe = _dot(p_ref[0, :, bounds[i]:bounds[i + 1]], v_lo)
        po = _dot(p_ref[1, :, bounds[i]:bounds[i + 1]], v_hi)
        o_e = pe if o_e is None else o_e + pe
        o_o = po if o_o is None else o_o + po
    lo = lax.broadcasted_iota(jnp.int32, (rows, LANES), 1) < HEAD_DIM
    num = jnp.where(lo, o_e, o_o)
    den = pltpu.roll(jnp.where(lo, o_o, o_e), HEAD_DIM, 1) + jnp.where(lo, sk_ref[0], sk_ref[1])
    return num / den


def _pair_cols(g):
    pa, pb = 2 * g, 2 * g + 1
    return slice(pa * LANES, (pa + 1) * LANES), slice(pb * LANES, (pb + 1) * LANES)


def _attn_ctx_kernel(sink_ref, q_ref, kd_ref, vd_ref, o_ref, s_ref, p_ref, sk_ref, *, seqs):
    n = q_ref.shape[1]
    units = []
    for s in range(seqs):
        for g in range(N_KV_HEADS):
            buf = len(units) % 2
            cols = slice(g * LANES, (g + 1) * LANES)
            qa, qb = _pair_cols(g)

            def scores(s=s, cols=cols, qa=qa, qb=qb, buf=buf):
                q_stack = jnp.concatenate([q_ref[s, :, qa], q_ref[s, :, qb]], axis=0)
                _group_scores(q_stack, [kd_ref[s, :, cols]], s_ref.at[buf])

            def finish(s=s, g=g, cols=cols, qa=qa, qb=qb, buf=buf):
                sinks = [sink_ref[GROUP * g + h] * LOG2_E for h in range(GROUP)]
                o = _group_softmax_pv([vd_ref[s, :, cols]], sinks, None, s_ref.at[buf],
                                      p_ref.at[buf], sk_ref.at[buf], n, CTX_SOFTMAX_ROWS)
                o_ref[s, :, qa] = o[0:n].astype(BF16)
                o_ref[s, :, qb] = o[n:2 * n].astype(BF16)

            units.append((scores, finish))
    _run_units(units)


def _attn_ctx(sinks, q3, kd3, vd3):
    b, n, _ = q3.shape
    seqs = CTX_SEQS_PER_STEP
    blk = lambda w: pl.BlockSpec((seqs, n, w), lambda i: (i, 0, 0))
    return pl.pallas_call(
        functools.partial(_attn_ctx_kernel, seqs=seqs),
        grid=(b // seqs,),
        in_specs=[pl.BlockSpec(memory_space=pltpu.SMEM),
                  blk(ATTN_WIDTH), blk(2 * KV_WIDTH), blk(2 * KV_WIDTH)],
        out_specs=blk(ATTN_WIDTH),
        out_shape=pltpu.HBM(q3.shape, BF16),
        scratch_shapes=[pltpu.VMEM((2, 2, 2 * n, n), F32),
                        pltpu.VMEM((2, 2, 2 * n, n), BF16),
                        pltpu.VMEM((2, 2, 2 * n, LANES), F32)],
        compiler_params=pltpu.CompilerParams(
            dimension_semantics=("arbitrary",), vmem_limit_bytes=_vmem_limit(32 * 2 ** 20)),
        name="attn_ctx",
    )(sinks, _hbm(q3), _hbm(kd3), _hbm(vd3))


def _attn_lat_kernel(sink_ref, q_ref, kd_ref, vd_ref, ck_ref, cv_ref, o_ref,
                     s_ref, p_ref, sk_ref, dist_ref, ckv_ref, *, n, tq):
    t = pl.program_id(1)
    qb = WINDOW
    span = qb + 2 * WINDOW

    ckv_ref[0] = _dup_heads(ck_ref[0].T).astype(BF16)
    ckv_ref[1] = _dup_heads(cv_ref[0].T).astype(BF16)
    units = []
    for rr in range(tq // qb):
        q0 = t * tq + rr * qb
        start = pl.multiple_of(jnp.clip(q0 - WINDOW, 0, n - span), WINDOW)
        dist_ref[rr] = jnp.abs((q0 - start) + lax.broadcasted_iota(jnp.int32, (qb, span), 0)
                               - lax.broadcasted_iota(jnp.int32, (qb, span), 1))
        band = lambda r0, rows, rr=rr: dist_ref[rr, r0:r0 + rows, :] <= WINDOW
        rows = slice(rr * qb, (rr + 1) * qb)
        for g in range(N_KV_HEADS):
            buf = len(units) % 2
            cols = slice(g * LANES, (g + 1) * LANES)
            qa, qbc = _pair_cols(g)

            def scores(rows=rows, start=start, cols=cols, qa=qa, qbc=qbc, buf=buf):
                q_stack = jnp.concatenate([q_ref[0, rows, qa], q_ref[0, rows, qbc]], axis=0)
                _group_scores(q_stack, [kd_ref[0, pl.ds(start, span), cols], ckv_ref[0, :, cols]],
                              s_ref.at[buf])

            def finish(rows=rows, start=start, g=g, cols=cols, qa=qa, qbc=qbc, buf=buf, band=band):
                sinks = [sink_ref[GROUP * g + h] * LOG2_E for h in range(GROUP)]
                o = _group_softmax_pv([vd_ref[0, pl.ds(start, span), cols], ckv_ref[1, :, cols]],
                                      sinks, band, s_ref.at[buf], p_ref.at[buf], sk_ref.at[buf],
                                      qb, LAT_SOFTMAX_ROWS)
                o_ref[0, rows, qa] = o[0:qb].astype(BF16)
                o_ref[0, rows, qbc] = o[qb:2 * qb].astype(BF16)

            units.append((scores, finish))
    _run_units(units)


def _attn_lat(sinks, q3, kd3, vd3, ck3, cv3):
    b, n, _ = q3.shape
    past = ck3.shape[2]
    tq = Q_TILE
    keys = 3 * WINDOW + past
    qblk = pl.BlockSpec((1, tq, ATTN_WIDTH), lambda i, t: (i, t, 0))
    seq = lambda rows, w: pl.BlockSpec((1, rows, w), lambda i, t: (i, 0, 0))
    return pl.pallas_call(
        functools.partial(_attn_lat_kernel, n=n, tq=tq),
        grid=(b, n // tq),
        in_specs=[pl.BlockSpec(memory_space=pltpu.SMEM), qblk,
                  seq(n, 2 * KV_WIDTH), seq(n, 2 * KV_WIDTH),
                  seq(KV_WIDTH, past), seq(KV_WIDTH, past)],
        out_specs=qblk,
        out_shape=pltpu.HBM(q3.shape, BF16),
        scratch_shapes=[pltpu.VMEM((2, 2, 2 * WINDOW, keys), F32),
                        pltpu.VMEM((2, 2, 2 * WINDOW, keys), BF16),
                        pltpu.VMEM((2, 2, 2 * WINDOW, LANES), F32),
                        pltpu.VMEM((tq // WINDOW, WINDOW, 3 * WINDOW), jnp.int32),
                        pltpu.VMEM((2, past, 2 * KV_WIDTH), BF16)],
        compiler_params=pltpu.CompilerParams(
            dimension_semantics=("arbitrary", "arbitrary"),
            vmem_limit_bytes=_vmem_limit(32 * 2 ** 20)),
        name="attn_lat",
    )(sinks, _hbm(q3), _hbm(kd3), _hbm(vd3), _hbm(ck3), _hbm(cv3))


def _tail_kernel(x_ref, mod_ref, mix_ref, att_ref, gate_ref, g2_ref,
                 wf_ref, wao_ref, wout_ref, wup_ref, wdn_ref, o_ref, *, tiles_per_mod_row):
    gt1, sh2, sc2, gt2 = _mod_vectors(mod_ref, tiles_per_mod_row)[2:6]

    yf = _dot(mix_ref[...], wf_ref[...])
    ya = _dot(att_ref[...], wao_ref[...])
    merged = (gate_ref[:, 0:D_MODEL].astype(F32) * yf
              + gate_ref[:, D_MODEL:2 * D_MODEL].astype(F32) * ya)
    x1 = x_ref[...] + gt1 * _dot(merged.astype(BF16), wout_ref[...])

    h = x1 * lax.rsqrt(jnp.mean(x1 * x1, axis=-1, keepdims=True) + EPS) * g2_ref[...]
    hb = (h * (1.0 + sc2) + sh2).astype(BF16)
    d_ff = wdn_ref.shape[0]
    acc = None
    for c in range(d_ff // FF_CHUNK):
        a = _dot(hb, wup_ref[:, c * FF_CHUNK:(c + 1) * FF_CHUNK])
        u = _dot(hb, wup_ref[:, d_ff + c * FF_CHUNK:d_ff + (c + 1) * FF_CHUNK])
        act = (a * _sigmoid(a) * u).astype(BF16)
        d = _dot(act, wdn_ref[c * FF_CHUNK:(c + 1) * FF_CHUNK, :])
        acc = d if acc is None else acc + d
    o_ref[...] = x1 + gt2 * acc


def _tail(x, mod3, mix, att, gate, g2, wf, wao, wout, wup, wdn, mod_row, rows_per_mod_row):
    t = x.shape[0]
    tm = TAIL_TOKEN_TILE
    row = lambda i: (i, 0)
    tiles_per_mod_row = None if rows_per_mod_row is None else rows_per_mod_row // tm
    return pl.pallas_call(
        functools.partial(_tail_kernel, tiles_per_mod_row=tiles_per_mod_row),
        grid=(t // tm,),
        in_specs=[pl.BlockSpec((tm, D_MODEL), row),
                  _mod_spec(mod_row),
                  pl.BlockSpec((tm, F_WIDTH), row),
                  pl.BlockSpec((tm, ATTN_WIDTH), row),
                  pl.BlockSpec((tm, 2 * D_MODEL), row),
                  _resident((1, D_MODEL)),
                  _resident(wf.shape), _resident(wao.shape), _resident(wout.shape),
                  _resident(wup.shape), _resident(wdn.shape)],
        out_specs=pl.BlockSpec((tm, D_MODEL), row),
        out_shape=pltpu.HBM((t, D_MODEL), F32),
        compiler_params=pltpu.CompilerParams(
            dimension_semantics=("arbitrary",), vmem_limit_bytes=_vmem_limit(56 * 2 ** 20)),
        name="tail",
    )(*[_hbm(a) for a in (x, mod3, mix, att, gate, g2, wf, wao, wout, wup, wdn)])


def _rope_tables(n):
    rows = n // GRID_W
    row = np.repeat(np.arange(rows, dtype=np.float64), GRID_W)
    col = np.tile(np.arange(GRID_W, dtype=np.float64), rows)
    axis_dim = HEAD_DIM // 2
    inv_freq = ROPE_THETA ** (-np.arange(0, axis_dim, 2, dtype=np.float64) / axis_dim)

    def axis_tabs(pos):
        ang = pos[:, None] * inv_freq[None, :]
        cos, sin = np.cos(ang), np.sin(ang)
        return np.concatenate([cos, cos], axis=-1), np.concatenate([-sin, sin], axis=-1)

    cr, sr = axis_tabs(row)
    cc, sc = axis_tabs(col)
    cos = np.tile(np.concatenate([cr, cc], axis=-1), (1, LANES // HEAD_DIM))
    sin = np.tile(np.concatenate([sr, sc], axis=-1), (1, LANES // HEAD_DIM))
    return jnp.asarray(cos.astype(np.float32)), jnp.asarray(sin.astype(np.float32))


@jax.jit
def _layer(xp, xs, ck, cv, c, c_ctx, w_ada, b_ada, g_norm1, g_norm2, w_in, g_q, g_k, sink,
           w_f, w_ao, w_out, w_up, w_down):
    bp, sp, _ = xp.shape
    bs, ss, _ = xs.shape
    past = ck.shape[1]

    assert bs <= CTX_MOD_ROW, "latent conditioning rows must fit below the context row"
    cvec = jnp.concatenate(
        [c, jnp.zeros((CTX_MOD_ROW - bs, D_MODEL), F32), c_ctx[None, :],
         jnp.zeros((MOD_ROWS - CTX_MOD_ROW - 1, D_MODEL), F32)], axis=0)
    mod3 = _ada(cvec, w_ada, b_ada[None, :])

    w_in_b = w_in.astype(BF16)
    g1, g2 = g_norm1[None, :], g_norm2[None, :]
    gq2 = jnp.tile(g_q, LANES // HEAD_DIM)[None, :]
    gk2 = jnp.tile(g_k, LANES // HEAD_DIM)[None, :]
    ctx_mod = (CTX_MOD_ROW, None)
    lat_mod = (0, ss)

    xp2 = xp.reshape(bp * sp, D_MODEL)
    xs2 = xs.reshape(bs * ss, D_MODEL)
    zf, q, kd, vd, gate, k_new, v_new, wup = _proj(
        xp2, mod3, g1, w_in_b, gq2, gk2, None, *ctx_mod, sp, [w_up])
    zf_s, q_s, kd_s, vd_s, gate_s, wf, wao, wout, wdn = _proj(
        xs2, mod3, g1, w_in_b, gq2, gk2, _rope_tables(ss), *lat_mod, 0, [w_f, w_ao, w_out, w_down])

    mix = _fourier(zf.reshape(bp, sp, F_WIDTH), CTX_SEQS_PER_STEP)
    att = _attn_ctx(sink, q.reshape(bp, sp, ATTN_WIDTH), kd.reshape(bp, sp, 2 * KV_WIDTH),
                    vd.reshape(bp, sp, 2 * KV_WIDTH))
    yp = _tail(xp2, mod3, mix.reshape(bp * sp, F_WIDTH), att.reshape(bp * sp, ATTN_WIDTH), gate,
               g2, wf, wao, wout, wup, wdn, *ctx_mod)

    def cache_t(t):
        return t.transpose(0, 2, 3, 1).reshape(bs, KV_WIDTH, past)

    zf, q, kd, vd, gate = zf_s, q_s, kd_s, vd_s, gate_s
    mix = _fourier(zf.reshape(bs, ss, F_WIDTH), 1)
    att = _attn_lat(sink, q.reshape(bs, ss, ATTN_WIDTH), kd.reshape(bs, ss, 2 * KV_WIDTH),
                    vd.reshape(bs, ss, 2 * KV_WIDTH), cache_t(ck), cache_t(cv))
    ys = _tail(xs2, mod3, mix.reshape(bs * ss, F_WIDTH), att.reshape(bs * ss, ATTN_WIDTH), gate,
               g2, wf, wao, wout, wup, wdn, *lat_mod)

    def cache_layout(t):
        return t.reshape(bp, N_KV_HEADS, HEAD_DIM, sp).transpose(0, 3, 1, 2)

    return yp.reshape(xp.shape), ys.reshape(xs.shape), cache_layout(k_new), cache_layout(v_new)


def kernel(x_prompt, x_sample, cache_k, cache_v, c, c_ctx, w_ada, b_ada, g_norm1, g_norm2,
           w_in, g_q, g_k, sinks, w_f, w_ao, w_out, w_up, w_down):
    depth = w_in.shape[0]
    xp, xs = x_prompt, x_sample
    new_k, new_v = [], []
    for l in range(depth):
        xp, xs, k_ctx, v_ctx = _layer(
            xp, xs, cache_k[:, l], cache_v[:, l], c, c_ctx, w_ada[l], b_ada[l], g_norm1[l],
            g_norm2[l], w_in[l], g_q[l], g_k[l], sinks[l], w_f[l], w_ao[l], w_out[l], w_up[l],
            w_down[l])
        new_k.append(k_ctx)
        new_v.append(v_ctx)
    return (xp, xs, jnp.stack(new_k, axis=1), jnp.stack(new_v, axis=1))
```

```python
import functools

import numpy as np
import jax
import jax.numpy as jnp
from jax import lax
from jax.experimental import pallas as pl
from jax.experimental.pallas import tpu as pltpu

D_MODEL = 1024
GRID_W = 64
N_HEADS = 8
N_KV_HEADS = 2
GROUP = N_HEADS // N_KV_HEADS
HEAD_DIM = 64
WINDOW = 128
F_GROUPS = 4
F_GROUP_DIM = 128
F_WIDTH = F_GROUPS * F_GROUP_DIM
ATTN_WIDTH = N_HEADS * HEAD_DIM
KV_WIDTH = N_KV_HEADS * HEAD_DIM
ROPE_THETA = 10000.0
EPS = 1e-6
NEG_INF = -1e30
LOG2_E = 1.4426950408889634

LANES = 128
MXU_DIM = 256
VMEM_BYTES_V7X = 64 * 1024 * 1024

_Q0 = F_WIDTH
_K0 = _Q0 + ATTN_WIDTH
_V0 = _K0 + KV_WIDTH
_G0 = _V0 + KV_WIDTH
IN_WIDTH = _G0 + 2 * D_MODEL

SUBLANES = 8
N_MOD = 6
MOD_ROWS = 16
CTX_MOD_ROW = 8

PROJ_TOKEN_TILE = 1024
TAIL_TOKEN_TILE = 512
FF_CHUNK = MXU_DIM
Q_TILE = 512
CTX_SEQS_PER_STEP = 4
CTX_SOFTMAX_ROWS = 64
LAT_SOFTMAX_ROWS = 32

F32 = jnp.float32
BF16 = jnp.bfloat16


def _vmem_limit(nbytes):
    return int(min(nbytes, VMEM_BYTES_V7X - 4 * 1024 * 1024))


def _dot(a, b):
    return jnp.dot(a, b, preferred_element_type=F32)


def _dot_nt(a, b):
    return lax.dot_general(a, b, (((1,), (1,)), ((), ())), preferred_element_type=F32)


def _sigmoid(x):
    return 1.0 / (1.0 + jnp.exp(-x))


def _resident(shape):
    zeros = (0,) * len(shape)
    return pl.BlockSpec(shape, lambda *_: zeros, pipeline_mode=pl.Buffered(1))


def _hbm(x):
    return pltpu.with_memory_space_constraint(x, pltpu.HBM)


def _run_units(units):
    units[0][0]()
    for k, (_, second) in enumerate(units):
        if k + 1 < len(units):
            units[k + 1][0]()
        second()


def _ada_kernel(c_ref, w_ref, b_ref, o_ref):
    c = c_ref[...]
    s = c * _sigmoid(c)
    o_ref[0] = _dot(s.astype(BF16), w_ref[...].astype(BF16)) + b_ref[...]


def _ada(cvec, w_ada, b_ada):
    return pl.pallas_call(
        _ada_kernel,
        grid=(N_MOD,),
        in_specs=[pl.BlockSpec((MOD_ROWS, D_MODEL), lambda j: (0, 0)),
                  pl.BlockSpec((D_MODEL, D_MODEL), lambda j: (0, j)),
                  pl.BlockSpec((1, D_MODEL), lambda j: (0, j))],
        out_specs=pl.BlockSpec((1, MOD_ROWS, D_MODEL), lambda j: (j, 0, 0)),
        out_shape=pltpu.HBM((N_MOD, MOD_ROWS, D_MODEL), F32),
        name="ada",
    )(cvec, w_ada, b_ada)


def _mod_spec(mod_row):
    return pl.BlockSpec((N_MOD, SUBLANES, D_MODEL), lambda i: (0, mod_row // SUBLANES, 0))


def _mod_vectors(mod_ref, tile, tiles_per_row):
    if tiles_per_row is None:
        return [mod_ref[j, 0:1, :] for j in range(N_MOD)]
    r = tile // tiles_per_row
    return [mod_ref[j, pl.ds(r, 1), :] for j in range(N_MOD)]


def _head_norm(z, g):
    lo = lax.broadcasted_iota(jnp.int32, z.shape, 1) < HEAD_DIM
    s = z * z
    s_lo = jnp.sum(jnp.where(lo, s, 0.0), axis=-1, keepdims=True)
    s_hi = jnp.sum(jnp.where(lo, 0.0, s), axis=-1, keepdims=True)
    ms = jnp.where(lo, s_lo, s_hi) * (1.0 / HEAD_DIM)
    return z * lax.rsqrt(ms + EPS) * g


def _rope(y, cos, sin):
    lane = lax.broadcasted_iota(jnp.int32, y.shape, 1)
    first = (lane % 32) < 16
    partner = jnp.where(first, pltpu.roll(y, LANES - 16, 1), pltpu.roll(y, 16, 1))
    return y * cos + partner * sin


def _dup_heads(y):
    lo = lax.broadcasted_iota(jnp.int32, y.shape, 1) < HEAD_DIM
    sw = pltpu.roll(y, HEAD_DIM, 1)
    return jnp.concatenate([jnp.where(lo, y, sw), jnp.where(lo, sw, y)], axis=-1)


def _proj_kernel(*refs, rope, cache_seq, n_cast, tiles_per_mod_row):
    x_ref, mod_ref, g1_ref, w_ref, gq_ref, gk_ref = refs[:6]
    refs = refs[6:]
    if rope:
        cos_ref, sin_ref = refs[:2]
        refs = refs[2:]
    cast_in, refs = refs[:n_cast], refs[n_cast:]
    zf_ref, q_ref, kd_ref, vd_ref, gate_ref = refs[:5]
    refs = refs[5:]
    if cache_seq:
        kc_ref, vc_ref = refs[:2]
        refs = refs[2:]
    for src_ref, dst_ref in zip(cast_in, refs):
        dst_ref[...] = src_ref[...].astype(BF16)

    x = x_ref[...]
    sh1, sc1 = _mod_vectors(mod_ref, pl.program_id(0), tiles_per_mod_row)[0:2]
    h = x * lax.rsqrt(jnp.mean(x * x, axis=-1, keepdims=True) + EPS) * g1_ref[...]
    hb = (h * (1.0 + sc1) + sh1).astype(BF16)

    zq = _dot(hb, w_ref[:, _Q0:_K0])
    for j in range(ATTN_WIDTH // LANES):
        y = _head_norm(zq[:, j * LANES:(j + 1) * LANES], gq_ref[...])
        if rope:
            y = _rope(y, cos_ref[...], sin_ref[...])
        q_ref[:, j * LANES:(j + 1) * LANES] = (y * (HEAD_DIM ** -0.5 * LOG2_E)).astype(BF16)

    zkv = _dot(hb, w_ref[:, _K0:_G0])
    k = _head_norm(zkv[:, 0:KV_WIDTH], gk_ref[...])
    v = zkv[:, KV_WIDTH:2 * KV_WIDTH]
    if cache_seq:
        for s in range(x.shape[0] // cache_seq):
            rows = slice(s * cache_seq, (s + 1) * cache_seq)
            kc_ref[s] = k[rows, :].T
            vc_ref[s] = v[rows, :].T
    if rope:
        k = _rope(k, cos_ref[...], sin_ref[...])
    kt = k.T
    kd_ref[...] = jnp.concatenate(
        [kt[0:HEAD_DIM], kt[0:HEAD_DIM], kt[HEAD_DIM:], kt[HEAD_DIM:]], axis=0).astype(BF16)
    vd_ref[...] = _dup_heads(v).astype(BF16)

    gc = 512
    for j in range(2 * D_MODEL // gc):
        zg = _dot(hb, w_ref[:, _G0 + j * gc:_G0 + (j + 1) * gc])
        gate_ref[:, j * gc:(j + 1) * gc] = _sigmoid(zg).astype(BF16)

    zf_ref[...] = _dot(hb, w_ref[:, 0:_Q0]).astype(BF16)


def _proj(x, mod3, g1, w_in_b, gq2, gk2, rope_tabs, mod_row, rows_per_mod_row, cache_seq,
          cast_weights):
    t = x.shape[0]
    tm = PROJ_TOKEN_TILE
    steps = t // tm
    rope = rope_tabs is not None
    row = lambda i: (i, 0)
    tiles_per_mod_row = None if rows_per_mod_row is None else rows_per_mod_row // tm
    in_specs = [pl.BlockSpec((tm, D_MODEL), row),
                _mod_spec(mod_row),
                _resident((1, D_MODEL)),
                _resident((D_MODEL, IN_WIDTH)),
                _resident((1, LANES)),
                _resident((1, LANES))]
    args = [x, mod3, g1, w_in_b, gq2, gk2]
    if rope:
        tiles_per_seq = rope_tabs[0].shape[0] // tm
        tab = pl.BlockSpec((tm, LANES), lambda i: (i % tiles_per_seq, 0))
        in_specs += [tab, tab]
        args += list(rope_tabs)
    slabs = [pl.BlockSpec((w.shape[0] // steps, w.shape[1]), row) for w in cast_weights]
    in_specs += slabs
    args += list(cast_weights)
    out_specs = [pl.BlockSpec((tm, F_WIDTH), row),
                 pl.BlockSpec((tm, ATTN_WIDTH), row),
                 pl.BlockSpec((2 * KV_WIDTH, tm), lambda i: (0, i)),
                 pl.BlockSpec((tm, 2 * KV_WIDTH), row),
                 pl.BlockSpec((tm, 2 * D_MODEL), row)]
    out_shape = [pltpu.HBM((t, F_WIDTH), BF16),
                 pltpu.HBM((t, ATTN_WIDTH), BF16),
                 pltpu.HBM((2 * KV_WIDTH, t), BF16),
                 pltpu.HBM((t, 2 * KV_WIDTH), BF16),
                 pltpu.HBM((t, 2 * D_MODEL), BF16)]
    if cache_seq:
        seqs = tm // cache_seq
        out_specs += [pl.BlockSpec((seqs, KV_WIDTH, cache_seq), lambda i: (i, 0, 0))] * 2
        out_shape += [pltpu.HBM((t // cache_seq, KV_WIDTH, cache_seq), F32)] * 2
    out_specs += slabs
    out_shape += [pltpu.HBM(w.shape, BF16) for w in cast_weights]
    return pl.pallas_call(
        functools.partial(_proj_kernel, rope=rope, cache_seq=cache_seq, n_cast=len(cast_weights),
                          tiles_per_mod_row=tiles_per_mod_row),
        grid=(steps,),
        in_specs=in_specs,
        out_specs=out_specs,
        out_shape=out_shape,
        compiler_params=pltpu.CompilerParams(
            dimension_semantics=("arbitrary",), vmem_limit_bytes=_vmem_limit(58 * 2 ** 20)),
        name="proj_lat" if rope else "proj_ctx",
    )(*[_hbm(a) for a in args])


def _dft_tables(n):
    def cs(m):
        idx = np.arange(m, dtype=np.int64)
        ang = 2.0 * np.pi * ((idx[:, None] * idx[None, :]) % m).astype(np.float64) / m
        return np.cos(ang) / np.sqrt(m), np.sin(ang) / np.sqrt(m)
    cd, sd = cs(F_GROUP_DIM)
    cn, sn = cs(n)
    csd = np.concatenate([cd, sd], axis=1).astype(np.float32)
    csn = np.concatenate([cn, -sn], axis=1).astype(np.float32)
    return jnp.asarray(csd).astype(BF16), jnp.asarray(csn).astype(BF16)


def _fourier_kernel(zf_ref, csd_ref, csn_ref, o_ref, ab_ref, *, seqs, n):
    units = []
    for s in range(seqs):
        buf = s % 2

        def channels(s=s, buf=buf):
            for g in range(F_GROUPS):
                cols = slice(g * F_GROUP_DIM, (g + 1) * F_GROUP_DIM)
                ab = _dot(zf_ref[s, :, cols], csd_ref[...])
                ab_ref[buf, 0:n, cols] = ab[:, 0:F_GROUP_DIM].astype(BF16)
                ab_ref[buf, n:2 * n, cols] = ab[:, F_GROUP_DIM:].astype(BF16)

        def positions(s=s, buf=buf):
            o_ref[s] = _dot(csn_ref[...], ab_ref[buf]).astype(BF16)

        units.append((channels, positions))
    _run_units(units)


def _fourier(zf3, seqs):
    b, n, _ = zf3.shape
    csd, csn = _dft_tables(n)
    blk = pl.BlockSpec((seqs, n, F_WIDTH), lambda i: (i, 0, 0))
    return pl.pallas_call(
        functools.partial(_fourier_kernel, seqs=seqs, n=n),
        grid=(b // seqs,),
        in_specs=[blk, _resident(csd.shape), _resident(csn.shape)],
        out_specs=blk,
        out_shape=pltpu.HBM(zf3.shape, BF16),
        scratch_shapes=[pltpu.VMEM((2, 2 * n, F_WIDTH), BF16)],
        compiler_params=pltpu.CompilerParams(
            dimension_semantics=("arbitrary",), vmem_limit_bytes=_vmem_limit(32 * 2 ** 20)),
        name=f"fourier_{n}",
    )(_hbm(zf3), csd, csn)


def _split_lo_hi(xd, fill):
    lo = lax.broadcasted_iota(jnp.int32, xd.shape, 1) < HEAD_DIM
    other = jnp.full_like(xd, fill)
    return jnp.where(lo, xd, other), jnp.where(lo, other, xd)


def _group_scores(q_stack, keys, s_ref):
    c0 = 0
    for kt in keys:
        top = lax.broadcasted_iota(jnp.int32, kt.shape, 0) < HEAD_DIM
        zero = jnp.zeros_like(kt)
        c1 = c0 + kt.shape[1]
        s_ref[0, :, c0:c1] = _dot(q_stack, jnp.where(top, kt, zero))
        s_ref[1, :, c0:c1] = _dot(q_stack, jnp.where(top, zero, kt))
        c0 = c1


def _group_softmax_pv(values, sinks, band, s_ref, p_ref, sk_ref, rows_per_pair, chunk):
    rows = s_ref.shape[1]
    bounds = [0]
    for vd in values:
        bounds.append(bounds[-1] + vd.shape[0])
    sources = values

    for half in range(2):
        for r0 in range(0, rows, chunk):
            rs = slice(r0, r0 + chunk)
            sink = sinks[2 * (r0 // rows_per_pair) + half]
            parts = [s_ref[half, rs, bounds[i]:bounds[i + 1]] for i in range(len(sources))]
            if band is not None:
                parts[0] = jnp.where(band(r0 % rows_per_pair, chunk), parts[0], NEG_INF)
            m = sink
            for s in parts:
                m = jnp.maximum(m, jnp.max(s, axis=-1, keepdims=True))
            for i, s in enumerate(parts):
                p_ref[half, rs, bounds[i]:bounds[i + 1]] = jnp.exp2((s - m).astype(BF16))
            sk_ref[half, rs, :] = jnp.broadcast_to(jnp.exp2(sink - m), (chunk, LANES))

    o_e = o_o = None
    for i, vd in enumerate(values):
        v_lo, v_hi = _split_lo_hi(vd, 1.0)
        pe = _dot(p_ref[0, :, bounds[i]:bounds[i + 1]], v_lo)
        po = _dot(p_ref[1, :, bounds[i]:bounds[i + 1]], v_hi)
        o_e = pe if o_e is None else o_e + pe
        o_o = po if o_o is None else o_o + po
    lo = lax.broadcasted_iota(jnp.int32, (rows, LANES), 1) < HEAD_DIM
    num = jnp.where(lo, o_e, o_o)
    den = pltpu.roll(jnp.where(lo, o_o, o_e), HEAD_DIM, 1) + jnp.where(lo, sk_ref[0], sk_ref[1])
    return num / den


def _pair_cols(g):
    pa, pb = 2 * g, 2 * g + 1
    return slice(pa * LANES, (pa + 1) * LANES), slice(pb * LANES, (pb + 1) * LANES)


def _attn_ctx_kernel(sink_ref, q_ref, kd_ref, vd_ref, o_ref, s_ref, p_ref, sk_ref, *, seqs):
    n = q_ref.shape[1]
    units = []
    for s in range(seqs):
        for g in range(N_KV_HEADS):
            buf = len(units) % 2
            cols = slice(g * LANES, (g + 1) * LANES)
            qa, qb = _pair_cols(g)

            def scores(s=s, cols=cols, qa=qa, qb=qb, buf=buf):
                q_stack = jnp.concatenate([q_ref[s, :, qa], q_ref[s, :, qb]], axis=0)
                _group_scores(q_stack, [kd_ref[s, :, cols]], s_ref.at[buf])

            def finish(s=s, g=g, cols=cols, qa=qa, qb=qb, buf=buf):
                sinks = [sink_ref[GROUP * g + h] * LOG2_E for h in range(GROUP)]
                o = _group_softmax_pv([vd_ref[s, :, cols]], sinks, None, s_ref.at[buf],
                                      p_ref.at[buf], sk_ref.at[buf], n, CTX_SOFTMAX_ROWS)
                o_ref[s, :, qa] = o[0:n].astype(BF16)
                o_ref[s, :, qb] = o[n:2 * n].astype(BF16)

            units.append((scores, finish))
    _run_units(units)


def _attn_ctx(sinks, q3, kd3, vd3):
    b, n, _ = q3.shape
    seqs = CTX_SEQS_PER_STEP
    blk = lambda w: pl.BlockSpec((seqs, n, w), lambda i: (i, 0, 0))
    return pl.pallas_call(
        functools.partial(_attn_ctx_kernel, seqs=seqs),
        grid=(b // seqs,),
        in_specs=[pl.BlockSpec(memory_space=pltpu.SMEM),
                  blk(ATTN_WIDTH), blk(2 * KV_WIDTH), blk(2 * KV_WIDTH)],
        out_specs=blk(ATTN_WIDTH),
        out_shape=pltpu.HBM(q3.shape, BF16),
        scratch_shapes=[pltpu.VMEM((2, 2, 2 * n, n), F32),
                        pltpu.VMEM((2, 2, 2 * n, n), BF16),
                        pltpu.VMEM((2, 2, 2 * n, LANES), F32)],
        compiler_params=pltpu.CompilerParams(
            dimension_semantics=("arbitrary",), vmem_limit_bytes=_vmem_limit(32 * 2 ** 20)),
        name="attn_ctx",
    )(sinks, _hbm(q3), _hbm(kd3), _hbm(vd3))


def _attn_lat_kernel(sink_ref, q_ref, kd_ref, vd_ref, ck_ref, cv_ref, o_ref,
                     s_ref, p_ref, sk_ref, dist_ref, ckv_ref, *, n, tq):
    t = pl.program_id(1)
    qb = WINDOW
    span = qb + 2 * WINDOW

    ckv_ref[0] = _dup_heads(ck_ref[0].T).astype(BF16)
    ckv_ref[1] = _dup_heads(cv_ref[0].T).astype(BF16)
    units = []
    for rr in range(tq // qb):
        q0 = t * tq + rr * qb
        start = pl.multiple_of(jnp.clip(q0 - WINDOW, 0, n - span), WINDOW)
        dist_ref[rr] = jnp.abs((q0 - start) + lax.broadcasted_iota(jnp.int32, (qb, span), 0)
                               - lax.broadcasted_iota(jnp.int32, (qb, span), 1))
        band = lambda r0, rows, rr=rr: dist_ref[rr, r0:r0 + rows, :] <= WINDOW
        rows = slice(rr * qb, (rr + 1) * qb)
        for g in range(N_KV_HEADS):
            buf = len(units) % 2
            cols = slice(g * LANES, (g + 1) * LANES)
            qa, qbc = _pair_cols(g)

            def scores(rows=rows, start=start, cols=cols, qa=qa, qbc=qbc, buf=buf):
                q_stack = jnp.concatenate([q_ref[0, rows, qa], q_ref[0, rows, qbc]], axis=0)
                _group_scores(q_stack, [kd_ref[0, pl.ds(start, span), cols], ckv_ref[0, :, cols]],
                              s_ref.at[buf])

            def finish(rows=rows, start=start, g=g, cols=cols, qa=qa, qbc=qbc, buf=buf, band=band):
                sinks = [sink_ref[GROUP * g + h] * LOG2_E for h in range(GROUP)]
                o = _group_softmax_pv([vd_ref[0, pl.ds(start, span), cols], ckv_ref[1, :, cols]],
                                      sinks, band, s_ref.at[buf], p_ref.at[buf], sk_ref.at[buf],
                                      qb, LAT_SOFTMAX_ROWS)
                o_ref[0, rows, qa] = o[0:qb].astype(BF16)
                o_ref[0, rows, qbc] = o[qb:2 * qb].astype(BF16)

            units.append((scores, finish))
    _run_units(units)


def _attn_lat(sinks, q3, kd3, vd3, ck3, cv3):
    b, n, _ = q3.shape
    past = ck3.shape[2]
    tq = Q_TILE
    keys = 3 * WINDOW + past
    qblk = pl.BlockSpec((1, tq, ATTN_WIDTH), lambda i, t: (i, t, 0))
    seq = lambda rows, w: pl.BlockSpec((1, rows, w), lambda i, t: (i, 0, 0))
    return pl.pallas_call(
        functools.partial(_attn_lat_kernel, n=n, tq=tq),
        grid=(b, n // tq),
        in_specs=[pl.BlockSpec(memory_space=pltpu.SMEM), qblk,
                  seq(n, 2 * KV_WIDTH), seq(n, 2 * KV_WIDTH),
                  seq(KV_WIDTH, past), seq(KV_WIDTH, past)],
        out_specs=qblk,
        out_shape=pltpu.HBM(q3.shape, BF16),
        scratch_shapes=[pltpu.VMEM((2, 2, 2 * WINDOW, keys), F32),
                        pltpu.VMEM((2, 2, 2 * WINDOW, keys), BF16),
                        pltpu.VMEM((2, 2, 2 * WINDOW, LANES), F32),
                        pltpu.VMEM((tq // WINDOW, WINDOW, 3 * WINDOW), jnp.int32),
                        pltpu.VMEM((2, past, 2 * KV_WIDTH), BF16)],
        compiler_params=pltpu.CompilerParams(
            dimension_semantics=("arbitrary", "arbitrary"),
            vmem_limit_bytes=_vmem_limit(32 * 2 ** 20)),
        name="attn_lat",
    )(sinks, _hbm(q3), _hbm(kd3), _hbm(vd3), _hbm(ck3), _hbm(cv3))


def _tail_kernel(x_ref, mod_ref, mix_ref, att_ref, gate_ref, g2_ref,
                 wf_ref, wao_ref, wout_ref, wup_ref, wdn_ref, o_ref, *, tiles_per_mod_row):
    gt1, sh2, sc2, gt2 = _mod_vectors(mod_ref, pl.program_id(0), tiles_per_mod_row)[2:6]

    yf = _dot(mix_ref[...], wf_ref[...])
    ya = _dot(att_ref[...], wao_ref[...])
    merged = (gate_ref[:, 0:D_MODEL].astype(F32) * yf
              + gate_ref[:, D_MODEL:2 * D_MODEL].astype(F32) * ya)
    x1 = x_ref[...] + gt1 * _dot(merged.astype(BF16), wout_ref[...])

    h = x1 * lax.rsqrt(jnp.mean(x1 * x1, axis=-1, keepdims=True) + EPS) * g2_ref[...]
    hb = (h * (1.0 + sc2) + sh2).astype(BF16)
    d_ff = wdn_ref.shape[0]
    acc = None
    for c in range(d_ff // FF_CHUNK):
        a = _dot(hb, wup_ref[:, c * FF_CHUNK:(c + 1) * FF_CHUNK])
        u = _dot(hb, wup_ref[:, d_ff + c * FF_CHUNK:d_ff + (c + 1) * FF_CHUNK])
        act = (a * _sigmoid(a) * u).astype(BF16)
        d = _dot(act, wdn_ref[c * FF_CHUNK:(c + 1) * FF_CHUNK, :])
        acc = d if acc is None else acc + d
    o_ref[...] = x1 + gt2 * acc


def _tail(x, mod3, mix, att, gate, g2, wf, wao, wout, wup, wdn, mod_row, rows_per_mod_row):
    t = x.shape[0]
    tm = TAIL_TOKEN_TILE
    row = lambda i: (i, 0)
    tiles_per_mod_row = None if rows_per_mod_row is None else rows_per_mod_row // tm
    return pl.pallas_call(
        functools.partial(_tail_kernel, tiles_per_mod_row=tiles_per_mod_row),
        grid=(t // tm,),
        in_specs=[pl.BlockSpec((tm, D_MODEL), row),
                  _mod_spec(mod_row),
                  pl.BlockSpec((tm, F_WIDTH), row),
                  pl.BlockSpec((tm, ATTN_WIDTH), row),
                  pl.BlockSpec((tm, 2 * D_MODEL), row),
                  _resident((1, D_MODEL)),
                  _resident(wf.shape), _resident(wao.shape), _resident(wout.shape),
                  _resident(wup.shape), _resident(wdn.shape)],
        out_specs=pl.BlockSpec((tm, D_MODEL), row),
        out_shape=pltpu.HBM((t, D_MODEL), F32),
        compiler_params=pltpu.CompilerParams(
            dimension_semantics=("arbitrary",), vmem_limit_bytes=_vmem_limit(56 * 2 ** 20)),
        name="tail",
    )(*[_hbm(a) for a in (x, mod3, mix, att, gate, g2, wf, wao, wout, wup, wdn)])


def _attn_tail_kernel(*refs, lat, n_tiles, tm, n, tiles_per_mod_row):
    sink_ref, q_ref, kd_ref, vd_ref = refs[:4]
    refs = refs[4:]
    if lat:
        ck_ref, cv_ref = refs[:2]
        refs = refs[2:]
    (x_ref, mod_ref, mix_ref, gate_ref, g2_ref,
     wf_ref, wao_ref, wout_ref, wup_ref, wdn_ref, o_ref, att_ref, s_ref, p_ref, sk_ref) = refs[:15]
    if lat:
        dist_ref, ckt_ref, cvd_ref = refs[15:]
    i = pl.program_id(0)

    def attention_stages():
        units = []
        if lat:
            qb = WINDOW
            span = qb + 2 * WINDOW
            t = jnp.minimum(i, n_tiles - 1) % (n // tm)
            ckt = ck_ref[0]
            ckt_ref[...] = jnp.concatenate(
                [ckt[0:HEAD_DIM], ckt[0:HEAD_DIM], ckt[HEAD_DIM:], ckt[HEAD_DIM:]], axis=0).astype(BF16)
            cvd_ref[...] = _dup_heads(cv_ref[0].T).astype(BF16)
            for rr in range(tm // qb):
                q0 = t * tm + rr * qb
                start = pl.multiple_of(jnp.clip(q0 - WINDOW, 0, n - span), WINDOW)
                dist_ref[rr] = jnp.abs((q0 - start) + lax.broadcasted_iota(jnp.int32, (qb, span), 0)
                                       - lax.broadcasted_iota(jnp.int32, (qb, span), 1))
                band = lambda r0, rows, rr=rr: dist_ref[rr, r0:r0 + rows, :] <= WINDOW
                rows = slice(rr * qb, (rr + 1) * qb)
                for g in range(N_KV_HEADS):
                    buf = len(units) % 2
                    cols = slice(g * LANES, (g + 1) * LANES)
                    qa, qbc = _pair_cols(g)

                    def scores(rows=rows, start=start, cols=cols, qa=qa, qbc=qbc, buf=buf):
                        q_stack = jnp.concatenate([q_ref[rows, qa], q_ref[rows, qbc]], axis=0)
                        _group_scores(q_stack, [kd_ref[cols, pl.ds(start, span)], ckt_ref[cols, :]],
                                      s_ref.at[buf])

                    def finish(rows=rows, start=start, g=g, cols=cols, qa=qa, qbc=qbc, buf=buf,
                               band=band):
                        sinks = [sink_ref[GROUP * g + h] * LOG2_E for h in range(GROUP)]
                        o = _group_softmax_pv(
                            [vd_ref[pl.ds(start, span), cols], cvd_ref[:, cols]], sinks, band,
                            s_ref.at[buf], p_ref.at[buf], sk_ref.at[buf], qb, LAT_SOFTMAX_ROWS)
                        att_ref[rows, qa] = o[0:qb].astype(BF16)
                        att_ref[rows, qbc] = o[qb:2 * qb].astype(BF16)

                    units.append((scores, finish))
        else:
            for s in range(tm // n):
                rows = slice(s * n, (s + 1) * n)
                for g in range(N_KV_HEADS):
                    buf = len(units) % 2
                    cols = slice(g * LANES, (g + 1) * LANES)
                    qa, qbc = _pair_cols(g)

                    def scores(rows=rows, cols=cols, qa=qa, qbc=qbc, buf=buf):
                        q_stack = jnp.concatenate([q_ref[rows, qa], q_ref[rows, qbc]], axis=0)
                        _group_scores(q_stack, [kd_ref[cols, rows]], s_ref.at[buf])

                    def finish(rows=rows, g=g, cols=cols, qa=qa, qbc=qbc, buf=buf):
                        sinks = [sink_ref[GROUP * g + h] * LOG2_E for h in range(GROUP)]
                        o = _group_softmax_pv([vd_ref[rows, cols]], sinks, None, s_ref.at[buf],
                                              p_ref.at[buf], sk_ref.at[buf], n, CTX_SOFTMAX_ROWS)
                        att_ref[rows, qa] = o[0:n].astype(BF16)
                        att_ref[rows, qbc] = o[n:2 * n].astype(BF16)

                    units.append((scores, finish))
        stages = [units[0][0]]
        for k, (_, finish) in enumerate(units):
            if k + 1 < len(units):
                stages.append(units[k + 1][0])
            stages.append(finish)
        return stages

    def tail(stages):
        gt1, sh2, sc2, gt2 = _mod_vectors(mod_ref, i - 1, tiles_per_mod_row)[2:6]
        yf = _dot(mix_ref[...], wf_ref[...])
        ya = _dot(att_ref[...], wao_ref[...])
        merged = (gate_ref[:, 0:D_MODEL].astype(F32) * yf
                  + gate_ref[:, D_MODEL:2 * D_MODEL].astype(F32) * ya)
        x1 = x_ref[...] + gt1 * _dot(merged.astype(BF16), wout_ref[...])

        h = x1 * lax.rsqrt(jnp.mean(x1 * x1, axis=-1, keepdims=True) + EPS) * g2_ref[...]
        hb = (h * (1.0 + sc2) + sh2).astype(BF16)
        d_ff = wdn_ref.shape[0]
        n_chunks = d_ff // FF_CHUNK
        acc = None
        issued = 0
        for c in range(n_chunks):
            a = _dot(hb, wup_ref[:, c * FF_CHUNK:(c + 1) * FF_CHUNK])
            u = _dot(hb, wup_ref[:, d_ff + c * FF_CHUNK:d_ff + (c + 1) * FF_CHUNK])
            act = (a * _sigmoid(a) * u).astype(BF16)
            d = _dot(act, wdn_ref[c * FF_CHUNK:(c + 1) * FF_CHUNK, :])
            acc = d if acc is None else acc + d
            while issued < len(stages) and issued * n_chunks < (c + 1) * len(stages):
                stages[issued]()
                issued += 1
        o_ref[...] = x1 + gt2 * acc

    @pl.when(i == 0)
    def _():
        for stage in attention_stages():
            stage()

    @pl.when(jnp.logical_and(i > 0, i < n_tiles))
    def _():
        tail(attention_stages())

    @pl.when(i == n_tiles)
    def _():
        tail([])


def _attn_tail(sinks, q, kd, vd, cache, x, mod3, mix, gate, g2, wf, wao, wout, wup, wdn,
               mod_row, rows_per_mod_row, n):
    t = x.shape[0]
    tm = TAIL_TOKEN_TILE
    n_tiles = t // tm
    lat = cache is not None
    tiles_per_mod_row = None if rows_per_mod_row is None else rows_per_mod_row // tm
    front = lambda i: (jnp.minimum(i, n_tiles - 1), 0)
    back = lambda i: (jnp.maximum(i - 1, 0), 0)
    in_specs = [pl.BlockSpec(memory_space=pltpu.SMEM), pl.BlockSpec((tm, ATTN_WIDTH), front)]
    args = [sinks, q, kd, vd]
    if lat:
        past = cache[0].shape[2]
        seq_of = lambda i: jnp.minimum(i, n_tiles - 1) // (n // tm)
        in_specs += [pl.BlockSpec((2 * KV_WIDTH, n), lambda i: (0, seq_of(i))),
                     pl.BlockSpec((n, 2 * KV_WIDTH), lambda i: (seq_of(i), 0))]
        in_specs += [pl.BlockSpec((1, KV_WIDTH, past), lambda i: (seq_of(i), 0, 0))] * 2
        args += list(cache)
        m_rows, keys = 2 * WINDOW, 3 * WINDOW + past
    else:
        in_specs += [pl.BlockSpec((2 * KV_WIDTH, tm), lambda i: (0, front(i)[0])),
                     pl.BlockSpec((tm, 2 * KV_WIDTH), front)]
        m_rows, keys = 2 * n, n
    in_specs += [pl.BlockSpec((tm, D_MODEL), back),
                 _mod_spec(mod_row),
                 pl.BlockSpec((tm, F_WIDTH), back),
                 pl.BlockSpec((tm, 2 * D_MODEL), back),
                 _resident((1, D_MODEL)),
                 _resident(wf.shape), _resident(wao.shape), _resident(wout.shape),
                 _resident(wup.shape), _resident(wdn.shape)]
    args += [x, mod3, mix, gate, g2, wf, wao, wout, wup, wdn]
    scratch = [pltpu.VMEM((tm, ATTN_WIDTH), BF16),
               pltpu.VMEM((2, 2, m_rows, keys), F32),
               pltpu.VMEM((2, 2, m_rows, keys), BF16),
               pltpu.VMEM((2, 2, m_rows, LANES), F32)]
    if lat:
        scratch += [pltpu.VMEM((tm // WINDOW, WINDOW, 3 * WINDOW), jnp.int32),
                    pltpu.VMEM((2 * KV_WIDTH, past), BF16),
                    pltpu.VMEM((past, 2 * KV_WIDTH), BF16)]
    return pl.pallas_call(
        functools.partial(_attn_tail_kernel, lat=lat, n_tiles=n_tiles, tm=tm, n=n,
                          tiles_per_mod_row=tiles_per_mod_row),
        grid=(n_tiles + 1,),
        in_specs=in_specs,
        out_specs=pl.BlockSpec((tm, D_MODEL), back),
        out_shape=pltpu.HBM((t, D_MODEL), F32),
        scratch_shapes=scratch,
        compiler_params=pltpu.CompilerParams(
            dimension_semantics=("arbitrary",), vmem_limit_bytes=_vmem_limit(58 * 2 ** 20)),
        name="attn_tail_lat" if lat else "attn_tail_ctx",
    )(sinks, *[_hbm(a) for a in args[1:]])


def _rope_tables(n):
    rows = n // GRID_W
    row = np.repeat(np.arange(rows, dtype=np.float64), GRID_W)
    col = np.tile(np.arange(GRID_W, dtype=np.float64), rows)
    axis_dim = HEAD_DIM // 2
    inv_freq = ROPE_THETA ** (-np.arange(0, axis_dim, 2, dtype=np.float64) / axis_dim)

    def axis_tabs(pos):
        ang = pos[:, None] * inv_freq[None, :]
        cos, sin = np.cos(ang), np.sin(ang)
        return np.concatenate([cos, cos], axis=-1), np.concatenate([-sin, sin], axis=-1)

    cr, sr = axis_tabs(row)
    cc, sc = axis_tabs(col)
    cos = np.tile(np.concatenate([cr, cc], axis=-1), (1, LANES // HEAD_DIM))
    sin = np.tile(np.concatenate([sr, sc], axis=-1), (1, LANES // HEAD_DIM))
    return jnp.asarray(cos.astype(np.float32)), jnp.asarray(sin.astype(np.float32))


@jax.jit
def _layer(xp, xs, ck, cv, c, c_ctx, w_ada, b_ada, g_norm1, g_norm2, w_in, g_q, g_k, sink,
           w_f, w_ao, w_out, w_up, w_down):
    bp, sp, _ = xp.shape
    bs, ss, _ = xs.shape
    past = ck.shape[1]

    assert bs <= CTX_MOD_ROW, "latent conditioning rows must fit below the context row"
    cvec = jnp.concatenate(
        [c, jnp.zeros((CTX_MOD_ROW - bs, D_MODEL), F32), c_ctx[None, :],
         jnp.zeros((MOD_ROWS - CTX_MOD_ROW - 1, D_MODEL), F32)], axis=0)
    mod3 = _ada(cvec, w_ada, b_ada[None, :])

    w_in_b = w_in.astype(BF16)
    g1, g2 = g_norm1[None, :], g_norm2[None, :]
    gq2 = jnp.tile(g_q, LANES // HEAD_DIM)[None, :]
    gk2 = jnp.tile(g_k, LANES // HEAD_DIM)[None, :]
    ctx_mod = (CTX_MOD_ROW, None)
    lat_mod = (0, ss)

    xp2 = xp.reshape(bp * sp, D_MODEL)
    xs2 = xs.reshape(bs * ss, D_MODEL)
    zf, q, kd, vd, gate, k_new, v_new, wup = _proj(
        xp2, mod3, g1, w_in_b, gq2, gk2, None, *ctx_mod, sp, [w_up])
    zf_s, q_s, kd_s, vd_s, gate_s, wf, wao, wout, wdn = _proj(
        xs2, mod3, g1, w_in_b, gq2, gk2, _rope_tables(ss), *lat_mod, 0, [w_f, w_ao, w_out, w_down])

    mix = _fourier(zf.reshape(bp, sp, F_WIDTH), CTX_SEQS_PER_STEP)
    yp = _attn_tail(sink, q, kd, vd, None, xp2, mod3, mix.reshape(bp * sp, F_WIDTH), gate, g2,
                    wf, wao, wout, wup, wdn, *ctx_mod, sp)

    def cache_t(t):
        return t.transpose(0, 2, 3, 1).reshape(bs, KV_WIDTH, past)

    zf, q, kd, vd, gate = zf_s, q_s, kd_s, vd_s, gate_s
    mix = _fourier(zf.reshape(bs, ss, F_WIDTH), 1)
    ys = _attn_tail(sink, q, kd, vd, (cache_t(ck), cache_t(cv)), xs2, mod3,
                    mix.reshape(bs * ss, F_WIDTH), gate, g2, wf, wao, wout, wup, wdn, *lat_mod, ss)

    def cache_layout(t):
        return t.reshape(bp, N_KV_HEADS, HEAD_DIM, sp).transpose(0, 3, 1, 2)

    return yp.reshape(xp.shape), ys.reshape(xs.shape), cache_layout(k_new), cache_layout(v_new)


def kernel(x_prompt, x_sample, cache_k, cache_v, c, c_ctx, w_ada, b_ada, g_norm1, g_norm2,
           w_in, g_q, g_k, sinks, w_f, w_ao, w_out, w_up, w_down):
    depth = w_in.shape[0]
    xp, xs = x_prompt, x_sample
    new_k, new_v = [], []
    for l in range(depth):
        xp, xs, k_ctx, v_ctx = _layer(
            xp, xs, cache_k[:, l], cache_v[:, l], c, c_ctx, w_ada[l], b_ada[l], g_norm1[l],
            g_norm2[l], w_in[l], g_q[l], g_k[l], sinks[l], w_f[l], w_ao[l], w_out[l], w_up[l],
            w_down[l])
        new_k.append(k_ctx)
        new_v.append(v_ctx)
    return (xp, xs, jnp.stack(new_k, axis=1), jnp.stack(new_v, axis=1))
```

```python
import functools

import numpy as np
import jax
import jax.numpy as jnp
from jax import lax
from jax.experimental import pallas as pl
from jax.experimental.pallas import tpu as pltpu

D_MODEL = 1024
GRID_W = 64
N_HEADS = 8
N_KV_HEADS = 2
GROUP = N_HEADS // N_KV_HEADS
HEAD_DIM = 64
WINDOW = 128
F_GROUPS = 4
F_GROUP_DIM = 128
F_WIDTH = F_GROUPS * F_GROUP_DIM
ATTN_WIDTH = N_HEADS * HEAD_DIM
KV_WIDTH = N_KV_HEADS * HEAD_DIM
ROPE_THETA = 10000.0
EPS = 1e-6
NEG_INF = -1e30
LOG2_E = 1.4426950408889634

LANES = 128
MXU_DIM = 256
VMEM_BYTES_V7X = 64 * 1024 * 1024

_Q0 = F_WIDTH
_K0 = _Q0 + ATTN_WIDTH
_V0 = _K0 + KV_WIDTH
_G0 = _V0 + KV_WIDTH
IN_WIDTH = _G0 + 2 * D_MODEL

SUBLANES = 8
N_MOD = 6
MOD_ROWS = 16
CTX_MOD_ROW = 8
ADA_STEPS = 8

PROJ_TOKEN_TILE = 1024
TAIL_TOKEN_TILE = 512
FF_CHUNK = MXU_DIM
CTX_SEQS_PER_STEP = 4
CTX_SOFTMAX_ROWS = 64
LAT_SOFTMAX_ROWS = 32

F32 = jnp.float32
BF16 = jnp.bfloat16


def _vmem_limit(nbytes):
    return int(min(nbytes, VMEM_BYTES_V7X - 4 * 1024 * 1024))


def _dot(a, b):
    return jnp.dot(a, b, preferred_element_type=F32)


def _sigmoid(x):
    return 1.0 / (1.0 + jnp.exp(-x))


def _resident(shape):
    zeros = (0,) * len(shape)
    return pl.BlockSpec(shape, lambda *_: zeros, pipeline_mode=pl.Buffered(1))


def _hbm(x):
    return pltpu.with_memory_space_constraint(x, pltpu.HBM)


def _run_units(units):
    units[0][0]()
    for k, (_, second) in enumerate(units):
        if k + 1 < len(units):
            units[k + 1][0]()
        second()


def _ada_kernel(c_ref, w_ref, b_ref, win_ref, o_ref, winb_ref):
    c = c_ref[...]
    s = c * _sigmoid(c)
    o_ref[...] = _dot(s.astype(BF16), w_ref[...].astype(BF16)) + b_ref[...]
    winb_ref[...] = win_ref[...].astype(BF16)


def _ada(cvec, w_ada, b_ada, w_in):
    n = w_ada.shape[1]
    bn = n // ADA_STEPS
    slab = pl.BlockSpec((w_in.shape[0] // ADA_STEPS, w_in.shape[1]), lambda j: (j, 0))
    return pl.pallas_call(
        _ada_kernel,
        grid=(ADA_STEPS,),
        in_specs=[pl.BlockSpec((MOD_ROWS, D_MODEL), lambda j: (0, 0)),
                  pl.BlockSpec((D_MODEL, bn), lambda j: (0, j)),
                  pl.BlockSpec((1, bn), lambda j: (0, j)),
                  slab],
        out_specs=[pl.BlockSpec((MOD_ROWS, bn), lambda j: (0, j)), slab],
        out_shape=[pltpu.HBM((MOD_ROWS, n), F32), pltpu.HBM(w_in.shape, BF16)],
        name="ada",
    )(cvec, w_ada, b_ada, w_in)


def _mod_spec(mod_row):
    return pl.BlockSpec((SUBLANES, N_MOD * D_MODEL), lambda i: (mod_row // SUBLANES, 0))


def _mod_vectors(mod_ref, tile, tiles_per_row):
    cols = [slice(j * D_MODEL, (j + 1) * D_MODEL) for j in range(N_MOD)]
    if tiles_per_row is None:
        return [mod_ref[0:1, c] for c in cols]
    r = tile // tiles_per_row
    return [mod_ref[pl.ds(r, 1), c] for c in cols]


def _head_norm(z, g):
    lo = lax.broadcasted_iota(jnp.int32, z.shape, 1) < HEAD_DIM
    s = z * z
    s_lo = jnp.sum(jnp.where(lo, s, 0.0), axis=-1, keepdims=True)
    s_hi = jnp.sum(jnp.where(lo, 0.0, s), axis=-1, keepdims=True)
    ms = jnp.where(lo, s_lo, s_hi) * (1.0 / HEAD_DIM)
    return z * lax.rsqrt(ms + EPS) * g


def _rope(y, cos, sin):
    lane = lax.broadcasted_iota(jnp.int32, y.shape, 1)
    first = (lane % 32) < 16
    partner = jnp.where(first, pltpu.roll(y, LANES - 16, 1), pltpu.roll(y, 16, 1))
    return y * cos + partner * sin


def _dup_heads(y):
    lo = lax.broadcasted_iota(jnp.int32, y.shape, 1) < HEAD_DIM
    sw = pltpu.roll(y, HEAD_DIM, 1)
    return jnp.concatenate([jnp.where(lo, y, sw), jnp.where(lo, sw, y)], axis=-1)


def _proj_kernel(*refs, rope, cache_seq, n_cast, tiles_per_mod_row):
    x_ref, mod_ref, g1_ref, w_ref, gq_ref, gk_ref = refs[:6]
    refs = refs[6:]
    if rope:
        cos_ref, sin_ref = refs[:2]
        refs = refs[2:]
    cast_in, refs = refs[:n_cast], refs[n_cast:]
    zf_ref, q_ref, kd_ref, vd_ref, gate_ref = refs[:5]
    refs = refs[5:]
    if cache_seq:
        kc_ref, vc_ref = refs[:2]
        refs = refs[2:]
    for src_ref, dst_ref in zip(cast_in, refs):
        dst_ref[...] = src_ref[...].astype(BF16)

    x = x_ref[...]
    sh1, sc1 = _mod_vectors(mod_ref, pl.program_id(0), tiles_per_mod_row)[0:2]
    h = x * lax.rsqrt(jnp.mean(x * x, axis=-1, keepdims=True) + EPS) * g1_ref[...]
    hb = (h * (1.0 + sc1) + sh1).astype(BF16)

    zq = _dot(hb, w_ref[:, _Q0:_K0])
    for j in range(ATTN_WIDTH // LANES):
        y = _head_norm(zq[:, j * LANES:(j + 1) * LANES], gq_ref[...])
        if rope:
            y = _rope(y, cos_ref[...], sin_ref[...])
        q_ref[:, j * LANES:(j + 1) * LANES] = (y * (HEAD_DIM ** -0.5 * LOG2_E)).astype(BF16)

    zkv = _dot(hb, w_ref[:, _K0:_G0])
    k = _head_norm(zkv[:, 0:KV_WIDTH], gk_ref[...])
    v = zkv[:, KV_WIDTH:2 * KV_WIDTH]
    if cache_seq:
        for s in range(x.shape[0] // cache_seq):
            rows = slice(s * cache_seq, (s + 1) * cache_seq)
            kc_ref[s] = k[rows, :].T
            vc_ref[s] = v[rows, :].T
    if rope:
        k = _rope(k, cos_ref[...], sin_ref[...])
    kt = k.T
    kd_ref[...] = jnp.concatenate(
        [kt[0:HEAD_DIM], kt[0:HEAD_DIM], kt[HEAD_DIM:], kt[HEAD_DIM:]], axis=0).astype(BF16)
    vd_ref[...] = _dup_heads(v).astype(BF16)

    gc = 512
    for j in range(2 * D_MODEL // gc):
        zg = _dot(hb, w_ref[:, _G0 + j * gc:_G0 + (j + 1) * gc])
        gate_ref[:, j * gc:(j + 1) * gc] = _sigmoid(zg).astype(BF16)

    zf_ref[...] = _dot(hb, w_ref[:, 0:_Q0]).astype(BF16)


def _proj(x, mod3, g1, w_in_b, gq2, gk2, rope_tabs, mod_row, rows_per_mod_row, cache_seq,
          cast_weights):
    t = x.shape[0]
    tm = PROJ_TOKEN_TILE
    steps = t // tm
    rope = rope_tabs is not None
    row = lambda i: (i, 0)
    tiles_per_mod_row = None if rows_per_mod_row is None else rows_per_mod_row // tm
    in_specs = [pl.BlockSpec((tm, D_MODEL), row),
                _mod_spec(mod_row),
                _resident((1, D_MODEL)),
                _resident((D_MODEL, IN_WIDTH)),
                _resident((1, LANES)),
                _resident((1, LANES))]
    args = [x, mod3, g1, w_in_b, gq2, gk2]
    if rope:
        tiles_per_seq = rope_tabs[0].shape[0] // tm
        tab = pl.BlockSpec((tm, LANES), lambda i: (i % tiles_per_seq, 0))
        in_specs += [tab, tab]
        args += list(rope_tabs)
    slabs = [pl.BlockSpec((w.shape[0] // steps, w.shape[1]), row) for w in cast_weights]
    in_specs += slabs
    args += list(cast_weights)
    out_specs = [pl.BlockSpec((tm, F_WIDTH), row),
                 pl.BlockSpec((tm, ATTN_WIDTH), row),
                 pl.BlockSpec((2 * KV_WIDTH, tm), lambda i: (0, i)),
                 pl.BlockSpec((tm, 2 * KV_WIDTH), row),
                 pl.BlockSpec((tm, 2 * D_MODEL), row)]
    out_shape = [pltpu.HBM((t, F_WIDTH), BF16),
                 pltpu.HBM((t, ATTN_WIDTH), BF16),
                 pltpu.HBM((2 * KV_WIDTH, t), BF16),
                 pltpu.HBM((t, 2 * KV_WIDTH), BF16),
                 pltpu.HBM((t, 2 * D_MODEL), BF16)]
    if cache_seq:
        seqs = tm // cache_seq
        out_specs += [pl.BlockSpec((seqs, KV_WIDTH, cache_seq), lambda i: (i, 0, 0))] * 2
        out_shape += [pltpu.HBM((t // cache_seq, KV_WIDTH, cache_seq), F32)] * 2
    out_specs += slabs
    out_shape += [pltpu.HBM(w.shape, BF16) for w in cast_weights]
    return pl.pallas_call(
        functools.partial(_proj_kernel, rope=rope, cache_seq=cache_seq, n_cast=len(cast_weights),
                          tiles_per_mod_row=tiles_per_mod_row),
        grid=(steps,),
        in_specs=in_specs,
        out_specs=out_specs,
        out_shape=out_shape,
        compiler_params=pltpu.CompilerParams(
            dimension_semantics=("arbitrary",), vmem_limit_bytes=_vmem_limit(58 * 2 ** 20)),
        name="proj_lat" if rope else "proj_ctx",
    )(*[_hbm(a) for a in args])


def _dft_tables(n):
    def cs(m):
        idx = np.arange(m, dtype=np.int64)
        ang = 2.0 * np.pi * ((idx[:, None] * idx[None, :]) % m).astype(np.float64) / m
        return np.cos(ang) / np.sqrt(m), np.sin(ang) / np.sqrt(m)
    cd, sd = cs(F_GROUP_DIM)
    cn, sn = cs(n)
    csd = np.concatenate([cd, sd], axis=1).astype(np.float32)
    csn = np.concatenate([cn, -sn], axis=1).astype(np.float32)
    return jnp.asarray(csd).astype(BF16), jnp.asarray(csn).astype(BF16)


def _fourier_kernel(zf_ref, csd_ref, csn_ref, o_ref, ab_ref, *, seqs, n):
    units = []
    for s in range(seqs):
        buf = s % 2

        def channels(s=s, buf=buf):
            for g in range(F_GROUPS):
                cols = slice(g * F_GROUP_DIM, (g + 1) * F_GROUP_DIM)
                ab = _dot(zf_ref[s, :, cols], csd_ref[...])
                ab_ref[buf, 0:n, cols] = ab[:, 0:F_GROUP_DIM].astype(BF16)
                ab_ref[buf, n:2 * n, cols] = ab[:, F_GROUP_DIM:].astype(BF16)

        def positions(s=s, buf=buf):
            o_ref[s] = _dot(csn_ref[...], ab_ref[buf]).astype(BF16)

        units.append((channels, positions))
    _run_units(units)


def _fourier(zf3, seqs):
    b, n, _ = zf3.shape
    csd, csn = _dft_tables(n)
    blk = pl.BlockSpec((seqs, n, F_WIDTH), lambda i: (i, 0, 0))
    return pl.pallas_call(
        functools.partial(_fourier_kernel, seqs=seqs, n=n),
        grid=(b // seqs,),
        in_specs=[blk, _resident(csd.shape), _resident(csn.shape)],
        out_specs=blk,
        out_shape=pltpu.HBM(zf3.shape, BF16),
        scratch_shapes=[pltpu.VMEM((2, 2 * n, F_WIDTH), BF16)],
        compiler_params=pltpu.CompilerParams(
            dimension_semantics=("arbitrary",), vmem_limit_bytes=_vmem_limit(32 * 2 ** 20)),
        name=f"fourier_{n}",
    )(_hbm(zf3), csd, csn)


def _split_lo_hi(xd, fill):
    lo = lax.broadcasted_iota(jnp.int32, xd.shape, 1) < HEAD_DIM
    other = jnp.full_like(xd, fill)
    return jnp.where(lo, xd, other), jnp.where(lo, other, xd)


def _group_scores(q_stack, keys, s_ref):
    c0 = 0
    for kt in keys:
        top = lax.broadcasted_iota(jnp.int32, kt.shape, 0) < HEAD_DIM
        zero = jnp.zeros_like(kt)
        c1 = c0 + kt.shape[1]
        s_ref[0, :, c0:c1] = _dot(q_stack, jnp.where(top, kt, zero))
        s_ref[1, :, c0:c1] = _dot(q_stack, jnp.where(top, zero, kt))
        c0 = c1


def _group_softmax_pv(values, sinks, band, s_ref, p_ref, sk_ref, rows_per_pair, chunk):
    rows = s_ref.shape[1]
    bounds = [0]
    for vd in values:
        bounds.append(bounds[-1] + vd.shape[0])
    sources = values

    for half in range(2):
        for r0 in range(0, rows, chunk):
            rs = slice(r0, r0 + chunk)
            sink = sinks[2 * (r0 // rows_per_pair) + half]
            parts = [s_ref[half, rs, bounds[i]:bounds[i + 1]] for i in range(len(sources))]
            if band is not None:
                parts[0] = jnp.where(band(r0 % rows_per_pair, chunk), parts[0], NEG_INF)
            m = sink
            for s in parts:
                m = jnp.maximum(m, jnp.max(s, axis=-1, keepdims=True))
            for i, s in enumerate(parts):
                p_ref[half, rs, bounds[i]:bounds[i + 1]] = jnp.exp2((s - m).astype(BF16))
            sk_ref[half, rs, :] = jnp.broadcast_to(jnp.exp2(sink - m), (chunk, LANES))

    o_e = o_o = None
    for i, vd in enumerate(values):
        v_lo, v_hi = _split_lo_hi(vd, 1.0)
        pe = _dot(p_ref[0, :, bounds[i]:bounds[i + 1]], v_lo)
        po = _dot(p_ref[1, :, bounds[i]:bounds[i + 1]], v_hi)
        o_e = pe if o_e is None else o_e + pe
        o_o = po if o_o is None else o_o + po
    lo = lax.broadcasted_iota(jnp.int32, (rows, LANES), 1) < HEAD_DIM
    num = jnp.where(lo, o_e, o_o)
    den = pltpu.roll(jnp.where(lo, o_o, o_e), HEAD_DIM, 1) + jnp.where(lo, sk_ref[0], sk_ref[1])
    return num / den


def _pair_cols(g):
    pa, pb = 2 * g, 2 * g + 1
    return slice(pa * LANES, (pa + 1) * LANES), slice(pb * LANES, (pb + 1) * LANES)


def _attn_tail_kernel(*refs, lat, n_tiles, tm, n, tiles_per_mod_row):
    sink_ref, q_ref, kd_ref, vd_ref = refs[:4]
    refs = refs[4:]
    if lat:
        ck_ref, cv_ref = refs[:2]
        refs = refs[2:]
    (x_ref, mod_ref, mix_ref, gate_ref, g2_ref,
     wf_ref, wao_ref, wout_ref, wup_ref, wdn_ref, o_ref, att_ref, s_ref, p_ref, sk_ref) = refs[:15]
    if lat:
        dist_ref, ckt_ref, cvd_ref = refs[15:]
    i = pl.program_id(0)

    def attention_stages():
        units = []
        if lat:
            qb = WINDOW
            span = qb + 2 * WINDOW
            t = jnp.minimum(i, n_tiles - 1) % (n // tm)
            ckt = ck_ref[0]
            ckt_ref[...] = jnp.concatenate(
                [ckt[0:HEAD_DIM], ckt[0:HEAD_DIM], ckt[HEAD_DIM:], ckt[HEAD_DIM:]], axis=0).astype(BF16)
            cvd_ref[...] = _dup_heads(cv_ref[0].T).astype(BF16)
            for rr in range(tm // qb):
                q0 = t * tm + rr * qb
                start = pl.multiple_of(jnp.clip(q0 - WINDOW, 0, n - span), WINDOW)
                dist_ref[rr] = jnp.abs((q0 - start) + lax.broadcasted_iota(jnp.int32, (qb, span), 0)
                                       - lax.broadcasted_iota(jnp.int32, (qb, span), 1))
                band = lambda r0, rows, rr=rr: dist_ref[rr, r0:r0 + rows, :] <= WINDOW
                rows = slice(rr * qb, (rr + 1) * qb)
                for g in range(N_KV_HEADS):
                    buf = len(units) % 2
                    cols = slice(g * LANES, (g + 1) * LANES)
                    qa, qbc = _pair_cols(g)

                    def scores(rows=rows, start=start, cols=cols, qa=qa, qbc=qbc, buf=buf):
                        q_stack = jnp.concatenate([q_ref[rows, qa], q_ref[rows, qbc]], axis=0)
                        _group_scores(q_stack, [kd_ref[cols, pl.ds(start, span)], ckt_ref[cols, :]],
                                      s_ref.at[buf])

                    def finish(rows=rows, start=start, g=g, cols=cols, qa=qa, qbc=qbc, buf=buf,
                               band=band):
                        sinks = [sink_ref[GROUP * g + h] * LOG2_E for h in range(GROUP)]
                        o = _group_softmax_pv(
                            [vd_ref[pl.ds(start, span), cols], cvd_ref[:, cols]], sinks, band,
                            s_ref.at[buf], p_ref.at[buf], sk_ref.at[buf], qb, LAT_SOFTMAX_ROWS)
                        att_ref[rows, qa] = o[0:qb].astype(BF16)
                        att_ref[rows, qbc] = o[qb:2 * qb].astype(BF16)

                    units.append((scores, finish))
        else:
            for s in range(tm // n):
                rows = slice(s * n, (s + 1) * n)
                for g in range(N_KV_HEADS):
                    buf = len(units) % 2
                    cols = slice(g * LANES, (g + 1) * LANES)
                    qa, qbc = _pair_cols(g)

                    def scores(rows=rows, cols=cols, qa=qa, qbc=qbc, buf=buf):
                        q_stack = jnp.concatenate([q_ref[rows, qa], q_ref[rows, qbc]], axis=0)
                        _group_scores(q_stack, [kd_ref[cols, rows]], s_ref.at[buf])

                    def finish(rows=rows, g=g, cols=cols, qa=qa, qbc=qbc, buf=buf):
                        sinks = [sink_ref[GROUP * g + h] * LOG2_E for h in range(GROUP)]
                        o = _group_softmax_pv([vd_ref[rows, cols]], sinks, None, s_ref.at[buf],
                                              p_ref.at[buf], sk_ref.at[buf], n, CTX_SOFTMAX_ROWS)
                        att_ref[rows, qa] = o[0:n].astype(BF16)
                        att_ref[rows, qbc] = o[n:2 * n].astype(BF16)

                    units.append((scores, finish))
        stages = [units[0][0]]
        for k, (_, finish) in enumerate(units):
            if k + 1 < len(units):
                stages.append(units[k + 1][0])
            stages.append(finish)
        return stages

    def tail(stages):
        gt1, sh2, sc2, gt2 = _mod_vectors(mod_ref, i - 1, tiles_per_mod_row)[2:6]
        yf = _dot(mix_ref[...], wf_ref[...])
        ya = _dot(att_ref[...], wao_ref[...])
        merged = (gate_ref[:, 0:D_MODEL].astype(F32) * yf
                  + gate_ref[:, D_MODEL:2 * D_MODEL].astype(F32) * ya)
        x1 = x_ref[...] + gt1 * _dot(merged.astype(BF16), wout_ref[...])

        h = x1 * lax.rsqrt(jnp.mean(x1 * x1, axis=-1, keepdims=True) + EPS) * g2_ref[...]
        hb = (h * (1.0 + sc2) + sh2).astype(BF16)
        d_ff = wdn_ref.shape[0]
        n_chunks = d_ff // FF_CHUNK
        acc = None
        issued = 0
        for c in range(n_chunks):
            a = _dot(hb, wup_ref[:, c * FF_CHUNK:(c + 1) * FF_CHUNK])
            u = _dot(hb, wup_ref[:, d_ff + c * FF_CHUNK:d_ff + (c + 1) * FF_CHUNK])
            act = (a * _sigmoid(a) * u).astype(BF16)
            d = _dot(act, wdn_ref[c * FF_CHUNK:(c + 1) * FF_CHUNK, :])
            acc = d if acc is None else acc + d
            while issued < len(stages) and issued * n_chunks < (c + 1) * len(stages):
                stages[issued]()
                issued += 1
        o_ref[...] = x1 + gt2 * acc

    @pl.when(i == 0)
    def _():
        for stage in attention_stages():
            stage()

    @pl.when(jnp.logical_and(i > 0, i < n_tiles))
    def _():
        tail(attention_stages())

    @pl.when(i == n_tiles)
    def _():
        tail([])


def _attn_tail(sinks, q, kd, vd, cache, x, mod3, mix, gate, g2, wf, wao, wout, wup, wdn,
               mod_row, rows_per_mod_row, n):
    t = x.shape[0]
    tm = TAIL_TOKEN_TILE
    n_tiles = t // tm
    lat = cache is not None
    tiles_per_mod_row = None if rows_per_mod_row is None else rows_per_mod_row // tm
    front = lambda i: (jnp.minimum(i, n_tiles - 1), 0)
    back = lambda i: (jnp.maximum(i - 1, 0), 0)
    in_specs = [pl.BlockSpec(memory_space=pltpu.SMEM), pl.BlockSpec((tm, ATTN_WIDTH), front)]
    args = [sinks, q, kd, vd]
    if lat:
        past = cache[0].shape[2]
        seq_of = lambda i: jnp.minimum(i, n_tiles - 1) // (n // tm)
        in_specs += [pl.BlockSpec((2 * KV_WIDTH, n), lambda i: (0, seq_of(i))),
                     pl.BlockSpec((n, 2 * KV_WIDTH), lambda i: (seq_of(i), 0))]
        in_specs += [pl.BlockSpec((1, KV_WIDTH, past), lambda i: (seq_of(i), 0, 0))] * 2
        args += list(cache)
        m_rows, keys = 2 * WINDOW, 3 * WINDOW + past
    else:
        in_specs += [pl.BlockSpec((2 * KV_WIDTH, tm), lambda i: (0, front(i)[0])),
                     pl.BlockSpec((tm, 2 * KV_WIDTH), front)]
        m_rows, keys = 2 * n, n
    in_specs += [pl.BlockSpec((tm, D_MODEL), back),
                 _mod_spec(mod_row),
                 pl.BlockSpec((tm, F_WIDTH), back),
                 pl.BlockSpec((tm, 2 * D_MODEL), back),
                 _resident((1, D_MODEL)),
                 _resident(wf.shape), _resident(wao.shape), _resident(wout.shape),
                 _resident(wup.shape), _resident(wdn.shape)]
    args += [x, mod3, mix, gate, g2, wf, wao, wout, wup, wdn]
    scratch = [pltpu.VMEM((tm, ATTN_WIDTH), BF16),
               pltpu.VMEM((2, 2, m_rows, keys), F32),
               pltpu.VMEM((2, 2, m_rows, keys), BF16),
               pltpu.VMEM((2, 2, m_rows, LANES), F32)]
    if lat:
        scratch += [pltpu.VMEM((tm // WINDOW, WINDOW, 3 * WINDOW), jnp.int32),
                    pltpu.VMEM((2 * KV_WIDTH, past), BF16),
                    pltpu.VMEM((past, 2 * KV_WIDTH), BF16)]
    return pl.pallas_call(
        functools.partial(_attn_tail_kernel, lat=lat, n_tiles=n_tiles, tm=tm, n=n,
                          tiles_per_mod_row=tiles_per_mod_row),
        grid=(n_tiles + 1,),
        in_specs=in_specs,
        out_specs=pl.BlockSpec((tm, D_MODEL), back),
        out_shape=pltpu.HBM((t, D_MODEL), F32),
        scratch_shapes=scratch,
        compiler_params=pltpu.CompilerParams(
            dimension_semantics=("arbitrary",), vmem_limit_bytes=_vmem_limit(58 * 2 ** 20)),
        name="attn_tail_lat" if lat else "attn_tail_ctx",
    )(sinks, *[_hbm(a) for a in args[1:]])


def _rope_tables(n):
    rows = n // GRID_W
    row = np.repeat(np.arange(rows, dtype=np.float64), GRID_W)
    col = np.tile(np.arange(GRID_W, dtype=np.float64), rows)
    axis_dim = HEAD_DIM // 2
    inv_freq = ROPE_THETA ** (-np.arange(0, axis_dim, 2, dtype=np.float64) / axis_dim)

    def axis_tabs(pos):
        ang = pos[:, None] * inv_freq[None, :]
        cos, sin = np.cos(ang), np.sin(ang)
        return np.concatenate([cos, cos], axis=-1), np.concatenate([-sin, sin], axis=-1)

    cr, sr = axis_tabs(row)
    cc, sc = axis_tabs(col)
    cos = np.tile(np.concatenate([cr, cc], axis=-1), (1, LANES // HEAD_DIM))
    sin = np.tile(np.concatenate([sr, sc], axis=-1), (1, LANES // HEAD_DIM))
    return jnp.asarray(cos.astype(np.float32)), jnp.asarray(sin.astype(np.float32))


@jax.jit
def _layer(xp, xs, ck, cv, c, c_ctx, w_ada, b_ada, g_norm1, g_norm2, w_in, g_q, g_k, sink,
           w_f, w_ao, w_out, w_up, w_down):
    bp, sp, _ = xp.shape
    bs, ss, _ = xs.shape
    past = ck.shape[1]

    assert bs <= CTX_MOD_ROW, "latent conditioning rows must fit below the context row"
    cvec = jnp.concatenate(
        [c, jnp.zeros((CTX_MOD_ROW - bs, D_MODEL), F32), c_ctx[None, :],
         jnp.zeros((MOD_ROWS - CTX_MOD_ROW - 1, D_MODEL), F32)], axis=0)
    mod3, w_in_b = _ada(cvec, w_ada, b_ada[None, :], w_in)
    g1, g2 = g_norm1[None, :], g_norm2[None, :]
    gq2 = jnp.tile(g_q, LANES // HEAD_DIM)[None, :]
    gk2 = jnp.tile(g_k, LANES // HEAD_DIM)[None, :]
    ctx_mod = (CTX_MOD_ROW, None)
    lat_mod = (0, ss)

    xp2 = xp.reshape(bp * sp, D_MODEL)
    xs2 = xs.reshape(bs * ss, D_MODEL)
    zf, q, kd, vd, gate, k_new, v_new, wup = _proj(
        xp2, mod3, g1, w_in_b, gq2, gk2, None, *ctx_mod, sp, [w_up])
    zf_s, q_s, kd_s, vd_s, gate_s, wf, wao, wout, wdn = _proj(
        xs2, mod3, g1, w_in_b, gq2, gk2, _rope_tables(ss), *lat_mod, 0, [w_f, w_ao, w_out, w_down])

    mix = _fourier(zf.reshape(bp, sp, F_WIDTH), CTX_SEQS_PER_STEP)
    yp = _attn_tail(sink, q, kd, vd, None, xp2, mod3, mix.reshape(bp * sp, F_WIDTH), gate, g2,
                    wf, wao, wout, wup, wdn, *ctx_mod, sp)

    def cache_t(t):
        return t.transpose(0, 2, 3, 1).reshape(bs, KV_WIDTH, past)

    zf, q, kd, vd, gate = zf_s, q_s, kd_s, vd_s, gate_s
    mix = _fourier(zf.reshape(bs, ss, F_WIDTH), 1)
    ys = _attn_tail(sink, q, kd, vd, (cache_t(ck), cache_t(cv)), xs2, mod3,
                    mix.reshape(bs * ss, F_WIDTH), gate, g2, wf, wao, wout, wup, wdn, *lat_mod, ss)

    def cache_layout(t):
        return t.reshape(bp, N_KV_HEADS, HEAD_DIM, sp).transpose(0, 3, 1, 2)

    return yp.reshape(xp.shape), ys.reshape(xs.shape), cache_layout(k_new), cache_layout(v_new)


def kernel(x_prompt, x_sample, cache_k, cache_v, c, c_ctx, w_ada, b_ada, g_norm1, g_norm2,
           w_in, g_q, g_k, sinks, w_f, w_ao, w_out, w_up, w_down):
    depth = w_in.shape[0]
    xp, xs = x_prompt, x_sample
    new_k, new_v = [], []
    for l in range(depth):
        xp, xs, k_ctx, v_ctx = _layer(
            xp, xs, cache_k[:, l], cache_v[:, l], c, c_ctx, w_ada[l], b_ada[l], g_norm1[l],
            g_norm2[l], w_in[l], g_q[l], g_k[l], sinks[l], w_f[l], w_ao[l], w_out[l], w_up[l],
            w_down[l])
        new_k.append(k_ctx)
        new_v.append(v_ctx)
    return (xp, xs, jnp.stack(new_k, axis=1), jnp.stack(new_v, axis=1))
```

```python
import functools

import numpy as np
import jax
import jax.numpy as jnp
from jax import lax
from jax.experimental import pallas as pl
from jax.experimental.pallas import tpu as pltpu

D_MODEL = 1024
GRID_W = 64
N_HEADS = 8
N_KV_HEADS = 2
GROUP = N_HEADS // N_KV_HEADS
HEAD_DIM = 64
WINDOW = 128
F_GROUPS = 4
F_GROUP_DIM = 128
F_WIDTH = F_GROUPS * F_GROUP_DIM
ATTN_WIDTH = N_HEADS * HEAD_DIM
KV_WIDTH = N_KV_HEADS * HEAD_DIM
ROPE_THETA = 10000.0
EPS = 1e-6
NEG_INF = -1e30
LOG2_E = 1.4426950408889634

LANES = 128
MXU_DIM = 256
VMEM_BYTES_V7X = 64 * 1024 * 1024

_Q0 = F_WIDTH
_K0 = _Q0 + ATTN_WIDTH
_V0 = _K0 + KV_WIDTH
_G0 = _V0 + KV_WIDTH
IN_WIDTH = _G0 + 2 * D_MODEL

SUBLANES = 8
N_MOD = 6
MOD_ROWS = 16
CTX_MOD_ROW = 8
ADA_STEPS = 8

PROJ_TOKEN_TILE = 1024
TAIL_TOKEN_TILE = 512
FF_CHUNK = MXU_DIM
GATE_CHUNK = 2 * MXU_DIM
CTX_SEQS_PER_STEP = 4
CTX_SOFTMAX_ROWS = 64
LAT_SOFTMAX_ROWS = 32

F32 = jnp.float32
BF16 = jnp.bfloat16


RESIDENT_WEIGHTS_VMEM = VMEM_BYTES_V7X - 6 * 2 ** 20
FOURIER_VMEM = VMEM_BYTES_V7X // 2


def _dot(a, b):
    return jnp.dot(a, b, preferred_element_type=F32)


def _sigmoid(x):
    return 1.0 / (1.0 + jnp.exp(-x))


def _resident(shape):
    zeros = (0,) * len(shape)
    return pl.BlockSpec(shape, lambda *_: zeros, pipeline_mode=pl.Buffered(1))


def _hbm(x):
    return pltpu.with_memory_space_constraint(x, pltpu.HBM)


def _run_units(units):
    units[0][0]()
    for k, (_, second) in enumerate(units):
        if k + 1 < len(units):
            units[k + 1][0]()
        second()


def _ada_kernel(c_ref, w_ref, b_ref, win_ref, o_ref, winb_ref):
    c = c_ref[...]
    s = c * _sigmoid(c)
    o_ref[...] = _dot(s.astype(BF16), w_ref[...].astype(BF16)) + b_ref[...]
    winb_ref[...] = win_ref[...].astype(BF16)


def _ada(cvec, w_ada, b_ada, w_in):
    n = w_ada.shape[1]
    bn = n // ADA_STEPS
    slab = pl.BlockSpec((w_in.shape[0] // ADA_STEPS, w_in.shape[1]), lambda j: (j, 0))
    return pl.pallas_call(
        _ada_kernel,
        grid=(ADA_STEPS,),
        in_specs=[pl.BlockSpec((MOD_ROWS, D_MODEL), lambda j: (0, 0)),
                  pl.BlockSpec((D_MODEL, bn), lambda j: (0, j)),
                  pl.BlockSpec((1, bn), lambda j: (0, j)),
                  slab],
        out_specs=[pl.BlockSpec((MOD_ROWS, bn), lambda j: (0, j)), slab],
        out_shape=[pltpu.HBM((MOD_ROWS, n), F32), pltpu.HBM(w_in.shape, BF16)],
        name="ada",
    )(cvec, w_ada, b_ada, w_in)


def _mod_spec(mod_row):
    return pl.BlockSpec((SUBLANES, N_MOD * D_MODEL), lambda i: (mod_row // SUBLANES, 0))


def _mod_vectors(mod_ref, tile, tiles_per_row):
    cols = [slice(j * D_MODEL, (j + 1) * D_MODEL) for j in range(N_MOD)]
    if tiles_per_row is None:
        return [mod_ref[0:1, c] for c in cols]
    r = tile // tiles_per_row
    return [mod_ref[pl.ds(r, 1), c] for c in cols]


def _head_norm(z, g):
    lo = lax.broadcasted_iota(jnp.int32, z.shape, 1) < HEAD_DIM
    s = z * z
    s_lo = jnp.sum(jnp.where(lo, s, 0.0), axis=-1, keepdims=True)
    s_hi = jnp.sum(jnp.where(lo, 0.0, s), axis=-1, keepdims=True)
    ms = jnp.where(lo, s_lo, s_hi) * (1.0 / HEAD_DIM)
    return z * lax.rsqrt(ms + EPS) * g


def _rope(y, cos, sin):
    lane = lax.broadcasted_iota(jnp.int32, y.shape, 1)
    first = (lane % 32) < 16
    partner = jnp.where(first, pltpu.roll(y, LANES - 16, 1), pltpu.roll(y, 16, 1))
    return y * cos + partner * sin


def _dup_heads(y):
    lo = lax.broadcasted_iota(jnp.int32, y.shape, 1) < HEAD_DIM
    sw = pltpu.roll(y, HEAD_DIM, 1)
    return jnp.concatenate([jnp.where(lo, y, sw), jnp.where(lo, sw, y)], axis=-1)


def _proj_kernel(*refs, rope, cache_seq, n_cast, tiles_per_mod_row):
    x_ref, mod_ref, g1_ref, w_ref, gq_ref, gk_ref = refs[:6]
    refs = refs[6:]
    if rope:
        cos_ref, sin_ref = refs[:2]
        refs = refs[2:]
    cast_in, refs = refs[:n_cast], refs[n_cast:]
    zf_ref, q_ref, kd_ref, vd_ref, gate_ref = refs[:5]
    refs = refs[5:]
    if cache_seq:
        kc_ref, vc_ref = refs[:2]
        refs = refs[2:]
    for src_ref, dst_ref in zip(cast_in, refs):
        dst_ref[...] = src_ref[...].astype(BF16)

    x = x_ref[...]
    sh1, sc1 = _mod_vectors(mod_ref, pl.program_id(0), tiles_per_mod_row)[0:2]
    h = x * lax.rsqrt(jnp.mean(x * x, axis=-1, keepdims=True) + EPS) * g1_ref[...]
    hb = (h * (1.0 + sc1) + sh1).astype(BF16)

    zq = _dot(hb, w_ref[:, _Q0:_K0])
    for j in range(ATTN_WIDTH // LANES):
        y = _head_norm(zq[:, j * LANES:(j + 1) * LANES], gq_ref[...])
        if rope:
            y = _rope(y, cos_ref[...], sin_ref[...])
        q_ref[:, j * LANES:(j + 1) * LANES] = (y * (HEAD_DIM ** -0.5 * LOG2_E)).astype(BF16)

    zkv = _dot(hb, w_ref[:, _K0:_G0])
    k = _head_norm(zkv[:, 0:KV_WIDTH], gk_ref[...])
    v = zkv[:, KV_WIDTH:2 * KV_WIDTH]
    if cache_seq:
        for s in range(x.shape[0] // cache_seq):
            rows = slice(s * cache_seq, (s + 1) * cache_seq)
            kc_ref[s] = k[rows, :].T
            vc_ref[s] = v[rows, :].T
    if rope:
        k = _rope(k, cos_ref[...], sin_ref[...])
    kt = k.T
    kd_ref[...] = jnp.concatenate(
        [kt[0:HEAD_DIM], kt[0:HEAD_DIM], kt[HEAD_DIM:], kt[HEAD_DIM:]], axis=0).astype(BF16)
    vd_ref[...] = _dup_heads(v).astype(BF16)

    for j in range(2 * D_MODEL // GATE_CHUNK):
        cols = slice(j * GATE_CHUNK, (j + 1) * GATE_CHUNK)
        zg = _dot(hb, w_ref[:, _G0 + cols.start:_G0 + cols.stop])
        gate_ref[:, cols] = _sigmoid(zg).astype(BF16)

    zf_ref[...] = _dot(hb, w_ref[:, 0:_Q0]).astype(BF16)


def _proj(x, mod3, g1, w_in_b, gq2, gk2, rope_tabs, mod_row, rows_per_mod_row, cache_seq,
          cast_weights):
    t = x.shape[0]
    tm = PROJ_TOKEN_TILE
    steps = t // tm
    rope = rope_tabs is not None
    row = lambda i: (i, 0)
    tiles_per_mod_row = None if rows_per_mod_row is None else rows_per_mod_row // tm
    in_specs = [pl.BlockSpec((tm, D_MODEL), row),
                _mod_spec(mod_row),
                _resident((1, D_MODEL)),
                _resident((D_MODEL, IN_WIDTH)),
                _resident((1, LANES)),
                _resident((1, LANES))]
    args = [x, mod3, g1, w_in_b, gq2, gk2]
    if rope:
        tiles_per_seq = rope_tabs[0].shape[0] // tm
        tab = pl.BlockSpec((tm, LANES), lambda i: (i % tiles_per_seq, 0))
        in_specs += [tab, tab]
        args += list(rope_tabs)
    slabs = [pl.BlockSpec((w.shape[0] // steps, w.shape[1]), row) for w in cast_weights]
    in_specs += slabs
    args += list(cast_weights)
    out_specs = [pl.BlockSpec((tm, F_WIDTH), row),
                 pl.BlockSpec((tm, ATTN_WIDTH), row),
                 pl.BlockSpec((2 * KV_WIDTH, tm), lambda i: (0, i)),
                 pl.BlockSpec((tm, 2 * KV_WIDTH), row),
                 pl.BlockSpec((tm, 2 * D_MODEL), row)]
    out_shape = [pltpu.HBM((t, F_WIDTH), BF16),
                 pltpu.HBM((t, ATTN_WIDTH), BF16),
                 pltpu.HBM((2 * KV_WIDTH, t), BF16),
                 pltpu.HBM((t, 2 * KV_WIDTH), BF16),
                 pltpu.HBM((t, 2 * D_MODEL), BF16)]
    if cache_seq:
        seqs = tm // cache_seq
        out_specs += [pl.BlockSpec((seqs, KV_WIDTH, cache_seq), lambda i: (i, 0, 0))] * 2
        out_shape += [pltpu.HBM((t // cache_seq, KV_WIDTH, cache_seq), F32)] * 2
    out_specs += slabs
    out_shape += [pltpu.HBM(w.shape, BF16) for w in cast_weights]
    return pl.pallas_call(
        functools.partial(_proj_kernel, rope=rope, cache_seq=cache_seq, n_cast=len(cast_weights),
                          tiles_per_mod_row=tiles_per_mod_row),
        grid=(steps,),
        in_specs=in_specs,
        out_specs=out_specs,
        out_shape=out_shape,
        compiler_params=pltpu.CompilerParams(
            dimension_semantics=("arbitrary",), vmem_limit_bytes=RESIDENT_WEIGHTS_VMEM),
        name="proj_lat" if rope else "proj_ctx",
    )(*[_hbm(a) for a in args])


def _dft_tables(n, fold):
    def cs(m):
        idx = np.arange(m, dtype=np.int64)
        ang = 2.0 * np.pi * ((idx[:, None] * idx[None, :]) % m).astype(np.float64) / m
        return np.cos(ang) / np.sqrt(m), np.sin(ang) / np.sqrt(m)
    cd, sd = cs(F_GROUP_DIM)
    cn, sn = cs(n)
    tabs = [np.concatenate([cd, sd], axis=1)]
    if fold:
        half = n // 2
        rev = np.zeros((half, half), np.float64)
        rev[np.arange(1, half), half - np.arange(1, half)] = 1.0
        tabs += [cn[:half], -sn[:half], rev]
    else:
        tabs += [np.concatenate([cn, -sn], axis=1)]
    return tuple(jnp.asarray(t.astype(np.float32)).astype(BF16) for t in tabs)


def _fourier_kernel(zf_ref, csd_ref, *refs, seqs, n, fold):
    if fold:
        ch_ref, sh_ref, rev_ref, o_ref, ab_ref = refs
    else:
        csn_ref, o_ref, ab_ref = refs
    half = n // 2
    units = []
    for s in range(seqs):
        buf = s % 2

        def channels(s=s, buf=buf):
            for g in range(F_GROUPS):
                cols = slice(g * F_GROUP_DIM, (g + 1) * F_GROUP_DIM)
                ab = _dot(zf_ref[s, :, cols], csd_ref[...])
                ab_ref[buf, 0:n, cols] = ab[:, 0:F_GROUP_DIM].astype(BF16)
                ab_ref[buf, n:2 * n, cols] = ab[:, F_GROUP_DIM:].astype(BF16)

        def positions(s=s, buf=buf):
            if not fold:
                o_ref[s] = _dot(csn_ref[...], ab_ref[buf]).astype(BF16)
                return
            p = _dot(ch_ref[...], ab_ref[buf, 0:n, :])
            q = _dot(sh_ref[...], ab_ref[buf, n:2 * n, :])
            o_ref[s, 0:half, :] = (p + q).astype(BF16)
            upper = _dot(rev_ref[...], (p - q).astype(BF16))
            a = ab_ref[buf, 0:n, :].astype(F32)
            even = lax.broadcasted_iota(jnp.int32, a.shape, 0) % 2 == 0
            mid = jnp.sum(jnp.where(even, a, -a), axis=0, keepdims=True) * (n ** -0.5)
            first = lax.broadcasted_iota(jnp.int32, upper.shape, 0) == 0
            o_ref[s, half:n, :] = jnp.where(first, mid, upper).astype(BF16)

        units.append((channels, positions))
    _run_units(units)


def _fourier(zf3, seqs):
    b, n, _ = zf3.shape
    fold = n // 2 >= 2 * MXU_DIM
    tabs = _dft_tables(n, fold)
    blk = pl.BlockSpec((seqs, n, F_WIDTH), lambda i: (i, 0, 0))
    return pl.pallas_call(
        functools.partial(_fourier_kernel, seqs=seqs, n=n, fold=fold),
        grid=(b // seqs,),
        in_specs=[blk] + [_resident(t.shape) for t in tabs],
        out_specs=blk,
        out_shape=pltpu.HBM(zf3.shape, BF16),
        scratch_shapes=[pltpu.VMEM((2, 2 * n, F_WIDTH), BF16)],
        compiler_params=pltpu.CompilerParams(
            dimension_semantics=("arbitrary",), vmem_limit_bytes=FOURIER_VMEM),
        name=f"fourier_{n}",
    )(_hbm(zf3), *tabs)


def _split_lo_hi(xd, fill):
    lo = lax.broadcasted_iota(jnp.int32, xd.shape, 1) < HEAD_DIM
    other = jnp.full_like(xd, fill)
    return jnp.where(lo, xd, other), jnp.where(lo, other, xd)


def _group_scores(q_stack, keys, s_ref):
    c0 = 0
    for kt in keys:
        top = lax.broadcasted_iota(jnp.int32, kt.shape, 0) < HEAD_DIM
        zero = jnp.zeros_like(kt)
        c1 = c0 + kt.shape[1]
        s_ref[0, :, c0:c1] = _dot(q_stack, jnp.where(top, kt, zero))
        s_ref[1, :, c0:c1] = _dot(q_stack, jnp.where(top, zero, kt))
        c0 = c1


def _group_softmax_pv(values, sinks, band, s_ref, p_ref, sk_ref, rows_per_pair, chunk):
    rows = s_ref.shape[1]
    bounds = [0]
    for vd in values:
        bounds.append(bounds[-1] + vd.shape[0])
    sources = values

    for half in range(2):
        for r0 in range(0, rows, chunk):
            rs = slice(r0, r0 + chunk)
            sink = sinks[2 * (r0 // rows_per_pair) + half]
            parts = [s_ref[half, rs, bounds[i]:bounds[i + 1]] for i in range(len(sources))]
            if band is not None:
                parts[0] = jnp.where(band(r0 % rows_per_pair, chunk), parts[0], NEG_INF)
            m = sink
            for s in parts:
                m = jnp.maximum(m, jnp.max(s, axis=-1, keepdims=True))
            for i, s in enumerate(parts):
                p_ref[half, rs, bounds[i]:bounds[i + 1]] = jnp.exp2(s - m).astype(BF16)
            sk_ref[half, rs, :] = jnp.broadcast_to(jnp.exp2(sink - m), (chunk, LANES))

    o_e = o_o = None
    for i, vd in enumerate(values):
        v_lo, v_hi = _split_lo_hi(vd, 1.0)
        pe = _dot(p_ref[0, :, bounds[i]:bounds[i + 1]], v_lo)
        po = _dot(p_ref[1, :, bounds[i]:bounds[i + 1]], v_hi)
        o_e = pe if o_e is None else o_e + pe
        o_o = po if o_o is None else o_o + po
    lo = lax.broadcasted_iota(jnp.int32, (rows, LANES), 1) < HEAD_DIM
    num = jnp.where(lo, o_e, o_o)
    den = pltpu.roll(jnp.where(lo, o_o, o_e), HEAD_DIM, 1) + jnp.where(lo, sk_ref[0], sk_ref[1])
    return num / den


def _pair_cols(g):
    pa, pb = 2 * g, 2 * g + 1
    return slice(pa * LANES, (pa + 1) * LANES), slice(pb * LANES, (pb + 1) * LANES)


def _attn_tail_kernel(*refs, lat, n_tiles, tm, n, tiles_per_mod_row):
    sink_ref, q_ref, kd_ref, vd_ref = refs[:4]
    refs = refs[4:]
    if lat:
        ck_ref, cv_ref = refs[:2]
        refs = refs[2:]
    (x_ref, mod_ref, mix_ref, gate_ref, g2_ref,
     wf_ref, wao_ref, wout_ref, wup_ref, wdn_ref, o_ref, att_ref, s_ref, p_ref, sk_ref) = refs[:15]
    if lat:
        dist_ref, ckt_ref, cvd_ref = refs[15:]
    i = pl.program_id(0)

    def attention_stages():
        units = []
        if lat:
            qb = WINDOW
            span = qb + 2 * WINDOW
            t = jnp.minimum(i, n_tiles - 1) % (n // tm)
            ckt = ck_ref[0]
            ckt_ref[...] = jnp.concatenate(
                [ckt[0:HEAD_DIM], ckt[0:HEAD_DIM], ckt[HEAD_DIM:], ckt[HEAD_DIM:]], axis=0).astype(BF16)
            cvd_ref[...] = _dup_heads(cv_ref[0].T).astype(BF16)
            for rr in range(tm // qb):
                q0 = t * tm + rr * qb
                start = pl.multiple_of(jnp.clip(q0 - WINDOW, 0, n - span), WINDOW)
                dist_ref[rr] = jnp.abs((q0 - start) + lax.broadcasted_iota(jnp.int32, (qb, span), 0)
                                       - lax.broadcasted_iota(jnp.int32, (qb, span), 1))
                band = lambda r0, rows, rr=rr: dist_ref[rr, r0:r0 + rows, :] <= WINDOW
                rows = slice(rr * qb, (rr + 1) * qb)
                for g in range(N_KV_HEADS):
                    buf = len(units) % 2
                    cols = slice(g * LANES, (g + 1) * LANES)
                    qa, qbc = _pair_cols(g)

                    def scores(rows=rows, start=start, cols=cols, qa=qa, qbc=qbc, buf=buf):
                        q_stack = jnp.concatenate([q_ref[rows, qa], q_ref[rows, qbc]], axis=0)
                        _group_scores(q_stack, [kd_ref[cols, pl.ds(start, span)], ckt_ref[cols, :]],
                                      s_ref.at[buf])

                    def finish(rows=rows, start=start, g=g, cols=cols, qa=qa, qbc=qbc, buf=buf,
                               band=band):
                        sinks = [sink_ref[GROUP * g + h] * LOG2_E for h in range(GROUP)]
                        o = _group_softmax_pv(
                            [vd_ref[pl.ds(start, span), cols], cvd_ref[:, cols]], sinks, band,
                            s_ref.at[buf], p_ref.at[buf], sk_ref.at[buf], qb, LAT_SOFTMAX_ROWS)
                        att_ref[rows, qa] = o[0:qb].astype(BF16)
                        att_ref[rows, qbc] = o[qb:2 * qb].astype(BF16)

                    units.append((scores, finish))
        else:
            for s in range(tm // n):
                rows = slice(s * n, (s + 1) * n)
                for g in range(N_KV_HEADS):
                    buf = len(units) % 2
                    cols = slice(g * LANES, (g + 1) * LANES)
                    qa, qbc = _pair_cols(g)

                    def scores(rows=rows, cols=cols, qa=qa, qbc=qbc, buf=buf):
                        q_stack = jnp.concatenate([q_ref[rows, qa], q_ref[rows, qbc]], axis=0)
                        _group_scores(q_stack, [kd_ref[cols, rows]], s_ref.at[buf])

                    def finish(rows=rows, g=g, cols=cols, qa=qa, qbc=qbc, buf=buf):
                        sinks = [sink_ref[GROUP * g + h] * LOG2_E for h in range(GROUP)]
                        o = _group_softmax_pv([vd_ref[rows, cols]], sinks, None, s_ref.at[buf],
                                              p_ref.at[buf], sk_ref.at[buf], n, CTX_SOFTMAX_ROWS)
                        att_ref[rows, qa] = o[0:n].astype(BF16)
                        att_ref[rows, qbc] = o[n:2 * n].astype(BF16)

                    units.append((scores, finish))
        stages = [units[0][0]]
        for k, (_, finish) in enumerate(units):
            if k + 1 < len(units):
                stages.append(units[k + 1][0])
            stages.append(finish)
        return stages

    def tail(stages):
        gt1, sh2, sc2, gt2 = _mod_vectors(mod_ref, i - 1, tiles_per_mod_row)[2:6]
        yf = _dot(mix_ref[...], wf_ref[...])
        ya = _dot(att_ref[...], wao_ref[...])
        merged = (gate_ref[:, 0:D_MODEL].astype(F32) * yf
                  + gate_ref[:, D_MODEL:2 * D_MODEL].astype(F32) * ya)
        x1 = x_ref[...] + gt1 * _dot(merged.astype(BF16), wout_ref[...])

        h = x1 * lax.rsqrt(jnp.mean(x1 * x1, axis=-1, keepdims=True) + EPS) * g2_ref[...]
        hb = (h * (1.0 + sc2) + sh2).astype(BF16)
        d_ff = wdn_ref.shape[0]
        n_chunks = d_ff // FF_CHUNK
        acc = None
        issued = 0
        for c in range(n_chunks):
            a = _dot(hb, wup_ref[:, c * FF_CHUNK:(c + 1) * FF_CHUNK])
            u = _dot(hb, wup_ref[:, d_ff + c * FF_CHUNK:d_ff + (c + 1) * FF_CHUNK])
            act = (a * _sigmoid(a) * u).astype(BF16)
            d = _dot(act, wdn_ref[c * FF_CHUNK:(c + 1) * FF_CHUNK, :])
            acc = d if acc is None else acc + d
            while issued < len(stages) and issued * n_chunks < (c + 1) * len(stages):
                stages[issued]()
                issued += 1
        o_ref[...] = x1 + gt2 * acc

    @pl.when(i == 0)
    def _():
        for stage in attention_stages():
            stage()

    @pl.when(jnp.logical_and(i > 0, i < n_tiles))
    def _():
        tail(attention_stages())

    @pl.when(i == n_tiles)
    def _():
        tail([])


def _attn_tail(sinks, q, kd, vd, cache, x, mod3, mix, gate, g2, wf, wao, wout, wup, wdn,
               mod_row, rows_per_mod_row, n):
    t = x.shape[0]
    tm = TAIL_TOKEN_TILE
    n_tiles = t // tm
    lat = cache is not None
    tiles_per_mod_row = None if rows_per_mod_row is None else rows_per_mod_row // tm
    front = lambda i: (jnp.minimum(i, n_tiles - 1), 0)
    back = lambda i: (jnp.maximum(i - 1, 0), 0)
    in_specs = [pl.BlockSpec(memory_space=pltpu.SMEM), pl.BlockSpec((tm, ATTN_WIDTH), front)]
    args = [sinks, q, kd, vd]
    if lat:
        past = cache[0].shape[2]
        seq_of = lambda i: jnp.minimum(i, n_tiles - 1) // (n // tm)
        in_specs += [pl.BlockSpec((2 * KV_WIDTH, n), lambda i: (0, seq_of(i))),
                     pl.BlockSpec((n, 2 * KV_WIDTH), lambda i: (seq_of(i), 0))]
        in_specs += [pl.BlockSpec((1, KV_WIDTH, past), lambda i: (seq_of(i), 0, 0))] * 2
        args += list(cache)
        m_rows, keys = 2 * WINDOW, 3 * WINDOW + past
    else:
        in_specs += [pl.BlockSpec((2 * KV_WIDTH, tm), lambda i: (0, front(i)[0])),
                     pl.BlockSpec((tm, 2 * KV_WIDTH), front)]
        m_rows, keys = 2 * n, n
    in_specs += [pl.BlockSpec((tm, D_MODEL), back),
                 _mod_spec(mod_row),
                 pl.BlockSpec((tm, F_WIDTH), back),
                 pl.BlockSpec((tm, 2 * D_MODEL), back),
                 _resident((1, D_MODEL)),
                 _resident(wf.shape), _resident(wao.shape), _resident(wout.shape),
                 _resident(wup.shape), _resident(wdn.shape)]
    args += [x, mod3, mix, gate, g2, wf, wao, wout, wup, wdn]
    scratch = [pltpu.VMEM((tm, ATTN_WIDTH), BF16),
               pltpu.VMEM((2, 2, m_rows, keys), F32),
               pltpu.VMEM((2, 2, m_rows, keys), BF16),
               pltpu.VMEM((2, 2, m_rows, LANES), F32)]
    if lat:
        scratch += [pltpu.VMEM((tm // WINDOW, WINDOW, 3 * WINDOW), jnp.int32),
                    pltpu.VMEM((2 * KV_WIDTH, past), BF16),
                    pltpu.VMEM((past, 2 * KV_WIDTH), BF16)]
    return pl.pallas_call(
        functools.partial(_attn_tail_kernel, lat=lat, n_tiles=n_tiles, tm=tm, n=n,
                          tiles_per_mod_row=tiles_per_mod_row),
        grid=(n_tiles + 1,),
        in_specs=in_specs,
        out_specs=pl.BlockSpec((tm, D_MODEL), back),
        out_shape=pltpu.HBM((t, D_MODEL), F32),
        scratch_shapes=scratch,
        compiler_params=pltpu.CompilerParams(
            dimension_semantics=("arbitrary",), vmem_limit_bytes=RESIDENT_WEIGHTS_VMEM),
        name="attn_tail_lat" if lat else "attn_tail_ctx",
    )(sinks, *[_hbm(a) for a in args[1:]])


def _rope_tables(n):
    rows = n // GRID_W
    row = np.repeat(np.arange(rows, dtype=np.float64), GRID_W)
    col = np.tile(np.arange(GRID_W, dtype=np.float64), rows)
    axis_dim = HEAD_DIM // 2
    inv_freq = ROPE_THETA ** (-np.arange(0, axis_dim, 2, dtype=np.float64) / axis_dim)

    def axis_tabs(pos):
        ang = pos[:, None] * inv_freq[None, :]
        cos, sin = np.cos(ang), np.sin(ang)
        return np.concatenate([cos, cos], axis=-1), np.concatenate([-sin, sin], axis=-1)

    cr, sr = axis_tabs(row)
    cc, sc = axis_tabs(col)
    cos = np.tile(np.concatenate([cr, cc], axis=-1), (1, LANES // HEAD_DIM))
    sin = np.tile(np.concatenate([sr, sc], axis=-1), (1, LANES // HEAD_DIM))
    return jnp.asarray(cos.astype(np.float32)), jnp.asarray(sin.astype(np.float32))


@jax.jit
def _layer(xp, xs, ck, cv, c, c_ctx, w_ada, b_ada, g_norm1, g_norm2, w_in, g_q, g_k, sink,
           w_f, w_ao, w_out, w_up, w_down):
    bp, sp, _ = xp.shape
    bs, ss, _ = xs.shape
    past = ck.shape[1]

    assert bs <= CTX_MOD_ROW, "latent conditioning rows must fit below the context row"
    cvec = jnp.concatenate(
        [c, jnp.zeros((CTX_MOD_ROW - bs, D_MODEL), F32), c_ctx[None, :],
         jnp.zeros((MOD_ROWS - CTX_MOD_ROW - 1, D_MODEL), F32)], axis=0)
    mod3, w_in_b = _ada(cvec, w_ada, b_ada[None, :], w_in)
    g1, g2 = g_norm1[None, :], g_norm2[None, :]
    gq2 = jnp.tile(g_q, LANES // HEAD_DIM)[None, :]
    gk2 = jnp.tile(g_k, LANES // HEAD_DIM)[None, :]
    ctx_mod = (CTX_MOD_ROW, None)
    lat_mod = (0, ss)

    xp2 = xp.reshape(bp * sp, D_MODEL)
    xs2 = xs.reshape(bs * ss, D_MODEL)
    zf, q, kd, vd, gate, k_new, v_new, wup = _proj(
        xp2, mod3, g1, w_in_b, gq2, gk2, None, *ctx_mod, sp, [w_up])
    zf_s, q_s, kd_s, vd_s, gate_s, wf, wao, wout, wdn = _proj(
        xs2, mod3, g1, w_in_b, gq2, gk2, _rope_tables(ss), *lat_mod, 0, [w_f, w_ao, w_out, w_down])

    mix = _fourier(zf.reshape(bp, sp, F_WIDTH), CTX_SEQS_PER_STEP)
    yp = _attn_tail(sink, q, kd, vd, None, xp2, mod3, mix.reshape(bp * sp, F_WIDTH), gate, g2,
                    wf, wao, wout, wup, wdn, *ctx_mod, sp)

    def cache_t(t):
        return t.transpose(0, 2, 3, 1).reshape(bs, KV_WIDTH, past)

    zf, q, kd, vd, gate = zf_s, q_s, kd_s, vd_s, gate_s
    mix = _fourier(zf.reshape(bs, ss, F_WIDTH), 1)
    ys = _attn_tail(sink, q, kd, vd, (cache_t(ck), cache_t(cv)), xs2, mod3,
                    mix.reshape(bs * ss, F_WIDTH), gate, g2, wf, wao, wout, wup, wdn, *lat_mod, ss)

    def cache_layout(t):
        return t.reshape(bp, N_KV_HEADS, HEAD_DIM, sp).transpose(0, 3, 1, 2)

    return yp.reshape(xp.shape), ys.reshape(xs.shape), cache_layout(k_new), cache_layout(v_new)


def kernel(x_prompt, x_sample, cache_k, cache_v, c, c_ctx, w_ada, b_ada, g_norm1, g_norm2,
           w_in, g_q, g_k, sinks, w_f, w_ao, w_out, w_up, w_down):
    depth = w_in.shape[0]
    xp, xs = x_prompt, x_sample
    new_k, new_v = [], []
    for l in range(depth):
        xp, xs, k_ctx, v_ctx = _layer(
            xp, xs, cache_k[:, l], cache_v[:, l], c, c_ctx, w_ada[l], b_ada[l], g_norm1[l],
            g_norm2[l], w_in[l], g_q[l], g_k[l], sinks[l], w_f[l], w_ao[l], w_out[l], w_up[l],
            w_down[l])
        new_k.append(k_ctx)
        new_v.append(v_ctx)
    return (xp, xs, jnp.stack(new_k, axis=1), jnp.stack(new_v, axis=1))
```

```python
import functools

import numpy as np
import jax
import jax.numpy as jnp
from jax import lax
from jax.experimental import pallas as pl
from jax.experimental.pallas import tpu as pltpu

D_MODEL = 1024
GRID_W = 64
N_HEADS = 8
N_KV_HEADS = 2
GROUP = N_HEADS // N_KV_HEADS
HEAD_DIM = 64
WINDOW = 128
F_GROUPS = 4
F_GROUP_DIM = 128
F_WIDTH = F_GROUPS * F_GROUP_DIM
ATTN_WIDTH = N_HEADS * HEAD_DIM
KV_WIDTH = N_KV_HEADS * HEAD_DIM
ROPE_THETA = 10000.0
EPS = 1e-6
NEG_INF = -1e30
LOG2_E = 1.4426950408889634

LANES = 128
MXU_DIM = 256
VMEM_BYTES_V7X = 64 * 1024 * 1024

_Q0 = F_WIDTH
_K0 = _Q0 + ATTN_WIDTH
_V0 = _K0 + KV_WIDTH
_G0 = _V0 + KV_WIDTH
IN_WIDTH = _G0 + 2 * D_MODEL

SUBLANES = 8
N_MOD = 6
MOD_ROWS = 16
CTX_MOD_ROW = 8
ADA_STEPS = 8

PROJ_TOKEN_TILE = 1024
TAIL_TOKEN_TILE = 512
FF_CHUNK = MXU_DIM
GATE_CHUNK = 2 * MXU_DIM
CTX_SEQS_PER_STEP = 4
CTX_SOFTMAX_ROWS = 64
LAT_SOFTMAX_ROWS = 32

F32 = jnp.float32
BF16 = jnp.bfloat16


RESIDENT_WEIGHTS_VMEM = VMEM_BYTES_V7X - 6 * 2 ** 20
FOURIER_VMEM = VMEM_BYTES_V7X // 2


def _dot(a, b):
    return jnp.dot(a, b, preferred_element_type=F32)


def _sigmoid(x):
    return 1.0 / (1.0 + jnp.exp(-x))


def _resident(shape):
    zeros = (0,) * len(shape)
    return pl.BlockSpec(shape, lambda *_: zeros, pipeline_mode=pl.Buffered(1))


def _hbm(x):
    return pltpu.with_memory_space_constraint(x, pltpu.HBM)


def _run_units(units):
    units[0][0]()
    for k, (_, second) in enumerate(units):
        if k + 1 < len(units):
            units[k + 1][0]()
        second()


def _ada_kernel(c_ref, cctx_ref, w_ref, b_ref, win_ref, o_ref, winb_ref):
    lat_rows = c_ref.shape[0]
    pieces = [c_ref[...]]
    if lat_rows < CTX_MOD_ROW:
        pieces.append(jnp.zeros((CTX_MOD_ROW - lat_rows, D_MODEL), F32))
    pieces.append(jnp.broadcast_to(cctx_ref[...], (MOD_ROWS - CTX_MOD_ROW, D_MODEL)))
    c = jnp.concatenate(pieces, axis=0)
    s = c * _sigmoid(c)
    o_ref[...] = _dot(s.astype(BF16), w_ref[...].astype(BF16)) + b_ref[...]
    winb_ref[...] = win_ref[...].astype(BF16)


def _ada(c, c_ctx, w_ada, b_ada, w_in):
    assert c.shape[0] <= CTX_MOD_ROW, "latent conditioning rows must fit below the context row"
    n = w_ada.shape[1]
    bn = n // ADA_STEPS
    slab = pl.BlockSpec((w_in.shape[0] // ADA_STEPS, w_in.shape[1]), lambda j: (j, 0))
    return pl.pallas_call(
        _ada_kernel,
        grid=(ADA_STEPS,),
        in_specs=[pl.BlockSpec(c.shape, lambda j: (0, 0)),
                  pl.BlockSpec((1, D_MODEL), lambda j: (0, 0)),
                  pl.BlockSpec((D_MODEL, bn), lambda j: (0, j)),
                  pl.BlockSpec((1, bn), lambda j: (0, j)),
                  slab],
        out_specs=[pl.BlockSpec((MOD_ROWS, bn), lambda j: (0, j)), slab],
        out_shape=[pltpu.HBM((MOD_ROWS, n), F32), pltpu.HBM(w_in.shape, BF16)],
        name="ada",
    )(c, c_ctx, w_ada, b_ada, w_in)


def _mod_spec(mod_row):
    return pl.BlockSpec((SUBLANES, N_MOD * D_MODEL), lambda i: (mod_row // SUBLANES, 0))


def _mod_vectors(mod_ref, tile, tiles_per_row):
    cols = [slice(j * D_MODEL, (j + 1) * D_MODEL) for j in range(N_MOD)]
    if tiles_per_row is None:
        return [mod_ref[0:1, c] for c in cols]
    r = tile // tiles_per_row
    return [mod_ref[pl.ds(r, 1), c] for c in cols]


def _head_norm(z, g):
    lo = lax.broadcasted_iota(jnp.int32, z.shape, 1) < HEAD_DIM
    s = z * z
    s_lo = jnp.sum(jnp.where(lo, s, 0.0), axis=-1, keepdims=True)
    s_hi = jnp.sum(jnp.where(lo, 0.0, s), axis=-1, keepdims=True)
    ms = jnp.where(lo, s_lo, s_hi) * (1.0 / HEAD_DIM)
    return z * lax.rsqrt(ms + EPS) * g


def _rope(y, cos, sin):
    lane = lax.broadcasted_iota(jnp.int32, y.shape, 1)
    first = (lane % 32) < 16
    partner = jnp.where(first, pltpu.roll(y, LANES - 16, 1), pltpu.roll(y, 16, 1))
    return y * cos + partner * sin


def _dup_heads(y):
    lo = lax.broadcasted_iota(jnp.int32, y.shape, 1) < HEAD_DIM
    sw = pltpu.roll(y, HEAD_DIM, 1)
    return jnp.concatenate([jnp.where(lo, y, sw), jnp.where(lo, sw, y)], axis=-1)


def _proj_kernel(*refs, rope, cache_seq, n_cast, tiles_per_mod_row):
    x_ref, mod_ref, g1_ref, w_ref, gq_ref, gk_ref = refs[:6]
    refs = refs[6:]
    if rope:
        cos_ref, sin_ref = refs[:2]
        refs = refs[2:]
    cast_in, refs = refs[:n_cast], refs[n_cast:]
    zf_ref, q_ref, kd_ref, vd_ref, gate_ref = refs[:5]
    refs = refs[5:]
    if cache_seq:
        kc_ref, vc_ref = refs[:2]
        refs = refs[2:]
    for src_ref, dst_ref in zip(cast_in, refs):
        dst_ref[...] = src_ref[...].astype(BF16)

    x = x_ref[...]
    sh1, sc1 = _mod_vectors(mod_ref, pl.program_id(0), tiles_per_mod_row)[0:2]
    h = x * lax.rsqrt(jnp.mean(x * x, axis=-1, keepdims=True) + EPS) * g1_ref[...]
    hb = (h * (1.0 + sc1) + sh1).astype(BF16)

    gq = jnp.concatenate([gq_ref[...]] * (LANES // HEAD_DIM), axis=-1)
    gk = jnp.concatenate([gk_ref[...]] * (LANES // HEAD_DIM), axis=-1)

    zq = _dot(hb, w_ref[:, _Q0:_K0])
    for j in range(ATTN_WIDTH // LANES):
        y = _head_norm(zq[:, j * LANES:(j + 1) * LANES], gq)
        if rope:
            y = _rope(y, cos_ref[...], sin_ref[...])
        q_ref[:, j * LANES:(j + 1) * LANES] = (y * (HEAD_DIM ** -0.5 * LOG2_E)).astype(BF16)

    zkv = _dot(hb, w_ref[:, _K0:_G0])
    k = _head_norm(zkv[:, 0:KV_WIDTH], gk)
    v = zkv[:, KV_WIDTH:2 * KV_WIDTH]
    if cache_seq:
        for s in range(x.shape[0] // cache_seq):
            rows = slice(s * cache_seq, (s + 1) * cache_seq)
            kc_ref[s] = k[rows, :].T
            vc_ref[s] = v[rows, :].T
    if rope:
        k = _rope(k, cos_ref[...], sin_ref[...])
    kt = k.T
    kd_ref[...] = jnp.concatenate(
        [kt[0:HEAD_DIM], kt[0:HEAD_DIM], kt[HEAD_DIM:], kt[HEAD_DIM:]], axis=0).astype(BF16)
    vd_ref[...] = _dup_heads(v).astype(BF16)

    for j in range(2 * D_MODEL // GATE_CHUNK):
        cols = slice(j * GATE_CHUNK, (j + 1) * GATE_CHUNK)
        zg = _dot(hb, w_ref[:, _G0 + cols.start:_G0 + cols.stop])
        gate_ref[:, cols] = _sigmoid(zg).astype(BF16)

    zf_ref[...] = _dot(hb, w_ref[:, 0:_Q0]).astype(BF16)


def _proj(x, mod3, g1, w_in_b, gq2, gk2, rope_tabs, mod_row, rows_per_mod_row, cache_seq,
          cast_weights):
    t = x.shape[0]
    tm = PROJ_TOKEN_TILE
    steps = t // tm
    rope = rope_tabs is not None
    row = lambda i: (i, 0)
    tiles_per_mod_row = None if rows_per_mod_row is None else rows_per_mod_row // tm
    in_specs = [pl.BlockSpec((tm, D_MODEL), row),
                _mod_spec(mod_row),
                _resident((1, D_MODEL)),
                _resident((D_MODEL, IN_WIDTH)),
                _resident((1, HEAD_DIM)),
                _resident((1, HEAD_DIM))]
    args = [x, mod3, g1, w_in_b, gq2, gk2]
    if rope:
        tiles_per_seq = rope_tabs[0].shape[0] // tm
        tab = pl.BlockSpec((tm, LANES), lambda i: (i % tiles_per_seq, 0))
        in_specs += [tab, tab]
        args += list(rope_tabs)
    slabs = [pl.BlockSpec((w.shape[0] // steps, w.shape[1]), row) for w in cast_weights]
    in_specs += slabs
    args += list(cast_weights)
    out_specs = [pl.BlockSpec((tm, F_WIDTH), row),
                 pl.BlockSpec((tm, ATTN_WIDTH), row),
                 pl.BlockSpec((2 * KV_WIDTH, tm), lambda i: (0, i)),
                 pl.BlockSpec((tm, 2 * KV_WIDTH), row),
                 pl.BlockSpec((tm, 2 * D_MODEL), row)]
    out_shape = [pltpu.HBM((t, F_WIDTH), BF16),
                 pltpu.HBM((t, ATTN_WIDTH), BF16),
                 pltpu.HBM((2 * KV_WIDTH, t), BF16),
                 pltpu.HBM((t, 2 * KV_WIDTH), BF16),
                 pltpu.HBM((t, 2 * D_MODEL), BF16)]
    if cache_seq:
        seqs = tm // cache_seq
        out_specs += [pl.BlockSpec((seqs, KV_WIDTH, cache_seq), lambda i: (i, 0, 0))] * 2
        out_shape += [pltpu.HBM((t // cache_seq, KV_WIDTH, cache_seq), F32)] * 2
    out_specs += slabs
    out_shape += [pltpu.HBM(w.shape, BF16) for w in cast_weights]
    return pl.pallas_call(
        functools.partial(_proj_kernel, rope=rope, cache_seq=cache_seq, n_cast=len(cast_weights),
                          tiles_per_mod_row=tiles_per_mod_row),
        grid=(steps,),
        in_specs=in_specs,
        out_specs=out_specs,
        out_shape=out_shape,
        compiler_params=pltpu.CompilerParams(
            dimension_semantics=("arbitrary",), vmem_limit_bytes=RESIDENT_WEIGHTS_VMEM),
        name="proj_lat" if rope else "proj_ctx",
    )(*[_hbm(a) for a in args])


def _dft_tables(n, fold):
    def cs(m):
        idx = np.arange(m, dtype=np.int64)
        ang = 2.0 * np.pi * ((idx[:, None] * idx[None, :]) % m).astype(np.float64) / m
        return np.cos(ang) / np.sqrt(m), np.sin(ang) / np.sqrt(m)
    cd, sd = cs(F_GROUP_DIM)
    cn, sn = cs(n)
    tabs = [np.concatenate([cd, sd], axis=1)]
    if fold:
        half = n // 2
        rev = np.zeros((half, half), np.float64)
        rev[np.arange(1, half), half - np.arange(1, half)] = 1.0
        tabs += [cn[:half], -sn[:half], rev]
    else:
        tabs += [np.concatenate([cn, -sn], axis=1)]
    return tuple(jnp.asarray(t.astype(np.float32)).astype(BF16) for t in tabs)


def _fourier_kernel(zf_ref, csd_ref, *refs, seqs, n, fold):
    if fold:
        ch_ref, sh_ref, rev_ref, o_ref, ab_ref = refs
    else:
        csn_ref, o_ref, ab_ref = refs
    half = n // 2
    units = []
    for s in range(seqs):
        buf = s % 2

        def channels(s=s, buf=buf):
            for g in range(F_GROUPS):
                cols = slice(g * F_GROUP_DIM, (g + 1) * F_GROUP_DIM)
                ab = _dot(zf_ref[s, :, cols], csd_ref[...])
                ab_ref[buf, 0:n, cols] = ab[:, 0:F_GROUP_DIM].astype(BF16)
                ab_ref[buf, n:2 * n, cols] = ab[:, F_GROUP_DIM:].astype(BF16)

        def positions(s=s, buf=buf):
            if not fold:
                o_ref[s] = _dot(csn_ref[...], ab_ref[buf]).astype(BF16)
                return
            p = _dot(ch_ref[...], ab_ref[buf, 0:n, :])
            q = _dot(sh_ref[...], ab_ref[buf, n:2 * n, :])
            o_ref[s, 0:half, :] = (p + q).astype(BF16)
            upper = _dot(rev_ref[...], (p - q).astype(BF16))
            a = ab_ref[buf, 0:n, :].astype(F32)
            even = lax.broadcasted_iota(jnp.int32, a.shape, 0) % 2 == 0
            mid = jnp.sum(jnp.where(even, a, -a), axis=0, keepdims=True) * (n ** -0.5)
            first = lax.broadcasted_iota(jnp.int32, upper.shape, 0) == 0
            o_ref[s, half:n, :] = jnp.where(first, mid, upper).astype(BF16)

        units.append((channels, positions))
    _run_units(units)


def _fourier(zf3, seqs):
    b, n, _ = zf3.shape
    fold = n // 2 >= 2 * MXU_DIM
    tabs = _dft_tables(n, fold)
    blk = pl.BlockSpec((seqs, n, F_WIDTH), lambda i: (i, 0, 0))
    return pl.pallas_call(
        functools.partial(_fourier_kernel, seqs=seqs, n=n, fold=fold),
        grid=(b // seqs,),
        in_specs=[blk] + [_resident(t.shape) for t in tabs],
        out_specs=blk,
        out_shape=pltpu.HBM(zf3.shape, BF16),
        scratch_shapes=[pltpu.VMEM((2, 2 * n, F_WIDTH), BF16)],
        compiler_params=pltpu.CompilerParams(
            dimension_semantics=("arbitrary",), vmem_limit_bytes=FOURIER_VMEM),
        name=f"fourier_{n}",
    )(_hbm(zf3), *tabs)


def _split_lo_hi(xd, fill):
    lo = lax.broadcasted_iota(jnp.int32, xd.shape, 1) < HEAD_DIM
    other = jnp.full_like(xd, fill)
    return jnp.where(lo, xd, other), jnp.where(lo, other, xd)


def _group_scores(q_stack, keys, s_ref):
    c0 = 0
    for kt in keys:
        top = lax.broadcasted_iota(jnp.int32, kt.shape, 0) < HEAD_DIM
        zero = jnp.zeros_like(kt)
        c1 = c0 + kt.shape[1]
        s_ref[0, :, c0:c1] = _dot(q_stack, jnp.where(top, kt, zero))
        s_ref[1, :, c0:c1] = _dot(q_stack, jnp.where(top, zero, kt))
        c0 = c1


def _group_softmax_pv(values, sinks, band, s_ref, p_ref, sk_ref, rows_per_pair, chunk):
    rows = s_ref.shape[1]
    bounds = [0]
    for vd in values:
        bounds.append(bounds[-1] + vd.shape[0])
    sources = values

    for half in range(2):
        for r0 in range(0, rows, chunk):
            rs = slice(r0, r0 + chunk)
            sink = sinks[2 * (r0 // rows_per_pair) + half]
            parts = [s_ref[half, rs, bounds[i]:bounds[i + 1]] for i in range(len(sources))]
            if band is not None:
                parts[0] = jnp.where(band(r0 % rows_per_pair, chunk), parts[0], NEG_INF)
            m = sink
            for s in parts:
                m = jnp.maximum(m, jnp.max(s, axis=-1, keepdims=True))
            for i, s in enumerate(parts):
                p_ref[half, rs, bounds[i]:bounds[i + 1]] = jnp.exp2(s - m).astype(BF16)
            sk_ref[half, rs, :] = jnp.broadcast_to(jnp.exp2(sink - m), (chunk, LANES))

    o_e = o_o = None
    for i, vd in enumerate(values):
        v_lo, v_hi = _split_lo_hi(vd, 1.0)
        pe = _dot(p_ref[0, :, bounds[i]:bounds[i + 1]], v_lo)
        po = _dot(p_ref[1, :, bounds[i]:bounds[i + 1]], v_hi)
        o_e = pe if o_e is None else o_e + pe
        o_o = po if o_o is None else o_o + po
    lo = lax.broadcasted_iota(jnp.int32, (rows, LANES), 1) < HEAD_DIM
    num = jnp.where(lo, o_e, o_o)
    den = pltpu.roll(jnp.where(lo, o_o, o_e), HEAD_DIM, 1) + jnp.where(lo, sk_ref[0], sk_ref[1])
    return num / den


def _pair_cols(g):
    pa, pb = 2 * g, 2 * g + 1
    return slice(pa * LANES, (pa + 1) * LANES), slice(pb * LANES, (pb + 1) * LANES)


def _attn_tail_kernel(*refs, lat, n_tiles, tm, n, tiles_per_mod_row):
    sink_ref, q_ref, kd_ref, vd_ref = refs[:4]
    refs = refs[4:]
    if lat:
        ck_ref, cv_ref = refs[:2]
        refs = refs[2:]
    (x_ref, mod_ref, mix_ref, gate_ref, g2_ref,
     wf_ref, wao_ref, wout_ref, wup_ref, wdn_ref, o_ref, att_ref, s_ref, p_ref, sk_ref) = refs[:15]
    if lat:
        dist_ref, ckt_ref, cvd_ref = refs[15:]
    i = pl.program_id(0)

    def attention_stages():
        units = []
        if lat:
            qb = WINDOW
            span = qb + 2 * WINDOW
            t = jnp.minimum(i, n_tiles - 1) % (n // tm)
            ckt = ck_ref[0]
            ckt_ref[...] = jnp.concatenate(
                [ckt[0:HEAD_DIM], ckt[0:HEAD_DIM], ckt[HEAD_DIM:], ckt[HEAD_DIM:]], axis=0).astype(BF16)
            cvd_ref[...] = _dup_heads(cv_ref[0].T).astype(BF16)
            for rr in range(tm // qb):
                q0 = t * tm + rr * qb
                start = pl.multiple_of(jnp.clip(q0 - WINDOW, 0, n - span), WINDOW)
                dist_ref[rr] = jnp.abs((q0 - start) + lax.broadcasted_iota(jnp.int32, (qb, span), 0)
                                       - lax.broadcasted_iota(jnp.int32, (qb, span), 1))
                band = lambda r0, rows, rr=rr: dist_ref[rr, r0:r0 + rows, :] <= WINDOW
                rows = slice(rr * qb, (rr + 1) * qb)
                for g in range(N_KV_HEADS):
                    buf = len(units) % 2
                    cols = slice(g * LANES, (g + 1) * LANES)
                    qa, qbc = _pair_cols(g)

                    def scores(rows=rows, start=start, cols=cols, qa=qa, qbc=qbc, buf=buf):
                        q_stack = jnp.concatenate([q_ref[rows, qa], q_ref[rows, qbc]], axis=0)
                        _group_scores(q_stack, [kd_ref[cols, pl.ds(start, span)], ckt_ref[cols, :]],
                                      s_ref.at[buf])

                    def finish(rows=rows, start=start, g=g, cols=cols, qa=qa, qbc=qbc, buf=buf,
                               band=band):
                        sinks = [sink_ref[GROUP * g + h] * LOG2_E for h in range(GROUP)]
                        o = _group_softmax_pv(
                            [vd_ref[pl.ds(start, span), cols], cvd_ref[:, cols]], sinks, band,
                            s_ref.at[buf], p_ref.at[buf], sk_ref.at[buf], qb, LAT_SOFTMAX_ROWS)
                        att_ref[rows, qa] = o[0:qb].astype(BF16)
                        att_ref[rows, qbc] = o[qb:2 * qb].astype(BF16)

                    units.append((scores, finish))
        else:
            for s in range(tm // n):
                rows = slice(s * n, (s + 1) * n)
                for g in range(N_KV_HEADS):
                    buf = len(units) % 2
                    cols = slice(g * LANES, (g + 1) * LANES)
                    qa, qbc = _pair_cols(g)

                    def scores(rows=rows, cols=cols, qa=qa, qbc=qbc, buf=buf):
                        q_stack = jnp.concatenate([q_ref[rows, qa], q_ref[rows, qbc]], axis=0)
                        _group_scores(q_stack, [kd_ref[cols, rows]], s_ref.at[buf])

                    def finish(rows=rows, g=g, cols=cols, qa=qa, qbc=qbc, buf=buf):
                        sinks = [sink_ref[GROUP * g + h] * LOG2_E for h in range(GROUP)]
                        o = _group_softmax_pv([vd_ref[rows, cols]], sinks, None, s_ref.at[buf],
                                              p_ref.at[buf], sk_ref.at[buf], n, CTX_SOFTMAX_ROWS)
                        att_ref[rows, qa] = o[0:n].astype(BF16)
                        att_ref[rows, qbc] = o[n:2 * n].astype(BF16)

                    units.append((scores, finish))
        stages = [units[0][0]]
        for k, (_, finish) in enumerate(units):
            if k + 1 < len(units):
                stages.append(units[k + 1][0])
            stages.append(finish)
        return stages

    def tail(stages):
        gt1, sh2, sc2, gt2 = _mod_vectors(mod_ref, i - 1, tiles_per_mod_row)[2:6]
        yf = _dot(mix_ref[...], wf_ref[...])
        ya = _dot(att_ref[...], wao_ref[...])
        merged = (gate_ref[:, 0:D_MODEL].astype(F32) * yf
                  + gate_ref[:, D_MODEL:2 * D_MODEL].astype(F32) * ya)
        x1 = x_ref[...] + gt1 * _dot(merged.astype(BF16), wout_ref[...])

        h = x1 * lax.rsqrt(jnp.mean(x1 * x1, axis=-1, keepdims=True) + EPS) * g2_ref[...]
        hb = (h * (1.0 + sc2) + sh2).astype(BF16)
        d_ff = wdn_ref.shape[0]
        n_chunks = d_ff // FF_CHUNK
        acc = None
        issued = 0
        for c in range(n_chunks):
            a = _dot(hb, wup_ref[:, c * FF_CHUNK:(c + 1) * FF_CHUNK])
            u = _dot(hb, wup_ref[:, d_ff + c * FF_CHUNK:d_ff + (c + 1) * FF_CHUNK])
            act = (a * _sigmoid(a) * u).astype(BF16)
            d = _dot(act, wdn_ref[c * FF_CHUNK:(c + 1) * FF_CHUNK, :])
            acc = d if acc is None else acc + d
            while issued < len(stages) and issued * n_chunks < (c + 1) * len(stages):
                stages[issued]()
                issued += 1
        o_ref[...] = x1 + gt2 * acc

    @pl.when(i == 0)
    def _():
        for stage in attention_stages():
            stage()

    @pl.when(jnp.logical_and(i > 0, i < n_tiles))
    def _():
        tail(attention_stages())

    @pl.when(i == n_tiles)
    def _():
        tail([])


def _attn_tail(sinks, q, kd, vd, cache, x, mod3, mix, gate, g2, wf, wao, wout, wup, wdn,
               mod_row, rows_per_mod_row, n):
    t = x.shape[0]
    tm = TAIL_TOKEN_TILE
    n_tiles = t // tm
    lat = cache is not None
    tiles_per_mod_row = None if rows_per_mod_row is None else rows_per_mod_row // tm
    front = lambda i: (jnp.minimum(i, n_tiles - 1), 0)
    back = lambda i: (jnp.maximum(i - 1, 0), 0)
    in_specs = [pl.BlockSpec(memory_space=pltpu.SMEM), pl.BlockSpec((tm, ATTN_WIDTH), front)]
    args = [sinks, q, kd, vd]
    if lat:
        past = cache[0].shape[2]
        seq_of = lambda i: jnp.minimum(i, n_tiles - 1) // (n // tm)
        in_specs += [pl.BlockSpec((2 * KV_WIDTH, n), lambda i: (0, seq_of(i))),
                     pl.BlockSpec((n, 2 * KV_WIDTH), lambda i: (seq_of(i), 0))]
        in_specs += [pl.BlockSpec((1, KV_WIDTH, past), lambda i: (seq_of(i), 0, 0))] * 2
        args += list(cache)
        m_rows, keys = 2 * WINDOW, 3 * WINDOW + past
    else:
        in_specs += [pl.BlockSpec((2 * KV_WIDTH, tm), lambda i: (0, front(i)[0])),
                     pl.BlockSpec((tm, 2 * KV_WIDTH), front)]
        m_rows, keys = 2 * n, n
    in_specs += [pl.BlockSpec((tm, D_MODEL), back),
                 _mod_spec(mod_row),
                 pl.BlockSpec((tm, F_WIDTH), back),
                 pl.BlockSpec((tm, 2 * D_MODEL), back),
                 _resident((1, D_MODEL)),
                 _resident(wf.shape), _resident(wao.shape), _resident(wout.shape),
                 _resident(wup.shape), _resident(wdn.shape)]
    args += [x, mod3, mix, gate, g2, wf, wao, wout, wup, wdn]
    scratch = [pltpu.VMEM((tm, ATTN_WIDTH), BF16),
               pltpu.VMEM((2, 2, m_rows, keys), F32),
               pltpu.VMEM((2, 2, m_rows, keys), BF16),
               pltpu.VMEM((2, 2, m_rows, LANES), F32)]
    if lat:
        scratch += [pltpu.VMEM((tm // WINDOW, WINDOW, 3 * WINDOW), jnp.int32),
                    pltpu.VMEM((2 * KV_WIDTH, past), BF16),
                    pltpu.VMEM((past, 2 * KV_WIDTH), BF16)]
    return pl.pallas_call(
        functools.partial(_attn_tail_kernel, lat=lat, n_tiles=n_tiles, tm=tm, n=n,
                          tiles_per_mod_row=tiles_per_mod_row),
        grid=(n_tiles + 1,),
        in_specs=in_specs,
        out_specs=pl.BlockSpec((tm, D_MODEL), back),
        out_shape=pltpu.HBM((t, D_MODEL), F32),
        scratch_shapes=scratch,
        compiler_params=pltpu.CompilerParams(
            dimension_semantics=("arbitrary",), vmem_limit_bytes=RESIDENT_WEIGHTS_VMEM),
        name="attn_tail_lat" if lat else "attn_tail_ctx",
    )(sinks, *[_hbm(a) for a in args[1:]])


def _rope_tables(n):
    rows = n // GRID_W
    row = np.repeat(np.arange(rows, dtype=np.float64), GRID_W)
    col = np.tile(np.arange(GRID_W, dtype=np.float64), rows)
    axis_dim = HEAD_DIM // 2
    inv_freq = ROPE_THETA ** (-np.arange(0, axis_dim, 2, dtype=np.float64) / axis_dim)

    def axis_tabs(pos):
        ang = pos[:, None] * inv_freq[None, :]
        cos, sin = np.cos(ang), np.sin(ang)
        return np.concatenate([cos, cos], axis=-1), np.concatenate([-sin, sin], axis=-1)

    cr, sr = axis_tabs(row)
    cc, sc = axis_tabs(col)
    cos = np.tile(np.concatenate([cr, cc], axis=-1), (1, LANES // HEAD_DIM))
    sin = np.tile(np.concatenate([sr, sc], axis=-1), (1, LANES // HEAD_DIM))
    return jnp.asarray(cos.astype(np.float32)), jnp.asarray(sin.astype(np.float32))


@jax.jit
def _layer(xp, xs, ck, cv, c, c_ctx, w_ada, b_ada, g_norm1, g_norm2, w_in, g_q, g_k, sink,
           w_f, w_ao, w_out, w_up, w_down):
    bp, sp, _ = xp.shape
    bs, ss, _ = xs.shape
    past = ck.shape[1]

    mod3, w_in_b = _ada(c, c_ctx[None, :], w_ada, b_ada[None, :], w_in)
    g1, g2 = g_norm1[None, :], g_norm2[None, :]
    gq2, gk2 = g_q[None, :], g_k[None, :]
    ctx_mod = (CTX_MOD_ROW, None)
    lat_mod = (0, ss)

    xp2 = xp.reshape(bp * sp, D_MODEL)
    xs2 = xs.reshape(bs * ss, D_MODEL)
    zf, q, kd, vd, gate, k_new, v_new, wup = _proj(
        xp2, mod3, g1, w_in_b, gq2, gk2, None, *ctx_mod, sp, [w_up])
    zf_s, q_s, kd_s, vd_s, gate_s, wf, wao, wout, wdn = _proj(
        xs2, mod3, g1, w_in_b, gq2, gk2, _rope_tables(ss), *lat_mod, 0, [w_f, w_ao, w_out, w_down])

    mix = _fourier(zf.reshape(bp, sp, F_WIDTH), CTX_SEQS_PER_STEP)
    yp = _attn_tail(sink, q, kd, vd, None, xp2, mod3, mix.reshape(bp * sp, F_WIDTH), gate, g2,
                    wf, wao, wout, wup, wdn, *ctx_mod, sp)

    def cache_t(t):
        return t.transpose(0, 2, 3, 1).reshape(bs, KV_WIDTH, past)

    zf, q, kd, vd, gate = zf_s, q_s, kd_s, vd_s, gate_s
    mix = _fourier(zf.reshape(bs, ss, F_WIDTH), 1)
    ys = _attn_tail(sink, q, kd, vd, (cache_t(ck), cache_t(cv)), xs2, mod3,
                    mix.reshape(bs * ss, F_WIDTH), gate, g2, wf, wao, wout, wup, wdn, *lat_mod, ss)

    def cache_layout(t):
        return t.reshape(bp, N_KV_HEADS, HEAD_DIM, sp).transpose(0, 3, 1, 2)

    return yp.reshape(xp.shape), ys.reshape(xs.shape), cache_layout(k_new), cache_layout(v_new)


def kernel(x_prompt, x_sample, cache_k, cache_v, c, c_ctx, w_ada, b_ada, g_norm1, g_norm2,
           w_in, g_q, g_k, sinks, w_f, w_ao, w_out, w_up, w_down):
    depth = w_in.shape[0]
    xp, xs = x_prompt, x_sample
    new_k, new_v = [], []
    for l in range(depth):
        xp, xs, k_ctx, v_ctx = _layer(
            xp, xs, cache_k[:, l], cache_v[:, l], c, c_ctx, w_ada[l], b_ada[l], g_norm1[l],
            g_norm2[l], w_in[l], g_q[l], g_k[l], sinks[l], w_f[l], w_ao[l], w_out[l], w_up[l],
            w_down[l])
        new_k.append(k_ctx)
        new_v.append(v_ctx)
    return (xp, xs, jnp.stack(new_k, axis=1), jnp.stack(new_v, axis=1))
```

```python
import functools

import numpy as np
import jax
import jax.numpy as jnp
from jax import lax
from jax.experimental import pallas as pl
from jax.experimental.pallas import tpu as pltpu

D_MODEL = 1024
GRID_W = 64
N_HEADS = 8
N_KV_HEADS = 2
GROUP = N_HEADS // N_KV_HEADS
HEAD_DIM = 64
WINDOW = 128
F_GROUPS = 4
F_GROUP_DIM = 128
F_WIDTH = F_GROUPS * F_GROUP_DIM
ATTN_WIDTH = N_HEADS * HEAD_DIM
KV_WIDTH = N_KV_HEADS * HEAD_DIM
ROPE_THETA = 10000.0
EPS = 1e-6
NEG_INF = -1e30
LOG2_E = 1.4426950408889634

LANES = 128
MXU_DIM = 256
VMEM_BYTES_V7X = 64 * 1024 * 1024

_Q0 = F_WIDTH
_K0 = _Q0 + ATTN_WIDTH
_V0 = _K0 + KV_WIDTH
_G0 = _V0 + KV_WIDTH
IN_WIDTH = _G0 + 2 * D_MODEL

SUBLANES = 8
N_MOD = 6
MOD_ROWS = 16
CTX_MOD_ROW = 8
ADA_STEPS = 8

PROJ_TOKEN_TILE = 1024
TAIL_TOKEN_TILE = 512
FF_CHUNK = MXU_DIM
GATE_CHUNK = 2 * MXU_DIM
CTX_SEQS_PER_STEP = 4
CTX_SOFTMAX_ROWS = 64
LAT_SOFTMAX_ROWS = 32

F32 = jnp.float32
BF16 = jnp.bfloat16


RESIDENT_WEIGHTS_VMEM = VMEM_BYTES_V7X - 6 * 2 ** 20
FOURIER_VMEM = VMEM_BYTES_V7X // 2


def _dot(a, b):
    return jnp.dot(a, b, preferred_element_type=F32)


def _sigmoid(x):
    return 1.0 / (1.0 + jnp.exp(-x))


def _resident(shape):
    zeros = (0,) * len(shape)
    return pl.BlockSpec(shape, lambda *_: zeros, pipeline_mode=pl.Buffered(1))


def _hbm(x):
    return pltpu.with_memory_space_constraint(x, pltpu.HBM)


def _run_units(units):
    units[0][0]()
    for k, (_, second) in enumerate(units):
        if k + 1 < len(units):
            units[k + 1][0]()
        second()


def _ada_kernel(c_ref, cctx_ref, w_ref, b_ref, win_ref, o_ref, winb_ref):
    lat_rows = c_ref.shape[0]
    pieces = [c_ref[...]]
    if lat_rows < CTX_MOD_ROW:
        pieces.append(jnp.zeros((CTX_MOD_ROW - lat_rows, D_MODEL), F32))
    pieces.append(jnp.broadcast_to(cctx_ref[...], (MOD_ROWS - CTX_MOD_ROW, D_MODEL)))
    c = jnp.concatenate(pieces, axis=0)
    s = c * _sigmoid(c)
    o_ref[...] = _dot(s.astype(BF16), w_ref[...].astype(BF16)) + b_ref[...]
    winb_ref[...] = win_ref[...].astype(BF16)


def _ada(c, c_ctx, w_ada, b_ada, w_in):
    assert c.shape[0] <= CTX_MOD_ROW, "latent conditioning rows must fit below the context row"
    n = w_ada.shape[1]
    bn = n // ADA_STEPS
    slab = pl.BlockSpec((w_in.shape[0] // ADA_STEPS, w_in.shape[1]), lambda j: (j, 0))
    return pl.pallas_call(
        _ada_kernel,
        grid=(ADA_STEPS,),
        in_specs=[pl.BlockSpec(c.shape, lambda j: (0, 0)),
                  pl.BlockSpec((1, D_MODEL), lambda j: (0, 0)),
                  pl.BlockSpec((D_MODEL, bn), lambda j: (0, j)),
                  pl.BlockSpec((1, bn), lambda j: (0, j)),
                  slab],
        out_specs=[pl.BlockSpec((MOD_ROWS, bn), lambda j: (0, j)), slab],
        out_shape=[pltpu.HBM((MOD_ROWS, n), F32), pltpu.HBM(w_in.shape, BF16)],
        name="ada",
    )(c, c_ctx, w_ada, b_ada, w_in)


def _mod_spec(mod_row):
    return pl.BlockSpec((SUBLANES, N_MOD * D_MODEL), lambda i: (mod_row // SUBLANES, 0))


def _mod_vectors(mod_ref, tile, tiles_per_row):
    cols = [slice(j * D_MODEL, (j + 1) * D_MODEL) for j in range(N_MOD)]
    if tiles_per_row is None:
        return [mod_ref[0:1, c] for c in cols]
    r = tile // tiles_per_row
    return [mod_ref[pl.ds(r, 1), c] for c in cols]


def _head_norm(z, g):
    lo = lax.broadcasted_iota(jnp.int32, z.shape, 1) < HEAD_DIM
    s = z * z
    s_lo = jnp.sum(jnp.where(lo, s, 0.0), axis=-1, keepdims=True)
    s_hi = jnp.sum(jnp.where(lo, 0.0, s), axis=-1, keepdims=True)
    ms = jnp.where(lo, s_lo, s_hi) * (1.0 / HEAD_DIM)
    return z * lax.rsqrt(ms + EPS) * g


def _rope(y, cos, sin):
    lane = lax.broadcasted_iota(jnp.int32, y.shape, 1)
    first = (lane % 32) < 16
    partner = jnp.where(first, pltpu.roll(y, LANES - 16, 1), pltpu.roll(y, 16, 1))
    return y * cos + partner * sin


def _dup_heads(y):
    lo = lax.broadcasted_iota(jnp.int32, y.shape, 1) < HEAD_DIM
    sw = pltpu.roll(y, HEAD_DIM, 1)
    return jnp.concatenate([jnp.where(lo, y, sw), jnp.where(lo, sw, y)], axis=-1)


def _proj_kernel(*refs, rope, cache_seq, n_cast, tiles_per_mod_row):
    x_ref, mod_ref, g1_ref, w_ref, gq_ref, gk_ref = refs[:6]
    refs = refs[6:]
    if rope:
        cos_ref, sin_ref = refs[:2]
        refs = refs[2:]
    cast_in, refs = refs[:n_cast], refs[n_cast:]
    zf_ref, q_ref, kd_ref, vd_ref, gate_ref = refs[:5]
    refs = refs[5:]
    if cache_seq:
        kc_ref, vc_ref = refs[:2]
        refs = refs[2:]
    for src_ref, dst_ref in zip(cast_in, refs):
        dst_ref[...] = src_ref[...].astype(BF16)

    x = x_ref[...]
    sh1, sc1 = _mod_vectors(mod_ref, pl.program_id(0), tiles_per_mod_row)[0:2]
    h = x * lax.rsqrt(jnp.mean(x * x, axis=-1, keepdims=True) + EPS) * g1_ref[...]
    hb = (h * (1.0 + sc1) + sh1).astype(BF16)

    gq = jnp.concatenate([gq_ref[...]] * (LANES // HEAD_DIM), axis=-1)
    gk = jnp.concatenate([gk_ref[...]] * (LANES // HEAD_DIM), axis=-1)

    zq = _dot(hb, w_ref[:, _Q0:_K0])
    for j in range(ATTN_WIDTH // LANES):
        y = _head_norm(zq[:, j * LANES:(j + 1) * LANES], gq)
        if rope:
            y = _rope(y, cos_ref[...], sin_ref[...])
        q_ref[:, j * LANES:(j + 1) * LANES] = (y * (HEAD_DIM ** -0.5 * LOG2_E)).astype(BF16)

    zkv = _dot(hb, w_ref[:, _K0:_G0])
    k = _head_norm(zkv[:, 0:KV_WIDTH], gk)
    v = zkv[:, KV_WIDTH:2 * KV_WIDTH]
    if cache_seq:
        for s in range(x.shape[0] // cache_seq):
            rows = slice(s * cache_seq, (s + 1) * cache_seq)
            kc_ref[s] = k[rows, :].T
            vc_ref[s] = v[rows, :].T
    if rope:
        k = _rope(k, cos_ref[...], sin_ref[...])
    kt = k.T
    kd_ref[...] = jnp.concatenate(
        [kt[0:HEAD_DIM], kt[0:HEAD_DIM], kt[HEAD_DIM:], kt[HEAD_DIM:]], axis=0).astype(BF16)
    vd_ref[...] = _dup_heads(v).astype(BF16)

    for j in range(2 * D_MODEL // GATE_CHUNK):
        cols = slice(j * GATE_CHUNK, (j + 1) * GATE_CHUNK)
        zg = _dot(hb, w_ref[:, _G0 + cols.start:_G0 + cols.stop])
        gate_ref[:, cols] = _sigmoid(zg).astype(BF16)

    zf_ref[...] = _dot(hb, w_ref[:, 0:_Q0]).astype(BF16)


def _proj(x, mod3, g1, w_in_b, gq2, gk2, rope_tabs, mod_row, rows_per_mod_row, cache_seq,
          cast_weights):
    t = x.shape[0]
    tm = PROJ_TOKEN_TILE
    steps = t // tm
    rope = rope_tabs is not None
    row = lambda i: (i, 0)
    tiles_per_mod_row = None if rows_per_mod_row is None else rows_per_mod_row // tm
    in_specs = [pl.BlockSpec((tm, D_MODEL), row),
                _mod_spec(mod_row),
                _resident((1, D_MODEL)),
                _resident((D_MODEL, IN_WIDTH)),
                _resident((1, HEAD_DIM)),
                _resident((1, HEAD_DIM))]
    args = [x, mod3, g1, w_in_b, gq2, gk2]
    if rope:
        tiles_per_seq = rope_tabs[0].shape[0] // tm
        tab = pl.BlockSpec((tm, LANES), lambda i: (i % tiles_per_seq, 0))
        in_specs += [tab, tab]
        args += list(rope_tabs)
    slabs = [pl.BlockSpec((w.shape[0] // steps, w.shape[1]), row) for w in cast_weights]
    in_specs += slabs
    args += list(cast_weights)
    out_specs = [pl.BlockSpec((tm, F_WIDTH), row),
                 pl.BlockSpec((tm, ATTN_WIDTH), row),
                 pl.BlockSpec((2 * KV_WIDTH, tm), lambda i: (0, i)),
                 pl.BlockSpec((tm, 2 * KV_WIDTH), row),
                 pl.BlockSpec((tm, 2 * D_MODEL), row)]
    out_shape = [pltpu.HBM((t, F_WIDTH), BF16),
                 pltpu.HBM((t, ATTN_WIDTH), BF16),
                 pltpu.HBM((2 * KV_WIDTH, t), BF16),
                 pltpu.HBM((t, 2 * KV_WIDTH), BF16),
                 pltpu.HBM((t, 2 * D_MODEL), BF16)]
    if cache_seq:
        seqs = tm // cache_seq
        out_specs += [pl.BlockSpec((seqs, KV_WIDTH, cache_seq), lambda i: (i, 0, 0))] * 2
        out_shape += [pltpu.HBM((t // cache_seq, KV_WIDTH, cache_seq), F32)] * 2
    out_specs += slabs
    out_shape += [pltpu.HBM(w.shape, BF16) for w in cast_weights]
    return pl.pallas_call(
        functools.partial(_proj_kernel, rope=rope, cache_seq=cache_seq, n_cast=len(cast_weights),
                          tiles_per_mod_row=tiles_per_mod_row),
        grid=(steps,),
        in_specs=in_specs,
        out_specs=out_specs,
        out_shape=out_shape,
        compiler_params=pltpu.CompilerParams(
            dimension_semantics=("arbitrary",), vmem_limit_bytes=RESIDENT_WEIGHTS_VMEM),
        name="proj_lat" if rope else "proj_ctx",
    )(*[_hbm(a) for a in args])


def _dft_tables(n, fold):
    def cs(m):
        idx = np.arange(m, dtype=np.int64)
        ang = 2.0 * np.pi * ((idx[:, None] * idx[None, :]) % m).astype(np.float64) / m
        return np.cos(ang) / np.sqrt(m), np.sin(ang) / np.sqrt(m)
    cd, sd = cs(F_GROUP_DIM)
    cn, sn = cs(n)
    tabs = [np.concatenate([cd, sd], axis=1)]
    if fold:
        half = n // 2
        rev = np.zeros((half, half), np.float64)
        rev[np.arange(1, half), half - np.arange(1, half)] = 1.0
        tabs += [cn[:half], -sn[:half], rev]
    else:
        tabs += [np.concatenate([cn, -sn], axis=1)]
    return tuple(jnp.asarray(t.astype(np.float32)).astype(BF16) for t in tabs)


def _fourier_kernel(zf_ref, csd_ref, *refs, seqs, n, fold, n_cast):
    n_tabs = 3 if fold else 1
    tabs, refs = refs[:n_tabs], refs[n_tabs:]
    cast_in, o_ref, cast_out, ab_ref = refs[:n_cast], refs[n_cast], refs[n_cast + 1:-1], refs[-1]
    if fold:
        ch_ref, sh_ref, rev_ref = tabs
    else:
        (csn_ref,) = tabs
    for src_ref, dst_ref in zip(cast_in, cast_out):
        dst_ref[...] = src_ref[...].astype(BF16)
    half = n // 2
    units = []
    for s in range(seqs):
        buf = s % 2

        def channels(s=s, buf=buf):
            for g in range(F_GROUPS):
                cols = slice(g * F_GROUP_DIM, (g + 1) * F_GROUP_DIM)
                ab = _dot(zf_ref[s, :, cols], csd_ref[...])
                ab_ref[buf, 0:n, cols] = ab[:, 0:F_GROUP_DIM].astype(BF16)
                ab_ref[buf, n:2 * n, cols] = ab[:, F_GROUP_DIM:].astype(BF16)

        def positions(s=s, buf=buf):
            if not fold:
                o_ref[s] = _dot(csn_ref[...], ab_ref[buf]).astype(BF16)
                return
            p = _dot(ch_ref[...], ab_ref[buf, 0:n, :])
            q = _dot(sh_ref[...], ab_ref[buf, n:2 * n, :])
            o_ref[s, 0:half, :] = (p + q).astype(BF16)
            upper = _dot(rev_ref[...], (p - q).astype(BF16))
            a = ab_ref[buf, 0:n, :].astype(F32)
            even = lax.broadcasted_iota(jnp.int32, a.shape, 0) % 2 == 0
            mid = jnp.sum(jnp.where(even, a, -a), axis=0, keepdims=True) * (n ** -0.5)
            first = lax.broadcasted_iota(jnp.int32, upper.shape, 0) == 0
            o_ref[s, half:n, :] = jnp.where(first, mid, upper).astype(BF16)

        units.append((channels, positions))
    _run_units(units)


def _fourier(zf3, seqs, cast_weights):
    b, n, _ = zf3.shape
    steps = b // seqs
    fold = n // 2 >= 2 * MXU_DIM
    tabs = _dft_tables(n, fold)
    blk = pl.BlockSpec((seqs, n, F_WIDTH), lambda i: (i, 0, 0))
    slabs = [pl.BlockSpec((w.shape[0] // steps, w.shape[1]), lambda i: (i, 0)) for w in cast_weights]
    return pl.pallas_call(
        functools.partial(_fourier_kernel, seqs=seqs, n=n, fold=fold, n_cast=len(cast_weights)),
        grid=(steps,),
        in_specs=[blk] + [_resident(t.shape) for t in tabs] + slabs,
        out_specs=[blk] + slabs,
        out_shape=[pltpu.HBM(zf3.shape, BF16)] + [pltpu.HBM(w.shape, BF16) for w in cast_weights],
        scratch_shapes=[pltpu.VMEM((2, 2 * n, F_WIDTH), BF16)],
        compiler_params=pltpu.CompilerParams(
            dimension_semantics=("arbitrary",), vmem_limit_bytes=FOURIER_VMEM),
        name=f"fourier_{n}",
    )(_hbm(zf3), *tabs, *[_hbm(w) for w in cast_weights])


def _split_lo_hi(xd, fill):
    lo = lax.broadcasted_iota(jnp.int32, xd.shape, 1) < HEAD_DIM
    other = jnp.full_like(xd, fill)
    return jnp.where(lo, xd, other), jnp.where(lo, other, xd)


def _group_scores(q_stack, keys, s_ref):
    c0 = 0
    for kt in keys:
        top = lax.broadcasted_iota(jnp.int32, kt.shape, 0) < HEAD_DIM
        zero = jnp.zeros_like(kt)
        c1 = c0 + kt.shape[1]
        s_ref[0, :, c0:c1] = _dot(q_stack, jnp.where(top, kt, zero))
        s_ref[1, :, c0:c1] = _dot(q_stack, jnp.where(top, zero, kt))
        c0 = c1


def _group_softmax_pv(values, sinks, band, s_ref, p_ref, sk_ref, rows_per_pair, chunk):
    rows = s_ref.shape[1]
    bounds = [0]
    for vd in values:
        bounds.append(bounds[-1] + vd.shape[0])
    sources = values

    for half in range(2):
        for r0 in range(0, rows, chunk):
            rs = slice(r0, r0 + chunk)
            sink = sinks[2 * (r0 // rows_per_pair) + half]
            parts = [s_ref[half, rs, bounds[i]:bounds[i + 1]] for i in range(len(sources))]
            if band is not None:
                parts[0] = jnp.where(band(r0 % rows_per_pair, chunk), parts[0], NEG_INF)
            m = sink
            for s in parts:
                m = jnp.maximum(m, jnp.max(s, axis=-1, keepdims=True))
            for i, s in enumerate(parts):
                p_ref[half, rs, bounds[i]:bounds[i + 1]] = jnp.exp2(s - m).astype(BF16)
            sk_ref[half, rs, :] = jnp.broadcast_to(jnp.exp2(sink - m), (chunk, LANES))

    o_e = o_o = None
    for i, vd in enumerate(values):
        v_lo, v_hi = _split_lo_hi(vd, 1.0)
        pe = _dot(p_ref[0, :, bounds[i]:bounds[i + 1]], v_lo)
        po = _dot(p_ref[1, :, bounds[i]:bounds[i + 1]], v_hi)
        o_e = pe if o_e is None else o_e + pe
        o_o = po if o_o is None else o_o + po
    lo = lax.broadcasted_iota(jnp.int32, (rows, LANES), 1) < HEAD_DIM
    num = jnp.where(lo, o_e, o_o)
    den = pltpu.roll(jnp.where(lo, o_o, o_e), HEAD_DIM, 1) + jnp.where(lo, sk_ref[0], sk_ref[1])
    return num / den


def _pair_cols(g):
    pa, pb = 2 * g, 2 * g + 1
    return slice(pa * LANES, (pa + 1) * LANES), slice(pb * LANES, (pb + 1) * LANES)


def _attn_tail_kernel(*refs, lat, n_tiles, tm, n, tiles_per_mod_row):
    sink_ref, q_ref, kd_ref, vd_ref = refs[:4]
    refs = refs[4:]
    if lat:
        ck_ref, cv_ref = refs[:2]
        refs = refs[2:]
    (x_ref, mod_ref, mix_ref, gate_ref, g2_ref,
     wf_ref, wao_ref, wout_ref, wup_ref, wdn_ref, o_ref, att_ref, s_ref, p_ref, sk_ref) = refs[:15]
    if lat:
        dist_ref, ckt_ref, cvd_ref = refs[15:]
    i = pl.program_id(0)

    def attention_stages():
        units = []
        if lat:
            qb = WINDOW
            span = qb + 2 * WINDOW
            t = jnp.minimum(i, n_tiles - 1) % (n // tm)
            ckt = ck_ref[0]
            ckt_ref[...] = jnp.concatenate(
                [ckt[0:HEAD_DIM], ckt[0:HEAD_DIM], ckt[HEAD_DIM:], ckt[HEAD_DIM:]], axis=0).astype(BF16)
            cvd_ref[...] = _dup_heads(cv_ref[0].T).astype(BF16)
            for rr in range(tm // qb):
                q0 = t * tm + rr * qb
                start = pl.multiple_of(jnp.clip(q0 - WINDOW, 0, n - span), WINDOW)
                dist_ref[rr] = jnp.abs((q0 - start) + lax.broadcasted_iota(jnp.int32, (qb, span), 0)
                                       - lax.broadcasted_iota(jnp.int32, (qb, span), 1))
                band = lambda r0, rows, rr=rr: dist_ref[rr, r0:r0 + rows, :] <= WINDOW
                rows = slice(rr * qb, (rr + 1) * qb)
                for g in range(N_KV_HEADS):
                    buf = len(units) % 2
                    cols = slice(g * LANES, (g + 1) * LANES)
                    qa, qbc = _pair_cols(g)

                    def scores(rows=rows, start=start, cols=cols, qa=qa, qbc=qbc, buf=buf):
                        q_stack = jnp.concatenate([q_ref[rows, qa], q_ref[rows, qbc]], axis=0)
                        _group_scores(q_stack, [kd_ref[cols, pl.ds(start, span)], ckt_ref[cols, :]],
                                      s_ref.at[buf])

                    def finish(rows=rows, start=start, g=g, cols=cols, qa=qa, qbc=qbc, buf=buf,
                               band=band):
                        sinks = [sink_ref[GROUP * g + h] * LOG2_E for h in range(GROUP)]
                        o = _group_softmax_pv(
                            [vd_ref[pl.ds(start, span), cols], cvd_ref[:, cols]], sinks, band,
                            s_ref.at[buf], p_ref.at[buf], sk_ref.at[buf], qb, LAT_SOFTMAX_ROWS)
                        att_ref[rows, qa] = o[0:qb].astype(BF16)
                        att_ref[rows, qbc] = o[qb:2 * qb].astype(BF16)

                    units.append((scores, finish))
        else:
            for s in range(tm // n):
                rows = slice(s * n, (s + 1) * n)
                for g in range(N_KV_HEADS):
                    buf = len(units) % 2
                    cols = slice(g * LANES, (g + 1) * LANES)
                    qa, qbc = _pair_cols(g)

                    def scores(rows=rows, cols=cols, qa=qa, qbc=qbc, buf=buf):
                        q_stack = jnp.concatenate([q_ref[rows, qa], q_ref[rows, qbc]], axis=0)
                        _group_scores(q_stack, [kd_ref[cols, rows]], s_ref.at[buf])

                    def finish(rows=rows, g=g, cols=cols, qa=qa, qbc=qbc, buf=buf):
                        sinks = [sink_ref[GROUP * g + h] * LOG2_E for h in range(GROUP)]
                        o = _group_softmax_pv([vd_ref[rows, cols]], sinks, None, s_ref.at[buf],
                                              p_ref.at[buf], sk_ref.at[buf], n, CTX_SOFTMAX_ROWS)
                        att_ref[rows, qa] = o[0:n].astype(BF16)
                        att_ref[rows, qbc] = o[n:2 * n].astype(BF16)

                    units.append((scores, finish))
        stages = [units[0][0]]
        for k, (_, finish) in enumerate(units):
            if k + 1 < len(units):
                stages.append(units[k + 1][0])
            stages.append(finish)
        return stages

    def tail(stages):
        gt1, sh2, sc2, gt2 = _mod_vectors(mod_ref, i - 1, tiles_per_mod_row)[2:6]
        yf = _dot(mix_ref[...], wf_ref[...])
        ya = _dot(att_ref[...], wao_ref[...])
        merged = (gate_ref[:, 0:D_MODEL].astype(F32) * yf
                  + gate_ref[:, D_MODEL:2 * D_MODEL].astype(F32) * ya)
        x1 = x_ref[...] + gt1 * _dot(merged.astype(BF16), wout_ref[...])

        h = x1 * lax.rsqrt(jnp.mean(x1 * x1, axis=-1, keepdims=True) + EPS) * g2_ref[...]
        hb = (h * (1.0 + sc2) + sh2).astype(BF16)
        d_ff = wdn_ref.shape[0]
        n_chunks = d_ff // FF_CHUNK
        acc = None
        issued = 0
        for c in range(n_chunks):
            a = _dot(hb, wup_ref[:, c * FF_CHUNK:(c + 1) * FF_CHUNK])
            u = _dot(hb, wup_ref[:, d_ff + c * FF_CHUNK:d_ff + (c + 1) * FF_CHUNK])
            act = (a * _sigmoid(a) * u).astype(BF16)
            d = _dot(act, wdn_ref[c * FF_CHUNK:(c + 1) * FF_CHUNK, :])
            acc = d if acc is None else acc + d
            while issued < len(stages) and issued * n_chunks < (c + 1) * len(stages):
                stages[issued]()
                issued += 1
        o_ref[...] = x1 + gt2 * acc

    @pl.when(i == 0)
    def _():
        for stage in attention_stages():
            stage()

    @pl.when(jnp.logical_and(i > 0, i < n_tiles))
    def _():
        tail(attention_stages())

    @pl.when(i == n_tiles)
    def _():
        tail([])


def _attn_tail(sinks, q, kd, vd, cache, x, mod3, mix, gate, g2, wf, wao, wout, wup, wdn,
               mod_row, rows_per_mod_row, n):
    t = x.shape[0]
    tm = TAIL_TOKEN_TILE
    n_tiles = t // tm
    lat = cache is not None
    tiles_per_mod_row = None if rows_per_mod_row is None else rows_per_mod_row // tm
    front = lambda i: (jnp.minimum(i, n_tiles - 1), 0)
    back = lambda i: (jnp.maximum(i - 1, 0), 0)
    in_specs = [pl.BlockSpec(memory_space=pltpu.SMEM), pl.BlockSpec((tm, ATTN_WIDTH), front)]
    args = [sinks, q, kd, vd]
    if lat:
        past = cache[0].shape[2]
        seq_of = lambda i: jnp.minimum(i, n_tiles - 1) // (n // tm)
        in_specs += [pl.BlockSpec((2 * KV_WIDTH, n), lambda i: (0, seq_of(i))),
                     pl.BlockSpec((n, 2 * KV_WIDTH), lambda i: (seq_of(i), 0))]
        in_specs += [pl.BlockSpec((1, KV_WIDTH, past), lambda i: (seq_of(i), 0, 0))] * 2
        args += list(cache)
        m_rows, keys = 2 * WINDOW, 3 * WINDOW + past
    else:
        in_specs += [pl.BlockSpec((2 * KV_WIDTH, tm), lambda i: (0, front(i)[0])),
                     pl.BlockSpec((tm, 2 * KV_WIDTH), front)]
        m_rows, keys = 2 * n, n
    in_specs += [pl.BlockSpec((tm, D_MODEL), back),
                 _mod_spec(mod_row),
                 pl.BlockSpec((tm, F_WIDTH), back),
                 pl.BlockSpec((tm, 2 * D_MODEL), back),
                 _resident((1, D_MODEL)),
                 _resident(wf.shape), _resident(wao.shape), _resident(wout.shape),
                 _resident(wup.shape), _resident(wdn.shape)]
    args += [x, mod3, mix, gate, g2, wf, wao, wout, wup, wdn]
    scratch = [pltpu.VMEM((tm, ATTN_WIDTH), BF16),
               pltpu.VMEM((2, 2, m_rows, keys), F32),
               pltpu.VMEM((2, 2, m_rows, keys), BF16),
               pltpu.VMEM((2, 2, m_rows, LANES), F32)]
    if lat:
        scratch += [pltpu.VMEM((tm // WINDOW, WINDOW, 3 * WINDOW), jnp.int32),
                    pltpu.VMEM((2 * KV_WIDTH, past), BF16),
                    pltpu.VMEM((past, 2 * KV_WIDTH), BF16)]
    return pl.pallas_call(
        functools.partial(_attn_tail_kernel, lat=lat, n_tiles=n_tiles, tm=tm, n=n,
                          tiles_per_mod_row=tiles_per_mod_row),
        grid=(n_tiles + 1,),
        in_specs=in_specs,
        out_specs=pl.BlockSpec((tm, D_MODEL), back),
        out_shape=pltpu.HBM((t, D_MODEL), F32),
        scratch_shapes=scratch,
        compiler_params=pltpu.CompilerParams(
            dimension_semantics=("arbitrary",), vmem_limit_bytes=RESIDENT_WEIGHTS_VMEM),
        name="attn_tail_lat" if lat else "attn_tail_ctx",
    )(sinks, *[_hbm(a) for a in args[1:]])


def _rope_tables(n):
    rows = n // GRID_W
    row = np.repeat(np.arange(rows, dtype=np.float64), GRID_W)
    col = np.tile(np.arange(GRID_W, dtype=np.float64), rows)
    axis_dim = HEAD_DIM // 2
    inv_freq = ROPE_THETA ** (-np.arange(0, axis_dim, 2, dtype=np.float64) / axis_dim)

    def axis_tabs(pos):
        ang = pos[:, None] * inv_freq[None, :]
        cos, sin = np.cos(ang), np.sin(ang)
        return np.concatenate([cos, cos], axis=-1), np.concatenate([-sin, sin], axis=-1)

    cr, sr = axis_tabs(row)
    cc, sc = axis_tabs(col)
    cos = np.tile(np.concatenate([cr, cc], axis=-1), (1, LANES // HEAD_DIM))
    sin = np.tile(np.concatenate([sr, sc], axis=-1), (1, LANES // HEAD_DIM))
    return jnp.asarray(cos.astype(np.float32)), jnp.asarray(sin.astype(np.float32))


@jax.jit
def _layer(xp, xs, ck, cv, c, c_ctx, w_ada, b_ada, g_norm1, g_norm2, w_in, g_q, g_k, sink,
           w_f, w_ao, w_out, w_up, w_down):
    bp, sp, _ = xp.shape
    bs, ss, _ = xs.shape
    past = ck.shape[1]

    mod3, w_in_b = _ada(c, c_ctx[None, :], w_ada, b_ada[None, :], w_in)
    g1, g2 = g_norm1[None, :], g_norm2[None, :]
    gq2, gk2 = g_q[None, :], g_k[None, :]
    ctx_mod = (CTX_MOD_ROW, None)
    lat_mod = (0, ss)

    xp2 = xp.reshape(bp * sp, D_MODEL)
    xs2 = xs.reshape(bs * ss, D_MODEL)
    zf, q, kd, vd, gate, k_new, v_new, wup = _proj(
        xp2, mod3, g1, w_in_b, gq2, gk2, None, *ctx_mod, sp, [w_up])
    zf_s, q_s, kd_s, vd_s, gate_s = _proj(
        xs2, mod3, g1, w_in_b, gq2, gk2, _rope_tables(ss), *lat_mod, 0, [])
    mix, wf, wao = _fourier(zf.reshape(bp, sp, F_WIDTH), CTX_SEQS_PER_STEP, [w_f, w_ao])
    mix_s, wout, wdn = _fourier(zf_s.reshape(bs, ss, F_WIDTH), 1, [w_out, w_down])

    yp = _attn_tail(sink, q, kd, vd, None, xp2, mod3, mix.reshape(bp * sp, F_WIDTH), gate, g2,
                    wf, wao, wout, wup, wdn, *ctx_mod, sp)

    def cache_t(t):
        return t.transpose(0, 2, 3, 1).reshape(bs, KV_WIDTH, past)

    ys = _attn_tail(sink, q_s, kd_s, vd_s, (cache_t(ck), cache_t(cv)), xs2, mod3,
                    mix_s.reshape(bs * ss, F_WIDTH), gate_s, g2, wf, wao, wout, wup, wdn, *lat_mod, ss)

    def cache_layout(t):
        return t.reshape(bp, N_KV_HEADS, HEAD_DIM, sp).transpose(0, 3, 1, 2)

    return yp.reshape(xp.shape), ys.reshape(xs.shape), cache_layout(k_new), cache_layout(v_new)


def kernel(x_prompt, x_sample, cache_k, cache_v, c, c_ctx, w_ada, b_ada, g_norm1, g_norm2,
           w_in, g_q, g_k, sinks, w_f, w_ao, w_out, w_up, w_down):
    depth = w_in.shape[0]
    xp, xs = x_prompt, x_sample
    new_k, new_v = [], []
    for l in range(depth):
        xp, xs, k_ctx, v_ctx = _layer(
            xp, xs, cache_k[:, l], cache_v[:, l], c, c_ctx, w_ada[l], b_ada[l], g_norm1[l],
            g_norm2[l], w_in[l], g_q[l], g_k[l], sinks[l], w_f[l], w_ao[l], w_out[l], w_up[l],
            w_down[l])
        new_k.append(k_ctx)
        new_v.append(v_ctx)
    return (xp, xs, jnp.stack(new_k, axis=1), jnp.stack(new_v, axis=1))
```

```python
import functools

import numpy as np
import jax
import jax.numpy as jnp
from jax import lax
from jax.experimental import pallas as pl
from jax.experimental.pallas import tpu as pltpu

D_MODEL = 1024
GRID_W = 64
N_HEADS = 8
N_KV_HEADS = 2
GROUP = N_HEADS // N_KV_HEADS
HEAD_DIM = 64
WINDOW = 128
F_GROUPS = 4
F_GROUP_DIM = 128
F_WIDTH = F_GROUPS * F_GROUP_DIM
ATTN_WIDTH = N_HEADS * HEAD_DIM
KV_WIDTH = N_KV_HEADS * HEAD_DIM
ROPE_THETA = 10000.0
EPS = 1e-6
NEG_INF = -1e30
LOG2_E = 1.4426950408889634

LANES = 128
MXU_DIM = 256
VMEM_BYTES_V7X = 64 * 1024 * 1024

_Q0 = F_WIDTH
_K0 = _Q0 + ATTN_WIDTH
_V0 = _K0 + KV_WIDTH
_G0 = _V0 + KV_WIDTH
IN_WIDTH = _G0 + 2 * D_MODEL

SUBLANES = 8
N_MOD = 6
MOD_ROWS = 16
CTX_MOD_ROW = 8
ADA_STEPS = 8

PROJ_TOKEN_TILE = 1024
TAIL_TOKEN_TILE = 512
FF_CHUNK = MXU_DIM
GATE_CHUNK = 2 * MXU_DIM
CTX_SOFTMAX_ROWS = 64
LAT_SOFTMAX_ROWS = 32

F32 = jnp.float32
BF16 = jnp.bfloat16


RESIDENT_WEIGHTS_VMEM = VMEM_BYTES_V7X - 6 * 2 ** 20


def _dot(a, b):
    return jnp.dot(a, b, preferred_element_type=F32)


def _sigmoid(x):
    return 1.0 / (1.0 + jnp.exp(-x))


def _resident(shape):
    zeros = (0,) * len(shape)
    return pl.BlockSpec(shape, lambda *_: zeros, pipeline_mode=pl.Buffered(1))


def _hbm(x):
    return pltpu.with_memory_space_constraint(x, pltpu.HBM)


def _pipelined(units):
    stages = [units[0][0]]
    for k, (_, second) in enumerate(units):
        if k + 1 < len(units):
            stages.append(units[k + 1][0])
        stages.append(second)
    return stages


def _ada_kernel(c_ref, cctx_ref, w_ref, b_ref, win_ref, o_ref, winb_ref):
    lat_rows = c_ref.shape[0]
    pieces = [c_ref[...]]
    if lat_rows < CTX_MOD_ROW:
        pieces.append(jnp.zeros((CTX_MOD_ROW - lat_rows, D_MODEL), F32))
    pieces.append(jnp.broadcast_to(cctx_ref[...], (MOD_ROWS - CTX_MOD_ROW, D_MODEL)))
    c = jnp.concatenate(pieces, axis=0)
    s = c * _sigmoid(c)
    o_ref[...] = _dot(s.astype(BF16), w_ref[...].astype(BF16)) + b_ref[...]
    winb_ref[...] = win_ref[...].astype(BF16)


def _ada(c, c_ctx, w_ada, b_ada, w_in):
    assert c.shape[0] <= CTX_MOD_ROW, "latent conditioning rows must fit below the context row"
    n = w_ada.shape[1]
    bn = n // ADA_STEPS
    slab = pl.BlockSpec((w_in.shape[0] // ADA_STEPS, w_in.shape[1]), lambda j: (j, 0))
    return pl.pallas_call(
        _ada_kernel,
        grid=(ADA_STEPS,),
        in_specs=[pl.BlockSpec(c.shape, lambda j: (0, 0)),
                  pl.BlockSpec((1, D_MODEL), lambda j: (0, 0)),
                  pl.BlockSpec((D_MODEL, bn), lambda j: (0, j)),
                  pl.BlockSpec((1, bn), lambda j: (0, j)),
                  slab],
        out_specs=[pl.BlockSpec((MOD_ROWS, bn), lambda j: (0, j)), slab],
        out_shape=[pltpu.HBM((MOD_ROWS, n), F32), pltpu.HBM(w_in.shape, BF16)],
        name="ada",
    )(c, c_ctx, w_ada, b_ada, w_in)


def _mod_spec(mod_row):
    return pl.BlockSpec((SUBLANES, N_MOD * D_MODEL), lambda i: (mod_row // SUBLANES, 0))


def _mod_vectors(mod_ref, tile, tiles_per_row):
    cols = [slice(j * D_MODEL, (j + 1) * D_MODEL) for j in range(N_MOD)]
    if tiles_per_row is None:
        return [mod_ref[0:1, c] for c in cols]
    r = tile // tiles_per_row
    return [mod_ref[pl.ds(r, 1), c] for c in cols]


def _head_norm(z, g):
    lo = lax.broadcasted_iota(jnp.int32, z.shape, 1) < HEAD_DIM
    s = z * z
    s_lo = jnp.sum(jnp.where(lo, s, 0.0), axis=-1, keepdims=True)
    s_hi = jnp.sum(jnp.where(lo, 0.0, s), axis=-1, keepdims=True)
    ms = jnp.where(lo, s_lo, s_hi) * (1.0 / HEAD_DIM)
    return z * lax.rsqrt(ms + EPS) * g


def _rope(y, cos, sin):
    lane = lax.broadcasted_iota(jnp.int32, y.shape, 1)
    first = (lane % 32) < 16
    partner = jnp.where(first, pltpu.roll(y, LANES - 16, 1), pltpu.roll(y, 16, 1))
    return y * cos + partner * sin


def _dup_heads(y):
    lo = lax.broadcasted_iota(jnp.int32, y.shape, 1) < HEAD_DIM
    sw = pltpu.roll(y, HEAD_DIM, 1)
    return jnp.concatenate([jnp.where(lo, y, sw), jnp.where(lo, sw, y)], axis=-1)


def _proj_kernel(*refs, rope, seq_len, fold, cache, n_cast, tiles_per_mod_row):
    x_ref, mod_ref, g1_ref, w_ref, gq_ref, gk_ref = refs[:6]
    refs = refs[6:]
    if rope:
        cos_ref, sin_ref = refs[:2]
        refs = refs[2:]
    n_tabs = 4 if fold else 2
    tabs, refs = refs[:n_tabs], refs[n_tabs:]
    cast_in, refs = refs[:n_cast], refs[n_cast:]
    mix_ref, q_ref, kd_ref, vd_ref, gate_ref = refs[:5]
    refs = refs[5:]
    if cache:
        kc_ref, vc_ref = refs[:2]
        refs = refs[2:]
    cast_out, (zf_ref, ab_ref) = refs[:n_cast], refs[n_cast:]
    for src_ref, dst_ref in zip(cast_in, cast_out):
        dst_ref[...] = src_ref[...].astype(BF16)

    x = x_ref[...]
    sh1, sc1 = _mod_vectors(mod_ref, pl.program_id(0), tiles_per_mod_row)[0:2]
    h = x * lax.rsqrt(jnp.mean(x * x, axis=-1, keepdims=True) + EPS) * g1_ref[...]
    hb = (h * (1.0 + sc1) + sh1).astype(BF16)

    zf_ref[...] = _dot(hb, w_ref[:, 0:_Q0]).astype(BF16)
    mixer = _fourier_stages(zf_ref, tabs, mix_ref, ab_ref, seq_len, fold)
    n_sections = 2 + 2 * D_MODEL // GATE_CHUNK
    issued = [0]

    def run_mixer_share(section):
        while issued[0] < len(mixer) and issued[0] * n_sections < (section + 1) * len(mixer):
            mixer[issued[0]]()
            issued[0] += 1

    gq = jnp.concatenate([gq_ref[...]] * (LANES // HEAD_DIM), axis=-1)
    gk = jnp.concatenate([gk_ref[...]] * (LANES // HEAD_DIM), axis=-1)

    zq = _dot(hb, w_ref[:, _Q0:_K0])
    for j in range(ATTN_WIDTH // LANES):
        y = _head_norm(zq[:, j * LANES:(j + 1) * LANES], gq)
        if rope:
            y = _rope(y, cos_ref[...], sin_ref[...])
        q_ref[:, j * LANES:(j + 1) * LANES] = (y * (HEAD_DIM ** -0.5 * LOG2_E)).astype(BF16)
    run_mixer_share(0)

    zkv = _dot(hb, w_ref[:, _K0:_G0])
    k = _head_norm(zkv[:, 0:KV_WIDTH], gk)
    v = zkv[:, KV_WIDTH:2 * KV_WIDTH]
    if cache:
        for s in range(x.shape[0] // seq_len):
            rows = slice(s * seq_len, (s + 1) * seq_len)
            kc_ref[s] = k[rows, :].T
            vc_ref[s] = v[rows, :].T
    if rope:
        k = _rope(k, cos_ref[...], sin_ref[...])
    kt = k.T
    kd_ref[...] = jnp.concatenate(
        [kt[0:HEAD_DIM], kt[0:HEAD_DIM], kt[HEAD_DIM:], kt[HEAD_DIM:]], axis=0).astype(BF16)
    vd_ref[...] = _dup_heads(v).astype(BF16)
    run_mixer_share(1)

    for j in range(2 * D_MODEL // GATE_CHUNK):
        cols = slice(j * GATE_CHUNK, (j + 1) * GATE_CHUNK)
        zg = _dot(hb, w_ref[:, _G0 + cols.start:_G0 + cols.stop])
        gate_ref[:, cols] = _sigmoid(zg).astype(BF16)
        run_mixer_share(2 + j)


def _proj(x, mod3, g1, w_in_b, gq2, gk2, seq_len, rope_tabs, mod_row, rows_per_mod_row, cache,
          cast_weights):
    t = x.shape[0]
    tm = PROJ_TOKEN_TILE
    assert tm % seq_len == 0, "a projection tile must hold whole sequences for the mixer"
    steps = t // tm
    rope = rope_tabs is not None
    row = lambda i: (i, 0)
    tiles_per_mod_row = None if rows_per_mod_row is None else rows_per_mod_row // tm
    in_specs = [pl.BlockSpec((tm, D_MODEL), row),
                _mod_spec(mod_row),
                _resident((1, D_MODEL)),
                _resident((D_MODEL, IN_WIDTH)),
                _resident((1, HEAD_DIM)),
                _resident((1, HEAD_DIM))]
    args = [x, mod3, g1, w_in_b, gq2, gk2]
    if rope:
        tiles_per_seq = rope_tabs[0].shape[0] // tm
        tab = pl.BlockSpec((tm, LANES), lambda i: (i % tiles_per_seq, 0))
        in_specs += [tab, tab]
        args += list(rope_tabs)
    fold = _fold_dft(seq_len)
    dft = _dft_tables(seq_len, fold)
    in_specs += [_resident(tab.shape) for tab in dft]
    args += list(dft)
    slabs = [pl.BlockSpec((w.shape[0] // steps, w.shape[1]), row) for w in cast_weights]
    in_specs += slabs
    args += list(cast_weights)
    out_specs = [pl.BlockSpec((tm, F_WIDTH), row),
                 pl.BlockSpec((tm, ATTN_WIDTH), row),
                 pl.BlockSpec((2 * KV_WIDTH, tm), lambda i: (0, i)),
                 pl.BlockSpec((tm, 2 * KV_WIDTH), row),
                 pl.BlockSpec((tm, 2 * D_MODEL), row)]
    out_shape = [pltpu.HBM((t, F_WIDTH), BF16),
                 pltpu.HBM((t, ATTN_WIDTH), BF16),
                 pltpu.HBM((2 * KV_WIDTH, t), BF16),
                 pltpu.HBM((t, 2 * KV_WIDTH), BF16),
                 pltpu.HBM((t, 2 * D_MODEL), BF16)]
    if cache:
        seqs = tm // seq_len
        out_specs += [pl.BlockSpec((seqs, KV_WIDTH, seq_len), lambda i: (i, 0, 0))] * 2
        out_shape += [pltpu.HBM((t // seq_len, KV_WIDTH, seq_len), F32)] * 2
    out_specs += slabs
    out_shape += [pltpu.HBM(w.shape, BF16) for w in cast_weights]
    return pl.pallas_call(
        functools.partial(_proj_kernel, rope=rope, seq_len=seq_len, fold=fold, cache=cache,
                          n_cast=len(cast_weights), tiles_per_mod_row=tiles_per_mod_row),
        grid=(steps,),
        in_specs=in_specs,
        out_specs=out_specs,
        out_shape=out_shape,
        scratch_shapes=[pltpu.VMEM((tm, F_WIDTH), BF16),
                        pltpu.VMEM((2, 2 * seq_len, F_WIDTH), BF16)],
        compiler_params=pltpu.CompilerParams(
            dimension_semantics=("arbitrary",), vmem_limit_bytes=RESIDENT_WEIGHTS_VMEM),
        name="proj_lat" if rope else "proj_ctx",
    )(*[_hbm(a) for a in args])


def _dft_tables(n, fold):
    def cs(m):
        idx = np.arange(m, dtype=np.int64)
        ang = 2.0 * np.pi * ((idx[:, None] * idx[None, :]) % m).astype(np.float64) / m
        return np.cos(ang) / np.sqrt(m), np.sin(ang) / np.sqrt(m)
    cd, sd = cs(F_GROUP_DIM)
    cn, sn = cs(n)
    tabs = [np.concatenate([cd, sd], axis=1)]
    if fold:
        half = n // 2
        rev = np.zeros((half, half), np.float64)
        rev[np.arange(1, half), half - np.arange(1, half)] = 1.0
        tabs += [cn[:half], -sn[:half], rev]
    else:
        tabs += [np.concatenate([cn, -sn], axis=1)]
    return tuple(jnp.asarray(t.astype(np.float32)).astype(BF16) for t in tabs)


def _fold_dft(n):
    return n // 2 >= 2 * MXU_DIM


def _fourier_stages(zf_ref, tabs, o_ref, ab_ref, n, fold):
    csd_ref = tabs[0]
    if fold:
        ch_ref, sh_ref, rev_ref = tabs[1:]
    else:
        (csn_ref,) = tabs[1:]
    half = n // 2
    units = []
    for s in range(zf_ref.shape[0] // n):
        buf = s % 2
        r0 = s * n

        def channels(r0=r0, buf=buf):
            for g in range(F_GROUPS):
                cols = slice(g * F_GROUP_DIM, (g + 1) * F_GROUP_DIM)
                ab = _dot(zf_ref[r0:r0 + n, cols], csd_ref[...])
                ab_ref[buf, 0:n, cols] = ab[:, 0:F_GROUP_DIM].astype(BF16)
                ab_ref[buf, n:2 * n, cols] = ab[:, F_GROUP_DIM:].astype(BF16)

        def positions(r0=r0, buf=buf):
            if not fold:
                o_ref[r0:r0 + n, :] = _dot(csn_ref[...], ab_ref[buf]).astype(BF16)
                return
            p = _dot(ch_ref[...], ab_ref[buf, 0:n, :])
            q = _dot(sh_ref[...], ab_ref[buf, n:2 * n, :])
            o_ref[r0:r0 + half, :] = (p + q).astype(BF16)
            upper = _dot(rev_ref[...], (p - q).astype(BF16))
            a = ab_ref[buf, 0:n, :].astype(F32)
            even = lax.broadcasted_iota(jnp.int32, a.shape, 0) % 2 == 0
            mid = jnp.sum(jnp.where(even, a, -a), axis=0, keepdims=True) * (n ** -0.5)
            first = lax.broadcasted_iota(jnp.int32, upper.shape, 0) == 0
            o_ref[r0 + half:r0 + n, :] = jnp.where(first, mid, upper).astype(BF16)

        units.append((channels, positions))
    return _pipelined(units)


def _split_lo_hi(xd, fill):
    lo = lax.broadcasted_iota(jnp.int32, xd.shape, 1) < HEAD_DIM
    other = jnp.full_like(xd, fill)
    return jnp.where(lo, xd, other), jnp.where(lo, other, xd)


def _group_scores(q_stack, keys, s_ref):
    c0 = 0
    for kt in keys:
        top = lax.broadcasted_iota(jnp.int32, kt.shape, 0) < HEAD_DIM
        zero = jnp.zeros_like(kt)
        c1 = c0 + kt.shape[1]
        s_ref[0, :, c0:c1] = _dot(q_stack, jnp.where(top, kt, zero))
        s_ref[1, :, c0:c1] = _dot(q_stack, jnp.where(top, zero, kt))
        c0 = c1


def _group_softmax_pv(values, sinks, band, s_ref, p_ref, sk_ref, rows_per_pair, chunk):
    rows = s_ref.shape[1]
    bounds = [0]
    for vd in values:
        bounds.append(bounds[-1] + vd.shape[0])
    sources = values

    for half in range(2):
        for r0 in range(0, rows, chunk):
            rs = slice(r0, r0 + chunk)
            sink = sinks[2 * (r0 // rows_per_pair) + half]
            parts = [s_ref[half, rs, bounds[i]:bounds[i + 1]] for i in range(len(sources))]
            if band is not None:
                parts[0] = jnp.where(band(r0 % rows_per_pair, chunk), parts[0], NEG_INF)
            m = sink
            for s in parts:
                m = jnp.maximum(m, jnp.max(s, axis=-1, keepdims=True))
            for i, s in enumerate(parts):
                p_ref[half, rs, bounds[i]:bounds[i + 1]] = jnp.exp2(s - m).astype(BF16)
            sk_ref[half, rs, :] = jnp.broadcast_to(jnp.exp2(sink - m), (chunk, LANES))

    o_e = o_o = None
    for i, vd in enumerate(values):
        v_lo, v_hi = _split_lo_hi(vd, 1.0)
        pe = _dot(p_ref[0, :, bounds[i]:bounds[i + 1]], v_lo)
        po = _dot(p_ref[1, :, bounds[i]:bounds[i + 1]], v_hi)
        o_e = pe if o_e is None else o_e + pe
        o_o = po if o_o is None else o_o + po
    lo = lax.broadcasted_iota(jnp.int32, (rows, LANES), 1) < HEAD_DIM
    num = jnp.where(lo, o_e, o_o)
    den = pltpu.roll(jnp.where(lo, o_o, o_e), HEAD_DIM, 1) + jnp.where(lo, sk_ref[0], sk_ref[1])
    return num / den


def _pair_cols(g):
    pa, pb = 2 * g, 2 * g + 1
    return slice(pa * LANES, (pa + 1) * LANES), slice(pb * LANES, (pb + 1) * LANES)


def _attn_tail_kernel(*refs, lat, n_tiles, tm, n, tiles_per_mod_row):
    sink_ref, q_ref, kd_ref, vd_ref = refs[:4]
    refs = refs[4:]
    if lat:
        ck_ref, cv_ref = refs[:2]
        refs = refs[2:]
    (x_ref, mod_ref, mix_ref, gate_ref, g2_ref,
     wf_ref, wao_ref, wout_ref, wup_ref, wdn_ref, o_ref, att_ref, s_ref, p_ref, sk_ref) = refs[:15]
    if lat:
        dist_ref, ckt_ref, cvd_ref = refs[15:]
    i = pl.program_id(0)

    def attention_stages():
        units = []
        if lat:
            qb = WINDOW
            span = qb + 2 * WINDOW
            t = jnp.minimum(i, n_tiles - 1) % (n // tm)
            ckt = ck_ref[0]
            ckt_ref[...] = jnp.concatenate(
                [ckt[0:HEAD_DIM], ckt[0:HEAD_DIM], ckt[HEAD_DIM:], ckt[HEAD_DIM:]], axis=0).astype(BF16)
            cvd_ref[...] = _dup_heads(cv_ref[0].T).astype(BF16)
            for rr in range(tm // qb):
                q0 = t * tm + rr * qb
                start = pl.multiple_of(jnp.clip(q0 - WINDOW, 0, n - span), WINDOW)
                dist_ref[rr] = jnp.abs((q0 - start) + lax.broadcasted_iota(jnp.int32, (qb, span), 0)
                                       - lax.broadcasted_iota(jnp.int32, (qb, span), 1))
                band = lambda r0, rows, rr=rr: dist_ref[rr, r0:r0 + rows, :] <= WINDOW
                rows = slice(rr * qb, (rr + 1) * qb)
                for g in range(N_KV_HEADS):
                    buf = len(units) % 2
                    cols = slice(g * LANES, (g + 1) * LANES)
                    qa, qbc = _pair_cols(g)

                    def scores(rows=rows, start=start, cols=cols, qa=qa, qbc=qbc, buf=buf):
                        q_stack = jnp.concatenate([q_ref[rows, qa], q_ref[rows, qbc]], axis=0)
                        _group_scores(q_stack, [kd_ref[cols, pl.ds(start, span)], ckt_ref[cols, :]],
                                      s_ref.at[buf])

                    def finish(rows=rows, start=start, g=g, cols=cols, qa=qa, qbc=qbc, buf=buf,
                               band=band):
                        sinks = [sink_ref[GROUP * g + h] * LOG2_E for h in range(GROUP)]
                        o = _group_softmax_pv(
                            [vd_ref[pl.ds(start, span), cols], cvd_ref[:, cols]], sinks, band,
                            s_ref.at[buf], p_ref.at[buf], sk_ref.at[buf], qb, LAT_SOFTMAX_ROWS)
                        att_ref[rows, qa] = o[0:qb].astype(BF16)
                        att_ref[rows, qbc] = o[qb:2 * qb].astype(BF16)

                    units.append((scores, finish))
        else:
            for s in range(tm // n):
                rows = slice(s * n, (s + 1) * n)
                for g in range(N_KV_HEADS):
                    buf = len(units) % 2
                    cols = slice(g * LANES, (g + 1) * LANES)
                    qa, qbc = _pair_cols(g)

                    def scores(rows=rows, cols=cols, qa=qa, qbc=qbc, buf=buf):
                        q_stack = jnp.concatenate([q_ref[rows, qa], q_ref[rows, qbc]], axis=0)
                        _group_scores(q_stack, [kd_ref[cols, rows]], s_ref.at[buf])

                    def finish(rows=rows, g=g, cols=cols, qa=qa, qbc=qbc, buf=buf):
                        sinks = [sink_ref[GROUP * g + h] * LOG2_E for h in range(GROUP)]
                        o = _group_softmax_pv([vd_ref[rows, cols]], sinks, None, s_ref.at[buf],
                                              p_ref.at[buf], sk_ref.at[buf], n, CTX_SOFTMAX_ROWS)
                        att_ref[rows, qa] = o[0:n].astype(BF16)
                        att_ref[rows, qbc] = o[n:2 * n].astype(BF16)

                    units.append((scores, finish))
        return _pipelined(units)

    def tail(stages):
        gt1, sh2, sc2, gt2 = _mod_vectors(mod_ref, i - 1, tiles_per_mod_row)[2:6]
        yf = _dot(mix_ref[...], wf_ref[...])
        ya = _dot(att_ref[...], wao_ref[...])
        merged = (gate_ref[:, 0:D_MODEL].astype(F32) * yf
                  + gate_ref[:, D_MODEL:2 * D_MODEL].astype(F32) * ya)
        x1 = x_ref[...] + gt1 * _dot(merged.astype(BF16), wout_ref[...])

        h = x1 * lax.rsqrt(jnp.mean(x1 * x1, axis=-1, keepdims=True) + EPS) * g2_ref[...]
        hb = (h * (1.0 + sc2) + sh2).astype(BF16)
        d_ff = wdn_ref.shape[0]
        n_chunks = d_ff // FF_CHUNK
        acc = None
        issued = 0
        for c in range(n_chunks):
            a = _dot(hb, wup_ref[:, c * FF_CHUNK:(c + 1) * FF_CHUNK])
            u = _dot(hb, wup_ref[:, d_ff + c * FF_CHUNK:d_ff + (c + 1) * FF_CHUNK])
            act = (a * _sigmoid(a) * u).astype(BF16)
            d = _dot(act, wdn_ref[c * FF_CHUNK:(c + 1) * FF_CHUNK, :])
            acc = d if acc is None else acc + d
            while issued < len(stages) and issued * n_chunks < (c + 1) * len(stages):
                stages[issued]()
                issued += 1
        o_ref[...] = x1 + gt2 * acc

    @pl.when(i == 0)
    def _():
        for stage in attention_stages():
            stage()

    @pl.when(jnp.logical_and(i > 0, i < n_tiles))
    def _():
        tail(attention_stages())

    @pl.when(i == n_tiles)
    def _():
        tail([])


def _attn_tail(sinks, q, kd, vd, cache, x, mod3, mix, gate, g2, wf, wao, wout, wup, wdn,
               mod_row, rows_per_mod_row, n):
    t = x.shape[0]
    tm = TAIL_TOKEN_TILE
    n_tiles = t // tm
    lat = cache is not None
    tiles_per_mod_row = None if rows_per_mod_row is None else rows_per_mod_row // tm
    front = lambda i: (jnp.minimum(i, n_tiles - 1), 0)
    back = lambda i: (jnp.maximum(i - 1, 0), 0)
    in_specs = [pl.BlockSpec(memory_space=pltpu.SMEM), pl.BlockSpec((tm, ATTN_WIDTH), front)]
    args = [sinks, q, kd, vd]
    if lat:
        past = cache[0].shape[2]
        seq_of = lambda i: jnp.minimum(i, n_tiles - 1) // (n // tm)
        in_specs += [pl.BlockSpec((2 * KV_WIDTH, n), lambda i: (0, seq_of(i))),
                     pl.BlockSpec((n, 2 * KV_WIDTH), lambda i: (seq_of(i), 0))]
        in_specs += [pl.BlockSpec((1, KV_WIDTH, past), lambda i: (seq_of(i), 0, 0))] * 2
        args += list(cache)
        m_rows, keys = 2 * WINDOW, 3 * WINDOW + past
    else:
        in_specs += [pl.BlockSpec((2 * KV_WIDTH, tm), lambda i: (0, front(i)[0])),
                     pl.BlockSpec((tm, 2 * KV_WIDTH), front)]
        m_rows, keys = 2 * n, n
    in_specs += [pl.BlockSpec((tm, D_MODEL), back),
                 _mod_spec(mod_row),
                 pl.BlockSpec((tm, F_WIDTH), back),
                 pl.BlockSpec((tm, 2 * D_MODEL), back),
                 _resident((1, D_MODEL)),
                 _resident(wf.shape), _resident(wao.shape), _resident(wout.shape),
                 _resident(wup.shape), _resident(wdn.shape)]
    args += [x, mod3, mix, gate, g2, wf, wao, wout, wup, wdn]
    scratch = [pltpu.VMEM((tm, ATTN_WIDTH), BF16),
               pltpu.VMEM((2, 2, m_rows, keys), F32),
               pltpu.VMEM((2, 2, m_rows, keys), BF16),
               pltpu.VMEM((2, 2, m_rows, LANES), F32)]
    if lat:
        scratch += [pltpu.VMEM((tm // WINDOW, WINDOW, 3 * WINDOW), jnp.int32),
                    pltpu.VMEM((2 * KV_WIDTH, past), BF16),
                    pltpu.VMEM((past, 2 * KV_WIDTH), BF16)]
    return pl.pallas_call(
        functools.partial(_attn_tail_kernel, lat=lat, n_tiles=n_tiles, tm=tm, n=n,
                          tiles_per_mod_row=tiles_per_mod_row),
        grid=(n_tiles + 1,),
        in_specs=in_specs,
        out_specs=pl.BlockSpec((tm, D_MODEL), back),
        out_shape=pltpu.HBM((t, D_MODEL), F32),
        scratch_shapes=scratch,
        compiler_params=pltpu.CompilerParams(
            dimension_semantics=("arbitrary",), vmem_limit_bytes=RESIDENT_WEIGHTS_VMEM),
        name="attn_tail_lat" if lat else "attn_tail_ctx",
    )(sinks, *[_hbm(a) for a in args[1:]])


def _rope_tables(n):
    rows = n // GRID_W
    row = np.repeat(np.arange(rows, dtype=np.float64), GRID_W)
    col = np.tile(np.arange(GRID_W, dtype=np.float64), rows)
    axis_dim = HEAD_DIM // 2
    inv_freq = ROPE_THETA ** (-np.arange(0, axis_dim, 2, dtype=np.float64) / axis_dim)

    def axis_tabs(pos):
        ang = pos[:, None] * inv_freq[None, :]
        cos, sin = np.cos(ang), np.sin(ang)
        return np.concatenate([cos, cos], axis=-1), np.concatenate([-sin, sin], axis=-1)

    cr, sr = axis_tabs(row)
    cc, sc = axis_tabs(col)
    cos = np.tile(np.concatenate([cr, cc], axis=-1), (1, LANES // HEAD_DIM))
    sin = np.tile(np.concatenate([sr, sc], axis=-1), (1, LANES // HEAD_DIM))
    return jnp.asarray(cos.astype(np.float32)), jnp.asarray(sin.astype(np.float32))


@jax.jit
def _layer(xp, xs, ck, cv, c, c_ctx, w_ada, b_ada, g_norm1, g_norm2, w_in, g_q, g_k, sink,
           w_f, w_ao, w_out, w_up, w_down):
    bp, sp, _ = xp.shape
    bs, ss, _ = xs.shape
    past = ck.shape[1]

    mod3, w_in_b = _ada(c, c_ctx[None, :], w_ada, b_ada[None, :], w_in)
    g1, g2 = g_norm1[None, :], g_norm2[None, :]
    gq2, gk2 = g_q[None, :], g_k[None, :]
    ctx_mod = (CTX_MOD_ROW, None)
    lat_mod = (0, ss)

    xp2 = xp.reshape(bp * sp, D_MODEL)
    xs2 = xs.reshape(bs * ss, D_MODEL)
    mix, q, kd, vd, gate, k_new, v_new, wup = _proj(
        xp2, mod3, g1, w_in_b, gq2, gk2, sp, None, *ctx_mod, True, [w_up])
    mix_s, q_s, kd_s, vd_s, gate_s, wf, wao, wout, wdn = _proj(
        xs2, mod3, g1, w_in_b, gq2, gk2, ss, _rope_tables(ss), *lat_mod, False,
        [w_f, w_ao, w_out, w_down])

    yp = _attn_tail(sink, q, kd, vd, None, xp2, mod3, mix, gate, g2,
                    wf, wao, wout, wup, wdn, *ctx_mod, sp)

    def cache_t(t):
        return t.transpose(0, 2, 3, 1).reshape(bs, KV_WIDTH, past)

    ys = _attn_tail(sink, q_s, kd_s, vd_s, (cache_t(ck), cache_t(cv)), xs2, mod3,
                    mix_s, gate_s, g2, wf, wao, wout, wup, wdn, *lat_mod, ss)

    def cache_layout(t):
        return t.reshape(bp, N_KV_HEADS, HEAD_DIM, sp).transpose(0, 3, 1, 2)

    return yp.reshape(xp.shape), ys.reshape(xs.shape), cache_layout(k_new), cache_layout(v_new)


def kernel(x_prompt, x_sample, cache_k, cache_v, c, c_ctx, w_ada, b_ada, g_norm1, g_norm2,
           w_in, g_q, g_k, sinks, w_f, w_ao, w_out, w_up, w_down):
    depth = w_in.shape[0]
    xp, xs = x_prompt, x_sample
    new_k, new_v = [], []
    for l in range(depth):
        xp, xs, k_ctx, v_ctx = _layer(
            xp, xs, cache_k[:, l], cache_v[:, l], c, c_ctx, w_ada[l], b_ada[l], g_norm1[l],
            g_norm2[l], w_in[l], g_q[l], g_k[l], sinks[l], w_f[l], w_ao[l], w_out[l], w_up[l],
            w_down[l])
        new_k.append(k_ctx)
        new_v.append(v_ctx)
    return (xp, xs, jnp.stack(new_k, axis=1), jnp.stack(new_v, axis=1))
```

```python
import functools

import numpy as np
import jax
import jax.numpy as jnp
from jax import lax
from jax.experimental import pallas as pl
from jax.experimental.pallas import tpu as pltpu

D_MODEL = 1024
GRID_W = 64
N_HEADS = 8
N_KV_HEADS = 2
GROUP = N_HEADS // N_KV_HEADS
HEAD_DIM = 64
WINDOW = 128
F_GROUPS = 4
F_GROUP_DIM = 128
F_WIDTH = F_GROUPS * F_GROUP_DIM
ATTN_WIDTH = N_HEADS * HEAD_DIM
KV_WIDTH = N_KV_HEADS * HEAD_DIM
ROPE_THETA = 10000.0
EPS = 1e-6
NEG_INF = -1e30
LOG2_E = 1.4426950408889634

LANES = 128
MXU_DIM = 256
VMEM_BYTES_V7X = 64 * 1024 * 1024

_Q0 = F_WIDTH
_K0 = _Q0 + ATTN_WIDTH
_V0 = _K0 + KV_WIDTH
_G0 = _V0 + KV_WIDTH
IN_WIDTH = _G0 + 2 * D_MODEL

SUBLANES = 8
N_MOD = 6
MOD_ROWS = 16
CTX_MOD_ROW = 8
ADA_STEPS = 8

PROJ_TOKEN_TILE = 1024
TAIL_TOKEN_TILE = 512
FF_CHUNK = MXU_DIM
GATE_CHUNK = 2 * MXU_DIM
CTX_SOFTMAX_ROWS = 64
LAT_SOFTMAX_ROWS = 32

F32 = jnp.float32
BF16 = jnp.bfloat16


RESIDENT_WEIGHTS_VMEM = VMEM_BYTES_V7X - 6 * 2 ** 20


def _dot(a, b):
    return jnp.dot(a, b, preferred_element_type=F32)


def _sigmoid(x):
    return 1.0 / (1.0 + jnp.exp(-x))


def _resident(shape):
    zeros = (0,) * len(shape)
    return pl.BlockSpec(shape, lambda *_: zeros, pipeline_mode=pl.Buffered(1))


def _hbm(x):
    return pltpu.with_memory_space_constraint(x, pltpu.HBM)


def _pipelined(units):
    stages = [units[0][0]]
    for k, (_, second) in enumerate(units):
        if k + 1 < len(units):
            stages.append(units[k + 1][0])
        stages.append(second)
    return stages


def _ada_kernel(c_ref, cctx_ref, w_ref, b_ref, win_ref, o_ref, winb_ref):
    lat_rows = c_ref.shape[0]
    pieces = [c_ref[...]]
    if lat_rows < CTX_MOD_ROW:
        pieces.append(jnp.zeros((CTX_MOD_ROW - lat_rows, D_MODEL), F32))
    pieces.append(jnp.broadcast_to(cctx_ref[...], (MOD_ROWS - CTX_MOD_ROW, D_MODEL)))
    c = jnp.concatenate(pieces, axis=0)
    s = c * _sigmoid(c)
    o_ref[...] = _dot(s.astype(BF16), w_ref[...].astype(BF16)) + b_ref[...]
    winb_ref[...] = win_ref[...].astype(BF16)


def _ada(c, c_ctx, w_ada, b_ada, w_in):
    assert c.shape[0] <= CTX_MOD_ROW, "latent conditioning rows must fit below the context row"
    n = w_ada.shape[1]
    bn = n // ADA_STEPS
    slab = pl.BlockSpec((w_in.shape[0] // ADA_STEPS, w_in.shape[1]), lambda j: (j, 0))
    return pl.pallas_call(
        _ada_kernel,
        grid=(ADA_STEPS,),
        in_specs=[pl.BlockSpec(c.shape, lambda j: (0, 0)),
                  pl.BlockSpec((1, D_MODEL), lambda j: (0, 0)),
                  pl.BlockSpec((D_MODEL, bn), lambda j: (0, j)),
                  pl.BlockSpec((1, bn), lambda j: (0, j)),
                  slab],
        out_specs=[pl.BlockSpec((MOD_ROWS, bn), lambda j: (0, j)), slab],
        out_shape=[pltpu.HBM((MOD_ROWS, n), F32), pltpu.HBM(w_in.shape, BF16)],
        name="ada",
    )(c, c_ctx, w_ada, b_ada, w_in)


def _mod_spec(mod_row):
    return pl.BlockSpec((SUBLANES, N_MOD * D_MODEL), lambda i: (mod_row // SUBLANES, 0))


def _mod_vectors(mod_ref, tile, tiles_per_row):
    cols = [slice(j * D_MODEL, (j + 1) * D_MODEL) for j in range(N_MOD)]
    if tiles_per_row is None:
        return [mod_ref[0:1, c] for c in cols]
    r = tile // tiles_per_row
    return [mod_ref[pl.ds(r, 1), c] for c in cols]


def _head_norm(z, g):
    lo = lax.broadcasted_iota(jnp.int32, z.shape, 1) < HEAD_DIM
    s = z * z
    s_lo = jnp.sum(jnp.where(lo, s, 0.0), axis=-1, keepdims=True)
    s_hi = jnp.sum(jnp.where(lo, 0.0, s), axis=-1, keepdims=True)
    ms = jnp.where(lo, s_lo, s_hi) * (1.0 / HEAD_DIM)
    return z * lax.rsqrt(ms + EPS) * g


def _rope(y, cos, sin):
    lane = lax.broadcasted_iota(jnp.int32, y.shape, 1)
    first = (lane % 32) < 16
    partner = jnp.where(first, pltpu.roll(y, LANES - 16, 1), pltpu.roll(y, 16, 1))
    return y * cos + partner * sin


def _dup_heads(y):
    lo = lax.broadcasted_iota(jnp.int32, y.shape, 1) < HEAD_DIM
    sw = pltpu.roll(y, HEAD_DIM, 1)
    return jnp.concatenate([jnp.where(lo, y, sw), jnp.where(lo, sw, y)], axis=-1)


def _proj_kernel(*refs, rope, seq_len, fold, cache, n_cast, tiles_per_mod_row):
    x_ref, mod_ref, g1_ref, w_ref, gq_ref, gk_ref = refs[:6]
    refs = refs[6:]
    if rope:
        cos_ref, sin_ref = refs[:2]
        refs = refs[2:]
    n_tabs = 4 if fold else 2
    tabs, refs = refs[:n_tabs], refs[n_tabs:]
    cast_in, refs = refs[:n_cast], refs[n_cast:]
    mix_ref, q_ref, kd_ref, vd_ref, gate_ref = refs[:5]
    refs = refs[5:]
    if cache:
        kc_ref, vc_ref = refs[:2]
        refs = refs[2:]
    cast_out, (zf_ref, ab_ref) = refs[:n_cast], refs[n_cast:]
    for src_ref, dst_ref in zip(cast_in, cast_out):
        dst_ref[...] = src_ref[...].astype(BF16)

    x = x_ref[...]
    sh1, sc1 = _mod_vectors(mod_ref, pl.program_id(0), tiles_per_mod_row)[0:2]
    h = x * lax.rsqrt(jnp.mean(x * x, axis=-1, keepdims=True) + EPS) * g1_ref[...]
    hb = (h * (1.0 + sc1) + sh1).astype(BF16)

    zf_ref[...] = _dot(hb, w_ref[:, 0:_Q0]).astype(BF16)
    mixer = _fourier_stages(zf_ref, tabs, mix_ref, ab_ref, seq_len, fold)
    n_sections = 2 + 2 * D_MODEL // GATE_CHUNK
    issued = [0]

    def run_mixer_share(section):
        while issued[0] < len(mixer) and issued[0] * n_sections < (section + 1) * len(mixer):
            mixer[issued[0]]()
            issued[0] += 1

    gq = jnp.concatenate([gq_ref[...]] * (LANES // HEAD_DIM), axis=-1)
    gk = jnp.concatenate([gk_ref[...]] * (LANES // HEAD_DIM), axis=-1)

    zq = _dot(hb, w_ref[:, _Q0:_K0])
    for j in range(ATTN_WIDTH // LANES):
        y = _head_norm(zq[:, j * LANES:(j + 1) * LANES], gq)
        if rope:
            y = _rope(y, cos_ref[...], sin_ref[...])
        q_ref[:, j * LANES:(j + 1) * LANES] = (y * (HEAD_DIM ** -0.5 * LOG2_E)).astype(BF16)
    run_mixer_share(0)

    zkv = _dot(hb, w_ref[:, _K0:_G0])
    k = _head_norm(zkv[:, 0:KV_WIDTH], gk)
    v = zkv[:, KV_WIDTH:2 * KV_WIDTH]
    if cache:
        for s in range(x.shape[0] // seq_len):
            rows = slice(s * seq_len, (s + 1) * seq_len)
            kc_ref[s] = k[rows, :].T
            vc_ref[s] = v[rows, :].T
    if rope:
        k = _rope(k, cos_ref[...], sin_ref[...])
    kt = k.T
    kd_ref[...] = jnp.concatenate(
        [kt[0:HEAD_DIM], kt[0:HEAD_DIM], kt[HEAD_DIM:], kt[HEAD_DIM:]], axis=0).astype(BF16)
    vd_ref[...] = _dup_heads(v).astype(BF16)
    run_mixer_share(1)

    for j in range(2 * D_MODEL // GATE_CHUNK):
        cols = slice(j * GATE_CHUNK, (j + 1) * GATE_CHUNK)
        zg = _dot(hb, w_ref[:, _G0 + cols.start:_G0 + cols.stop])
        gate_ref[:, cols] = _sigmoid(zg).astype(BF16)
        run_mixer_share(2 + j)


def _proj(x, mod3, g1, w_in_b, gq2, gk2, seq_len, rope_tabs, mod_row, rows_per_mod_row, cache,
          cast_weights):
    t = x.shape[0]
    tm = PROJ_TOKEN_TILE
    assert tm % seq_len == 0, "a projection tile must hold whole sequences for the mixer"
    steps = t // tm
    rope = rope_tabs is not None
    row = lambda i: (i, 0)
    tiles_per_mod_row = None if rows_per_mod_row is None else rows_per_mod_row // tm
    in_specs = [pl.BlockSpec((tm, D_MODEL), row),
                _mod_spec(mod_row),
                _resident((1, D_MODEL)),
                _resident((D_MODEL, IN_WIDTH)),
                _resident((1, HEAD_DIM)),
                _resident((1, HEAD_DIM))]
    args = [x, mod3, g1, w_in_b, gq2, gk2]
    if rope:
        tiles_per_seq = rope_tabs[0].shape[0] // tm
        tab = pl.BlockSpec((tm, LANES), lambda i: (i % tiles_per_seq, 0))
        in_specs += [tab, tab]
        args += list(rope_tabs)
    fold = _fold_dft(seq_len)
    dft = _dft_tables(seq_len, fold)
    in_specs += [_resident(tab.shape) for tab in dft]
    args += list(dft)
    slabs = [pl.BlockSpec((w.shape[0] // steps, w.shape[1]), row) for w in cast_weights]
    in_specs += slabs
    args += list(cast_weights)
    out_specs = [pl.BlockSpec((tm, F_WIDTH), row),
                 pl.BlockSpec((tm, ATTN_WIDTH), row),
                 pl.BlockSpec((2 * KV_WIDTH, tm), lambda i: (0, i)),
                 pl.BlockSpec((tm, 2 * KV_WIDTH), row),
                 pl.BlockSpec((tm, 2 * D_MODEL), row)]
    out_shape = [pltpu.HBM((t, F_WIDTH), BF16),
                 pltpu.HBM((t, ATTN_WIDTH), BF16),
                 pltpu.HBM((2 * KV_WIDTH, t), BF16),
                 pltpu.HBM((t, 2 * KV_WIDTH), BF16),
                 pltpu.HBM((t, 2 * D_MODEL), BF16)]
    if cache:
        seqs = tm // seq_len
        out_specs += [pl.BlockSpec((seqs, KV_WIDTH, seq_len), lambda i: (i, 0, 0))] * 2
        out_shape += [pltpu.HBM((t // seq_len, KV_WIDTH, seq_len), F32)] * 2
    out_specs += slabs
    out_shape += [pltpu.HBM(w.shape, BF16) for w in cast_weights]
    return pl.pallas_call(
        functools.partial(_proj_kernel, rope=rope, seq_len=seq_len, fold=fold, cache=cache,
                          n_cast=len(cast_weights), tiles_per_mod_row=tiles_per_mod_row),
        grid=(steps,),
        in_specs=in_specs,
        out_specs=out_specs,
        out_shape=out_shape,
        scratch_shapes=[pltpu.VMEM((tm, F_WIDTH), BF16),
                        pltpu.VMEM((2, 2 * seq_len, F_WIDTH), BF16)],
        compiler_params=pltpu.CompilerParams(
            dimension_semantics=("arbitrary",), vmem_limit_bytes=RESIDENT_WEIGHTS_VMEM),
        name="proj_lat" if rope else "proj_ctx",
    )(*[_hbm(a) for a in args])


def _dft_tables(n, fold):
    def cs(m):
        idx = np.arange(m, dtype=np.int64)
        ang = 2.0 * np.pi * ((idx[:, None] * idx[None, :]) % m).astype(np.float64) / m
        return np.cos(ang) / np.sqrt(m), np.sin(ang) / np.sqrt(m)
    cd, sd = cs(F_GROUP_DIM)
    cn, sn = cs(n)
    tabs = [np.concatenate([cd, sd], axis=1)]
    if fold:
        half = n // 2
        rev = np.zeros((half, half), np.float64)
        rev[np.arange(1, half), half - np.arange(1, half)] = 1.0
        tabs += [cn[:half], -sn[:half], rev]
    else:
        tabs += [np.concatenate([cn, -sn], axis=1)]
    return tuple(jnp.asarray(t.astype(np.float32)).astype(BF16) for t in tabs)


def _fold_dft(n):
    return n // 2 >= 2 * MXU_DIM


def _fourier_stages(zf_ref, tabs, o_ref, ab_ref, n, fold):
    csd_ref = tabs[0]
    if fold:
        ch_ref, sh_ref, rev_ref = tabs[1:]
    else:
        (csn_ref,) = tabs[1:]
    half = n // 2
    units = []
    for s in range(zf_ref.shape[0] // n):
        buf = s % 2
        r0 = s * n

        def channels(r0=r0, buf=buf):
            for g in range(F_GROUPS):
                cols = slice(g * F_GROUP_DIM, (g + 1) * F_GROUP_DIM)
                ab = _dot(zf_ref[r0:r0 + n, cols], csd_ref[...])
                ab_ref[buf, 0:n, cols] = ab[:, 0:F_GROUP_DIM].astype(BF16)
                ab_ref[buf, n:2 * n, cols] = ab[:, F_GROUP_DIM:].astype(BF16)

        def positions(r0=r0, buf=buf):
            if not fold:
                o_ref[r0:r0 + n, :] = _dot(csn_ref[...], ab_ref[buf]).astype(BF16)
                return
            p = _dot(ch_ref[...], ab_ref[buf, 0:n, :])
            q = _dot(sh_ref[...], ab_ref[buf, n:2 * n, :])
            o_ref[r0:r0 + half, :] = (p + q).astype(BF16)
            upper = _dot(rev_ref[...], (p - q).astype(BF16))
            a = ab_ref[buf, 0:n, :].astype(F32)
            even = lax.broadcasted_iota(jnp.int32, a.shape, 0) % 2 == 0
            mid = jnp.sum(jnp.where(even, a, -a), axis=0, keepdims=True) * (n ** -0.5)
            first = lax.broadcasted_iota(jnp.int32, upper.shape, 0) == 0
            o_ref[r0 + half:r0 + n, :] = jnp.where(first, mid, upper).astype(BF16)

        units.append((channels, positions))
    return _pipelined(units)


def _split_lo_hi(xd, fill):
    lo = lax.broadcasted_iota(jnp.int32, xd.shape, 1) < HEAD_DIM
    other = jnp.full_like(xd, fill)
    return jnp.where(lo, xd, other), jnp.where(lo, other, xd)


def _group_scores(q_stack, keys, s_ref):
    c0 = 0
    for kt in keys:
        top = lax.broadcasted_iota(jnp.int32, kt.shape, 0) < HEAD_DIM
        zero = jnp.zeros_like(kt)
        c1 = c0 + kt.shape[1]
        s_ref[0, :, c0:c1] = _dot(q_stack, jnp.where(top, kt, zero))
        s_ref[1, :, c0:c1] = _dot(q_stack, jnp.where(top, zero, kt))
        c0 = c1


def _group_softmax_pv(values, sinks, band, s_ref, p_ref, sk_ref, rows_per_pair, chunk):
    rows = s_ref.shape[1]
    bounds = [0]
    for vd in values:
        bounds.append(bounds[-1] + vd.shape[0])
    sources = values

    for half in range(2):
        for r0 in range(0, rows, chunk):
            rs = slice(r0, r0 + chunk)
            sink = sinks[2 * (r0 // rows_per_pair) + half]
            parts = [s_ref[half, rs, bounds[i]:bounds[i + 1]] for i in range(len(sources))]
            if band is not None:
                parts[0] = jnp.where(band(r0 % rows_per_pair, chunk), parts[0], NEG_INF)
            m = sink
            for s in parts:
                m = jnp.maximum(m, jnp.max(s, axis=-1, keepdims=True))
            for i, s in enumerate(parts):
                p_ref[half, rs, bounds[i]:bounds[i + 1]] = jnp.exp2(s - m).astype(BF16)
            sk_ref[half, rs, :] = jnp.broadcast_to(jnp.exp2(sink - m), (chunk, LANES))

    o_e = o_o = None
    for i, vd in enumerate(values):
        v_lo, v_hi = _split_lo_hi(vd, 1.0)
        pe = _dot(p_ref[0, :, bounds[i]:bounds[i + 1]], v_lo)
        po = _dot(p_ref[1, :, bounds[i]:bounds[i + 1]], v_hi)
        o_e = pe if o_e is None else o_e + pe
        o_o = po if o_o is None else o_o + po
    lo = lax.broadcasted_iota(jnp.int32, (rows, LANES), 1) < HEAD_DIM
    num = jnp.where(lo, o_e, o_o)
    den = pltpu.roll(jnp.where(lo, o_o, o_e), HEAD_DIM, 1) + jnp.where(lo, sk_ref[0], sk_ref[1])
    return num / den


def _pair_cols(g):
    pa, pb = 2 * g, 2 * g + 1
    return slice(pa * LANES, (pa + 1) * LANES), slice(pb * LANES, (pb + 1) * LANES)


def _attn_tail_kernel(*refs, lat, n_tiles, tm, n, tiles_per_mod_row):
    sink_ref, q_ref, kd_ref, vd_ref = refs[:4]
    refs = refs[4:]
    if lat:
        ck_ref, cv_ref = refs[:2]
        refs = refs[2:]
    (x_ref, mod_ref, mix_ref, gate_ref, g2_ref,
     wf_ref, wao_ref, wout_ref, wup_ref, wdn_ref, o_ref, att_ref, s_ref, p_ref, sk_ref) = refs[:15]
    if lat:
        dist_ref, ckt_ref, cvd_ref = refs[15:18]
    act_ref = refs[-1]
    i = pl.program_id(0)

    def attention_stages():
        units = []
        if lat:
            qb = WINDOW
            span = qb + 2 * WINDOW
            t = jnp.minimum(i, n_tiles - 1) % (n // tm)
            ckt = ck_ref[0]
            ckt_ref[...] = jnp.concatenate(
                [ckt[0:HEAD_DIM], ckt[0:HEAD_DIM], ckt[HEAD_DIM:], ckt[HEAD_DIM:]], axis=0).astype(BF16)
            cvd_ref[...] = _dup_heads(cv_ref[0].T).astype(BF16)
            for rr in range(tm // qb):
                q0 = t * tm + rr * qb
                start = pl.multiple_of(jnp.clip(q0 - WINDOW, 0, n - span), WINDOW)
                dist_ref[rr] = jnp.abs((q0 - start) + lax.broadcasted_iota(jnp.int32, (qb, span), 0)
                                       - lax.broadcasted_iota(jnp.int32, (qb, span), 1))
                band = lambda r0, rows, rr=rr: dist_ref[rr, r0:r0 + rows, :] <= WINDOW
                rows = slice(rr * qb, (rr + 1) * qb)
                for g in range(N_KV_HEADS):
                    buf = len(units) % 2
                    cols = slice(g * LANES, (g + 1) * LANES)
                    qa, qbc = _pair_cols(g)

                    def scores(rows=rows, start=start, cols=cols, qa=qa, qbc=qbc, buf=buf):
                        q_stack = jnp.concatenate([q_ref[rows, qa], q_ref[rows, qbc]], axis=0)
                        _group_scores(q_stack, [kd_ref[cols, pl.ds(start, span)], ckt_ref[cols, :]],
                                      s_ref.at[buf])

                    def finish(rows=rows, start=start, g=g, cols=cols, qa=qa, qbc=qbc, buf=buf,
                               band=band):
                        sinks = [sink_ref[GROUP * g + h] * LOG2_E for h in range(GROUP)]
                        o = _group_softmax_pv(
                            [vd_ref[pl.ds(start, span), cols], cvd_ref[:, cols]], sinks, band,
                            s_ref.at[buf], p_ref.at[buf], sk_ref.at[buf], qb, LAT_SOFTMAX_ROWS)
                        att_ref[rows, qa] = o[0:qb].astype(BF16)
                        att_ref[rows, qbc] = o[qb:2 * qb].astype(BF16)

                    units.append((scores, finish))
        else:
            for s in range(tm // n):
                rows = slice(s * n, (s + 1) * n)
                for g in range(N_KV_HEADS):
                    buf = len(units) % 2
                    cols = slice(g * LANES, (g + 1) * LANES)
                    qa, qbc = _pair_cols(g)

                    def scores(rows=rows, cols=cols, qa=qa, qbc=qbc, buf=buf):
                        q_stack = jnp.concatenate([q_ref[rows, qa], q_ref[rows, qbc]], axis=0)
                        _group_scores(q_stack, [kd_ref[cols, rows]], s_ref.at[buf])

                    def finish(rows=rows, g=g, cols=cols, qa=qa, qbc=qbc, buf=buf):
                        sinks = [sink_ref[GROUP * g + h] * LOG2_E for h in range(GROUP)]
                        o = _group_softmax_pv([vd_ref[rows, cols]], sinks, None, s_ref.at[buf],
                                              p_ref.at[buf], sk_ref.at[buf], n, CTX_SOFTMAX_ROWS)
                        att_ref[rows, qa] = o[0:n].astype(BF16)
                        att_ref[rows, qbc] = o[n:2 * n].astype(BF16)

                    units.append((scores, finish))
        return _pipelined(units)

    def tail(stages):
        gt1, sh2, sc2, gt2 = _mod_vectors(mod_ref, i - 1, tiles_per_mod_row)[2:6]
        yf = _dot(mix_ref[...], wf_ref[...])
        ya = _dot(att_ref[...], wao_ref[...])
        merged = (gate_ref[:, 0:D_MODEL].astype(F32) * yf
                  + gate_ref[:, D_MODEL:2 * D_MODEL].astype(F32) * ya)
        x1 = x_ref[...] + gt1 * _dot(merged.astype(BF16), wout_ref[...])

        h = x1 * lax.rsqrt(jnp.mean(x1 * x1, axis=-1, keepdims=True) + EPS) * g2_ref[...]
        hb = (h * (1.0 + sc2) + sh2).astype(BF16)
        d_ff = wdn_ref.shape[0]
        n_chunks = d_ff // FF_CHUNK
        issued = 0
        for c in range(n_chunks):
            cols = slice(c * FF_CHUNK, (c + 1) * FF_CHUNK)
            a = _dot(hb, wup_ref[:, cols])
            u = _dot(hb, wup_ref[:, d_ff + cols.start:d_ff + cols.stop])
            act_ref[:, cols] = (a * _sigmoid(a) * u).astype(BF16)
            while issued < len(stages) and issued * n_chunks < (c + 1) * len(stages):
                stages[issued]()
                issued += 1
        o_ref[...] = x1 + gt2 * _dot(act_ref[...], wdn_ref[...])

    @pl.when(i == 0)
    def _():
        for stage in attention_stages():
            stage()

    @pl.when(jnp.logical_and(i > 0, i < n_tiles))
    def _():
        tail(attention_stages())

    @pl.when(i == n_tiles)
    def _():
        tail([])


def _attn_tail(sinks, q, kd, vd, cache, x, mod3, mix, gate, g2, wf, wao, wout, wup, wdn,
               mod_row, rows_per_mod_row, n):
    t = x.shape[0]
    tm = TAIL_TOKEN_TILE
    n_tiles = t // tm
    lat = cache is not None
    tiles_per_mod_row = None if rows_per_mod_row is None else rows_per_mod_row // tm
    front = lambda i: (jnp.minimum(i, n_tiles - 1), 0)
    back = lambda i: (jnp.maximum(i - 1, 0), 0)
    in_specs = [pl.BlockSpec(memory_space=pltpu.SMEM), pl.BlockSpec((tm, ATTN_WIDTH), front)]
    args = [sinks, q, kd, vd]
    if lat:
        past = cache[0].shape[2]
        seq_of = lambda i: jnp.minimum(i, n_tiles - 1) // (n // tm)
        in_specs += [pl.BlockSpec((2 * KV_WIDTH, n), lambda i: (0, seq_of(i))),
                     pl.BlockSpec((n, 2 * KV_WIDTH), lambda i: (seq_of(i), 0))]
        in_specs += [pl.BlockSpec((1, KV_WIDTH, past), lambda i: (seq_of(i), 0, 0))] * 2
        args += list(cache)
        m_rows, keys = 2 * WINDOW, 3 * WINDOW + past
    else:
        in_specs += [pl.BlockSpec((2 * KV_WIDTH, tm), lambda i: (0, front(i)[0])),
                     pl.BlockSpec((tm, 2 * KV_WIDTH), front)]
        m_rows, keys = 2 * n, n
    in_specs += [pl.BlockSpec((tm, D_MODEL), back),
                 _mod_spec(mod_row),
                 pl.BlockSpec((tm, F_WIDTH), back),
                 pl.BlockSpec((tm, 2 * D_MODEL), back),
                 _resident((1, D_MODEL)),
                 _resident(wf.shape), _resident(wao.shape), _resident(wout.shape),
                 _resident(wup.shape), _resident(wdn.shape)]
    args += [x, mod3, mix, gate, g2, wf, wao, wout, wup, wdn]
    scratch = [pltpu.VMEM((tm, ATTN_WIDTH), BF16),
               pltpu.VMEM((2, 2, m_rows, keys), F32),
               pltpu.VMEM((2, 2, m_rows, keys), BF16),
               pltpu.VMEM((2, 2, m_rows, LANES), F32)]
    if lat:
        scratch += [pltpu.VMEM((tm // WINDOW, WINDOW, 3 * WINDOW), jnp.int32),
                    pltpu.VMEM((2 * KV_WIDTH, past), BF16),
                    pltpu.VMEM((past, 2 * KV_WIDTH), BF16)]
    scratch += [pltpu.VMEM((tm, wdn.shape[0]), BF16)]
    return pl.pallas_call(
        functools.partial(_attn_tail_kernel, lat=lat, n_tiles=n_tiles, tm=tm, n=n,
                          tiles_per_mod_row=tiles_per_mod_row),
        grid=(n_tiles + 1,),
        in_specs=in_specs,
        out_specs=pl.BlockSpec((tm, D_MODEL), back),
        out_shape=pltpu.HBM((t, D_MODEL), F32),
        scratch_shapes=scratch,
        compiler_params=pltpu.CompilerParams(
            dimension_semantics=("arbitrary",), vmem_limit_bytes=RESIDENT_WEIGHTS_VMEM),
        name="attn_tail_lat" if lat else "attn_tail_ctx",
    )(sinks, *[_hbm(a) for a in args[1:]])


def _rope_tables(n):
    rows = n // GRID_W
    row = np.repeat(np.arange(rows, dtype=np.float64), GRID_W)
    col = np.tile(np.arange(GRID_W, dtype=np.float64), rows)
    axis_dim = HEAD_DIM // 2
    inv_freq = ROPE_THETA ** (-np.arange(0, axis_dim, 2, dtype=np.float64) / axis_dim)

    def axis_tabs(pos):
        ang = pos[:, None] * inv_freq[None, :]
        cos, sin = np.cos(ang), np.sin(ang)
        return np.concatenate([cos, cos], axis=-1), np.concatenate([-sin, sin], axis=-1)

    cr, sr = axis_tabs(row)
    cc, sc = axis_tabs(col)
    cos = np.tile(np.concatenate([cr, cc], axis=-1), (1, LANES // HEAD_DIM))
    sin = np.tile(np.concatenate([sr, sc], axis=-1), (1, LANES // HEAD_DIM))
    return jnp.asarray(cos.astype(np.float32)), jnp.asarray(sin.astype(np.float32))


@jax.jit
def _layer(xp, xs, ck, cv, c, c_ctx, w_ada, b_ada, g_norm1, g_norm2, w_in, g_q, g_k, sink,
           w_f, w_ao, w_out, w_up, w_down):
    bp, sp, _ = xp.shape
    bs, ss, _ = xs.shape
    past = ck.shape[1]

    mod3, w_in_b = _ada(c, c_ctx[None, :], w_ada, b_ada[None, :], w_in)
    g1, g2 = g_norm1[None, :], g_norm2[None, :]
    gq2, gk2 = g_q[None, :], g_k[None, :]
    ctx_mod = (CTX_MOD_ROW, None)
    lat_mod = (0, ss)

    xp2 = xp.reshape(bp * sp, D_MODEL)
    xs2 = xs.reshape(bs * ss, D_MODEL)
    mix, q, kd, vd, gate, k_new, v_new, wup = _proj(
        xp2, mod3, g1, w_in_b, gq2, gk2, sp, None, *ctx_mod, True, [w_up])
    mix_s, q_s, kd_s, vd_s, gate_s, wf, wao, wout, wdn = _proj(
        xs2, mod3, g1, w_in_b, gq2, gk2, ss, _rope_tables(ss), *lat_mod, False,
        [w_f, w_ao, w_out, w_down])

    yp = _attn_tail(sink, q, kd, vd, None, xp2, mod3, mix, gate, g2,
                    wf, wao, wout, wup, wdn, *ctx_mod, sp)

    def cache_t(t):
        return t.transpose(0, 2, 3, 1).reshape(bs, KV_WIDTH, past)

    ys = _attn_tail(sink, q_s, kd_s, vd_s, (cache_t(ck), cache_t(cv)), xs2, mod3,
                    mix_s, gate_s, g2, wf, wao, wout, wup, wdn, *lat_mod, ss)

    def cache_layout(t):
        return t.reshape(bp, N_KV_HEADS, HEAD_DIM, sp).transpose(0, 3, 1, 2)

    return yp.reshape(xp.shape), ys.reshape(xs.shape), cache_layout(k_new), cache_layout(v_new)


def kernel(x_prompt, x_sample, cache_k, cache_v, c, c_ctx, w_ada, b_ada, g_norm1, g_norm2,
           w_in, g_q, g_k, sinks, w_f, w_ao, w_out, w_up, w_down):
    depth = w_in.shape[0]
    xp, xs = x_prompt, x_sample
    new_k, new_v = [], []
    for l in range(depth):
        xp, xs, k_ctx, v_ctx = _layer(
            xp, xs, cache_k[:, l], cache_v[:, l], c, c_ctx, w_ada[l], b_ada[l], g_norm1[l],
            g_norm2[l], w_in[l], g_q[l], g_k[l], sinks[l], w_f[l], w_ao[l], w_out[l], w_up[l],
            w_down[l])
        new_k.append(k_ctx)
        new_v.append(v_ctx)
    return (xp, xs, jnp.stack(new_k, axis=1), jnp.stack(new_v, axis=1))
```

```python
import functools

import numpy as np
import jax
import jax.numpy as jnp
from jax import lax
from jax.experimental import pallas as pl
from jax.experimental.pallas import tpu as pltpu

D_MODEL = 1024
GRID_W = 64
N_HEADS = 8
N_KV_HEADS = 2
GROUP = N_HEADS // N_KV_HEADS
HEAD_DIM = 64
WINDOW = 128
F_GROUPS = 4
F_GROUP_DIM = 128
F_WIDTH = F_GROUPS * F_GROUP_DIM
ATTN_WIDTH = N_HEADS * HEAD_DIM
KV_WIDTH = N_KV_HEADS * HEAD_DIM
ROPE_THETA = 10000.0
EPS = 1e-6
NEG_INF = -1e30
LOG2_E = 1.4426950408889634

LANES = 128
MXU_DIM = 256
VMEM_BYTES_V7X = 64 * 1024 * 1024

_Q0 = F_WIDTH
_K0 = _Q0 + ATTN_WIDTH
_V0 = _K0 + KV_WIDTH
_G0 = _V0 + KV_WIDTH
IN_WIDTH = _G0 + 2 * D_MODEL

SUBLANES = 8
N_MOD = 6
MOD_ROWS = 16
CTX_MOD_ROW = 8
ADA_STEPS = 8

PROJ_TOKEN_TILE = 1024
TAIL_TOKEN_TILE = 512
FF_CHUNK = MXU_DIM
GATE_CHUNK = 2 * MXU_DIM
CTX_SOFTMAX_ROWS = 64
LAT_SOFTMAX_ROWS = 32

F32 = jnp.float32
BF16 = jnp.bfloat16


RESIDENT_WEIGHTS_VMEM = VMEM_BYTES_V7X - 6 * 2 ** 20


def _dot(a, b):
    return jnp.dot(a, b, preferred_element_type=F32)


def _sigmoid(x):
    return 1.0 / (1.0 + jnp.exp(-x))


def _resident(shape):
    zeros = (0,) * len(shape)
    return pl.BlockSpec(shape, lambda *_: zeros, pipeline_mode=pl.Buffered(1))


def _hbm(x):
    return pltpu.with_memory_space_constraint(x, pltpu.HBM)


def _pipelined(units):
    stages = [units[0][0]]
    for k, (_, second) in enumerate(units):
        if k + 1 < len(units):
            stages.append(units[k + 1][0])
        stages.append(second)
    return stages


def _ada_kernel(c_ref, cctx_ref, w_ref, b_ref, win_ref, o_ref, winb_ref):
    lat_rows = c_ref.shape[0]
    pieces = [c_ref[...]]
    if lat_rows < CTX_MOD_ROW:
        pieces.append(jnp.zeros((CTX_MOD_ROW - lat_rows, D_MODEL), F32))
    pieces.append(jnp.broadcast_to(cctx_ref[...], (MOD_ROWS - CTX_MOD_ROW, D_MODEL)))
    c = jnp.concatenate(pieces, axis=0)
    s = c * _sigmoid(c)
    o_ref[...] = _dot(s.astype(BF16), w_ref[...].astype(BF16)) + b_ref[...]
    winb_ref[...] = win_ref[...].astype(BF16)


def _ada(c, c_ctx, w_ada, b_ada, w_in):
    assert c.shape[0] <= CTX_MOD_ROW, "latent conditioning rows must fit below the context row"
    n = w_ada.shape[1]
    bn = n // ADA_STEPS
    slab = pl.BlockSpec((w_in.shape[0] // ADA_STEPS, w_in.shape[1]), lambda j: (j, 0))
    return pl.pallas_call(
        _ada_kernel,
        grid=(ADA_STEPS,),
        in_specs=[pl.BlockSpec(c.shape, lambda j: (0, 0)),
                  pl.BlockSpec((1, D_MODEL), lambda j: (0, 0)),
                  pl.BlockSpec((D_MODEL, bn), lambda j: (0, j)),
                  pl.BlockSpec((1, bn), lambda j: (0, j)),
                  slab],
        out_specs=[pl.BlockSpec((MOD_ROWS, bn), lambda j: (0, j)), slab],
        out_shape=[pltpu.HBM((MOD_ROWS, n), F32), pltpu.HBM(w_in.shape, BF16)],
        name="ada",
    )(c, c_ctx, w_ada, b_ada, w_in)


def _mod_spec(mod_row):
    return pl.BlockSpec((SUBLANES, N_MOD * D_MODEL), lambda i: (mod_row // SUBLANES, 0))


def _mod_vectors(mod_ref, tile, tiles_per_row):
    cols = [slice(j * D_MODEL, (j + 1) * D_MODEL) for j in range(N_MOD)]
    if tiles_per_row is None:
        return [mod_ref[0:1, c] for c in cols]
    r = tile // tiles_per_row
    return [mod_ref[pl.ds(r, 1), c] for c in cols]


def _head_norm(z, g):
    lo = lax.broadcasted_iota(jnp.int32, z.shape, 1) < HEAD_DIM
    s = z * z
    s_lo = jnp.sum(jnp.where(lo, s, 0.0), axis=-1, keepdims=True)
    s_hi = jnp.sum(jnp.where(lo, 0.0, s), axis=-1, keepdims=True)
    ms = jnp.where(lo, s_lo, s_hi) * (1.0 / HEAD_DIM)
    return z * lax.rsqrt(ms + EPS) * g


def _rope(y, cos, sin):
    lane = lax.broadcasted_iota(jnp.int32, y.shape, 1)
    first = (lane % 32) < 16
    partner = jnp.where(first, pltpu.roll(y, LANES - 16, 1), pltpu.roll(y, 16, 1))
    return y * cos + partner * sin


def _dup_heads(y):
    lo = lax.broadcasted_iota(jnp.int32, y.shape, 1) < HEAD_DIM
    sw = pltpu.roll(y, HEAD_DIM, 1)
    return jnp.concatenate([jnp.where(lo, y, sw), jnp.where(lo, sw, y)], axis=-1)


def _proj_kernel(*refs, rope, seq_len, fold, cache, n_cast, tiles_per_mod_row):
    x_ref, mod_ref, g1_ref, w_ref, gq_ref, gk_ref = refs[:6]
    refs = refs[6:]
    if rope:
        cos_ref, sin_ref = refs[:2]
        refs = refs[2:]
    n_tabs = 4 if fold else 2
    tabs, refs = refs[:n_tabs], refs[n_tabs:]
    cast_in, refs = refs[:n_cast], refs[n_cast:]
    mix_ref, q_ref, kd_ref, vd_ref, gate_ref = refs[:5]
    refs = refs[5:]
    if cache:
        kc_ref, vc_ref = refs[:2]
        refs = refs[2:]
    cast_out, (zf_ref, ab_ref) = refs[:n_cast], refs[n_cast:]
    for src_ref, dst_ref in zip(cast_in, cast_out):
        dst_ref[...] = src_ref[...].astype(BF16)

    x = x_ref[...]
    sh1, sc1 = _mod_vectors(mod_ref, pl.program_id(0), tiles_per_mod_row)[0:2]
    h = x * lax.rsqrt(jnp.mean(x * x, axis=-1, keepdims=True) + EPS) * g1_ref[...]
    hb = (h * (1.0 + sc1) + sh1).astype(BF16)

    zf_ref[...] = _dot(hb, w_ref[:, 0:_Q0]).astype(BF16)
    mixer = _fourier_stages(zf_ref, tabs, mix_ref, ab_ref, seq_len, fold)
    n_sections = 2 + 2 * D_MODEL // GATE_CHUNK
    issued = [0]

    def run_mixer_share(section):
        while issued[0] < len(mixer) and issued[0] * n_sections < (section + 1) * len(mixer):
            mixer[issued[0]]()
            issued[0] += 1

    gq = jnp.concatenate([gq_ref[...]] * (LANES // HEAD_DIM), axis=-1)
    gk = jnp.concatenate([gk_ref[...]] * (LANES // HEAD_DIM), axis=-1)

    zq = _dot(hb, w_ref[:, _Q0:_K0])
    for j in range(ATTN_WIDTH // LANES):
        y = _head_norm(zq[:, j * LANES:(j + 1) * LANES], gq)
        if rope:
            y = _rope(y, cos_ref[...], sin_ref[...])
        q_ref[:, j * LANES:(j + 1) * LANES] = (y * (HEAD_DIM ** -0.5 * LOG2_E)).astype(BF16)
    run_mixer_share(0)

    zkv = _dot(hb, w_ref[:, _K0:_G0])
    k = _head_norm(zkv[:, 0:KV_WIDTH], gk)
    v = zkv[:, KV_WIDTH:2 * KV_WIDTH]
    if cache:
        for s in range(x.shape[0] // seq_len):
            rows = slice(s * seq_len, (s + 1) * seq_len)
            kc_ref[s] = k[rows, :].T
            vc_ref[s] = v[rows, :].T
    if rope:
        k = _rope(k, cos_ref[...], sin_ref[...])
    kt = k.T
    kd_ref[...] = jnp.concatenate(
        [kt[0:HEAD_DIM], kt[0:HEAD_DIM], kt[HEAD_DIM:], kt[HEAD_DIM:]], axis=0).astype(BF16)
    vd_ref[...] = _dup_heads(v).astype(BF16)
    run_mixer_share(1)

    for j in range(2 * D_MODEL // GATE_CHUNK):
        cols = slice(j * GATE_CHUNK, (j + 1) * GATE_CHUNK)
        zg = _dot(hb, w_ref[:, _G0 + cols.start:_G0 + cols.stop])
        gate_ref[:, cols] = _sigmoid(zg).astype(BF16)
        run_mixer_share(2 + j)


def _proj(x, mod3, g1, w_in_b, gq2, gk2, seq_len, rope_tabs, mod_row, rows_per_mod_row, cache,
          cast_weights):
    t = x.shape[0]
    tm = PROJ_TOKEN_TILE
    assert tm % seq_len == 0, "a projection tile must hold whole sequences for the mixer"
    steps = t // tm
    rope = rope_tabs is not None
    row = lambda i: (i, 0)
    tiles_per_mod_row = None if rows_per_mod_row is None else rows_per_mod_row // tm
    in_specs = [pl.BlockSpec((tm, D_MODEL), row),
                _mod_spec(mod_row),
                _resident((1, D_MODEL)),
                _resident((D_MODEL, IN_WIDTH)),
                _resident((1, HEAD_DIM)),
                _resident((1, HEAD_DIM))]
    args = [x, mod3, g1, w_in_b, gq2, gk2]
    if rope:
        tiles_per_seq = rope_tabs[0].shape[0] // tm
        tab = pl.BlockSpec((tm, LANES), lambda i: (i % tiles_per_seq, 0))
        in_specs += [tab, tab]
        args += list(rope_tabs)
    fold = _fold_dft(seq_len)
    dft = _dft_tables(seq_len, fold)
    in_specs += [_resident(tab.shape) for tab in dft]
    args += list(dft)
    slabs = [pl.BlockSpec((w.shape[0] // steps, w.shape[1]), row) for w in cast_weights]
    in_specs += slabs
    args += list(cast_weights)
    out_specs = [pl.BlockSpec((tm, F_WIDTH), row),
                 pl.BlockSpec((tm, ATTN_WIDTH), row),
                 pl.BlockSpec((2 * KV_WIDTH, tm), lambda i: (0, i)),
                 pl.BlockSpec((tm, 2 * KV_WIDTH), row),
                 pl.BlockSpec((tm, 2 * D_MODEL), row)]
    out_shape = [pltpu.HBM((t, F_WIDTH), BF16),
                 pltpu.HBM((t, ATTN_WIDTH), BF16),
                 pltpu.HBM((2 * KV_WIDTH, t), BF16),
                 pltpu.HBM((t, 2 * KV_WIDTH), BF16),
                 pltpu.HBM((t, 2 * D_MODEL), BF16)]
    if cache:
        seqs = tm // seq_len
        out_specs += [pl.BlockSpec((seqs, KV_WIDTH, seq_len), lambda i: (i, 0, 0))] * 2
        out_shape += [pltpu.HBM((t // seq_len, KV_WIDTH, seq_len), F32)] * 2
    out_specs += slabs
    out_shape += [pltpu.HBM(w.shape, BF16) for w in cast_weights]
    return pl.pallas_call(
        functools.partial(_proj_kernel, rope=rope, seq_len=seq_len, fold=fold, cache=cache,
                          n_cast=len(cast_weights), tiles_per_mod_row=tiles_per_mod_row),
        grid=(steps,),
        in_specs=in_specs,
        out_specs=out_specs,
        out_shape=out_shape,
        scratch_shapes=[pltpu.VMEM((tm, F_WIDTH), BF16),
                        pltpu.VMEM((2, 2 * seq_len, F_WIDTH), BF16)],
        compiler_params=pltpu.CompilerParams(
            dimension_semantics=("arbitrary",), vmem_limit_bytes=RESIDENT_WEIGHTS_VMEM),
        name="proj_lat" if rope else "proj_ctx",
    )(*[_hbm(a) for a in args])


def _dft_tables(n, fold):
    def cs(m):
        idx = np.arange(m, dtype=np.int64)
        ang = 2.0 * np.pi * ((idx[:, None] * idx[None, :]) % m).astype(np.float64) / m
        return np.cos(ang) / np.sqrt(m), np.sin(ang) / np.sqrt(m)
    cd, sd = cs(F_GROUP_DIM)
    cn, sn = cs(n)
    tabs = [np.concatenate([cd, sd], axis=1)]
    if fold:
        half = n // 2
        rev = np.zeros((half, half), np.float64)
        rev[np.arange(1, half), half - np.arange(1, half)] = 1.0
        tabs += [cn[:half], -sn[:half], rev]
    else:
        tabs += [np.concatenate([cn, -sn], axis=1)]
    return tuple(jnp.asarray(t.astype(np.float32)).astype(BF16) for t in tabs)


def _fold_dft(n):
    return n // 2 >= 2 * MXU_DIM


def _fourier_stages(zf_ref, tabs, o_ref, ab_ref, n, fold):
    csd_ref = tabs[0]
    if fold:
        ch_ref, sh_ref, rev_ref = tabs[1:]
    else:
        (csn_ref,) = tabs[1:]
    half = n // 2
    units = []
    for s in range(zf_ref.shape[0] // n):
        buf = s % 2
        r0 = s * n

        def channels(r0=r0, buf=buf):
            for g in range(F_GROUPS):
                cols = slice(g * F_GROUP_DIM, (g + 1) * F_GROUP_DIM)
                ab = _dot(zf_ref[r0:r0 + n, cols], csd_ref[...])
                ab_ref[buf, 0:n, cols] = ab[:, 0:F_GROUP_DIM].astype(BF16)
                ab_ref[buf, n:2 * n, cols] = ab[:, F_GROUP_DIM:].astype(BF16)

        def positions(r0=r0, buf=buf):
            if not fold:
                o_ref[r0:r0 + n, :] = _dot(csn_ref[...], ab_ref[buf]).astype(BF16)
                return
            p = _dot(ch_ref[...], ab_ref[buf, 0:n, :])
            q = _dot(sh_ref[...], ab_ref[buf, n:2 * n, :])
            o_ref[r0:r0 + half, :] = (p + q).astype(BF16)
            upper = _dot(rev_ref[...], (p - q).astype(BF16))
            a = ab_ref[buf, 0:n, :].astype(F32)
            even = lax.broadcasted_iota(jnp.int32, a.shape, 0) % 2 == 0
            mid = jnp.sum(jnp.where(even, a, -a), axis=0, keepdims=True) * (n ** -0.5)
            first = lax.broadcasted_iota(jnp.int32, upper.shape, 0) == 0
            o_ref[r0 + half:r0 + n, :] = jnp.where(first, mid, upper).astype(BF16)

        units.append((channels, positions))
    return _pipelined(units)


def _split_lo_hi(xd, fill):
    lo = lax.broadcasted_iota(jnp.int32, xd.shape, 1) < HEAD_DIM
    other = jnp.full_like(xd, fill)
    return jnp.where(lo, xd, other), jnp.where(lo, other, xd)


def _group_scores(q_stack, keys, s_ref):
    kt = keys[0] if len(keys) == 1 else jnp.concatenate(keys, axis=1)
    top = lax.broadcasted_iota(jnp.int32, kt.shape, 0) < HEAD_DIM
    zero = jnp.zeros_like(kt)
    s_ref[0] = _dot(q_stack, jnp.where(top, kt, zero))
    s_ref[1] = _dot(q_stack, jnp.where(top, zero, kt))


def _group_softmax_pv(values, sinks, band, s_ref, p_ref, sk_ref, rows_per_pair, chunk):
    rows = s_ref.shape[1]
    bounds = [0]
    for vd in values:
        bounds.append(bounds[-1] + vd.shape[0])
    sources = values

    for half in range(2):
        for r0 in range(0, rows, chunk):
            rs = slice(r0, r0 + chunk)
            sink = sinks[2 * (r0 // rows_per_pair) + half]
            parts = [s_ref[half, rs, bounds[i]:bounds[i + 1]] for i in range(len(sources))]
            if band is not None:
                parts[0] = jnp.where(band(r0 % rows_per_pair, chunk), parts[0], NEG_INF)
            m = sink
            for s in parts:
                m = jnp.maximum(m, jnp.max(s, axis=-1, keepdims=True))
            for i, s in enumerate(parts):
                p_ref[half, rs, bounds[i]:bounds[i + 1]] = jnp.exp2(s - m).astype(BF16)
            sk_ref[half, rs, :] = jnp.broadcast_to(jnp.exp2(sink - m), (chunk, LANES))

    v_lo, v_hi = _split_lo_hi(values[0] if len(values) == 1 else jnp.concatenate(values, axis=0), 1.0)
    o_e = _dot(p_ref[0], v_lo)
    o_o = _dot(p_ref[1], v_hi)
    lo = lax.broadcasted_iota(jnp.int32, (rows, LANES), 1) < HEAD_DIM
    num = jnp.where(lo, o_e, o_o)
    den = pltpu.roll(jnp.where(lo, o_o, o_e), HEAD_DIM, 1) + jnp.where(lo, sk_ref[0], sk_ref[1])
    return num / den


def _pair_cols(g):
    pa, pb = 2 * g, 2 * g + 1
    return slice(pa * LANES, (pa + 1) * LANES), slice(pb * LANES, (pb + 1) * LANES)


def _attn_tail_kernel(*refs, lat, n_tiles, tm, n, tiles_per_mod_row):
    sink_ref, q_ref, kd_ref, vd_ref = refs[:4]
    refs = refs[4:]
    if lat:
        ck_ref, cv_ref = refs[:2]
        refs = refs[2:]
    (x_ref, mod_ref, mix_ref, gate_ref, g2_ref,
     wf_ref, wao_ref, wout_ref, wup_ref, wdn_ref, o_ref, att_ref, s_ref, p_ref, sk_ref) = refs[:15]
    if lat:
        dist_ref, ckt_ref, cvd_ref = refs[15:18]
    act_ref = refs[-1]
    i = pl.program_id(0)

    def attention_stages():
        units = []
        if lat:
            qb = WINDOW
            span = qb + 2 * WINDOW
            t = jnp.minimum(i, n_tiles - 1) % (n // tm)
            ckt = ck_ref[0]
            ckt_ref[...] = jnp.concatenate(
                [ckt[0:HEAD_DIM], ckt[0:HEAD_DIM], ckt[HEAD_DIM:], ckt[HEAD_DIM:]], axis=0).astype(BF16)
            cvd_ref[...] = _dup_heads(cv_ref[0].T).astype(BF16)
            for rr in range(tm // qb):
                q0 = t * tm + rr * qb
                start = pl.multiple_of(jnp.clip(q0 - WINDOW, 0, n - span), WINDOW)
                dist_ref[rr] = jnp.abs((q0 - start) + lax.broadcasted_iota(jnp.int32, (qb, span), 0)
                                       - lax.broadcasted_iota(jnp.int32, (qb, span), 1))
                band = lambda r0, rows, rr=rr: dist_ref[rr, r0:r0 + rows, :] <= WINDOW
                rows = slice(rr * qb, (rr + 1) * qb)
                for g in range(N_KV_HEADS):
                    buf = len(units) % 2
                    cols = slice(g * LANES, (g + 1) * LANES)
                    qa, qbc = _pair_cols(g)

                    def scores(rows=rows, start=start, cols=cols, qa=qa, qbc=qbc, buf=buf):
                        q_stack = jnp.concatenate([q_ref[rows, qa], q_ref[rows, qbc]], axis=0)
                        _group_scores(q_stack, [kd_ref[cols, pl.ds(start, span)], ckt_ref[cols, :]],
                                      s_ref.at[buf])

                    def finish(rows=rows, start=start, g=g, cols=cols, qa=qa, qbc=qbc, buf=buf,
                               band=band):
                        sinks = [sink_ref[GROUP * g + h] * LOG2_E for h in range(GROUP)]
                        o = _group_softmax_pv(
                            [vd_ref[pl.ds(start, span), cols], cvd_ref[:, cols]], sinks, band,
                            s_ref.at[buf], p_ref.at[buf], sk_ref.at[buf], qb, LAT_SOFTMAX_ROWS)
                        att_ref[rows, qa] = o[0:qb].astype(BF16)
                        att_ref[rows, qbc] = o[qb:2 * qb].astype(BF16)

                    units.append((scores, finish))
        else:
            for s in range(tm // n):
                rows = slice(s * n, (s + 1) * n)
                for g in range(N_KV_HEADS):
                    buf = len(units) % 2
                    cols = slice(g * LANES, (g + 1) * LANES)
                    qa, qbc = _pair_cols(g)

                    def scores(rows=rows, cols=cols, qa=qa, qbc=qbc, buf=buf):
                        q_stack = jnp.concatenate([q_ref[rows, qa], q_ref[rows, qbc]], axis=0)
                        _group_scores(q_stack, [kd_ref[cols, rows]], s_ref.at[buf])

                    def finish(rows=rows, g=g, cols=cols, qa=qa, qbc=qbc, buf=buf):
                        sinks = [sink_ref[GROUP * g + h] * LOG2_E for h in range(GROUP)]
                        o = _group_softmax_pv([vd_ref[rows, cols]], sinks, None, s_ref.at[buf],
                                              p_ref.at[buf], sk_ref.at[buf], n, CTX_SOFTMAX_ROWS)
                        att_ref[rows, qa] = o[0:n].astype(BF16)
                        att_ref[rows, qbc] = o[n:2 * n].astype(BF16)

                    units.append((scores, finish))
        return _pipelined(units)

    def tail(stages):
        gt1, sh2, sc2, gt2 = _mod_vectors(mod_ref, i - 1, tiles_per_mod_row)[2:6]
        yf = _dot(mix_ref[...], wf_ref[...])
        ya = _dot(att_ref[...], wao_ref[...])
        merged = (gate_ref[:, 0:D_MODEL].astype(F32) * yf
                  + gate_ref[:, D_MODEL:2 * D_MODEL].astype(F32) * ya)
        x1 = x_ref[...] + gt1 * _dot(merged.astype(BF16), wout_ref[...])

        h = x1 * lax.rsqrt(jnp.mean(x1 * x1, axis=-1, keepdims=True) + EPS) * g2_ref[...]
        hb = (h * (1.0 + sc2) + sh2).astype(BF16)
        d_ff = wdn_ref.shape[0]
        n_chunks = d_ff // FF_CHUNK
        issued = 0
        for c in range(n_chunks):
            cols = slice(c * FF_CHUNK, (c + 1) * FF_CHUNK)
            a = _dot(hb, wup_ref[:, cols])
            u = _dot(hb, wup_ref[:, d_ff + cols.start:d_ff + cols.stop])
            act_ref[:, cols] = (a * _sigmoid(a) * u).astype(BF16)
            while issued < len(stages) and issued * n_chunks < (c + 1) * len(stages):
                stages[issued]()
                issued += 1
        o_ref[...] = x1 + gt2 * _dot(act_ref[...], wdn_ref[...])

    @pl.when(i == 0)
    def _():
        for stage in attention_stages():
            stage()

    @pl.when(jnp.logical_and(i > 0, i < n_tiles))
    def _():
        tail(attention_stages())

    @pl.when(i == n_tiles)
    def _():
        tail([])


def _attn_tail(sinks, q, kd, vd, cache, x, mod3, mix, gate, g2, wf, wao, wout, wup, wdn,
               mod_row, rows_per_mod_row, n):
    t = x.shape[0]
    tm = TAIL_TOKEN_TILE
    n_tiles = t // tm
    lat = cache is not None
    tiles_per_mod_row = None if rows_per_mod_row is None else rows_per_mod_row // tm
    front = lambda i: (jnp.minimum(i, n_tiles - 1), 0)
    back = lambda i: (jnp.maximum(i - 1, 0), 0)
    in_specs = [pl.BlockSpec(memory_space=pltpu.SMEM), pl.BlockSpec((tm, ATTN_WIDTH), front)]
    args = [sinks, q, kd, vd]
    if lat:
        past = cache[0].shape[2]
        seq_of = lambda i: jnp.minimum(i, n_tiles - 1) // (n // tm)
        in_specs += [pl.BlockSpec((2 * KV_WIDTH, n), lambda i: (0, seq_of(i))),
                     pl.BlockSpec((n, 2 * KV_WIDTH), lambda i: (seq_of(i), 0))]
        in_specs += [pl.BlockSpec((1, KV_WIDTH, past), lambda i: (seq_of(i), 0, 0))] * 2
        args += list(cache)
        m_rows, keys = 2 * WINDOW, 3 * WINDOW + past
    else:
        in_specs += [pl.BlockSpec((2 * KV_WIDTH, tm), lambda i: (0, front(i)[0])),
                     pl.BlockSpec((tm, 2 * KV_WIDTH), front)]
        m_rows, keys = 2 * n, n
    in_specs += [pl.BlockSpec((tm, D_MODEL), back),
                 _mod_spec(mod_row),
                 pl.BlockSpec((tm, F_WIDTH), back),
                 pl.BlockSpec((tm, 2 * D_MODEL), back),
                 _resident((1, D_MODEL)),
                 _resident(wf.shape), _resident(wao.shape), _resident(wout.shape),
                 _resident(wup.shape), _resident(wdn.shape)]
    args += [x, mod3, mix, gate, g2, wf, wao, wout, wup, wdn]
    scratch = [pltpu.VMEM((tm, ATTN_WIDTH), BF16),
               pltpu.VMEM((2, 2, m_rows, keys), F32),
               pltpu.VMEM((2, 2, m_rows, keys), BF16),
               pltpu.VMEM((2, 2, m_rows, LANES), F32)]
    if lat:
        scratch += [pltpu.VMEM((tm // WINDOW, WINDOW, 3 * WINDOW), jnp.int32),
                    pltpu.VMEM((2 * KV_WIDTH, past), BF16),
                    pltpu.VMEM((past, 2 * KV_WIDTH), BF16)]
    scratch += [pltpu.VMEM((tm, wdn.shape[0]), BF16)]
    return pl.pallas_call(
        functools.partial(_attn_tail_kernel, lat=lat, n_tiles=n_tiles, tm=tm, n=n,
                          tiles_per_mod_row=tiles_per_mod_row),
        grid=(n_tiles + 1,),
        in_specs=in_specs,
        out_specs=pl.BlockSpec((tm, D_MODEL), back),
        out_shape=pltpu.HBM((t, D_MODEL), F32),
        scratch_shapes=scratch,
        compiler_params=pltpu.CompilerParams(
            dimension_semantics=("arbitrary",), vmem_limit_bytes=RESIDENT_WEIGHTS_VMEM),
        name="attn_tail_lat" if lat else "attn_tail_ctx",
    )(sinks, *[_hbm(a) for a in args[1:]])


def _rope_tables(n):
    rows = n // GRID_W
    row = np.repeat(np.arange(rows, dtype=np.float64), GRID_W)
    col = np.tile(np.arange(GRID_W, dtype=np.float64), rows)
    axis_dim = HEAD_DIM // 2
    inv_freq = ROPE_THETA ** (-np.arange(0, axis_dim, 2, dtype=np.float64) / axis_dim)

    def axis_tabs(pos):
        ang = pos[:, None] * inv_freq[None, :]
        cos, sin = np.cos(ang), np.sin(ang)
        return np.concatenate([cos, cos], axis=-1), np.concatenate([-sin, sin], axis=-1)

    cr, sr = axis_tabs(row)
    cc, sc = axis_tabs(col)
    cos = np.tile(np.concatenate([cr, cc], axis=-1), (1, LANES // HEAD_DIM))
    sin = np.tile(np.concatenate([sr, sc], axis=-1), (1, LANES // HEAD_DIM))
    return jnp.asarray(cos.astype(np.float32)), jnp.asarray(sin.astype(np.float32))


@jax.jit
def _layer(xp, xs, ck, cv, c, c_ctx, w_ada, b_ada, g_norm1, g_norm2, w_in, g_q, g_k, sink,
           w_f, w_ao, w_out, w_up, w_down):
    bp, sp, _ = xp.shape
    bs, ss, _ = xs.shape
    past = ck.shape[1]

    mod3, w_in_b = _ada(c, c_ctx[None, :], w_ada, b_ada[None, :], w_in)
    g1, g2 = g_norm1[None, :], g_norm2[None, :]
    gq2, gk2 = g_q[None, :], g_k[None, :]
    ctx_mod = (CTX_MOD_ROW, None)
    lat_mod = (0, ss)

    xp2 = xp.reshape(bp * sp, D_MODEL)
    xs2 = xs.reshape(bs * ss, D_MODEL)
    mix, q, kd, vd, gate, k_new, v_new, wup = _proj(
        xp2, mod3, g1, w_in_b, gq2, gk2, sp, None, *ctx_mod, True, [w_up])
    mix_s, q_s, kd_s, vd_s, gate_s, wf, wao, wout, wdn = _proj(
        xs2, mod3, g1, w_in_b, gq2, gk2, ss, _rope_tables(ss), *lat_mod, False,
        [w_f, w_ao, w_out, w_down])

    yp = _attn_tail(sink, q, kd, vd, None, xp2, mod3, mix, gate, g2,
                    wf, wao, wout, wup, wdn, *ctx_mod, sp)

    def cache_t(t):
        return t.transpose(0, 2, 3, 1).reshape(bs, KV_WIDTH, past)

    ys = _attn_tail(sink, q_s, kd_s, vd_s, (cache_t(ck), cache_t(cv)), xs2, mod3,
                    mix_s, gate_s, g2, wf, wao, wout, wup, wdn, *lat_mod, ss)

    def cache_layout(t):
        return t.reshape(bp, N_KV_HEADS, HEAD_DIM, sp).transpose(0, 3, 1, 2)

    return yp.reshape(xp.shape), ys.reshape(xs.shape), cache_layout(k_new), cache_layout(v_new)


def kernel(x_prompt, x_sample, cache_k, cache_v, c, c_ctx, w_ada, b_ada, g_norm1, g_norm2,
           w_in, g_q, g_k, sinks, w_f, w_ao, w_out, w_up, w_down):
    depth = w_in.shape[0]
    xp, xs = x_prompt, x_sample
    new_k, new_v = [], []
    for l in range(depth):
        xp, xs, k_ctx, v_ctx = _layer(
            xp, xs, cache_k[:, l], cache_v[:, l], c, c_ctx, w_ada[l], b_ada[l], g_norm1[l],
            g_norm2[l], w_in[l], g_q[l], g_k[l], sinks[l], w_f[l], w_ao[l], w_out[l], w_up[l],
            w_down[l])
        new_k.append(k_ctx)
        new_v.append(v_ctx)
    return (xp, xs, jnp.stack(new_k, axis=1), jnp.stack(new_v, axis=1))
```

```python
import functools

import numpy as np
import jax
import jax.numpy as jnp
from jax import lax
from jax.experimental import pallas as pl
from jax.experimental.pallas import tpu as pltpu

D_MODEL = 1024
GRID_W = 64
N_HEADS = 8
N_KV_HEADS = 2
GROUP = N_HEADS // N_KV_HEADS
HEAD_DIM = 64
WINDOW = 128
F_GROUPS = 4
F_GROUP_DIM = 128
F_WIDTH = F_GROUPS * F_GROUP_DIM
ATTN_WIDTH = N_HEADS * HEAD_DIM
KV_WIDTH = N_KV_HEADS * HEAD_DIM
ROPE_THETA = 10000.0
EPS = 1e-6
NEG_INF = -1e30
LOG2_E = 1.4426950408889634

LANES = 128
MXU_DIM = 256
VMEM_BYTES_V7X = 64 * 1024 * 1024

_Q0 = F_WIDTH
_K0 = _Q0 + ATTN_WIDTH
_V0 = _K0 + KV_WIDTH
_G0 = _V0 + KV_WIDTH
IN_WIDTH = _G0 + 2 * D_MODEL

SUBLANES = 8
N_MOD = 6
MOD_ROWS = 16
CTX_MOD_ROW = 8
ADA_STEPS = 8

PROJ_TOKEN_TILE = 1024
TAIL_TOKEN_TILE = 512
FF_CHUNK = MXU_DIM
GATE_CHUNK = 4 * MXU_DIM
CTX_SOFTMAX_ROWS = 64
LAT_SOFTMAX_ROWS = 32

F32 = jnp.float32
BF16 = jnp.bfloat16


RESIDENT_WEIGHTS_VMEM = VMEM_BYTES_V7X - 6 * 2 ** 20


def _dot(a, b):
    return jnp.dot(a, b, preferred_element_type=F32)


def _sigmoid(x):
    return 1.0 / (1.0 + jnp.exp(-x))


def _resident(shape):
    zeros = (0,) * len(shape)
    return pl.BlockSpec(shape, lambda *_: zeros, pipeline_mode=pl.Buffered(1))


def _hbm(x):
    return pltpu.with_memory_space_constraint(x, pltpu.HBM)


def _pipelined(units):
    stages = [units[0][0]]
    for k, (_, second) in enumerate(units):
        if k + 1 < len(units):
            stages.append(units[k + 1][0])
        stages.append(second)
    return stages


def _ada_kernel(c_ref, cctx_ref, w_ref, b_ref, win_ref, o_ref, winb_ref):
    lat_rows = c_ref.shape[0]
    pieces = [c_ref[...]]
    if lat_rows < CTX_MOD_ROW:
        pieces.append(jnp.zeros((CTX_MOD_ROW - lat_rows, D_MODEL), F32))
    pieces.append(jnp.broadcast_to(cctx_ref[...], (MOD_ROWS - CTX_MOD_ROW, D_MODEL)))
    c = jnp.concatenate(pieces, axis=0)
    s = c * _sigmoid(c)
    o_ref[...] = _dot(s.astype(BF16), w_ref[...].astype(BF16)) + b_ref[...]
    winb_ref[...] = win_ref[...].astype(BF16)


def _ada(c, c_ctx, w_ada, b_ada, w_in):
    assert c.shape[0] <= CTX_MOD_ROW, "latent conditioning rows must fit below the context row"
    n = w_ada.shape[1]
    bn = n // ADA_STEPS
    slab = pl.BlockSpec((w_in.shape[0] // ADA_STEPS, w_in.shape[1]), lambda j: (j, 0))
    return pl.pallas_call(
        _ada_kernel,
        grid=(ADA_STEPS,),
        in_specs=[pl.BlockSpec(c.shape, lambda j: (0, 0)),
                  pl.BlockSpec((1, D_MODEL), lambda j: (0, 0)),
                  pl.BlockSpec((D_MODEL, bn), lambda j: (0, j)),
                  pl.BlockSpec((1, bn), lambda j: (0, j)),
                  slab],
        out_specs=[pl.BlockSpec((MOD_ROWS, bn), lambda j: (0, j)), slab],
        out_shape=[pltpu.HBM((MOD_ROWS, n), F32), pltpu.HBM(w_in.shape, BF16)],
        name="ada",
    )(c, c_ctx, w_ada, b_ada, w_in)


def _mod_spec(mod_row):
    return pl.BlockSpec((SUBLANES, N_MOD * D_MODEL), lambda i: (mod_row // SUBLANES, 0))


def _mod_vectors(mod_ref, tile, tiles_per_row):
    cols = [slice(j * D_MODEL, (j + 1) * D_MODEL) for j in range(N_MOD)]
    if tiles_per_row is None:
        return [mod_ref[0:1, c] for c in cols]
    r = tile // tiles_per_row
    return [mod_ref[pl.ds(r, 1), c] for c in cols]


def _head_norm(z, g):
    lo = lax.broadcasted_iota(jnp.int32, z.shape, 1) < HEAD_DIM
    s = z * z
    s_lo = jnp.sum(jnp.where(lo, s, 0.0), axis=-1, keepdims=True)
    s_hi = jnp.sum(jnp.where(lo, 0.0, s), axis=-1, keepdims=True)
    ms = jnp.where(lo, s_lo, s_hi) * (1.0 / HEAD_DIM)
    return z * lax.rsqrt(ms + EPS) * g


def _rope(y, cos, sin):
    lane = lax.broadcasted_iota(jnp.int32, y.shape, 1)
    first = (lane % 32) < 16
    partner = jnp.where(first, pltpu.roll(y, LANES - 16, 1), pltpu.roll(y, 16, 1))
    return y * cos + partner * sin


def _dup_heads(y):
    lo = lax.broadcasted_iota(jnp.int32, y.shape, 1) < HEAD_DIM
    sw = pltpu.roll(y, HEAD_DIM, 1)
    return jnp.concatenate([jnp.where(lo, y, sw), jnp.where(lo, sw, y)], axis=-1)


def _proj_kernel(*refs, rope, seq_len, fold, cache, n_cast, tiles_per_mod_row):
    x_ref, mod_ref, g1_ref, w_ref, gq_ref, gk_ref = refs[:6]
    refs = refs[6:]
    if rope:
        cos_ref, sin_ref = refs[:2]
        refs = refs[2:]
    n_tabs = 4 if fold else 2
    tabs, refs = refs[:n_tabs], refs[n_tabs:]
    cast_in, refs = refs[:n_cast], refs[n_cast:]
    mix_ref, q_ref, kd_ref, vd_ref, gate_ref = refs[:5]
    refs = refs[5:]
    if cache:
        kc_ref, vc_ref = refs[:2]
        refs = refs[2:]
    cast_out, (zf_ref, ab_ref) = refs[:n_cast], refs[n_cast:]
    for src_ref, dst_ref in zip(cast_in, cast_out):
        dst_ref[...] = src_ref[...].astype(BF16)

    x = x_ref[...]
    sh1, sc1 = _mod_vectors(mod_ref, pl.program_id(0), tiles_per_mod_row)[0:2]
    h = x * lax.rsqrt(jnp.mean(x * x, axis=-1, keepdims=True) + EPS) * g1_ref[...]
    hb = (h * (1.0 + sc1) + sh1).astype(BF16)

    zf_ref[...] = _dot(hb, w_ref[:, 0:_Q0]).astype(BF16)
    mixer = _fourier_stages(zf_ref, tabs, mix_ref, ab_ref, seq_len, fold)
    n_sections = 2 + 2 * D_MODEL // GATE_CHUNK
    issued = [0]

    def run_mixer_share(section):
        while issued[0] < len(mixer) and issued[0] * n_sections < (section + 1) * len(mixer):
            mixer[issued[0]]()
            issued[0] += 1

    gq = jnp.concatenate([gq_ref[...]] * (LANES // HEAD_DIM), axis=-1)
    gk = jnp.concatenate([gk_ref[...]] * (LANES // HEAD_DIM), axis=-1)

    zq = _dot(hb, w_ref[:, _Q0:_K0])
    for j in range(ATTN_WIDTH // LANES):
        y = _head_norm(zq[:, j * LANES:(j + 1) * LANES], gq)
        if rope:
            y = _rope(y, cos_ref[...], sin_ref[...])
        q_ref[:, j * LANES:(j + 1) * LANES] = (y * (HEAD_DIM ** -0.5 * LOG2_E)).astype(BF16)
    run_mixer_share(0)

    zkv = _dot(hb, w_ref[:, _K0:_G0])
    k = _head_norm(zkv[:, 0:KV_WIDTH], gk)
    v = zkv[:, KV_WIDTH:2 * KV_WIDTH]
    if cache:
        for s in range(x.shape[0] // seq_len):
            rows = slice(s * seq_len, (s + 1) * seq_len)
            kc_ref[s] = k[rows, :].T
            vc_ref[s] = v[rows, :].T
    if rope:
        k = _rope(k, cos_ref[...], sin_ref[...])
    kt = k.T
    kd_ref[...] = jnp.concatenate(
        [kt[0:HEAD_DIM], kt[0:HEAD_DIM], kt[HEAD_DIM:], kt[HEAD_DIM:]], axis=0).astype(BF16)
    vd_ref[...] = _dup_heads(v).astype(BF16)
    run_mixer_share(1)

    for j in range(2 * D_MODEL // GATE_CHUNK):
        cols = slice(j * GATE_CHUNK, (j + 1) * GATE_CHUNK)
        zg = _dot(hb, w_ref[:, _G0 + cols.start:_G0 + cols.stop])
        gate_ref[:, cols] = _sigmoid(zg).astype(BF16)
        run_mixer_share(2 + j)


def _proj(x, mod3, g1, w_in_b, gq2, gk2, seq_len, rope_tabs, mod_row, rows_per_mod_row, cache,
          cast_weights):
    t = x.shape[0]
    tm = PROJ_TOKEN_TILE
    assert tm % seq_len == 0, "a projection tile must hold whole sequences for the mixer"
    steps = t // tm
    rope = rope_tabs is not None
    row = lambda i: (i, 0)
    tiles_per_mod_row = None if rows_per_mod_row is None else rows_per_mod_row // tm
    in_specs = [pl.BlockSpec((tm, D_MODEL), row),
                _mod_spec(mod_row),
                _resident((1, D_MODEL)),
                _resident((D_MODEL, IN_WIDTH)),
                _resident((1, HEAD_DIM)),
                _resident((1, HEAD_DIM))]
    args = [x, mod3, g1, w_in_b, gq2, gk2]
    if rope:
        tiles_per_seq = rope_tabs[0].shape[0] // tm
        tab = pl.BlockSpec((tm, LANES), lambda i: (i % tiles_per_seq, 0))
        in_specs += [tab, tab]
        args += list(rope_tabs)
    fold = _fold_dft(seq_len)
    dft = _dft_tables(seq_len, fold)
    in_specs += [_resident(tab.shape) for tab in dft]
    args += list(dft)
    slabs = [pl.BlockSpec((w.shape[0] // steps, w.shape[1]), row) for w in cast_weights]
    in_specs += slabs
    args += list(cast_weights)
    out_specs = [pl.BlockSpec((tm, F_WIDTH), row),
                 pl.BlockSpec((tm, ATTN_WIDTH), row),
                 pl.BlockSpec((2 * KV_WIDTH, tm), lambda i: (0, i)),
                 pl.BlockSpec((tm, 2 * KV_WIDTH), row),
                 pl.BlockSpec((tm, 2 * D_MODEL), row)]
    out_shape = [pltpu.HBM((t, F_WIDTH), BF16),
                 pltpu.HBM((t, ATTN_WIDTH), BF16),
                 pltpu.HBM((2 * KV_WIDTH, t), BF16),
                 pltpu.HBM((t, 2 * KV_WIDTH), BF16),
                 pltpu.HBM((t, 2 * D_MODEL), BF16)]
    if cache:
        seqs = tm // seq_len
        out_specs += [pl.BlockSpec((seqs, KV_WIDTH, seq_len), lambda i: (i, 0, 0))] * 2
        out_shape += [pltpu.HBM((t // seq_len, KV_WIDTH, seq_len), F32)] * 2
    out_specs += slabs
    out_shape += [pltpu.HBM(w.shape, BF16) for w in cast_weights]
    return pl.pallas_call(
        functools.partial(_proj_kernel, rope=rope, seq_len=seq_len, fold=fold, cache=cache,
                          n_cast=len(cast_weights), tiles_per_mod_row=tiles_per_mod_row),
        grid=(steps,),
        in_specs=in_specs,
        out_specs=out_specs,
        out_shape=out_shape,
        scratch_shapes=[pltpu.VMEM((tm, F_WIDTH), BF16),
                        pltpu.VMEM((2, 2 * seq_len, F_WIDTH), BF16)],
        compiler_params=pltpu.CompilerParams(
            dimension_semantics=("arbitrary",), vmem_limit_bytes=RESIDENT_WEIGHTS_VMEM),
        name="proj_lat" if rope else "proj_ctx",
    )(*[_hbm(a) for a in args])


def _dft_tables(n, fold):
    def cs(m):
        idx = np.arange(m, dtype=np.int64)
        ang = 2.0 * np.pi * ((idx[:, None] * idx[None, :]) % m).astype(np.float64) / m
        return np.cos(ang) / np.sqrt(m), np.sin(ang) / np.sqrt(m)
    cd, sd = cs(F_GROUP_DIM)
    cn, sn = cs(n)
    tabs = [np.concatenate([cd, sd], axis=1)]
    if fold:
        half = n // 2
        rev = np.zeros((half, half), np.float64)
        rev[np.arange(1, half), half - np.arange(1, half)] = 1.0
        tabs += [cn[:half], -sn[:half], rev]
    else:
        tabs += [np.concatenate([cn, -sn], axis=1)]
    return tuple(jnp.asarray(t.astype(np.float32)).astype(BF16) for t in tabs)


def _fold_dft(n):
    return n // 2 >= 2 * MXU_DIM


def _fourier_stages(zf_ref, tabs, o_ref, ab_ref, n, fold):
    csd_ref = tabs[0]
    if fold:
        ch_ref, sh_ref, rev_ref = tabs[1:]
    else:
        (csn_ref,) = tabs[1:]
    half = n // 2
    units = []
    for s in range(zf_ref.shape[0] // n):
        buf = s % 2
        r0 = s * n

        def channels(r0=r0, buf=buf):
            for g in range(F_GROUPS):
                cols = slice(g * F_GROUP_DIM, (g + 1) * F_GROUP_DIM)
                ab = _dot(zf_ref[r0:r0 + n, cols], csd_ref[...])
                ab_ref[buf, 0:n, cols] = ab[:, 0:F_GROUP_DIM].astype(BF16)
                ab_ref[buf, n:2 * n, cols] = ab[:, F_GROUP_DIM:].astype(BF16)

        def positions(r0=r0, buf=buf):
            if not fold:
                o_ref[r0:r0 + n, :] = _dot(csn_ref[...], ab_ref[buf]).astype(BF16)
                return
            p = _dot(ch_ref[...], ab_ref[buf, 0:n, :])
            q = _dot(sh_ref[...], ab_ref[buf, n:2 * n, :])
            o_ref[r0:r0 + half, :] = (p + q).astype(BF16)
            upper = _dot(rev_ref[...], (p - q).astype(BF16))
            a = ab_ref[buf, 0:n, :].astype(F32)
            even = lax.broadcasted_iota(jnp.int32, a.shape, 0) % 2 == 0
            mid = jnp.sum(jnp.where(even, a, -a), axis=0, keepdims=True) * (n ** -0.5)
            first = lax.broadcasted_iota(jnp.int32, upper.shape, 0) == 0
            o_ref[r0 + half:r0 + n, :] = jnp.where(first, mid, upper).astype(BF16)

        units.append((channels, positions))
    return _pipelined(units)


def _group_scores(q_stack, keys, s_ref):
    kt = keys[0] if len(keys) == 1 else jnp.concatenate(keys, axis=1)
    top = lax.broadcasted_iota(jnp.int32, kt.shape, 0) < HEAD_DIM
    zero = jnp.zeros_like(kt)
    s_ref[...] = _dot(q_stack, jnp.concatenate(
        [jnp.where(top, kt, zero), jnp.where(top, zero, kt)], axis=1))


def _group_softmax_pv(values, sinks, band, s_ref, p_ref, sk_ref, rows_per_pair, chunk):
    rows = s_ref.shape[0]
    keys = s_ref.shape[1] // 2
    bounds = [0]
    for vd in values:
        bounds.append(bounds[-1] + vd.shape[0])
    lo = lax.broadcasted_iota(jnp.int32, (chunk, LANES), 1) < HEAD_DIM

    for r0 in range(0, rows, chunk):
        rs = slice(r0, r0 + chunk)
        sink_terms = []
        for half in range(2):
            sink = sinks[2 * (r0 // rows_per_pair) + half]
            cols = [slice(half * keys + bounds[i], half * keys + bounds[i + 1])
                    for i in range(len(values))]
            parts = [s_ref[rs, c] for c in cols]
            if band is not None:
                parts[0] = jnp.where(band(r0 % rows_per_pair, chunk), parts[0], NEG_INF)
            m = sink
            for s in parts:
                m = jnp.maximum(m, jnp.max(s, axis=-1, keepdims=True))
            for c, s in zip(cols, parts):
                p_ref[rs, c] = jnp.exp2(s - m).astype(BF16)
            sink_terms.append(jnp.exp2(sink - m))
        sk_ref[rs, :] = jnp.where(lo, sink_terms[0], sink_terms[1])

    vd = values[0] if len(values) == 1 else jnp.concatenate(values, axis=0)
    v_lo = lax.broadcasted_iota(jnp.int32, vd.shape, 1) < HEAD_DIM
    zero = jnp.zeros_like(vd)
    ones_lo = jnp.where(v_lo, 1.0, 0.0).astype(BF16)
    ones_hi = jnp.where(v_lo, 0.0, 1.0).astype(BF16)
    operand = jnp.concatenate(
        [jnp.concatenate([jnp.where(v_lo, vd, zero), ones_lo], axis=1),
         jnp.concatenate([jnp.where(v_lo, zero, vd), ones_hi], axis=1)], axis=0)
    out = _dot(p_ref[...], operand)
    return out[:, 0:LANES] / (out[:, LANES:2 * LANES] + sk_ref[...])


def _pair_cols(g):
    pa, pb = 2 * g, 2 * g + 1
    return slice(pa * LANES, (pa + 1) * LANES), slice(pb * LANES, (pb + 1) * LANES)


def _attn_tail_kernel(*refs, lat, n_tiles, tm, n, tiles_per_mod_row):
    sink_ref, q_ref, kd_ref, vd_ref = refs[:4]
    refs = refs[4:]
    if lat:
        ck_ref, cv_ref = refs[:2]
        refs = refs[2:]
    (x_ref, mod_ref, mix_ref, gate_ref, g2_ref,
     wf_ref, wao_ref, wout_ref, wup_ref, wdn_ref, o_ref, att_ref, s_ref, p_ref, sk_ref) = refs[:15]
    if lat:
        dist_ref, ckt_ref, cvd_ref = refs[15:18]
    act_ref = refs[-1]
    i = pl.program_id(0)

    def attention_stages():
        units = []
        if lat:
            qb = WINDOW
            span = qb + 2 * WINDOW
            t = jnp.minimum(i, n_tiles - 1) % (n // tm)
            ckt = ck_ref[0]
            ckt_ref[...] = jnp.concatenate(
                [ckt[0:HEAD_DIM], ckt[0:HEAD_DIM], ckt[HEAD_DIM:], ckt[HEAD_DIM:]], axis=0).astype(BF16)
            cvd_ref[...] = _dup_heads(cv_ref[0].T).astype(BF16)
            for rr in range(tm // qb):
                q0 = t * tm + rr * qb
                start = pl.multiple_of(jnp.clip(q0 - WINDOW, 0, n - span), WINDOW)
                dist_ref[rr] = jnp.abs((q0 - start) + lax.broadcasted_iota(jnp.int32, (qb, span), 0)
                                       - lax.broadcasted_iota(jnp.int32, (qb, span), 1))
                band = lambda r0, rows, rr=rr: dist_ref[rr, r0:r0 + rows, :] <= WINDOW
                rows = slice(rr * qb, (rr + 1) * qb)
                for g in range(N_KV_HEADS):
                    buf = len(units) % 2
                    cols = slice(g * LANES, (g + 1) * LANES)
                    qa, qbc = _pair_cols(g)

                    def scores(rows=rows, start=start, cols=cols, qa=qa, qbc=qbc, buf=buf):
                        q_stack = jnp.concatenate([q_ref[rows, qa], q_ref[rows, qbc]], axis=0)
                        _group_scores(q_stack, [kd_ref[cols, pl.ds(start, span)], ckt_ref[cols, :]],
                                      s_ref.at[buf])

                    def finish(rows=rows, start=start, g=g, cols=cols, qa=qa, qbc=qbc, buf=buf,
                               band=band):
                        sinks = [sink_ref[GROUP * g + h] * LOG2_E for h in range(GROUP)]
                        o = _group_softmax_pv(
                            [vd_ref[pl.ds(start, span), cols], cvd_ref[:, cols]], sinks, band,
                            s_ref.at[buf], p_ref.at[buf], sk_ref.at[buf], qb, LAT_SOFTMAX_ROWS)
                        att_ref[rows, qa] = o[0:qb].astype(BF16)
                        att_ref[rows, qbc] = o[qb:2 * qb].astype(BF16)

                    units.append((scores, finish))
        else:
            for s in range(tm // n):
                rows = slice(s * n, (s + 1) * n)
                for g in range(N_KV_HEADS):
                    buf = len(units) % 2
                    cols = slice(g * LANES, (g + 1) * LANES)
                    qa, qbc = _pair_cols(g)

                    def scores(rows=rows, cols=cols, qa=qa, qbc=qbc, buf=buf):
                        q_stack = jnp.concatenate([q_ref[rows, qa], q_ref[rows, qbc]], axis=0)
                        _group_scores(q_stack, [kd_ref[cols, rows]], s_ref.at[buf])

                    def finish(rows=rows, g=g, cols=cols, qa=qa, qbc=qbc, buf=buf):
                        sinks = [sink_ref[GROUP * g + h] * LOG2_E for h in range(GROUP)]
                        o = _group_softmax_pv([vd_ref[rows, cols]], sinks, None, s_ref.at[buf],
                                              p_ref.at[buf], sk_ref.at[buf], n, CTX_SOFTMAX_ROWS)
                        att_ref[rows, qa] = o[0:n].astype(BF16)
                        att_ref[rows, qbc] = o[n:2 * n].astype(BF16)

                    units.append((scores, finish))
        return _pipelined(units)

    def tail(stages):
        gt1, sh2, sc2, gt2 = _mod_vectors(mod_ref, i - 1, tiles_per_mod_row)[2:6]
        yf = _dot(mix_ref[...], wf_ref[...])
        ya = _dot(att_ref[...], wao_ref[...])
        merged = (gate_ref[:, 0:D_MODEL].astype(F32) * yf
                  + gate_ref[:, D_MODEL:2 * D_MODEL].astype(F32) * ya)
        x1 = x_ref[...] + gt1 * _dot(merged.astype(BF16), wout_ref[...])

        h = x1 * lax.rsqrt(jnp.mean(x1 * x1, axis=-1, keepdims=True) + EPS) * g2_ref[...]
        hb = (h * (1.0 + sc2) + sh2).astype(BF16)
        d_ff = wdn_ref.shape[0]
        n_chunks = d_ff // FF_CHUNK
        issued = 0
        for c in range(n_chunks):
            cols = slice(c * FF_CHUNK, (c + 1) * FF_CHUNK)
            a = _dot(hb, wup_ref[:, cols])
            u = _dot(hb, wup_ref[:, d_ff + cols.start:d_ff + cols.stop])
            act_ref[:, cols] = (a * _sigmoid(a) * u).astype(BF16)
            while issued < len(stages) and issued * n_chunks < (c + 1) * len(stages):
                stages[issued]()
                issued += 1
        o_ref[...] = x1 + gt2 * _dot(act_ref[...], wdn_ref[...])

    @pl.when(i == 0)
    def _():
        for stage in attention_stages():
            stage()

    @pl.when(jnp.logical_and(i > 0, i < n_tiles))
    def _():
        tail(attention_stages())

    @pl.when(i == n_tiles)
    def _():
        tail([])


def _attn_tail(sinks, q, kd, vd, cache, x, mod3, mix, gate, g2, wf, wao, wout, wup, wdn,
               mod_row, rows_per_mod_row, n):
    t = x.shape[0]
    tm = TAIL_TOKEN_TILE
    n_tiles = t // tm
    lat = cache is not None
    tiles_per_mod_row = None if rows_per_mod_row is None else rows_per_mod_row // tm
    front = lambda i: (jnp.minimum(i, n_tiles - 1), 0)
    back = lambda i: (jnp.maximum(i - 1, 0), 0)
    in_specs = [pl.BlockSpec(memory_space=pltpu.SMEM), pl.BlockSpec((tm, ATTN_WIDTH), front)]
    args = [sinks, q, kd, vd]
    if lat:
        past = cache[0].shape[2]
        seq_of = lambda i: jnp.minimum(i, n_tiles - 1) // (n // tm)
        in_specs += [pl.BlockSpec((2 * KV_WIDTH, n), lambda i: (0, seq_of(i))),
                     pl.BlockSpec((n, 2 * KV_WIDTH), lambda i: (seq_of(i), 0))]
        in_specs += [pl.BlockSpec((1, KV_WIDTH, past), lambda i: (seq_of(i), 0, 0))] * 2
        args += list(cache)
        m_rows, keys = 2 * WINDOW, 3 * WINDOW + past
    else:
        in_specs += [pl.BlockSpec((2 * KV_WIDTH, tm), lambda i: (0, front(i)[0])),
                     pl.BlockSpec((tm, 2 * KV_WIDTH), front)]
        m_rows, keys = 2 * n, n
    in_specs += [pl.BlockSpec((tm, D_MODEL), back),
                 _mod_spec(mod_row),
                 pl.BlockSpec((tm, F_WIDTH), back),
                 pl.BlockSpec((tm, 2 * D_MODEL), back),
                 _resident((1, D_MODEL)),
                 _resident(wf.shape), _resident(wao.shape), _resident(wout.shape),
                 _resident(wup.shape), _resident(wdn.shape)]
    args += [x, mod3, mix, gate, g2, wf, wao, wout, wup, wdn]
    scratch = [pltpu.VMEM((tm, ATTN_WIDTH), BF16),
               pltpu.VMEM((2, m_rows, 2 * keys), F32),
               pltpu.VMEM((2, m_rows, 2 * keys), BF16),
               pltpu.VMEM((2, m_rows, LANES), F32)]
    if lat:
        scratch += [pltpu.VMEM((tm // WINDOW, WINDOW, 3 * WINDOW), jnp.int32),
                    pltpu.VMEM((2 * KV_WIDTH, past), BF16),
                    pltpu.VMEM((past, 2 * KV_WIDTH), BF16)]
    scratch += [pltpu.VMEM((tm, wdn.shape[0]), BF16)]
    return pl.pallas_call(
        functools.partial(_attn_tail_kernel, lat=lat, n_tiles=n_tiles, tm=tm, n=n,
                          tiles_per_mod_row=tiles_per_mod_row),
        grid=(n_tiles + 1,),
        in_specs=in_specs,
        out_specs=pl.BlockSpec((tm, D_MODEL), back),
        out_shape=pltpu.HBM((t, D_MODEL), F32),
        scratch_shapes=scratch,
        compiler_params=pltpu.CompilerParams(
            dimension_semantics=("arbitrary",), vmem_limit_bytes=RESIDENT_WEIGHTS_VMEM),
        name="attn_tail_lat" if lat else "attn_tail_ctx",
    )(sinks, *[_hbm(a) for a in args[1:]])


def _rope_tables(n):
    rows = n // GRID_W
    row = np.repeat(np.arange(rows, dtype=np.float64), GRID_W)
    col = np.tile(np.arange(GRID_W, dtype=np.float64), rows)
    axis_dim = HEAD_DIM // 2
    inv_freq = ROPE_THETA ** (-np.arange(0, axis_dim, 2, dtype=np.float64) / axis_dim)

    def axis_tabs(pos):
        ang = pos[:, None] * inv_freq[None, :]
        cos, sin = np.cos(ang), np.sin(ang)
        return np.concatenate([cos, cos], axis=-1), np.concatenate([-sin, sin], axis=-1)

    cr, sr = axis_tabs(row)
    cc, sc = axis_tabs(col)
    cos = np.tile(np.concatenate([cr, cc], axis=-1), (1, LANES // HEAD_DIM))
    sin = np.tile(np.concatenate([sr, sc], axis=-1), (1, LANES // HEAD_DIM))
    return jnp.asarray(cos.astype(np.float32)), jnp.asarray(sin.astype(np.float32))


@jax.jit
def _layer(xp, xs, ck, cv, c, c_ctx, w_ada, b_ada, g_norm1, g_norm2, w_in, g_q, g_k, sink,
           w_f, w_ao, w_out, w_up, w_down):
    bp, sp, _ = xp.shape
    bs, ss, _ = xs.shape
    past = ck.shape[1]

    mod3, w_in_b = _ada(c, c_ctx[None, :], w_ada, b_ada[None, :], w_in)
    g1, g2 = g_norm1[None, :], g_norm2[None, :]
    gq2, gk2 = g_q[None, :], g_k[None, :]
    ctx_mod = (CTX_MOD_ROW, None)
    lat_mod = (0, ss)

    xp2 = xp.reshape(bp * sp, D_MODEL)
    xs2 = xs.reshape(bs * ss, D_MODEL)
    mix, q, kd, vd, gate, k_new, v_new, wup = _proj(
        xp2, mod3, g1, w_in_b, gq2, gk2, sp, None, *ctx_mod, True, [w_up])
    mix_s, q_s, kd_s, vd_s, gate_s, wf, wao, wout, wdn = _proj(
        xs2, mod3, g1, w_in_b, gq2, gk2, ss, _rope_tables(ss), *lat_mod, False,
        [w_f, w_ao, w_out, w_down])

    yp = _attn_tail(sink, q, kd, vd, None, xp2, mod3, mix, gate, g2,
                    wf, wao, wout, wup, wdn, *ctx_mod, sp)

    def cache_t(t):
        return t.transpose(0, 2, 3, 1).reshape(bs, KV_WIDTH, past)

    ys = _attn_tail(sink, q_s, kd_s, vd_s, (cache_t(ck), cache_t(cv)), xs2, mod3,
                    mix_s, gate_s, g2, wf, wao, wout, wup, wdn, *lat_mod, ss)

    def cache_layout(t):
        return t.reshape(bp, N_KV_HEADS, HEAD_DIM, sp).transpose(0, 3, 1, 2)

    return yp.reshape(xp.shape), ys.reshape(xs.shape), cache_layout(k_new), cache_layout(v_new)


def kernel(x_prompt, x_sample, cache_k, cache_v, c, c_ctx, w_ada, b_ada, g_norm1, g_norm2,
           w_in, g_q, g_k, sinks, w_f, w_ao, w_out, w_up, w_down):
    depth = w_in.shape[0]
    xp, xs = x_prompt, x_sample
    new_k, new_v = [], []
    for l in range(depth):
        xp, xs, k_ctx, v_ctx = _layer(
            xp, xs, cache_k[:, l], cache_v[:, l], c, c_ctx, w_ada[l], b_ada[l], g_norm1[l],
            g_norm2[l], w_in[l], g_q[l], g_k[l], sinks[l], w_f[l], w_ao[l], w_out[l], w_up[l],
            w_down[l])
        new_k.append(k_ctx)
        new_v.append(v_ctx)
    return (xp, xs, jnp.stack(new_k, axis=1), jnp.stack(new_v, axis=1))
```

```python
import functools

import numpy as np
import jax
import jax.numpy as jnp
from jax import lax
from jax.experimental import pallas as pl
from jax.experimental.pallas import tpu as pltpu

D_MODEL = 1024
GRID_W = 64
N_HEADS = 8
N_KV_HEADS = 2
GROUP = N_HEADS // N_KV_HEADS
HEAD_DIM = 64
WINDOW = 128
F_GROUPS = 4
F_GROUP_DIM = 128
F_WIDTH = F_GROUPS * F_GROUP_DIM
ATTN_WIDTH = N_HEADS * HEAD_DIM
KV_WIDTH = N_KV_HEADS * HEAD_DIM
ROPE_THETA = 10000.0
ROPE_AXIS_DIM = HEAD_DIM // 2
EPS = 1e-6
NEG_INF = -1e30
LOG2_E = 1.4426950408889634

LANES = 128
MXU_DIM = 256
VMEM_BYTES_V7X = 64 * 1024 * 1024

_Q0 = F_WIDTH
_K0 = _Q0 + ATTN_WIDTH
_V0 = _K0 + KV_WIDTH
_G0 = _V0 + KV_WIDTH
IN_WIDTH = _G0 + 2 * D_MODEL

SUBLANES = 8
N_MOD = 6
MOD_ROWS = 16
CTX_MOD_ROW = 8
ADA_STEPS = 8

PROJ_TOKEN_TILE = 1024
TAIL_TOKEN_TILE = 512
FF_CHUNK = MXU_DIM
GATE_CHUNK = 4 * MXU_DIM
CTX_SOFTMAX_ROWS = 64
LAT_SOFTMAX_ROWS = 32

F32 = jnp.float32
BF16 = jnp.bfloat16


RESIDENT_WEIGHTS_VMEM = VMEM_BYTES_V7X - 6 * 2 ** 20


def _dot(a, b):
    return jnp.dot(a, b, preferred_element_type=F32)


def _sigmoid(x):
    return 1.0 / (1.0 + jnp.exp(-x))


def _resident(shape):
    zeros = (0,) * len(shape)
    return pl.BlockSpec(shape, lambda *_: zeros, pipeline_mode=pl.Buffered(1))


def _hbm(x):
    return pltpu.with_memory_space_constraint(x, pltpu.HBM)


def _pipelined(units):
    stages = [units[0][0]]
    for k, (_, second) in enumerate(units):
        if k + 1 < len(units):
            stages.append(units[k + 1][0])
        stages.append(second)
    return stages


def _ada_kernel(c_ref, cctx_ref, w_ref, b_ref, win_ref, o_ref, winb_ref):
    lat_rows = c_ref.shape[0]
    pieces = [c_ref[...]]
    if lat_rows < CTX_MOD_ROW:
        pieces.append(jnp.zeros((CTX_MOD_ROW - lat_rows, D_MODEL), F32))
    pieces.append(jnp.broadcast_to(cctx_ref[...], (MOD_ROWS - CTX_MOD_ROW, D_MODEL)))
    c = jnp.concatenate(pieces, axis=0)
    s = c * _sigmoid(c)
    o_ref[...] = _dot(s.astype(BF16), w_ref[...].astype(BF16)) + b_ref[...]
    winb_ref[...] = win_ref[...].astype(BF16)


def _ada(c, c_ctx, w_ada, b_ada, w_in):
    assert c.shape[0] <= CTX_MOD_ROW, "latent conditioning rows must fit below the context row"
    n = w_ada.shape[1]
    bn = n // ADA_STEPS
    slab = pl.BlockSpec((w_in.shape[0] // ADA_STEPS, w_in.shape[1]), lambda j: (j, 0))
    return pl.pallas_call(
        _ada_kernel,
        grid=(ADA_STEPS,),
        in_specs=[pl.BlockSpec(c.shape, lambda j: (0, 0)),
                  pl.BlockSpec((1, D_MODEL), lambda j: (0, 0)),
                  pl.BlockSpec((D_MODEL, bn), lambda j: (0, j)),
                  pl.BlockSpec((1, bn), lambda j: (0, j)),
                  slab],
        out_specs=[pl.BlockSpec((MOD_ROWS, bn), lambda j: (0, j)), slab],
        out_shape=[pltpu.HBM((MOD_ROWS, n), F32), pltpu.HBM(w_in.shape, BF16)],
        name="ada",
    )(c, c_ctx, w_ada, b_ada, w_in)


def _mod_spec(mod_row):
    return pl.BlockSpec((SUBLANES, N_MOD * D_MODEL), lambda i: (mod_row // SUBLANES, 0))


def _mod_vectors(mod_ref, tile, tiles_per_row):
    cols = [slice(j * D_MODEL, (j + 1) * D_MODEL) for j in range(N_MOD)]
    if tiles_per_row is None:
        return [mod_ref[0:1, c] for c in cols]
    r = tile // tiles_per_row
    return [mod_ref[pl.ds(r, 1), c] for c in cols]


def _head_norm(z, g):
    lo = lax.broadcasted_iota(jnp.int32, z.shape, 1) < HEAD_DIM
    s = z * z
    s_lo = jnp.sum(jnp.where(lo, s, 0.0), axis=-1, keepdims=True)
    s_hi = jnp.sum(jnp.where(lo, 0.0, s), axis=-1, keepdims=True)
    ms = jnp.where(lo, s_lo, s_hi) * (1.0 / HEAD_DIM)
    return z * lax.rsqrt(ms + EPS) * g


def _rope(y, cos, sin):
    half = ROPE_AXIS_DIM // 2
    lane = lax.broadcasted_iota(jnp.int32, y.shape, 1)
    first = (lane % ROPE_AXIS_DIM) < half
    partner = jnp.where(first, pltpu.roll(y, LANES - half, 1), pltpu.roll(y, half, 1))
    return y * cos + partner * sin


def _dup_heads(y):
    lo = lax.broadcasted_iota(jnp.int32, y.shape, 1) < HEAD_DIM
    sw = pltpu.roll(y, HEAD_DIM, 1)
    return jnp.concatenate([jnp.where(lo, y, sw), jnp.where(lo, sw, y)], axis=-1)


def _proj_kernel(*refs, rope, seq_len, fold, cache, n_cast, tiles_per_mod_row):
    x_ref, mod_ref, g1_ref, w_ref, gq_ref, gk_ref = refs[:6]
    refs = refs[6:]
    if rope:
        cos_ref, sin_ref = refs[:2]
        refs = refs[2:]
    n_tabs = 4 if fold else 2
    tabs, refs = refs[:n_tabs], refs[n_tabs:]
    cast_in, refs = refs[:n_cast], refs[n_cast:]
    mix_ref, q_ref, kd_ref, vd_ref, gate_ref = refs[:5]
    refs = refs[5:]
    if cache:
        kc_ref, vc_ref = refs[:2]
        refs = refs[2:]
    cast_out, (zf_ref, ab_ref) = refs[:n_cast], refs[n_cast:]
    for src_ref, dst_ref in zip(cast_in, cast_out):
        dst_ref[...] = src_ref[...].astype(BF16)

    x = x_ref[...]
    sh1, sc1 = _mod_vectors(mod_ref, pl.program_id(0), tiles_per_mod_row)[0:2]
    h = x * lax.rsqrt(jnp.mean(x * x, axis=-1, keepdims=True) + EPS) * g1_ref[...]
    hb = (h * (1.0 + sc1) + sh1).astype(BF16)

    zf_ref[...] = _dot(hb, w_ref[:, 0:_Q0]).astype(BF16)
    mixer = _fourier_stages(zf_ref, tabs, mix_ref, ab_ref, seq_len, fold)
    n_sections = 2 + 2 * D_MODEL // GATE_CHUNK
    issued = [0]

    def run_mixer_share(section):
        while issued[0] < len(mixer) and issued[0] * n_sections < (section + 1) * len(mixer):
            mixer[issued[0]]()
            issued[0] += 1

    gq = jnp.concatenate([gq_ref[...]] * (LANES // HEAD_DIM), axis=-1)
    gk = jnp.concatenate([gk_ref[...]] * (LANES // HEAD_DIM), axis=-1)

    zq = _dot(hb, w_ref[:, _Q0:_K0])
    for j in range(ATTN_WIDTH // LANES):
        y = _head_norm(zq[:, j * LANES:(j + 1) * LANES], gq)
        if rope:
            y = _rope(y, cos_ref[...], sin_ref[...])
        q_ref[:, j * LANES:(j + 1) * LANES] = (y * (HEAD_DIM ** -0.5 * LOG2_E)).astype(BF16)
    run_mixer_share(0)

    zkv = _dot(hb, w_ref[:, _K0:_G0])
    k = _head_norm(zkv[:, 0:KV_WIDTH], gk)
    v = zkv[:, KV_WIDTH:2 * KV_WIDTH]
    if cache:
        for s in range(x.shape[0] // seq_len):
            rows = slice(s * seq_len, (s + 1) * seq_len)
            kc_ref[s] = k[rows, :].T
            vc_ref[s] = v[rows, :].T
    if rope:
        k = _rope(k, cos_ref[...], sin_ref[...])
    kt = k.T
    kd_ref[...] = jnp.concatenate(
        [kt[0:HEAD_DIM], kt[0:HEAD_DIM], kt[HEAD_DIM:], kt[HEAD_DIM:]], axis=0).astype(BF16)
    vd_ref[...] = _dup_heads(v).astype(BF16)
    run_mixer_share(1)

    for j in range(2 * D_MODEL // GATE_CHUNK):
        cols = slice(j * GATE_CHUNK, (j + 1) * GATE_CHUNK)
        zg = _dot(hb, w_ref[:, _G0 + cols.start:_G0 + cols.stop])
        gate_ref[:, cols] = _sigmoid(zg).astype(BF16)
        run_mixer_share(2 + j)


def _proj(x, mod3, g1, w_in_b, gq2, gk2, seq_len, rope_tabs, mod_row, rows_per_mod_row, cache,
          cast_weights):
    t = x.shape[0]
    tm = PROJ_TOKEN_TILE
    assert tm % seq_len == 0, "a projection tile must hold whole sequences for the mixer"
    steps = t // tm
    rope = rope_tabs is not None
    row = lambda i: (i, 0)
    tiles_per_mod_row = None if rows_per_mod_row is None else rows_per_mod_row // tm
    in_specs = [pl.BlockSpec((tm, D_MODEL), row),
                _mod_spec(mod_row),
                _resident((1, D_MODEL)),
                _resident((D_MODEL, IN_WIDTH)),
                _resident((1, HEAD_DIM)),
                _resident((1, HEAD_DIM))]
    args = [x, mod3, g1, w_in_b, gq2, gk2]
    if rope:
        tiles_per_seq = rope_tabs[0].shape[0] // tm
        tab = pl.BlockSpec((tm, LANES), lambda i: (i % tiles_per_seq, 0))
        in_specs += [tab, tab]
        args += list(rope_tabs)
    fold = _fold_dft(seq_len)
    dft = _dft_tables(seq_len, fold)
    in_specs += [_resident(tab.shape) for tab in dft]
    args += list(dft)
    slabs = [pl.BlockSpec((w.shape[0] // steps, w.shape[1]), row) for w in cast_weights]
    in_specs += slabs
    args += list(cast_weights)
    out_specs = [pl.BlockSpec((tm, F_WIDTH), row),
                 pl.BlockSpec((tm, ATTN_WIDTH), row),
                 pl.BlockSpec((2 * KV_WIDTH, tm), lambda i: (0, i)),
                 pl.BlockSpec((tm, 2 * KV_WIDTH), row),
                 pl.BlockSpec((tm, 2 * D_MODEL), row)]
    out_shape = [pltpu.HBM((t, F_WIDTH), BF16),
                 pltpu.HBM((t, ATTN_WIDTH), BF16),
                 pltpu.HBM((2 * KV_WIDTH, t), BF16),
                 pltpu.HBM((t, 2 * KV_WIDTH), BF16),
                 pltpu.HBM((t, 2 * D_MODEL), BF16)]
    if cache:
        seqs = tm // seq_len
        out_specs += [pl.BlockSpec((seqs, KV_WIDTH, seq_len), lambda i: (i, 0, 0))] * 2
        out_shape += [pltpu.HBM((t // seq_len, KV_WIDTH, seq_len), F32)] * 2
    out_specs += slabs
    out_shape += [pltpu.HBM(w.shape, BF16) for w in cast_weights]
    return pl.pallas_call(
        functools.partial(_proj_kernel, rope=rope, seq_len=seq_len, fold=fold, cache=cache,
                          n_cast=len(cast_weights), tiles_per_mod_row=tiles_per_mod_row),
        grid=(steps,),
        in_specs=in_specs,
        out_specs=out_specs,
        out_shape=out_shape,
        scratch_shapes=[pltpu.VMEM((tm, F_WIDTH), BF16),
                        pltpu.VMEM((2, 2 * seq_len, F_WIDTH), BF16)],
        compiler_params=pltpu.CompilerParams(
            dimension_semantics=("arbitrary",), vmem_limit_bytes=RESIDENT_WEIGHTS_VMEM),
        name="proj_lat" if rope else "proj_ctx",
    )(*[_hbm(a) for a in args])


def _dft_tables(n, fold):
    def cs(m):
        idx = np.arange(m, dtype=np.int64)
        ang = 2.0 * np.pi * ((idx[:, None] * idx[None, :]) % m).astype(np.float64) / m
        return np.cos(ang) / np.sqrt(m), np.sin(ang) / np.sqrt(m)
    cd, sd = cs(F_GROUP_DIM)
    cn, sn = cs(n)
    tabs = [np.concatenate([cd, sd], axis=1)]
    if fold:
        half = n // 2
        rev = np.zeros((half, half), np.float64)
        rev[np.arange(1, half), half - np.arange(1, half)] = 1.0
        tabs += [cn[:half], -sn[:half], rev]
    else:
        tabs += [np.concatenate([cn, -sn], axis=1)]
    return tuple(jnp.asarray(t.astype(np.float32)).astype(BF16) for t in tabs)


def _fold_dft(n):
    return n // 2 >= 2 * MXU_DIM


def _fourier_stages(zf_ref, tabs, o_ref, ab_ref, n, fold):
    csd_ref = tabs[0]
    if fold:
        ch_ref, sh_ref, rev_ref = tabs[1:]
    else:
        (csn_ref,) = tabs[1:]
    half = n // 2
    units = []
    for s in range(zf_ref.shape[0] // n):
        buf = s % 2
        r0 = s * n

        def channels(r0=r0, buf=buf):
            for g in range(F_GROUPS):
                cols = slice(g * F_GROUP_DIM, (g + 1) * F_GROUP_DIM)
                ab = _dot(zf_ref[r0:r0 + n, cols], csd_ref[...])
                ab_ref[buf, 0:n, cols] = ab[:, 0:F_GROUP_DIM].astype(BF16)
                ab_ref[buf, n:2 * n, cols] = ab[:, F_GROUP_DIM:].astype(BF16)

        def positions(r0=r0, buf=buf):
            if not fold:
                o_ref[r0:r0 + n, :] = _dot(csn_ref[...], ab_ref[buf]).astype(BF16)
                return
            p = _dot(ch_ref[...], ab_ref[buf, 0:n, :])
            q = _dot(sh_ref[...], ab_ref[buf, n:2 * n, :])
            o_ref[r0:r0 + half, :] = (p + q).astype(BF16)
            upper = _dot(rev_ref[...], (p - q).astype(BF16))
            a = ab_ref[buf, 0:n, :].astype(F32)
            even = lax.broadcasted_iota(jnp.int32, a.shape, 0) % 2 == 0
            mid = jnp.sum(jnp.where(even, a, -a), axis=0, keepdims=True) * (n ** -0.5)
            first = lax.broadcasted_iota(jnp.int32, upper.shape, 0) == 0
            o_ref[r0 + half:r0 + n, :] = jnp.where(first, mid, upper).astype(BF16)

        units.append((channels, positions))
    return _pipelined(units)


def _group_scores(q_stack, keys, s_ref):
    kt = keys[0] if len(keys) == 1 else jnp.concatenate(keys, axis=1)
    top = lax.broadcasted_iota(jnp.int32, kt.shape, 0) < HEAD_DIM
    zero = jnp.zeros_like(kt)
    s_ref[...] = _dot(q_stack, jnp.concatenate(
        [jnp.where(top, kt, zero), jnp.where(top, zero, kt)], axis=1))


def _group_softmax_pv(values, sinks, band, s_ref, p_ref, sk_ref, rows_per_pair, chunk):
    rows = s_ref.shape[0]
    keys = s_ref.shape[1] // 2
    bounds = [0]
    for vd in values:
        bounds.append(bounds[-1] + vd.shape[0])
    lo = lax.broadcasted_iota(jnp.int32, (chunk, LANES), 1) < HEAD_DIM

    for r0 in range(0, rows, chunk):
        rs = slice(r0, r0 + chunk)
        sink_terms = []
        for half in range(2):
            sink = sinks[2 * (r0 // rows_per_pair) + half]
            cols = [slice(half * keys + bounds[i], half * keys + bounds[i + 1])
                    for i in range(len(values))]
            parts = [s_ref[rs, c] for c in cols]
            if band is not None:
                parts[0] = jnp.where(band(r0 % rows_per_pair, chunk), parts[0], NEG_INF)
            m = sink
            for s in parts:
                m = jnp.maximum(m, jnp.max(s, axis=-1, keepdims=True))
            for c, s in zip(cols, parts):
                p_ref[rs, c] = jnp.exp2(s - m).astype(BF16)
            sink_terms.append(jnp.exp2(sink - m))
        sk_ref[rs, :] = jnp.where(lo, sink_terms[0], sink_terms[1])

    vd = values[0] if len(values) == 1 else jnp.concatenate(values, axis=0)
    v_lo = lax.broadcasted_iota(jnp.int32, vd.shape, 1) < HEAD_DIM
    zero = jnp.zeros_like(vd)
    ones_lo = jnp.where(v_lo, 1.0, 0.0).astype(BF16)
    ones_hi = jnp.where(v_lo, 0.0, 1.0).astype(BF16)
    operand = jnp.concatenate(
        [jnp.concatenate([jnp.where(v_lo, vd, zero), ones_lo], axis=1),
         jnp.concatenate([jnp.where(v_lo, zero, vd), ones_hi], axis=1)], axis=0)
    out = _dot(p_ref[...], operand)
    return out[:, 0:LANES] / (out[:, LANES:2 * LANES] + sk_ref[...])


def _pair_cols(g):
    pa, pb = 2 * g, 2 * g + 1
    return slice(pa * LANES, (pa + 1) * LANES), slice(pb * LANES, (pb + 1) * LANES)


def _attn_tail_kernel(*refs, lat, n_tiles, tm, n, tiles_per_mod_row):
    sink_ref, q_ref, kd_ref, vd_ref = refs[:4]
    refs = refs[4:]
    if lat:
        ck_ref, cv_ref = refs[:2]
        refs = refs[2:]
    (x_ref, mod_ref, mix_ref, gate_ref, g2_ref,
     wf_ref, wao_ref, wout_ref, wup_ref, wdn_ref, o_ref, att_ref, s_ref, p_ref, sk_ref) = refs[:15]
    if lat:
        dist_ref, ckt_ref, cvd_ref = refs[15:18]
    act_ref = refs[-1]
    i = pl.program_id(0)

    def attention_stages():
        units = []
        if lat:
            qb = WINDOW
            span = qb + 2 * WINDOW
            t = jnp.minimum(i, n_tiles - 1) % (n // tm)
            ckt = ck_ref[0]
            ckt_ref[...] = jnp.concatenate(
                [ckt[0:HEAD_DIM], ckt[0:HEAD_DIM], ckt[HEAD_DIM:], ckt[HEAD_DIM:]], axis=0).astype(BF16)
            cvd_ref[...] = _dup_heads(cv_ref[0].T).astype(BF16)
            for rr in range(tm // qb):
                q0 = t * tm + rr * qb
                start = pl.multiple_of(jnp.clip(q0 - WINDOW, 0, n - span), WINDOW)
                dist_ref[rr] = jnp.abs((q0 - start) + lax.broadcasted_iota(jnp.int32, (qb, span), 0)
                                       - lax.broadcasted_iota(jnp.int32, (qb, span), 1))
                band = lambda r0, rows, rr=rr: dist_ref[rr, r0:r0 + rows, :] <= WINDOW
                rows = slice(rr * qb, (rr + 1) * qb)
                for g in range(N_KV_HEADS):
                    buf = len(units) % 2
                    cols = slice(g * LANES, (g + 1) * LANES)
                    qa, qbc = _pair_cols(g)

                    def scores(rows=rows, start=start, cols=cols, qa=qa, qbc=qbc, buf=buf):
                        q_stack = jnp.concatenate([q_ref[rows, qa], q_ref[rows, qbc]], axis=0)
                        _group_scores(q_stack, [kd_ref[cols, pl.ds(start, span)], ckt_ref[cols, :]],
                                      s_ref.at[buf])

                    def finish(rows=rows, start=start, g=g, cols=cols, qa=qa, qbc=qbc, buf=buf,
                               band=band):
                        sinks = [sink_ref[GROUP * g + h] * LOG2_E for h in range(GROUP)]
                        o = _group_softmax_pv(
                            [vd_ref[pl.ds(start, span), cols], cvd_ref[:, cols]], sinks, band,
                            s_ref.at[buf], p_ref.at[buf], sk_ref.at[buf], qb, LAT_SOFTMAX_ROWS)
                        att_ref[rows, qa] = o[0:qb].astype(BF16)
                        att_ref[rows, qbc] = o[qb:2 * qb].astype(BF16)

                    units.append((scores, finish))
        else:
            for s in range(tm // n):
                rows = slice(s * n, (s + 1) * n)
                for g in range(N_KV_HEADS):
                    buf = len(units) % 2
                    cols = slice(g * LANES, (g + 1) * LANES)
                    qa, qbc = _pair_cols(g)

                    def scores(rows=rows, cols=cols, qa=qa, qbc=qbc, buf=buf):
                        q_stack = jnp.concatenate([q_ref[rows, qa], q_ref[rows, qbc]], axis=0)
                        _group_scores(q_stack, [kd_ref[cols, rows]], s_ref.at[buf])

                    def finish(rows=rows, g=g, cols=cols, qa=qa, qbc=qbc, buf=buf):
                        sinks = [sink_ref[GROUP * g + h] * LOG2_E for h in range(GROUP)]
                        o = _group_softmax_pv([vd_ref[rows, cols]], sinks, None, s_ref.at[buf],
                                              p_ref.at[buf], sk_ref.at[buf], n, CTX_SOFTMAX_ROWS)
                        att_ref[rows, qa] = o[0:n].astype(BF16)
                        att_ref[rows, qbc] = o[n:2 * n].astype(BF16)

                    units.append((scores, finish))
        return _pipelined(units)

    def tail(stages):
        gt1, sh2, sc2, gt2 = _mod_vectors(mod_ref, i - 1, tiles_per_mod_row)[2:6]
        yf = _dot(mix_ref[...], wf_ref[...])
        ya = _dot(att_ref[...], wao_ref[...])
        merged = (gate_ref[:, 0:D_MODEL].astype(F32) * yf
                  + gate_ref[:, D_MODEL:2 * D_MODEL].astype(F32) * ya)
        x1 = x_ref[...] + gt1 * _dot(merged.astype(BF16), wout_ref[...])

        h = x1 * lax.rsqrt(jnp.mean(x1 * x1, axis=-1, keepdims=True) + EPS) * g2_ref[...]
        hb = (h * (1.0 + sc2) + sh2).astype(BF16)
        d_ff = wdn_ref.shape[0]
        n_chunks = d_ff // FF_CHUNK
        issued = 0
        for c in range(n_chunks):
            cols = slice(c * FF_CHUNK, (c + 1) * FF_CHUNK)
            a = _dot(hb, wup_ref[:, cols])
            u = _dot(hb, wup_ref[:, d_ff + cols.start:d_ff + cols.stop])
            act_ref[:, cols] = (a * _sigmoid(a) * u).astype(BF16)
            while issued < len(stages) and issued * n_chunks < (c + 1) * len(stages):
                stages[issued]()
                issued += 1
        o_ref[...] = x1 + gt2 * _dot(act_ref[...], wdn_ref[...])

    @pl.when(i == 0)
    def _():
        for stage in attention_stages():
            stage()

    @pl.when(jnp.logical_and(i > 0, i < n_tiles))
    def _():
        tail(attention_stages())

    @pl.when(i == n_tiles)
    def _():
        tail([])


def _attn_tail(sinks, q, kd, vd, cache, x, mod3, mix, gate, g2, wf, wao, wout, wup, wdn,
               mod_row, rows_per_mod_row, n):
    t = x.shape[0]
    tm = TAIL_TOKEN_TILE
    n_tiles = t // tm
    lat = cache is not None
    tiles_per_mod_row = None if rows_per_mod_row is None else rows_per_mod_row // tm
    front = lambda i: (jnp.minimum(i, n_tiles - 1), 0)
    back = lambda i: (jnp.maximum(i - 1, 0), 0)
    in_specs = [pl.BlockSpec(memory_space=pltpu.SMEM), pl.BlockSpec((tm, ATTN_WIDTH), front)]
    args = [sinks, q, kd, vd]
    if lat:
        past = cache[0].shape[2]
        seq_of = lambda i: jnp.minimum(i, n_tiles - 1) // (n // tm)
        in_specs += [pl.BlockSpec((2 * KV_WIDTH, n), lambda i: (0, seq_of(i))),
                     pl.BlockSpec((n, 2 * KV_WIDTH), lambda i: (seq_of(i), 0))]
        in_specs += [pl.BlockSpec((1, KV_WIDTH, past), lambda i: (seq_of(i), 0, 0))] * 2
        args += list(cache)
        m_rows, keys = 2 * WINDOW, 3 * WINDOW + past
    else:
        in_specs += [pl.BlockSpec((2 * KV_WIDTH, tm), lambda i: (0, front(i)[0])),
                     pl.BlockSpec((tm, 2 * KV_WIDTH), front)]
        m_rows, keys = 2 * n, n
    in_specs += [pl.BlockSpec((tm, D_MODEL), back),
                 _mod_spec(mod_row),
                 pl.BlockSpec((tm, F_WIDTH), back),
                 pl.BlockSpec((tm, 2 * D_MODEL), back),
                 _resident((1, D_MODEL)),
                 _resident(wf.shape), _resident(wao.shape), _resident(wout.shape),
                 _resident(wup.shape), _resident(wdn.shape)]
    args += [x, mod3, mix, gate, g2, wf, wao, wout, wup, wdn]
    scratch = [pltpu.VMEM((tm, ATTN_WIDTH), BF16),
               pltpu.VMEM((2, m_rows, 2 * keys), F32),
               pltpu.VMEM((2, m_rows, 2 * keys), BF16),
               pltpu.VMEM((2, m_rows, LANES), F32)]
    if lat:
        scratch += [pltpu.VMEM((tm // WINDOW, WINDOW, 3 * WINDOW), jnp.int32),
                    pltpu.VMEM((2 * KV_WIDTH, past), BF16),
                    pltpu.VMEM((past, 2 * KV_WIDTH), BF16)]
    scratch += [pltpu.VMEM((tm, wdn.shape[0]), BF16)]
    return pl.pallas_call(
        functools.partial(_attn_tail_kernel, lat=lat, n_tiles=n_tiles, tm=tm, n=n,
                          tiles_per_mod_row=tiles_per_mod_row),
        grid=(n_tiles + 1,),
        in_specs=in_specs,
        out_specs=pl.BlockSpec((tm, D_MODEL), back),
        out_shape=pltpu.HBM((t, D_MODEL), F32),
        scratch_shapes=scratch,
        compiler_params=pltpu.CompilerParams(
            dimension_semantics=("arbitrary",), vmem_limit_bytes=RESIDENT_WEIGHTS_VMEM),
        name="attn_tail_lat" if lat else "attn_tail_ctx",
    )(sinks, *[_hbm(a) for a in args[1:]])


def _rope_tables(n):
    rows = n // GRID_W
    row = np.repeat(np.arange(rows, dtype=np.float64), GRID_W)
    col = np.tile(np.arange(GRID_W, dtype=np.float64), rows)
    inv_freq = ROPE_THETA ** (-np.arange(0, ROPE_AXIS_DIM, 2, dtype=np.float64) / ROPE_AXIS_DIM)

    def axis_tabs(pos):
        ang = pos[:, None] * inv_freq[None, :]
        cos, sin = np.cos(ang), np.sin(ang)
        return np.concatenate([cos, cos], axis=-1), np.concatenate([-sin, sin], axis=-1)

    cr, sr = axis_tabs(row)
    cc, sc = axis_tabs(col)
    cos = np.tile(np.concatenate([cr, cc], axis=-1), (1, LANES // HEAD_DIM))
    sin = np.tile(np.concatenate([sr, sc], axis=-1), (1, LANES // HEAD_DIM))
    return jnp.asarray(cos.astype(np.float32)), jnp.asarray(sin.astype(np.float32))


@jax.jit
def _layer(xp, xs, ck, cv, c, c_ctx, w_ada, b_ada, g_norm1, g_norm2, w_in, g_q, g_k, sink,
           w_f, w_ao, w_out, w_up, w_down):
    bp, sp, _ = xp.shape
    bs, ss, _ = xs.shape
    past = ck.shape[1]

    mod3, w_in_b = _ada(c, c_ctx[None, :], w_ada, b_ada[None, :], w_in)
    g1, g2 = g_norm1[None, :], g_norm2[None, :]
    gq2, gk2 = g_q[None, :], g_k[None, :]
    ctx_mod = (CTX_MOD_ROW, None)
    lat_mod = (0, ss)

    xp2 = xp.reshape(bp * sp, D_MODEL)
    xs2 = xs.reshape(bs * ss, D_MODEL)
    mix, q, kd, vd, gate, k_new, v_new, wup = _proj(
        xp2, mod3, g1, w_in_b, gq2, gk2, sp, None, *ctx_mod, True, [w_up])
    mix_s, q_s, kd_s, vd_s, gate_s, wf, wao, wout, wdn = _proj(
        xs2, mod3, g1, w_in_b, gq2, gk2, ss, _rope_tables(ss), *lat_mod, False,
        [w_f, w_ao, w_out, w_down])

    yp = _attn_tail(sink, q, kd, vd, None, xp2, mod3, mix, gate, g2,
                    wf, wao, wout, wup, wdn, *ctx_mod, sp)

    def cache_t(t):
        return t.transpose(0, 2, 3, 1).reshape(bs, KV_WIDTH, past)

    ys = _attn_tail(sink, q_s, kd_s, vd_s, (cache_t(ck), cache_t(cv)), xs2, mod3,
                    mix_s, gate_s, g2, wf, wao, wout, wup, wdn, *lat_mod, ss)

    def cache_layout(t):
        return t.reshape(bp, N_KV_HEADS, HEAD_DIM, sp).transpose(0, 3, 1, 2)

    return yp.reshape(xp.shape), ys.reshape(xs.shape), cache_layout(k_new), cache_layout(v_new)


def kernel(x_prompt, x_sample, cache_k, cache_v, c, c_ctx, w_ada, b_ada, g_norm1, g_norm2,
           w_in, g_q, g_k, sinks, w_f, w_ao, w_out, w_up, w_down):
    depth = w_in.shape[0]
    xp, xs = x_prompt, x_sample
    new_k, new_v = [], []
    for l in range(depth):
        xp, xs, k_ctx, v_ctx = _layer(
            xp, xs, cache_k[:, l], cache_v[:, l], c, c_ctx, w_ada[l], b_ada[l], g_norm1[l],
            g_norm2[l], w_in[l], g_q[l], g_k[l], sinks[l], w_f[l], w_ao[l], w_out[l], w_up[l],
            w_down[l])
        new_k.append(k_ctx)
        new_v.append(v_ctx)
    return (xp, xs, jnp.stack(new_k, axis=1), jnp.stack(new_v, axis=1))
```

```python
import functools

import numpy as np
import jax
import jax.numpy as jnp
from jax import lax
from jax.experimental import pallas as pl
from jax.experimental.pallas import tpu as pltpu

D_MODEL = 1024
GRID_W = 64
N_HEADS = 8
N_KV_HEADS = 2
GROUP = N_HEADS // N_KV_HEADS
HEAD_DIM = 64
WINDOW = 128
F_GROUPS = 4
F_GROUP_DIM = 128
F_WIDTH = F_GROUPS * F_GROUP_DIM
ATTN_WIDTH = N_HEADS * HEAD_DIM
KV_WIDTH = N_KV_HEADS * HEAD_DIM
ROPE_THETA = 10000.0
ROPE_AXIS_DIM = HEAD_DIM // 2
EPS = 1e-6
NEG_INF = -1e30
LOG2_E = 1.4426950408889634

LANES = 128
MXU_DIM = 256
VMEM_BYTES_V7X = 64 * 1024 * 1024

_Q0 = F_WIDTH
_K0 = _Q0 + ATTN_WIDTH
_V0 = _K0 + KV_WIDTH
_G0 = _V0 + KV_WIDTH
IN_WIDTH = _G0 + 2 * D_MODEL

SUBLANES = 8
N_MOD = 6
MOD_ROWS = 16
CTX_MOD_ROW = 8
ADA_STEPS = 8

PROJ_TOKEN_TILE = 1024
TAIL_TOKEN_TILE = 512
FF_CHUNK = MXU_DIM
GATE_CHUNK = 4 * MXU_DIM
CTX_SOFTMAX_ROWS = 64
LAT_SOFTMAX_ROWS = 32

F32 = jnp.float32
BF16 = jnp.bfloat16


RESIDENT_WEIGHTS_VMEM = VMEM_BYTES_V7X - 6 * 2 ** 20


def _dot(a, b):
    return jnp.dot(a, b, preferred_element_type=F32)


def _sigmoid(x):
    return 1.0 / (1.0 + jnp.exp(-x))


def _resident(shape):
    zeros = (0,) * len(shape)
    return pl.BlockSpec(shape, lambda *_: zeros, pipeline_mode=pl.Buffered(1))


def _hbm(x):
    try:
        return pltpu.with_memory_space_constraint(x, pltpu.HBM)
    except ValueError:
        return x


def _pipelined(units):
    stages = [units[0][0]]
    for k, (_, second) in enumerate(units):
        if k + 1 < len(units):
            stages.append(units[k + 1][0])
        stages.append(second)
    return stages


def _ada_kernel(c_ref, cctx_ref, w_ref, b_ref, win_ref, o_ref, winb_ref):
    lat_rows = c_ref.shape[0]
    pieces = [c_ref[...]]
    if lat_rows < CTX_MOD_ROW:
        pieces.append(jnp.zeros((CTX_MOD_ROW - lat_rows, D_MODEL), F32))
    pieces.append(jnp.broadcast_to(cctx_ref[...], (MOD_ROWS - CTX_MOD_ROW, D_MODEL)))
    c = jnp.concatenate(pieces, axis=0)
    s = c * _sigmoid(c)
    o_ref[...] = _dot(s.astype(BF16), w_ref[...].astype(BF16)) + b_ref[...]
    winb_ref[...] = win_ref[...].astype(BF16)


def _ada(c, c_ctx, w_ada, b_ada, w_in):
    assert c.shape[0] <= CTX_MOD_ROW, "latent conditioning rows must fit below the context row"
    n = w_ada.shape[1]
    bn = n // ADA_STEPS
    slab = pl.BlockSpec((w_in.shape[0] // ADA_STEPS, w_in.shape[1]), lambda j: (j, 0))
    return pl.pallas_call(
        _ada_kernel,
        grid=(ADA_STEPS,),
        in_specs=[pl.BlockSpec(c.shape, lambda j: (0, 0)),
                  pl.BlockSpec((1, D_MODEL), lambda j: (0, 0)),
                  pl.BlockSpec((D_MODEL, bn), lambda j: (0, j)),
                  pl.BlockSpec((1, bn), lambda j: (0, j)),
                  slab],
        out_specs=[pl.BlockSpec((MOD_ROWS, bn), lambda j: (0, j)), slab],
        out_shape=[pltpu.HBM((MOD_ROWS, n), F32), pltpu.HBM(w_in.shape, BF16)],
        name="ada",
    )(c, c_ctx, w_ada, b_ada, w_in)


def _mod_spec(mod_row):
    return pl.BlockSpec((SUBLANES, N_MOD * D_MODEL), lambda i: (mod_row // SUBLANES, 0))


def _mod_vectors(mod_ref, tile, tiles_per_row):
    cols = [slice(j * D_MODEL, (j + 1) * D_MODEL) for j in range(N_MOD)]
    if tiles_per_row is None:
        return [mod_ref[0:1, c] for c in cols]
    r = tile // tiles_per_row
    return [mod_ref[pl.ds(r, 1), c] for c in cols]


def _head_norm(z, g):
    lo = lax.broadcasted_iota(jnp.int32, z.shape, 1) < HEAD_DIM
    s = z * z
    s_lo = jnp.sum(jnp.where(lo, s, 0.0), axis=-1, keepdims=True)
    s_hi = jnp.sum(jnp.where(lo, 0.0, s), axis=-1, keepdims=True)
    ms = jnp.where(lo, s_lo, s_hi) * (1.0 / HEAD_DIM)
    return z * lax.rsqrt(ms + EPS) * g


def _rope(y, cos, sin):
    half = ROPE_AXIS_DIM // 2
    lane = lax.broadcasted_iota(jnp.int32, y.shape, 1)
    first = (lane % ROPE_AXIS_DIM) < half
    partner = jnp.where(first, pltpu.roll(y, LANES - half, 1), pltpu.roll(y, half, 1))
    return y * cos + partner * sin


def _dup_heads(y):
    lo = lax.broadcasted_iota(jnp.int32, y.shape, 1) < HEAD_DIM
    sw = pltpu.roll(y, HEAD_DIM, 1)
    return jnp.concatenate([jnp.where(lo, y, sw), jnp.where(lo, sw, y)], axis=-1)


def _proj_kernel(*refs, rope, seq_len, fold, cache, n_cast, tiles_per_mod_row):
    x_ref, mod_ref, g1_ref, w_ref, gq_ref, gk_ref = refs[:6]
    refs = refs[6:]
    if rope:
        cos_ref, sin_ref = refs[:2]
        refs = refs[2:]
    n_tabs = 4 if fold else 2
    tabs, refs = refs[:n_tabs], refs[n_tabs:]
    cast_in, refs = refs[:n_cast], refs[n_cast:]
    mix_ref, q_ref, kd_ref, vd_ref, gate_ref = refs[:5]
    refs = refs[5:]
    if cache:
        kc_ref, vc_ref = refs[:2]
        refs = refs[2:]
    cast_out, (zf_ref, ab_ref) = refs[:n_cast], refs[n_cast:]
    for src_ref, dst_ref in zip(cast_in, cast_out):
        dst_ref[...] = src_ref[...].astype(BF16)

    x = x_ref[...]
    sh1, sc1 = _mod_vectors(mod_ref, pl.program_id(0), tiles_per_mod_row)[0:2]
    h = x * lax.rsqrt(jnp.mean(x * x, axis=-1, keepdims=True) + EPS) * g1_ref[...]
    hb = (h * (1.0 + sc1) + sh1).astype(BF16)

    zf_ref[...] = _dot(hb, w_ref[:, 0:_Q0]).astype(BF16)
    mixer = _fourier_stages(zf_ref, tabs, mix_ref, ab_ref, seq_len, fold)
    n_sections = 2 + 2 * D_MODEL // GATE_CHUNK
    issued = [0]

    def run_mixer_share(section):
        while issued[0] < len(mixer) and issued[0] * n_sections < (section + 1) * len(mixer):
            mixer[issued[0]]()
            issued[0] += 1

    gq = jnp.concatenate([gq_ref[...]] * (LANES // HEAD_DIM), axis=-1)
    gk = jnp.concatenate([gk_ref[...]] * (LANES // HEAD_DIM), axis=-1)

    zq = _dot(hb, w_ref[:, _Q0:_K0])
    for j in range(ATTN_WIDTH // LANES):
        y = _head_norm(zq[:, j * LANES:(j + 1) * LANES], gq)
        if rope:
            y = _rope(y, cos_ref[...], sin_ref[...])
        q_ref[:, j * LANES:(j + 1) * LANES] = (y * (HEAD_DIM ** -0.5 * LOG2_E)).astype(BF16)
    run_mixer_share(0)

    zkv = _dot(hb, w_ref[:, _K0:_G0])
    k = _head_norm(zkv[:, 0:KV_WIDTH], gk)
    v = zkv[:, KV_WIDTH:2 * KV_WIDTH]
    if cache:
        for s in range(x.shape[0] // seq_len):
            rows = slice(s * seq_len, (s + 1) * seq_len)
            kc_ref[s] = k[rows, :].T
            vc_ref[s] = v[rows, :].T
    if rope:
        k = _rope(k, cos_ref[...], sin_ref[...])
    kt = k.T
    kd_ref[...] = jnp.concatenate(
        [kt[0:HEAD_DIM], kt[0:HEAD_DIM], kt[HEAD_DIM:], kt[HEAD_DIM:]], axis=0).astype(BF16)
    vd_ref[...] = _dup_heads(v).astype(BF16)
    run_mixer_share(1)

    for j in range(2 * D_MODEL // GATE_CHUNK):
        cols = slice(j * GATE_CHUNK, (j + 1) * GATE_CHUNK)
        zg = _dot(hb, w_ref[:, _G0 + cols.start:_G0 + cols.stop])
        gate_ref[:, cols] = _sigmoid(zg).astype(BF16)
        run_mixer_share(2 + j)


def _proj(x, mod3, g1, w_in_b, gq2, gk2, seq_len, rope_tabs, mod_row, rows_per_mod_row, cache,
          cast_weights):
    t = x.shape[0]
    tm = PROJ_TOKEN_TILE
    assert tm % seq_len == 0, "a projection tile must hold whole sequences for the mixer"
    steps = t // tm
    rope = rope_tabs is not None
    row = lambda i: (i, 0)
    tiles_per_mod_row = None if rows_per_mod_row is None else rows_per_mod_row // tm
    in_specs = [pl.BlockSpec((tm, D_MODEL), row),
                _mod_spec(mod_row),
                _resident((1, D_MODEL)),
                _resident((D_MODEL, IN_WIDTH)),
                _resident((1, HEAD_DIM)),
                _resident((1, HEAD_DIM))]
    args = [x, mod3, g1, w_in_b, gq2, gk2]
    if rope:
        tiles_per_seq = rope_tabs[0].shape[0] // tm
        tab = pl.BlockSpec((tm, LANES), lambda i: (i % tiles_per_seq, 0))
        in_specs += [tab, tab]
        args += list(rope_tabs)
    fold = _fold_dft(seq_len)
    dft = _dft_tables(seq_len, fold)
    in_specs += [_resident(tab.shape) for tab in dft]
    args += list(dft)
    slabs = [pl.BlockSpec((w.shape[0] // steps, w.shape[1]), row) for w in cast_weights]
    in_specs += slabs
    args += list(cast_weights)
    out_specs = [pl.BlockSpec((tm, F_WIDTH), row),
                 pl.BlockSpec((tm, ATTN_WIDTH), row),
                 pl.BlockSpec((2 * KV_WIDTH, tm), lambda i: (0, i)),
                 pl.BlockSpec((tm, 2 * KV_WIDTH), row),
                 pl.BlockSpec((tm, 2 * D_MODEL), row)]
    out_shape = [pltpu.HBM((t, F_WIDTH), BF16),
                 pltpu.HBM((t, ATTN_WIDTH), BF16),
                 pltpu.HBM((2 * KV_WIDTH, t), BF16),
                 pltpu.HBM((t, 2 * KV_WIDTH), BF16),
                 pltpu.HBM((t, 2 * D_MODEL), BF16)]
    if cache:
        seqs = tm // seq_len
        out_specs += [pl.BlockSpec((seqs, KV_WIDTH, seq_len), lambda i: (i, 0, 0))] * 2
        out_shape += [pltpu.HBM((t // seq_len, KV_WIDTH, seq_len), F32)] * 2
    out_specs += slabs
    out_shape += [pltpu.HBM(w.shape, BF16) for w in cast_weights]
    return pl.pallas_call(
        functools.partial(_proj_kernel, rope=rope, seq_len=seq_len, fold=fold, cache=cache,
                          n_cast=len(cast_weights), tiles_per_mod_row=tiles_per_mod_row),
        grid=(steps,),
        in_specs=in_specs,
        out_specs=out_specs,
        out_shape=out_shape,
        scratch_shapes=[pltpu.VMEM((tm, F_WIDTH), BF16),
                        pltpu.VMEM((2, 2 * seq_len, F_WIDTH), BF16)],
        compiler_params=pltpu.CompilerParams(
            dimension_semantics=("arbitrary",), vmem_limit_bytes=RESIDENT_WEIGHTS_VMEM),
        name="proj_lat" if rope else "proj_ctx",
    )(*[_hbm(a) for a in args])


def _dft_tables(n, fold):
    def cs(m):
        idx = np.arange(m, dtype=np.int64)
        ang = 2.0 * np.pi * ((idx[:, None] * idx[None, :]) % m).astype(np.float64) / m
        return np.cos(ang) / np.sqrt(m), np.sin(ang) / np.sqrt(m)
    cd, sd = cs(F_GROUP_DIM)
    cn, sn = cs(n)
    tabs = [np.concatenate([cd, sd], axis=1)]
    if fold:
        half = n // 2
        rev = np.zeros((half, half), np.float64)
        rev[np.arange(1, half), half - np.arange(1, half)] = 1.0
        tabs += [cn[:half], -sn[:half], rev]
    else:
        tabs += [np.concatenate([cn, -sn], axis=1)]
    return tuple(jnp.asarray(t.astype(np.float32)).astype(BF16) for t in tabs)


def _fold_dft(n):
    return n // 2 >= 2 * MXU_DIM


def _fourier_stages(zf_ref, tabs, o_ref, ab_ref, n, fold):
    csd_ref = tabs[0]
    if fold:
        ch_ref, sh_ref, rev_ref = tabs[1:]
    else:
        (csn_ref,) = tabs[1:]
    half = n // 2
    units = []
    for s in range(zf_ref.shape[0] // n):
        buf = s % 2
        r0 = s * n

        def channels(r0=r0, buf=buf):
            for g in range(F_GROUPS):
                cols = slice(g * F_GROUP_DIM, (g + 1) * F_GROUP_DIM)
                ab = _dot(zf_ref[r0:r0 + n, cols], csd_ref[...])
                ab_ref[buf, 0:n, cols] = ab[:, 0:F_GROUP_DIM].astype(BF16)
                ab_ref[buf, n:2 * n, cols] = ab[:, F_GROUP_DIM:].astype(BF16)

        def positions(r0=r0, buf=buf):
            if not fold:
                o_ref[r0:r0 + n, :] = _dot(csn_ref[...], ab_ref[buf]).astype(BF16)
                return
            p = _dot(ch_ref[...], ab_ref[buf, 0:n, :])
            q = _dot(sh_ref[...], ab_ref[buf, n:2 * n, :])
            o_ref[r0:r0 + half, :] = (p + q).astype(BF16)
            upper = _dot(rev_ref[...], (p - q).astype(BF16))
            a = ab_ref[buf, 0:n, :].astype(F32)
            even = lax.broadcasted_iota(jnp.int32, a.shape, 0) % 2 == 0
            mid = jnp.sum(jnp.where(even, a, -a), axis=0, keepdims=True) * (n ** -0.5)
            first = lax.broadcasted_iota(jnp.int32, upper.shape, 0) == 0
            o_ref[r0 + half:r0 + n, :] = jnp.where(first, mid, upper).astype(BF16)

        units.append((channels, positions))
    return _pipelined(units)


def _group_scores(q_stack, keys, s_ref):
    kt = keys[0] if len(keys) == 1 else jnp.concatenate(keys, axis=1)
    top = lax.broadcasted_iota(jnp.int32, kt.shape, 0) < HEAD_DIM
    zero = jnp.zeros_like(kt)
    s_ref[...] = _dot(q_stack, jnp.concatenate(
        [jnp.where(top, kt, zero), jnp.where(top, zero, kt)], axis=1))


def _group_softmax_pv(values, sinks, band, s_ref, p_ref, sk_ref, rows_per_pair, chunk):
    rows = s_ref.shape[0]
    keys = s_ref.shape[1] // 2
    bounds = [0]
    for vd in values:
        bounds.append(bounds[-1] + vd.shape[0])
    lo = lax.broadcasted_iota(jnp.int32, (chunk, LANES), 1) < HEAD_DIM

    for r0 in range(0, rows, chunk):
        rs = slice(r0, r0 + chunk)
        sink_terms = []
        for half in range(2):
            sink = sinks[2 * (r0 // rows_per_pair) + half]
            cols = [slice(half * keys + bounds[i], half * keys + bounds[i + 1])
                    for i in range(len(values))]
            parts = [s_ref[rs, c] for c in cols]
            if band is not None:
                parts[0] = jnp.where(band(r0 % rows_per_pair, chunk), parts[0], NEG_INF)
            m = sink
            for s in parts:
                m = jnp.maximum(m, jnp.max(s, axis=-1, keepdims=True))
            for c, s in zip(cols, parts):
                p_ref[rs, c] = jnp.exp2(s - m).astype(BF16)
            sink_terms.append(jnp.exp2(sink - m))
        sk_ref[rs, :] = jnp.where(lo, sink_terms[0], sink_terms[1])

    vd = values[0] if len(values) == 1 else jnp.concatenate(values, axis=0)
    v_lo = lax.broadcasted_iota(jnp.int32, vd.shape, 1) < HEAD_DIM
    zero = jnp.zeros_like(vd)
    ones_lo = jnp.where(v_lo, 1.0, 0.0).astype(BF16)
    ones_hi = jnp.where(v_lo, 0.0, 1.0).astype(BF16)
    operand = jnp.concatenate(
        [jnp.concatenate([jnp.where(v_lo, vd, zero), ones_lo], axis=1),
         jnp.concatenate([jnp.where(v_lo, zero, vd), ones_hi], axis=1)], axis=0)
    out = _dot(p_ref[...], operand)
    return out[:, 0:LANES] / (out[:, LANES:2 * LANES] + sk_ref[...])


def _pair_cols(g):
    pa, pb = 2 * g, 2 * g + 1
    return slice(pa * LANES, (pa + 1) * LANES), slice(pb * LANES, (pb + 1) * LANES)


def _attn_tail_kernel(*refs, lat, n_tiles, tm, n, tiles_per_mod_row):
    sink_ref, q_ref, kd_ref, vd_ref = refs[:4]
    refs = refs[4:]
    if lat:
        ck_ref, cv_ref = refs[:2]
        refs = refs[2:]
    (x_ref, mod_ref, mix_ref, gate_ref, g2_ref,
     wf_ref, wao_ref, wout_ref, wup_ref, wdn_ref, o_ref, att_ref, s_ref, p_ref, sk_ref) = refs[:15]
    if lat:
        dist_ref, ckt_ref, cvd_ref = refs[15:18]
    act_ref = refs[-1]
    i = pl.program_id(0)

    def attention_stages():
        units = []
        if lat:
            qb = WINDOW
            span = qb + 2 * WINDOW
            t = jnp.minimum(i, n_tiles - 1) % (n // tm)
            ckt = ck_ref[0]
            ckt_ref[...] = jnp.concatenate(
                [ckt[0:HEAD_DIM], ckt[0:HEAD_DIM], ckt[HEAD_DIM:], ckt[HEAD_DIM:]], axis=0).astype(BF16)
            cvd_ref[...] = _dup_heads(cv_ref[0].T).astype(BF16)
            for rr in range(tm // qb):
                q0 = t * tm + rr * qb
                start = pl.multiple_of(jnp.clip(q0 - WINDOW, 0, n - span), WINDOW)
                dist_ref[rr] = jnp.abs((q0 - start) + lax.broadcasted_iota(jnp.int32, (qb, span), 0)
                                       - lax.broadcasted_iota(jnp.int32, (qb, span), 1))
                band = lambda r0, rows, rr=rr: dist_ref[rr, r0:r0 + rows, :] <= WINDOW
                rows = slice(rr * qb, (rr + 1) * qb)
                for g in range(N_KV_HEADS):
                    buf = len(units) % 2
                    cols = slice(g * LANES, (g + 1) * LANES)
                    qa, qbc = _pair_cols(g)

                    def scores(rows=rows, start=start, cols=cols, qa=qa, qbc=qbc, buf=buf):
                        q_stack = jnp.concatenate([q_ref[rows, qa], q_ref[rows, qbc]], axis=0)
                        _group_scores(q_stack, [kd_ref[cols, pl.ds(start, span)], ckt_ref[cols, :]],
                                      s_ref.at[buf])

                    def finish(rows=rows, start=start, g=g, cols=cols, qa=qa, qbc=qbc, buf=buf,
                               band=band):
                        sinks = [sink_ref[GROUP * g + h] * LOG2_E for h in range(GROUP)]
                        o = _group_softmax_pv(
                            [vd_ref[pl.ds(start, span), cols], cvd_ref[:, cols]], sinks, band,
                            s_ref.at[buf], p_ref.at[buf], sk_ref.at[buf], qb, LAT_SOFTMAX_ROWS)
                        att_ref[rows, qa] = o[0:qb].astype(BF16)
                        att_ref[rows, qbc] = o[qb:2 * qb].astype(BF16)

                    units.append((scores, finish))
        else:
            for s in range(tm // n):
                rows = slice(s * n, (s + 1) * n)
                for g in range(N_KV_HEADS):
                    buf = len(units) % 2
                    cols = slice(g * LANES, (g + 1) * LANES)
                    qa, qbc = _pair_cols(g)

                    def scores(rows=rows, cols=cols, qa=qa, qbc=qbc, buf=buf):
                        q_stack = jnp.concatenate([q_ref[rows, qa], q_ref[rows, qbc]], axis=0)
                        _group_scores(q_stack, [kd_ref[cols, rows]], s_ref.at[buf])

                    def finish(rows=rows, g=g, cols=cols, qa=qa, qbc=qbc, buf=buf):
                        sinks = [sink_ref[GROUP * g + h] * LOG2_E for h in range(GROUP)]
                        o = _group_softmax_pv([vd_ref[rows, cols]], sinks, None, s_ref.at[buf],
                                              p_ref.at[buf], sk_ref.at[buf], n, CTX_SOFTMAX_ROWS)
                        att_ref[rows, qa] = o[0:n].astype(BF16)
                        att_ref[rows, qbc] = o[n:2 * n].astype(BF16)

                    units.append((scores, finish))
        return _pipelined(units)

    def tail(stages):
        gt1, sh2, sc2, gt2 = _mod_vectors(mod_ref, i - 1, tiles_per_mod_row)[2:6]
        yf = _dot(mix_ref[...], wf_ref[...])
        ya = _dot(att_ref[...], wao_ref[...])
        merged = (gate_ref[:, 0:D_MODEL].astype(F32) * yf
                  + gate_ref[:, D_MODEL:2 * D_MODEL].astype(F32) * ya)
        x1 = x_ref[...] + gt1 * _dot(merged.astype(BF16), wout_ref[...])

        h = x1 * lax.rsqrt(jnp.mean(x1 * x1, axis=-1, keepdims=True) + EPS) * g2_ref[...]
        hb = (h * (1.0 + sc2) + sh2).astype(BF16)
        d_ff = wdn_ref.shape[0]
        n_chunks = d_ff // FF_CHUNK
        issued = 0
        for c in range(n_chunks):
            cols = slice(c * FF_CHUNK, (c + 1) * FF_CHUNK)
            a = _dot(hb, wup_ref[:, cols])
            u = _dot(hb, wup_ref[:, d_ff + cols.start:d_ff + cols.stop])
            act_ref[:, cols] = (a * _sigmoid(a) * u).astype(BF16)
            while issued < len(stages) and issued * n_chunks < (c + 1) * len(stages):
                stages[issued]()
                issued += 1
        o_ref[...] = x1 + gt2 * _dot(act_ref[...], wdn_ref[...])

    @pl.when(i == 0)
    def _():
        for stage in attention_stages():
            stage()

    @pl.when(jnp.logical_and(i > 0, i < n_tiles))
    def _():
        tail(attention_stages())

    @pl.when(i == n_tiles)
    def _():
        tail([])


def _attn_tail(sinks, q, kd, vd, cache, x, mod3, mix, gate, g2, wf, wao, wout, wup, wdn,
               mod_row, rows_per_mod_row, n):
    t = x.shape[0]
    tm = TAIL_TOKEN_TILE
    n_tiles = t // tm
    lat = cache is not None
    tiles_per_mod_row = None if rows_per_mod_row is None else rows_per_mod_row // tm
    front = lambda i: (jnp.minimum(i, n_tiles - 1), 0)
    back = lambda i: (jnp.maximum(i - 1, 0), 0)
    in_specs = [pl.BlockSpec(memory_space=pltpu.SMEM), pl.BlockSpec((tm, ATTN_WIDTH), front)]
    args = [sinks, q, kd, vd]
    if lat:
        past = cache[0].shape[2]
        seq_of = lambda i: jnp.minimum(i, n_tiles - 1) // (n // tm)
        in_specs += [pl.BlockSpec((2 * KV_WIDTH, n), lambda i: (0, seq_of(i))),
                     pl.BlockSpec((n, 2 * KV_WIDTH), lambda i: (seq_of(i), 0))]
        in_specs += [pl.BlockSpec((1, KV_WIDTH, past), lambda i: (seq_of(i), 0, 0))] * 2
        args += list(cache)
        m_rows, keys = 2 * WINDOW, 3 * WINDOW + past
    else:
        in_specs += [pl.BlockSpec((2 * KV_WIDTH, tm), lambda i: (0, front(i)[0])),
                     pl.BlockSpec((tm, 2 * KV_WIDTH), front)]
        m_rows, keys = 2 * n, n
    in_specs += [pl.BlockSpec((tm, D_MODEL), back),
                 _mod_spec(mod_row),
                 pl.BlockSpec((tm, F_WIDTH), back),
                 pl.BlockSpec((tm, 2 * D_MODEL), back),
                 _resident((1, D_MODEL)),
                 _resident(wf.shape), _resident(wao.shape), _resident(wout.shape),
                 _resident(wup.shape), _resident(wdn.shape)]
    args += [x, mod3, mix, gate, g2, wf, wao, wout, wup, wdn]
    scratch = [pltpu.VMEM((tm, ATTN_WIDTH), BF16),
               pltpu.VMEM((2, m_rows, 2 * keys), F32),
               pltpu.VMEM((2, m_rows, 2 * keys), BF16),
               pltpu.VMEM((2, m_rows, LANES), F32)]
    if lat:
        scratch += [pltpu.VMEM((tm // WINDOW, WINDOW, 3 * WINDOW), jnp.int32),
                    pltpu.VMEM((2 * KV_WIDTH, past), BF16),
                    pltpu.VMEM((past, 2 * KV_WIDTH), BF16)]
    scratch += [pltpu.VMEM((tm, wdn.shape[0]), BF16)]
    return pl.pallas_call(
        functools.partial(_attn_tail_kernel, lat=lat, n_tiles=n_tiles, tm=tm, n=n,
                          tiles_per_mod_row=tiles_per_mod_row),
        grid=(n_tiles + 1,),
        in_specs=in_specs,
        out_specs=pl.BlockSpec((tm, D_MODEL), back),
        out_shape=pltpu.HBM((t, D_MODEL), F32),
        scratch_shapes=scratch,
        compiler_params=pltpu.CompilerParams(
            dimension_semantics=("arbitrary",), vmem_limit_bytes=RESIDENT_WEIGHTS_VMEM),
        name="attn_tail_lat" if lat else "attn_tail_ctx",
    )(sinks, *[_hbm(a) for a in args[1:]])


def _rope_tables(n):
    rows = n // GRID_W
    row = np.repeat(np.arange(rows, dtype=np.float64), GRID_W)
    col = np.tile(np.arange(GRID_W, dtype=np.float64), rows)
    inv_freq = ROPE_THETA ** (-np.arange(0, ROPE_AXIS_DIM, 2, dtype=np.float64) / ROPE_AXIS_DIM)

    def axis_tabs(pos):
        ang = pos[:, None] * inv_freq[None, :]
        cos, sin = np.cos(ang), np.sin(ang)
        return np.concatenate([cos, cos], axis=-1), np.concatenate([-sin, sin], axis=-1)

    cr, sr = axis_tabs(row)
    cc, sc = axis_tabs(col)
    cos = np.tile(np.concatenate([cr, cc], axis=-1), (1, LANES // HEAD_DIM))
    sin = np.tile(np.concatenate([sr, sc], axis=-1), (1, LANES // HEAD_DIM))
    return jnp.asarray(cos.astype(np.float32)), jnp.asarray(sin.astype(np.float32))


@jax.jit
def _layer(xp, xs, ck, cv, c, c_ctx, w_ada, b_ada, g_norm1, g_norm2, w_in, g_q, g_k, sink,
           w_f, w_ao, w_out, w_up, w_down):
    bp, sp, _ = xp.shape
    bs, ss, _ = xs.shape
    past = ck.shape[1]

    mod3, w_in_b = _ada(c, c_ctx[None, :], w_ada, b_ada[None, :], w_in)
    g1, g2 = g_norm1[None, :], g_norm2[None, :]
    gq2, gk2 = g_q[None, :], g_k[None, :]
    ctx_mod = (CTX_MOD_ROW, None)
    lat_mod = (0, ss)

    xp2 = xp.reshape(bp * sp, D_MODEL)
    xs2 = xs.reshape(bs * ss, D_MODEL)
    mix, q, kd, vd, gate, k_new, v_new, wup = _proj(
        xp2, mod3, g1, w_in_b, gq2, gk2, sp, None, *ctx_mod, True, [w_up])
    mix_s, q_s, kd_s, vd_s, gate_s, wf, wao, wout, wdn = _proj(
        xs2, mod3, g1, w_in_b, gq2, gk2, ss, _rope_tables(ss), *lat_mod, False,
        [w_f, w_ao, w_out, w_down])

    yp = _attn_tail(sink, q, kd, vd, None, xp2, mod3, mix, gate, g2,
                    wf, wao, wout, wup, wdn, *ctx_mod, sp)

    def cache_t(t):
        return t.transpose(0, 2, 3, 1).reshape(bs, KV_WIDTH, past)

    ys = _attn_tail(sink, q_s, kd_s, vd_s, (cache_t(ck), cache_t(cv)), xs2, mod3,
                    mix_s, gate_s, g2, wf, wao, wout, wup, wdn, *lat_mod, ss)

    def cache_layout(t):
        return t.reshape(bp, N_KV_HEADS, HEAD_DIM, sp).transpose(0, 3, 1, 2)

    return yp.reshape(xp.shape), ys.reshape(xs.shape), cache_layout(k_new), cache_layout(v_new)


def kernel(x_prompt, x_sample, cache_k, cache_v, c, c_ctx, w_ada, b_ada, g_norm1, g_norm2,
           w_in, g_q, g_k, sinks, w_f, w_ao, w_out, w_up, w_down):
    depth = w_in.shape[0]
    xp, xs = x_prompt, x_sample
    new_k, new_v = [], []
    for l in range(depth):
        xp, xs, k_ctx, v_ctx = _layer(
            xp, xs, cache_k[:, l], cache_v[:, l], c, c_ctx, w_ada[l], b_ada[l], g_norm1[l],
            g_norm2[l], w_in[l], g_q[l], g_k[l], sinks[l], w_f[l], w_ao[l], w_out[l], w_up[l],
            w_down[l])
        new_k.append(k_ctx)
        new_v.append(v_ctx)
    return (xp, xs, jnp.stack(new_k, axis=1), jnp.stack(new_v, axis=1))
```

```python
import functools

import numpy as np
import jax
import jax.numpy as jnp
from jax import lax
from jax.experimental import pallas as pl
from jax.experimental.pallas import tpu as pltpu

D_MODEL = 1024
GRID_W = 64
N_HEADS = 8
N_KV_HEADS = 2
GROUP = N_HEADS // N_KV_HEADS
HEAD_DIM = 64
WINDOW = 128
F_GROUPS = 4
F_GROUP_DIM = 128
F_WIDTH = F_GROUPS * F_GROUP_DIM
ATTN_WIDTH = N_HEADS * HEAD_DIM
KV_WIDTH = N_KV_HEADS * HEAD_DIM
ROPE_THETA = 10000.0
ROPE_AXIS_DIM = HEAD_DIM // 2
EPS = 1e-6
NEG_INF = -1e30
LOG2_E = 1.4426950408889634

LANES = 128
MXU_DIM = 256
VMEM_BYTES_V7X = 64 * 1024 * 1024

_Q0 = F_WIDTH
_K0 = _Q0 + ATTN_WIDTH
_V0 = _K0 + KV_WIDTH
_G0 = _V0 + KV_WIDTH
IN_WIDTH = _G0 + 2 * D_MODEL

SUBLANES = 8
N_MOD = 6
MOD_ROWS = 16
CTX_MOD_ROW = 8
ADA_STEPS = 8

PROJ_TOKEN_TILE = 1024
TAIL_TOKEN_TILE = 512
FF_CHUNK = MXU_DIM
GATE_CHUNK = 4 * MXU_DIM
CTX_SOFTMAX_ROWS = 128
LAT_SOFTMAX_ROWS = 64

F32 = jnp.float32
BF16 = jnp.bfloat16


RESIDENT_WEIGHTS_VMEM = VMEM_BYTES_V7X - 6 * 2 ** 20


def _dot(a, b):
    return jnp.dot(a, b, preferred_element_type=F32)


def _sigmoid(x):
    return 1.0 / (1.0 + jnp.exp(-x))


def _resident(shape):
    zeros = (0,) * len(shape)
    return pl.BlockSpec(shape, lambda *_: zeros, pipeline_mode=pl.Buffered(1))


def _hbm(x):
    try:
        return pltpu.with_memory_space_constraint(x, pltpu.HBM)
    except ValueError:
        return x


def _pipelined(units):
    stages = [units[0][0]]
    for k, (_, second) in enumerate(units):
        if k + 1 < len(units):
            stages.append(units[k + 1][0])
        stages.append(second)
    return stages


def _ada_kernel(c_ref, cctx_ref, w_ref, b_ref, win_ref, o_ref, winb_ref):
    lat_rows = c_ref.shape[0]
    pieces = [c_ref[...]]
    if lat_rows < CTX_MOD_ROW:
        pieces.append(jnp.zeros((CTX_MOD_ROW - lat_rows, D_MODEL), F32))
    pieces.append(jnp.broadcast_to(cctx_ref[...], (MOD_ROWS - CTX_MOD_ROW, D_MODEL)))
    c = jnp.concatenate(pieces, axis=0)
    s = c * _sigmoid(c)
    o_ref[...] = _dot(s.astype(BF16), w_ref[...].astype(BF16)) + b_ref[...]
    winb_ref[...] = win_ref[...].astype(BF16)


def _ada(c, c_ctx, w_ada, b_ada, w_in):
    assert c.shape[0] <= CTX_MOD_ROW, "latent conditioning rows must fit below the context row"
    n = w_ada.shape[1]
    bn = n // ADA_STEPS
    slab = pl.BlockSpec((w_in.shape[0] // ADA_STEPS, w_in.shape[1]), lambda j: (j, 0))
    return pl.pallas_call(
        _ada_kernel,
        grid=(ADA_STEPS,),
        in_specs=[pl.BlockSpec(c.shape, lambda j: (0, 0)),
                  pl.BlockSpec((1, D_MODEL), lambda j: (0, 0)),
                  pl.BlockSpec((D_MODEL, bn), lambda j: (0, j)),
                  pl.BlockSpec((1, bn), lambda j: (0, j)),
                  slab],
        out_specs=[pl.BlockSpec((MOD_ROWS, bn), lambda j: (0, j)), slab],
        out_shape=[pltpu.HBM((MOD_ROWS, n), F32), pltpu.HBM(w_in.shape, BF16)],
        name="ada",
    )(c, c_ctx, w_ada, b_ada, w_in)


def _mod_spec(mod_row):
    return pl.BlockSpec((SUBLANES, N_MOD * D_MODEL), lambda i: (mod_row // SUBLANES, 0))


def _mod_vectors(mod_ref, tile, tiles_per_row):
    cols = [slice(j * D_MODEL, (j + 1) * D_MODEL) for j in range(N_MOD)]
    if tiles_per_row is None:
        return [mod_ref[0:1, c] for c in cols]
    r = tile // tiles_per_row
    return [mod_ref[pl.ds(r, 1), c] for c in cols]


def _head_norm(z, g):
    lo = lax.broadcasted_iota(jnp.int32, z.shape, 1) < HEAD_DIM
    s = z * z
    s_lo = jnp.sum(jnp.where(lo, s, 0.0), axis=-1, keepdims=True)
    s_hi = jnp.sum(jnp.where(lo, 0.0, s), axis=-1, keepdims=True)
    ms = jnp.where(lo, s_lo, s_hi) * (1.0 / HEAD_DIM)
    return z * lax.rsqrt(ms + EPS) * g


def _rope(y, cos, sin):
    half = ROPE_AXIS_DIM // 2
    lane = lax.broadcasted_iota(jnp.int32, y.shape, 1)
    first = (lane % ROPE_AXIS_DIM) < half
    partner = jnp.where(first, pltpu.roll(y, LANES - half, 1), pltpu.roll(y, half, 1))
    return y * cos + partner * sin


def _dup_heads(y):
    lo = lax.broadcasted_iota(jnp.int32, y.shape, 1) < HEAD_DIM
    sw = pltpu.roll(y, HEAD_DIM, 1)
    return jnp.concatenate([jnp.where(lo, y, sw), jnp.where(lo, sw, y)], axis=-1)


def _proj_kernel(*refs, rope, seq_len, fold, cache, n_cast, tiles_per_mod_row):
    x_ref, mod_ref, g1_ref, w_ref, gq_ref, gk_ref = refs[:6]
    refs = refs[6:]
    if rope:
        cos_ref, sin_ref = refs[:2]
        refs = refs[2:]
    n_tabs = 4 if fold else 2
    tabs, refs = refs[:n_tabs], refs[n_tabs:]
    cast_in, refs = refs[:n_cast], refs[n_cast:]
    mix_ref, q_ref, kd_ref, vd_ref, gate_ref = refs[:5]
    refs = refs[5:]
    if cache:
        kc_ref, vc_ref = refs[:2]
        refs = refs[2:]
    cast_out, (zf_ref, ab_ref) = refs[:n_cast], refs[n_cast:]
    for src_ref, dst_ref in zip(cast_in, cast_out):
        dst_ref[...] = src_ref[...].astype(BF16)

    x = x_ref[...]
    sh1, sc1 = _mod_vectors(mod_ref, pl.program_id(0), tiles_per_mod_row)[0:2]
    h = x * lax.rsqrt(jnp.mean(x * x, axis=-1, keepdims=True) + EPS) * g1_ref[...]
    hb = (h * (1.0 + sc1) + sh1).astype(BF16)

    zf_ref[...] = _dot(hb, w_ref[:, 0:_Q0]).astype(BF16)
    mixer = _fourier_stages(zf_ref, tabs, mix_ref, ab_ref, seq_len, fold)
    n_sections = 2 + 2 * D_MODEL // GATE_CHUNK
    issued = [0]

    def run_mixer_share(section):
        while issued[0] < len(mixer) and issued[0] * n_sections < (section + 1) * len(mixer):
            mixer[issued[0]]()
            issued[0] += 1

    gq = jnp.concatenate([gq_ref[...]] * (LANES // HEAD_DIM), axis=-1)
    gk = jnp.concatenate([gk_ref[...]] * (LANES // HEAD_DIM), axis=-1)

    zq = _dot(hb, w_ref[:, _Q0:_K0])
    for j in range(ATTN_WIDTH // LANES):
        y = _head_norm(zq[:, j * LANES:(j + 1) * LANES], gq)
        if rope:
            y = _rope(y, cos_ref[...], sin_ref[...])
        q_ref[:, j * LANES:(j + 1) * LANES] = (y * (HEAD_DIM ** -0.5 * LOG2_E)).astype(BF16)
    run_mixer_share(0)

    zkv = _dot(hb, w_ref[:, _K0:_G0])
    k = _head_norm(zkv[:, 0:KV_WIDTH], gk)
    v = zkv[:, KV_WIDTH:2 * KV_WIDTH]
    if cache:
        for s in range(x.shape[0] // seq_len):
            rows = slice(s * seq_len, (s + 1) * seq_len)
            kc_ref[s] = k[rows, :].T
            vc_ref[s] = v[rows, :].T
    if rope:
        k = _rope(k, cos_ref[...], sin_ref[...])
    kt = k.T
    kd_ref[...] = jnp.concatenate(
        [kt[0:HEAD_DIM], kt[0:HEAD_DIM], kt[HEAD_DIM:], kt[HEAD_DIM:]], axis=0).astype(BF16)
    vd_ref[...] = _dup_heads(v).astype(BF16)
    run_mixer_share(1)

    for j in range(2 * D_MODEL // GATE_CHUNK):
        cols = slice(j * GATE_CHUNK, (j + 1) * GATE_CHUNK)
        zg = _dot(hb, w_ref[:, _G0 + cols.start:_G0 + cols.stop])
        gate_ref[:, cols] = _sigmoid(zg).astype(BF16)
        run_mixer_share(2 + j)


def _proj(x, mod3, g1, w_in_b, gq2, gk2, seq_len, rope_tabs, mod_row, rows_per_mod_row, cache,
          cast_weights):
    t = x.shape[0]
    tm = PROJ_TOKEN_TILE
    assert tm % seq_len == 0, "a projection tile must hold whole sequences for the mixer"
    steps = t // tm
    rope = rope_tabs is not None
    row = lambda i: (i, 0)
    tiles_per_mod_row = None if rows_per_mod_row is None else rows_per_mod_row // tm
    in_specs = [pl.BlockSpec((tm, D_MODEL), row),
                _mod_spec(mod_row),
                _resident((1, D_MODEL)),
                _resident((D_MODEL, IN_WIDTH)),
                _resident((1, HEAD_DIM)),
                _resident((1, HEAD_DIM))]
    args = [x, mod3, g1, w_in_b, gq2, gk2]
    if rope:
        tiles_per_seq = rope_tabs[0].shape[0] // tm
        tab = pl.BlockSpec((tm, LANES), lambda i: (i % tiles_per_seq, 0))
        in_specs += [tab, tab]
        args += list(rope_tabs)
    fold = _fold_dft(seq_len)
    dft = _dft_tables(seq_len, fold)
    in_specs += [_resident(tab.shape) for tab in dft]
    args += list(dft)
    slabs = [pl.BlockSpec((w.shape[0] // steps, w.shape[1]), row) for w in cast_weights]
    in_specs += slabs
    args += list(cast_weights)
    out_specs = [pl.BlockSpec((tm, F_WIDTH), row),
                 pl.BlockSpec((tm, ATTN_WIDTH), row),
                 pl.BlockSpec((2 * KV_WIDTH, tm), lambda i: (0, i)),
                 pl.BlockSpec((tm, 2 * KV_WIDTH), row),
                 pl.BlockSpec((tm, 2 * D_MODEL), row)]
    out_shape = [pltpu.HBM((t, F_WIDTH), BF16),
                 pltpu.HBM((t, ATTN_WIDTH), BF16),
                 pltpu.HBM((2 * KV_WIDTH, t), BF16),
                 pltpu.HBM((t, 2 * KV_WIDTH), BF16),
                 pltpu.HBM((t, 2 * D_MODEL), BF16)]
    if cache:
        seqs = tm // seq_len
        out_specs += [pl.BlockSpec((seqs, KV_WIDTH, seq_len), lambda i: (i, 0, 0))] * 2
        out_shape += [pltpu.HBM((t // seq_len, KV_WIDTH, seq_len), F32)] * 2
    out_specs += slabs
    out_shape += [pltpu.HBM(w.shape, BF16) for w in cast_weights]
    return pl.pallas_call(
        functools.partial(_proj_kernel, rope=rope, seq_len=seq_len, fold=fold, cache=cache,
                          n_cast=len(cast_weights), tiles_per_mod_row=tiles_per_mod_row),
        grid=(steps,),
        in_specs=in_specs,
        out_specs=out_specs,
        out_shape=out_shape,
        scratch_shapes=[pltpu.VMEM((tm, F_WIDTH), BF16),
                        pltpu.VMEM((2, 2 * seq_len, F_WIDTH), BF16)],
        compiler_params=pltpu.CompilerParams(
            dimension_semantics=("arbitrary",), vmem_limit_bytes=RESIDENT_WEIGHTS_VMEM),
        name="proj_lat" if rope else "proj_ctx",
    )(*[_hbm(a) for a in args])


def _dft_tables(n, fold):
    def cs(m):
        idx = np.arange(m, dtype=np.int64)
        ang = 2.0 * np.pi * ((idx[:, None] * idx[None, :]) % m).astype(np.float64) / m
        return np.cos(ang) / np.sqrt(m), np.sin(ang) / np.sqrt(m)
    cd, sd = cs(F_GROUP_DIM)
    cn, sn = cs(n)
    tabs = [np.concatenate([cd, sd], axis=1)]
    if fold:
        half = n // 2
        rev = np.zeros((half, half), np.float64)
        rev[np.arange(1, half), half - np.arange(1, half)] = 1.0
        tabs += [cn[:half], -sn[:half], rev]
    else:
        tabs += [np.concatenate([cn, -sn], axis=1)]
    return tuple(jnp.asarray(t.astype(np.float32)).astype(BF16) for t in tabs)


def _fold_dft(n):
    return n // 2 >= 2 * MXU_DIM


def _fourier_stages(zf_ref, tabs, o_ref, ab_ref, n, fold):
    csd_ref = tabs[0]
    if fold:
        ch_ref, sh_ref, rev_ref = tabs[1:]
    else:
        (csn_ref,) = tabs[1:]
    half = n // 2
    units = []
    for s in range(zf_ref.shape[0] // n):
        buf = s % 2
        r0 = s * n

        def channels(r0=r0, buf=buf):
            for g in range(F_GROUPS):
                cols = slice(g * F_GROUP_DIM, (g + 1) * F_GROUP_DIM)
                ab = _dot(zf_ref[r0:r0 + n, cols], csd_ref[...])
                ab_ref[buf, 0:n, cols] = ab[:, 0:F_GROUP_DIM].astype(BF16)
                ab_ref[buf, n:2 * n, cols] = ab[:, F_GROUP_DIM:].astype(BF16)

        def positions(r0=r0, buf=buf):
            if not fold:
                o_ref[r0:r0 + n, :] = _dot(csn_ref[...], ab_ref[buf]).astype(BF16)
                return
            p = _dot(ch_ref[...], ab_ref[buf, 0:n, :])
            q = _dot(sh_ref[...], ab_ref[buf, n:2 * n, :])
            o_ref[r0:r0 + half, :] = (p + q).astype(BF16)
            upper = _dot(rev_ref[...], (p - q).astype(BF16))
            a = ab_ref[buf, 0:n, :].astype(F32)
            even = lax.broadcasted_iota(jnp.int32, a.shape, 0) % 2 == 0
            mid = jnp.sum(jnp.where(even, a, -a), axis=0, keepdims=True) * (n ** -0.5)
            first = lax.broadcasted_iota(jnp.int32, upper.shape, 0) == 0
            o_ref[r0 + half:r0 + n, :] = jnp.where(first, mid, upper).astype(BF16)

        units.append((channels, positions))
    return _pipelined(units)


def _group_scores(q_stack, keys, s_ref):
    kt = keys[0] if len(keys) == 1 else jnp.concatenate(keys, axis=1)
    top = lax.broadcasted_iota(jnp.int32, kt.shape, 0) < HEAD_DIM
    zero = jnp.zeros_like(kt)
    s_ref[...] = _dot(q_stack, jnp.concatenate(
        [jnp.where(top, kt, zero), jnp.where(top, zero, kt)], axis=1))


def _group_softmax_pv(values, sinks, band, s_ref, p_ref, sk_ref, rows_per_pair, chunk):
    rows = s_ref.shape[0]
    keys = s_ref.shape[1] // 2
    bounds = [0]
    for vd in values:
        bounds.append(bounds[-1] + vd.shape[0])
    lo = lax.broadcasted_iota(jnp.int32, (chunk, LANES), 1) < HEAD_DIM

    for r0 in range(0, rows, chunk):
        rs = slice(r0, r0 + chunk)
        sink_terms = []
        for half in range(2):
            sink = sinks[2 * (r0 // rows_per_pair) + half]
            cols = [slice(half * keys + bounds[i], half * keys + bounds[i + 1])
                    for i in range(len(values))]
            parts = [s_ref[rs, c] for c in cols]
            if band is not None:
                parts[0] = jnp.where(band(r0 % rows_per_pair, chunk), parts[0], NEG_INF)
            m = sink
            for s in parts:
                m = jnp.maximum(m, jnp.max(s, axis=-1, keepdims=True))
            for c, s in zip(cols, parts):
                p_ref[rs, c] = jnp.exp2(s - m).astype(BF16)
            sink_terms.append(jnp.exp2(sink - m))
        sk_ref[rs, :] = jnp.where(lo, sink_terms[0], sink_terms[1])

    vd = values[0] if len(values) == 1 else jnp.concatenate(values, axis=0)
    v_lo = lax.broadcasted_iota(jnp.int32, vd.shape, 1) < HEAD_DIM
    zero = jnp.zeros_like(vd)
    ones_lo = jnp.where(v_lo, 1.0, 0.0).astype(BF16)
    ones_hi = jnp.where(v_lo, 0.0, 1.0).astype(BF16)
    operand = jnp.concatenate(
        [jnp.concatenate([jnp.where(v_lo, vd, zero), ones_lo], axis=1),
         jnp.concatenate([jnp.where(v_lo, zero, vd), ones_hi], axis=1)], axis=0)
    out = _dot(p_ref[...], operand)
    return out[:, 0:LANES] / (out[:, LANES:2 * LANES] + sk_ref[...])


def _pair_cols(g):
    pa, pb = 2 * g, 2 * g + 1
    return slice(pa * LANES, (pa + 1) * LANES), slice(pb * LANES, (pb + 1) * LANES)


def _attn_tail_kernel(*refs, lat, n_tiles, tm, n, tiles_per_mod_row):
    sink_ref, q_ref, kd_ref, vd_ref = refs[:4]
    refs = refs[4:]
    if lat:
        ck_ref, cv_ref = refs[:2]
        refs = refs[2:]
    (x_ref, mod_ref, mix_ref, gate_ref, g2_ref,
     wf_ref, wao_ref, wout_ref, wup_ref, wdn_ref, o_ref, att_ref, s_ref, p_ref, sk_ref) = refs[:15]
    if lat:
        dist_ref, ckt_ref, cvd_ref = refs[15:18]
    act_ref = refs[-1]
    i = pl.program_id(0)

    def attention_stages():
        units = []
        if lat:
            qb = WINDOW
            span = qb + 2 * WINDOW
            t = jnp.minimum(i, n_tiles - 1) % (n // tm)
            ckt = ck_ref[0]
            ckt_ref[...] = jnp.concatenate(
                [ckt[0:HEAD_DIM], ckt[0:HEAD_DIM], ckt[HEAD_DIM:], ckt[HEAD_DIM:]], axis=0).astype(BF16)
            cvd_ref[...] = _dup_heads(cv_ref[0].T).astype(BF16)
            for rr in range(tm // qb):
                q0 = t * tm + rr * qb
                start = pl.multiple_of(jnp.clip(q0 - WINDOW, 0, n - span), WINDOW)
                dist_ref[rr] = jnp.abs((q0 - start) + lax.broadcasted_iota(jnp.int32, (qb, span), 0)
                                       - lax.broadcasted_iota(jnp.int32, (qb, span), 1))
                band = lambda r0, rows, rr=rr: dist_ref[rr, r0:r0 + rows, :] <= WINDOW
                rows = slice(rr * qb, (rr + 1) * qb)
                for g in range(N_KV_HEADS):
                    buf = len(units) % 2
                    cols = slice(g * LANES, (g + 1) * LANES)
                    qa, qbc = _pair_cols(g)

                    def scores(rows=rows, start=start, cols=cols, qa=qa, qbc=qbc, buf=buf):
                        q_stack = jnp.concatenate([q_ref[rows, qa], q_ref[rows, qbc]], axis=0)
                        _group_scores(q_stack, [kd_ref[cols, pl.ds(start, span)], ckt_ref[cols, :]],
                                      s_ref.at[buf])

                    def finish(rows=rows, start=start, g=g, cols=cols, qa=qa, qbc=qbc, buf=buf,
                               band=band):
                        sinks = [sink_ref[GROUP * g + h] * LOG2_E for h in range(GROUP)]
                        o = _group_softmax_pv(
                            [vd_ref[pl.ds(start, span), cols], cvd_ref[:, cols]], sinks, band,
                            s_ref.at[buf], p_ref.at[buf], sk_ref.at[buf], qb, LAT_SOFTMAX_ROWS)
                        att_ref[rows, qa] = o[0:qb].astype(BF16)
                        att_ref[rows, qbc] = o[qb:2 * qb].astype(BF16)

                    units.append((scores, finish))
        else:
            for s in range(tm // n):
                rows = slice(s * n, (s + 1) * n)
                for g in range(N_KV_HEADS):
                    buf = len(units) % 2
                    cols = slice(g * LANES, (g + 1) * LANES)
                    qa, qbc = _pair_cols(g)

                    def scores(rows=rows, cols=cols, qa=qa, qbc=qbc, buf=buf):
                        q_stack = jnp.concatenate([q_ref[rows, qa], q_ref[rows, qbc]], axis=0)
                        _group_scores(q_stack, [kd_ref[cols, rows]], s_ref.at[buf])

                    def finish(rows=rows, g=g, cols=cols, qa=qa, qbc=qbc, buf=buf):
                        sinks = [sink_ref[GROUP * g + h] * LOG2_E for h in range(GROUP)]
                        o = _group_softmax_pv([vd_ref[rows, cols]], sinks, None, s_ref.at[buf],
                                              p_ref.at[buf], sk_ref.at[buf], n, CTX_SOFTMAX_ROWS)
                        att_ref[rows, qa] = o[0:n].astype(BF16)
                        att_ref[rows, qbc] = o[n:2 * n].astype(BF16)

                    units.append((scores, finish))
        return _pipelined(units)

    def tail(stages):
        gt1, sh2, sc2, gt2 = _mod_vectors(mod_ref, i - 1, tiles_per_mod_row)[2:6]
        yf = _dot(mix_ref[...], wf_ref[...])
        ya = _dot(att_ref[...], wao_ref[...])
        merged = (gate_ref[:, 0:D_MODEL].astype(F32) * yf
                  + gate_ref[:, D_MODEL:2 * D_MODEL].astype(F32) * ya)
        x1 = x_ref[...] + gt1 * _dot(merged.astype(BF16), wout_ref[...])

        h = x1 * lax.rsqrt(jnp.mean(x1 * x1, axis=-1, keepdims=True) + EPS) * g2_ref[...]
        hb = (h * (1.0 + sc2) + sh2).astype(BF16)
        d_ff = wdn_ref.shape[0]
        n_chunks = d_ff // FF_CHUNK
        issued = 0
        for c in range(n_chunks):
            cols = slice(c * FF_CHUNK, (c + 1) * FF_CHUNK)
            a = _dot(hb, wup_ref[:, cols])
            u = _dot(hb, wup_ref[:, d_ff + cols.start:d_ff + cols.stop])
            act_ref[:, cols] = (a * _sigmoid(a) * u).astype(BF16)
            while issued < len(stages) and issued * n_chunks < (c + 1) * len(stages):
                stages[issued]()
                issued += 1
        o_ref[...] = x1 + gt2 * _dot(act_ref[...], wdn_ref[...])

    @pl.when(i == 0)
    def _():
        for stage in attention_stages():
            stage()

    @pl.when(jnp.logical_and(i > 0, i < n_tiles))
    def _():
        tail(attention_stages())

    @pl.when(i == n_tiles)
    def _():
        tail([])


def _attn_tail(sinks, q, kd, vd, cache, x, mod3, mix, gate, g2, wf, wao, wout, wup, wdn,
               mod_row, rows_per_mod_row, n):
    t = x.shape[0]
    tm = TAIL_TOKEN_TILE
    n_tiles = t // tm
    lat = cache is not None
    tiles_per_mod_row = None if rows_per_mod_row is None else rows_per_mod_row // tm
    front = lambda i: (jnp.minimum(i, n_tiles - 1), 0)
    back = lambda i: (jnp.maximum(i - 1, 0), 0)
    in_specs = [pl.BlockSpec(memory_space=pltpu.SMEM), pl.BlockSpec((tm, ATTN_WIDTH), front)]
    args = [sinks, q, kd, vd]
    if lat:
        past = cache[0].shape[2]
        seq_of = lambda i: jnp.minimum(i, n_tiles - 1) // (n // tm)
        in_specs += [pl.BlockSpec((2 * KV_WIDTH, n), lambda i: (0, seq_of(i))),
                     pl.BlockSpec((n, 2 * KV_WIDTH), lambda i: (seq_of(i), 0))]
        in_specs += [pl.BlockSpec((1, KV_WIDTH, past), lambda i: (seq_of(i), 0, 0))] * 2
        args += list(cache)
        m_rows, keys = 2 * WINDOW, 3 * WINDOW + past
    else:
        in_specs += [pl.BlockSpec((2 * KV_WIDTH, tm), lambda i: (0, front(i)[0])),
                     pl.BlockSpec((tm, 2 * KV_WIDTH), front)]
        m_rows, keys = 2 * n, n
    in_specs += [pl.BlockSpec((tm, D_MODEL), back),
                 _mod_spec(mod_row),
                 pl.BlockSpec((tm, F_WIDTH), back),
                 pl.BlockSpec((tm, 2 * D_MODEL), back),
                 _resident((1, D_MODEL)),
                 _resident(wf.shape), _resident(wao.shape), _resident(wout.shape),
                 _resident(wup.shape), _resident(wdn.shape)]
    args += [x, mod3, mix, gate, g2, wf, wao, wout, wup, wdn]
    scratch = [pltpu.VMEM((tm, ATTN_WIDTH), BF16),
               pltpu.VMEM((2, m_rows, 2 * keys), F32),
               pltpu.VMEM((2, m_rows, 2 * keys), BF16),
               pltpu.VMEM((2, m_rows, LANES), F32)]
    if lat:
        scratch += [pltpu.VMEM((tm // WINDOW, WINDOW, 3 * WINDOW), jnp.int32),
                    pltpu.VMEM((2 * KV_WIDTH, past), BF16),
                    pltpu.VMEM((past, 2 * KV_WIDTH), BF16)]
    scratch += [pltpu.VMEM((tm, wdn.shape[0]), BF16)]
    return pl.pallas_call(
        functools.partial(_attn_tail_kernel, lat=lat, n_tiles=n_tiles, tm=tm, n=n,
                          tiles_per_mod_row=tiles_per_mod_row),
        grid=(n_tiles + 1,),
        in_specs=in_specs,
        out_specs=pl.BlockSpec((tm, D_MODEL), back),
        out_shape=pltpu.HBM((t, D_MODEL), F32),
        scratch_shapes=scratch,
        compiler_params=pltpu.CompilerParams(
            dimension_semantics=("arbitrary",), vmem_limit_bytes=RESIDENT_WEIGHTS_VMEM),
        name="attn_tail_lat" if lat else "attn_tail_ctx",
    )(sinks, *[_hbm(a) for a in args[1:]])


def _rope_tables(n):
    rows = n // GRID_W
    row = np.repeat(np.arange(rows, dtype=np.float64), GRID_W)
    col = np.tile(np.arange(GRID_W, dtype=np.float64), rows)
    inv_freq = ROPE_THETA ** (-np.arange(0, ROPE_AXIS_DIM, 2, dtype=np.float64) / ROPE_AXIS_DIM)

    def axis_tabs(pos):
        ang = pos[:, None] * inv_freq[None, :]
        cos, sin = np.cos(ang), np.sin(ang)
        return np.concatenate([cos, cos], axis=-1), np.concatenate([-sin, sin], axis=-1)

    cr, sr = axis_tabs(row)
    cc, sc = axis_tabs(col)
    cos = np.tile(np.concatenate([cr, cc], axis=-1), (1, LANES // HEAD_DIM))
    sin = np.tile(np.concatenate([sr, sc], axis=-1), (1, LANES // HEAD_DIM))
    return jnp.asarray(cos.astype(np.float32)), jnp.asarray(sin.astype(np.float32))


@jax.jit
def _layer(xp, xs, ck, cv, c, c_ctx, w_ada, b_ada, g_norm1, g_norm2, w_in, g_q, g_k, sink,
           w_f, w_ao, w_out, w_up, w_down):
    bp, sp, _ = xp.shape
    bs, ss, _ = xs.shape
    past = ck.shape[1]

    mod3, w_in_b = _ada(c, c_ctx[None, :], w_ada, b_ada[None, :], w_in)
    g1, g2 = g_norm1[None, :], g_norm2[None, :]
    gq2, gk2 = g_q[None, :], g_k[None, :]
    ctx_mod = (CTX_MOD_ROW, None)
    lat_mod = (0, ss)

    xp2 = xp.reshape(bp * sp, D_MODEL)
    xs2 = xs.reshape(bs * ss, D_MODEL)
    mix, q, kd, vd, gate, k_new, v_new, wup = _proj(
        xp2, mod3, g1, w_in_b, gq2, gk2, sp, None, *ctx_mod, True, [w_up])
    mix_s, q_s, kd_s, vd_s, gate_s, wf, wao, wout, wdn = _proj(
        xs2, mod3, g1, w_in_b, gq2, gk2, ss, _rope_tables(ss), *lat_mod, False,
        [w_f, w_ao, w_out, w_down])

    yp = _attn_tail(sink, q, kd, vd, None, xp2, mod3, mix, gate, g2,
                    wf, wao, wout, wup, wdn, *ctx_mod, sp)

    def cache_t(t):
        return t.transpose(0, 2, 3, 1).reshape(bs, KV_WIDTH, past)

    ys = _attn_tail(sink, q_s, kd_s, vd_s, (cache_t(ck), cache_t(cv)), xs2, mod3,
                    mix_s, gate_s, g2, wf, wao, wout, wup, wdn, *lat_mod, ss)

    def cache_layout(t):
        return t.reshape(bp, N_KV_HEADS, HEAD_DIM, sp).transpose(0, 3, 1, 2)

    return yp.reshape(xp.shape), ys.reshape(xs.shape), cache_layout(k_new), cache_layout(v_new)


def kernel(x_prompt, x_sample, cache_k, cache_v, c, c_ctx, w_ada, b_ada, g_norm1, g_norm2,
           w_in, g_q, g_k, sinks, w_f, w_ao, w_out, w_up, w_down):
    depth = w_in.shape[0]
    xp, xs = x_prompt, x_sample
    new_k, new_v = [], []
    for l in range(depth):
        xp, xs, k_ctx, v_ctx = _layer(
            xp, xs, cache_k[:, l], cache_v[:, l], c, c_ctx, w_ada[l], b_ada[l], g_norm1[l],
            g_norm2[l], w_in[l], g_q[l], g_k[l], sinks[l], w_f[l], w_ao[l], w_out[l], w_up[l],
            w_down[l])
        new_k.append(k_ctx)
        new_v.append(v_ctx)
    return (xp, xs, jnp.stack(new_k, axis=1), jnp.stack(new_v, axis=1))
```

```python
import functools

import numpy as np
import jax
import jax.numpy as jnp
from jax import lax
from jax.experimental import pallas as pl
from jax.experimental.pallas import tpu as pltpu

D_MODEL = 1024
GRID_W = 64
N_HEADS = 8
N_KV_HEADS = 2
GROUP = N_HEADS // N_KV_HEADS
HEAD_DIM = 64
WINDOW = 128
F_GROUPS = 4
F_GROUP_DIM = 128
F_WIDTH = F_GROUPS * F_GROUP_DIM
ATTN_WIDTH = N_HEADS * HEAD_DIM
KV_WIDTH = N_KV_HEADS * HEAD_DIM
ROPE_THETA = 10000.0
ROPE_AXIS_DIM = HEAD_DIM // 2
EPS = 1e-6
NEG_INF = -1e30
LOG2_E = 1.4426950408889634

LANES = 128
MXU_DIM = 256
VMEM_BYTES_V7X = 64 * 1024 * 1024

_Q0 = F_WIDTH
_K0 = _Q0 + ATTN_WIDTH
_V0 = _K0 + KV_WIDTH
_G0 = _V0 + KV_WIDTH
IN_WIDTH = _G0 + 2 * D_MODEL

SUBLANES = 8
N_MOD = 6
MOD_ROWS = 16
CTX_MOD_ROW = 8
ADA_STEPS = 8

PROJ_TOKEN_TILE = 1024
TAIL_TOKEN_TILE = 512
FF_CHUNK = MXU_DIM
GATE_COLUMNS = 2 * D_MODEL
CTX_SOFTMAX_ROWS = 128
LAT_SOFTMAX_ROWS = 64

F32 = jnp.float32
BF16 = jnp.bfloat16


RESIDENT_WEIGHTS_VMEM = VMEM_BYTES_V7X - 6 * 2 ** 20


def _dot(a, b):
    return jnp.dot(a, b, preferred_element_type=F32)


def _sigmoid(x):
    return 1.0 / (1.0 + jnp.exp(-x))


def _resident(shape):
    zeros = (0,) * len(shape)
    return pl.BlockSpec(shape, lambda *_: zeros, pipeline_mode=pl.Buffered(1))


def _hbm(x):
    try:
        return pltpu.with_memory_space_constraint(x, pltpu.HBM)
    except ValueError:
        return x


def _pipelined(units):
    stages = [units[0][0]]
    for k, (_, second) in enumerate(units):
        if k + 1 < len(units):
            stages.append(units[k + 1][0])
        stages.append(second)
    return stages


def _ada_kernel(c_ref, cctx_ref, w_ref, b_ref, win_ref, o_ref, winb_ref):
    lat_rows = c_ref.shape[0]
    pieces = [c_ref[...]]
    if lat_rows < CTX_MOD_ROW:
        pieces.append(jnp.zeros((CTX_MOD_ROW - lat_rows, D_MODEL), F32))
    pieces.append(jnp.broadcast_to(cctx_ref[...], (MOD_ROWS - CTX_MOD_ROW, D_MODEL)))
    c = jnp.concatenate(pieces, axis=0)
    s = c * _sigmoid(c)
    o_ref[...] = _dot(s.astype(BF16), w_ref[...].astype(BF16)) + b_ref[...]
    winb_ref[...] = win_ref[...].astype(BF16)


def _ada(c, c_ctx, w_ada, b_ada, w_in):
    assert c.shape[0] <= CTX_MOD_ROW, "latent conditioning rows must fit below the context row"
    n = w_ada.shape[1]
    bn = n // ADA_STEPS
    slab = pl.BlockSpec((w_in.shape[0] // ADA_STEPS, w_in.shape[1]), lambda j: (j, 0))
    return pl.pallas_call(
        _ada_kernel,
        grid=(ADA_STEPS,),
        in_specs=[pl.BlockSpec(c.shape, lambda j: (0, 0)),
                  pl.BlockSpec((1, D_MODEL), lambda j: (0, 0)),
                  pl.BlockSpec((D_MODEL, bn), lambda j: (0, j)),
                  pl.BlockSpec((1, bn), lambda j: (0, j)),
                  slab],
        out_specs=[pl.BlockSpec((MOD_ROWS, bn), lambda j: (0, j)), slab],
        out_shape=[pltpu.HBM((MOD_ROWS, n), F32), pltpu.HBM(w_in.shape, BF16)],
        name="ada",
    )(c, c_ctx, w_ada, b_ada, w_in)


def _mod_spec(mod_row):
    return pl.BlockSpec((SUBLANES, N_MOD * D_MODEL), lambda i: (mod_row // SUBLANES, 0))


def _mod_vectors(mod_ref, tile, tiles_per_row):
    cols = [slice(j * D_MODEL, (j + 1) * D_MODEL) for j in range(N_MOD)]
    if tiles_per_row is None:
        return [mod_ref[0:1, c] for c in cols]
    r = tile // tiles_per_row
    return [mod_ref[pl.ds(r, 1), c] for c in cols]


def _head_norm(z, g):
    lo = lax.broadcasted_iota(jnp.int32, z.shape, 1) < HEAD_DIM
    s = z * z
    s_lo = jnp.sum(jnp.where(lo, s, 0.0), axis=-1, keepdims=True)
    s_hi = jnp.sum(jnp.where(lo, 0.0, s), axis=-1, keepdims=True)
    ms = jnp.where(lo, s_lo, s_hi) * (1.0 / HEAD_DIM)
    return z * lax.rsqrt(ms + EPS) * g


def _rope(y, cos, sin):
    half = ROPE_AXIS_DIM // 2
    lane = lax.broadcasted_iota(jnp.int32, y.shape, 1)
    first = (lane % ROPE_AXIS_DIM) < half
    partner = jnp.where(first, pltpu.roll(y, LANES - half, 1), pltpu.roll(y, half, 1))
    return y * cos + partner * sin


def _dup_heads(y):
    lo = lax.broadcasted_iota(jnp.int32, y.shape, 1) < HEAD_DIM
    sw = pltpu.roll(y, HEAD_DIM, 1)
    return jnp.concatenate([jnp.where(lo, y, sw), jnp.where(lo, sw, y)], axis=-1)


def _proj_kernel(*refs, rope, seq_len, fold, cache, n_cast, gate_chunk, tiles_per_mod_row):
    x_ref, mod_ref, g1_ref, w_ref, gq_ref, gk_ref = refs[:6]
    refs = refs[6:]
    if rope:
        cos_ref, sin_ref = refs[:2]
        refs = refs[2:]
    n_tabs = 4 if fold else 2
    tabs, refs = refs[:n_tabs], refs[n_tabs:]
    cast_in, refs = refs[:n_cast], refs[n_cast:]
    mix_ref, q_ref, kd_ref, vd_ref, gate_ref = refs[:5]
    refs = refs[5:]
    if cache:
        kc_ref, vc_ref = refs[:2]
        refs = refs[2:]
    cast_out, (zf_ref, ab_ref) = refs[:n_cast], refs[n_cast:]
    for src_ref, dst_ref in zip(cast_in, cast_out):
        dst_ref[...] = src_ref[...].astype(BF16)

    x = x_ref[...]
    sh1, sc1 = _mod_vectors(mod_ref, pl.program_id(0), tiles_per_mod_row)[0:2]
    h = x * lax.rsqrt(jnp.mean(x * x, axis=-1, keepdims=True) + EPS) * g1_ref[...]
    hb = (h * (1.0 + sc1) + sh1).astype(BF16)

    zf_ref[...] = _dot(hb, w_ref[:, 0:_Q0]).astype(BF16)
    mixer = _fourier_stages(zf_ref, tabs, mix_ref, ab_ref, seq_len, fold)
    n_sections = 2 + 2 * D_MODEL // gate_chunk
    issued = [0]

    def run_mixer_share(section):
        while issued[0] < len(mixer) and issued[0] * n_sections < (section + 1) * len(mixer):
            mixer[issued[0]]()
            issued[0] += 1

    gq = jnp.concatenate([gq_ref[...]] * (LANES // HEAD_DIM), axis=-1)
    gk = jnp.concatenate([gk_ref[...]] * (LANES // HEAD_DIM), axis=-1)

    zq = _dot(hb, w_ref[:, _Q0:_K0])
    for j in range(ATTN_WIDTH // LANES):
        y = _head_norm(zq[:, j * LANES:(j + 1) * LANES], gq)
        if rope:
            y = _rope(y, cos_ref[...], sin_ref[...])
        q_ref[:, j * LANES:(j + 1) * LANES] = (y * (HEAD_DIM ** -0.5 * LOG2_E)).astype(BF16)
    run_mixer_share(0)

    zkv = _dot(hb, w_ref[:, _K0:_G0])
    k = _head_norm(zkv[:, 0:KV_WIDTH], gk)
    v = zkv[:, KV_WIDTH:2 * KV_WIDTH]
    if cache:
        for s in range(x.shape[0] // seq_len):
            rows = slice(s * seq_len, (s + 1) * seq_len)
            kc_ref[s] = k[rows, :].T
            vc_ref[s] = v[rows, :].T
    if rope:
        k = _rope(k, cos_ref[...], sin_ref[...])
    kt = k.T
    kd_ref[...] = jnp.concatenate(
        [kt[0:HEAD_DIM], kt[0:HEAD_DIM], kt[HEAD_DIM:], kt[HEAD_DIM:]], axis=0).astype(BF16)
    vd_ref[...] = _dup_heads(v).astype(BF16)
    run_mixer_share(1)

    for j in range(2 * D_MODEL // gate_chunk):
        cols = slice(j * gate_chunk, (j + 1) * gate_chunk)
        zg = _dot(hb, w_ref[:, _G0 + cols.start:_G0 + cols.stop])
        gate_ref[:, cols] = _sigmoid(zg).astype(BF16)
        run_mixer_share(2 + j)


def _proj(x, mod3, g1, w_in_b, gq2, gk2, seq_len, rope_tabs, mod_row, rows_per_mod_row, cache,
          cast_weights):
    t = x.shape[0]
    tm = PROJ_TOKEN_TILE
    assert tm % seq_len == 0, "a projection tile must hold whole sequences for the mixer"
    steps = t // tm
    rope = rope_tabs is not None
    row = lambda i: (i, 0)
    tiles_per_mod_row = None if rows_per_mod_row is None else rows_per_mod_row // tm
    in_specs = [pl.BlockSpec((tm, D_MODEL), row),
                _mod_spec(mod_row),
                _resident((1, D_MODEL)),
                _resident((D_MODEL, IN_WIDTH)),
                _resident((1, HEAD_DIM)),
                _resident((1, HEAD_DIM))]
    args = [x, mod3, g1, w_in_b, gq2, gk2]
    if rope:
        tiles_per_seq = rope_tabs[0].shape[0] // tm
        tab = pl.BlockSpec((tm, LANES), lambda i: (i % tiles_per_seq, 0))
        in_specs += [tab, tab]
        args += list(rope_tabs)
    fold = _fold_dft(seq_len)
    gate_chunk = GATE_COLUMNS // 2 if fold else GATE_COLUMNS
    dft = _dft_tables(seq_len, fold)
    in_specs += [_resident(tab.shape) for tab in dft]
    args += list(dft)
    slabs = [pl.BlockSpec((w.shape[0] // steps, w.shape[1]), row) for w in cast_weights]
    in_specs += slabs
    args += list(cast_weights)
    out_specs = [pl.BlockSpec((tm, F_WIDTH), row),
                 pl.BlockSpec((tm, ATTN_WIDTH), row),
                 pl.BlockSpec((2 * KV_WIDTH, tm), lambda i: (0, i)),
                 pl.BlockSpec((tm, 2 * KV_WIDTH), row),
                 pl.BlockSpec((tm, 2 * D_MODEL), row)]
    out_shape = [pltpu.HBM((t, F_WIDTH), BF16),
                 pltpu.HBM((t, ATTN_WIDTH), BF16),
                 pltpu.HBM((2 * KV_WIDTH, t), BF16),
                 pltpu.HBM((t, 2 * KV_WIDTH), BF16),
                 pltpu.HBM((t, 2 * D_MODEL), BF16)]
    if cache:
        seqs = tm // seq_len
        out_specs += [pl.BlockSpec((seqs, KV_WIDTH, seq_len), lambda i: (i, 0, 0))] * 2
        out_shape += [pltpu.HBM((t // seq_len, KV_WIDTH, seq_len), F32)] * 2
    out_specs += slabs
    out_shape += [pltpu.HBM(w.shape, BF16) for w in cast_weights]
    return pl.pallas_call(
        functools.partial(_proj_kernel, rope=rope, seq_len=seq_len, fold=fold, cache=cache,
                          n_cast=len(cast_weights), gate_chunk=gate_chunk,
                          tiles_per_mod_row=tiles_per_mod_row),
        grid=(steps,),
        in_specs=in_specs,
        out_specs=out_specs,
        out_shape=out_shape,
        scratch_shapes=[pltpu.VMEM((tm, F_WIDTH), BF16),
                        pltpu.VMEM((2, 2 * seq_len, F_WIDTH), BF16)],
        compiler_params=pltpu.CompilerParams(
            dimension_semantics=("arbitrary",), vmem_limit_bytes=RESIDENT_WEIGHTS_VMEM),
        name="proj_lat" if rope else "proj_ctx",
    )(*[_hbm(a) for a in args])


def _dft_tables(n, fold):
    def cs(m):
        idx = np.arange(m, dtype=np.int64)
        ang = 2.0 * np.pi * ((idx[:, None] * idx[None, :]) % m).astype(np.float64) / m
        return np.cos(ang) / np.sqrt(m), np.sin(ang) / np.sqrt(m)
    cd, sd = cs(F_GROUP_DIM)
    cn, sn = cs(n)
    tabs = [np.concatenate([cd, sd], axis=1)]
    if fold:
        half = n // 2
        rev = np.zeros((half, half), np.float64)
        rev[np.arange(1, half), half - np.arange(1, half)] = 1.0
        tabs += [cn[:half], -sn[:half], rev]
    else:
        tabs += [np.concatenate([cn, -sn], axis=1)]
    return tuple(jnp.asarray(t.astype(np.float32)).astype(BF16) for t in tabs)


def _fold_dft(n):
    return n // 2 >= 2 * MXU_DIM


def _fourier_stages(zf_ref, tabs, o_ref, ab_ref, n, fold):
    csd_ref = tabs[0]
    if fold:
        ch_ref, sh_ref, rev_ref = tabs[1:]
    else:
        (csn_ref,) = tabs[1:]
    half = n // 2
    units = []
    for s in range(zf_ref.shape[0] // n):
        buf = s % 2
        r0 = s * n

        def channels(r0=r0, buf=buf):
            for g in range(F_GROUPS):
                cols = slice(g * F_GROUP_DIM, (g + 1) * F_GROUP_DIM)
                ab = _dot(zf_ref[r0:r0 + n, cols], csd_ref[...])
                ab_ref[buf, 0:n, cols] = ab[:, 0:F_GROUP_DIM].astype(BF16)
                ab_ref[buf, n:2 * n, cols] = ab[:, F_GROUP_DIM:].astype(BF16)

        def positions(r0=r0, buf=buf):
            if not fold:
                o_ref[r0:r0 + n, :] = _dot(csn_ref[...], ab_ref[buf]).astype(BF16)
                return
            p = _dot(ch_ref[...], ab_ref[buf, 0:n, :])
            q = _dot(sh_ref[...], ab_ref[buf, n:2 * n, :])
            o_ref[r0:r0 + half, :] = (p + q).astype(BF16)
            upper = _dot(rev_ref[...], (p - q).astype(BF16))
            a = ab_ref[buf, 0:n, :].astype(F32)
            even = lax.broadcasted_iota(jnp.int32, a.shape, 0) % 2 == 0
            mid = jnp.sum(jnp.where(even, a, -a), axis=0, keepdims=True) * (n ** -0.5)
            first = lax.broadcasted_iota(jnp.int32, upper.shape, 0) == 0
            o_ref[r0 + half:r0 + n, :] = jnp.where(first, mid, upper).astype(BF16)

        units.append((channels, positions))
    return _pipelined(units)


def _group_scores(q_stack, keys, s_ref):
    kt = keys[0] if len(keys) == 1 else jnp.concatenate(keys, axis=1)
    top = lax.broadcasted_iota(jnp.int32, kt.shape, 0) < HEAD_DIM
    zero = jnp.zeros_like(kt)
    s_ref[...] = _dot(q_stack, jnp.concatenate(
        [jnp.where(top, kt, zero), jnp.where(top, zero, kt)], axis=1))


def _group_softmax_pv(values, sinks, band, s_ref, p_ref, sk_ref, rows_per_pair, chunk):
    rows = s_ref.shape[0]
    keys = s_ref.shape[1] // 2
    bounds = [0]
    for vd in values:
        bounds.append(bounds[-1] + vd.shape[0])
    lo = lax.broadcasted_iota(jnp.int32, (chunk, LANES), 1) < HEAD_DIM

    for r0 in range(0, rows, chunk):
        rs = slice(r0, r0 + chunk)
        sink_terms = []
        for half in range(2):
            sink = sinks[2 * (r0 // rows_per_pair) + half]
            cols = [slice(half * keys + bounds[i], half * keys + bounds[i + 1])
                    for i in range(len(values))]
            parts = [s_ref[rs, c] for c in cols]
            if band is not None:
                parts[0] = jnp.where(band(r0 % rows_per_pair, chunk), parts[0], NEG_INF)
            m = sink
            for s in parts:
                m = jnp.maximum(m, jnp.max(s, axis=-1, keepdims=True))
            for c, s in zip(cols, parts):
                p_ref[rs, c] = jnp.exp2(s - m).astype(BF16)
            sink_terms.append(jnp.exp2(sink - m))
        sk_ref[rs, :] = jnp.where(lo, sink_terms[0], sink_terms[1])

    vd = values[0] if len(values) == 1 else jnp.concatenate(values, axis=0)
    v_lo = lax.broadcasted_iota(jnp.int32, vd.shape, 1) < HEAD_DIM
    zero = jnp.zeros_like(vd)
    ones_lo = jnp.where(v_lo, 1.0, 0.0).astype(BF16)
    ones_hi = jnp.where(v_lo, 0.0, 1.0).astype(BF16)
    operand = jnp.concatenate(
        [jnp.concatenate([jnp.where(v_lo, vd, zero), ones_lo], axis=1),
         jnp.concatenate([jnp.where(v_lo, zero, vd), ones_hi], axis=1)], axis=0)
    out = _dot(p_ref[...], operand)
    return out[:, 0:LANES] / (out[:, LANES:2 * LANES] + sk_ref[...])


def _pair_cols(g):
    pa, pb = 2 * g, 2 * g + 1
    return slice(pa * LANES, (pa + 1) * LANES), slice(pb * LANES, (pb + 1) * LANES)


def _attn_tail_kernel(*refs, lat, n_tiles, tm, n, tiles_per_mod_row):
    sink_ref, q_ref, kd_ref, vd_ref = refs[:4]
    refs = refs[4:]
    if lat:
        ck_ref, cv_ref = refs[:2]
        refs = refs[2:]
    (x_ref, mod_ref, mix_ref, gate_ref, g2_ref,
     wf_ref, wao_ref, wout_ref, wup_ref, wdn_ref, o_ref, att_ref, s_ref, p_ref, sk_ref) = refs[:15]
    if lat:
        dist_ref, ckt_ref, cvd_ref = refs[15:18]
    act_ref = refs[-1]
    i = pl.program_id(0)

    def attention_stages():
        units = []
        if lat:
            qb = WINDOW
            span = qb + 2 * WINDOW
            t = jnp.minimum(i, n_tiles - 1) % (n // tm)
            ckt = ck_ref[0]
            ckt_ref[...] = jnp.concatenate(
                [ckt[0:HEAD_DIM], ckt[0:HEAD_DIM], ckt[HEAD_DIM:], ckt[HEAD_DIM:]], axis=0).astype(BF16)
            cvd_ref[...] = _dup_heads(cv_ref[0].T).astype(BF16)
            for rr in range(tm // qb):
                q0 = t * tm + rr * qb
                start = pl.multiple_of(jnp.clip(q0 - WINDOW, 0, n - span), WINDOW)
                dist_ref[rr] = jnp.abs((q0 - start) + lax.broadcasted_iota(jnp.int32, (qb, span), 0)
                                       - lax.broadcasted_iota(jnp.int32, (qb, span), 1))
                band = lambda r0, rows, rr=rr: dist_ref[rr, r0:r0 + rows, :] <= WINDOW
                rows = slice(rr * qb, (rr + 1) * qb)
                for g in range(N_KV_HEADS):
                    buf = len(units) % 2
                    cols = slice(g * LANES, (g + 1) * LANES)
                    qa, qbc = _pair_cols(g)

                    def scores(rows=rows, start=start, cols=cols, qa=qa, qbc=qbc, buf=buf):
                        q_stack = jnp.concatenate([q_ref[rows, qa], q_ref[rows, qbc]], axis=0)
                        _group_scores(q_stack, [kd_ref[cols, pl.ds(start, span)], ckt_ref[cols, :]],
                                      s_ref.at[buf])

                    def finish(rows=rows, start=start, g=g, cols=cols, qa=qa, qbc=qbc, buf=buf,
                               band=band):
                        sinks = [sink_ref[GROUP * g + h] * LOG2_E for h in range(GROUP)]
                        o = _group_softmax_pv(
                            [vd_ref[pl.ds(start, span), cols], cvd_ref[:, cols]], sinks, band,
                            s_ref.at[buf], p_ref.at[buf], sk_ref.at[buf], qb, LAT_SOFTMAX_ROWS)
                        att_ref[rows, qa] = o[0:qb].astype(BF16)
                        att_ref[rows, qbc] = o[qb:2 * qb].astype(BF16)

                    units.append((scores, finish))
        else:
            for s in range(tm // n):
                rows = slice(s * n, (s + 1) * n)
                for g in range(N_KV_HEADS):
                    buf = len(units) % 2
                    cols = slice(g * LANES, (g + 1) * LANES)
                    qa, qbc = _pair_cols(g)

                    def scores(rows=rows, cols=cols, qa=qa, qbc=qbc, buf=buf):
                        q_stack = jnp.concatenate([q_ref[rows, qa], q_ref[rows, qbc]], axis=0)
                        _group_scores(q_stack, [kd_ref[cols, rows]], s_ref.at[buf])

                    def finish(rows=rows, g=g, cols=cols, qa=qa, qbc=qbc, buf=buf):
                        sinks = [sink_ref[GROUP * g + h] * LOG2_E for h in range(GROUP)]
                        o = _group_softmax_pv([vd_ref[rows, cols]], sinks, None, s_ref.at[buf],
                                              p_ref.at[buf], sk_ref.at[buf], n, CTX_SOFTMAX_ROWS)
                        att_ref[rows, qa] = o[0:n].astype(BF16)
                        att_ref[rows, qbc] = o[n:2 * n].astype(BF16)

                    units.append((scores, finish))
        return _pipelined(units)

    def tail(stages):
        gt1, sh2, sc2, gt2 = _mod_vectors(mod_ref, i - 1, tiles_per_mod_row)[2:6]
        yf = _dot(mix_ref[...], wf_ref[...])
        ya = _dot(att_ref[...], wao_ref[...])
        merged = (gate_ref[:, 0:D_MODEL].astype(F32) * yf
                  + gate_ref[:, D_MODEL:2 * D_MODEL].astype(F32) * ya)
        x1 = x_ref[...] + gt1 * _dot(merged.astype(BF16), wout_ref[...])

        h = x1 * lax.rsqrt(jnp.mean(x1 * x1, axis=-1, keepdims=True) + EPS) * g2_ref[...]
        hb = (h * (1.0 + sc2) + sh2).astype(BF16)
        d_ff = wdn_ref.shape[0]
        n_chunks = d_ff // FF_CHUNK
        issued = 0
        for c in range(n_chunks):
            cols = slice(c * FF_CHUNK, (c + 1) * FF_CHUNK)
            a = _dot(hb, wup_ref[:, cols])
            u = _dot(hb, wup_ref[:, d_ff + cols.start:d_ff + cols.stop])
            act_ref[:, cols] = (a * _sigmoid(a) * u).astype(BF16)
            while issued < len(stages) and issued * n_chunks < (c + 1) * len(stages):
                stages[issued]()
                issued += 1
        o_ref[...] = x1 + gt2 * _dot(act_ref[...], wdn_ref[...])

    @pl.when(i == 0)
    def _():
        for stage in attention_stages():
            stage()

    @pl.when(jnp.logical_and(i > 0, i < n_tiles))
    def _():
        tail(attention_stages())

    @pl.when(i == n_tiles)
    def _():
        tail([])


def _attn_tail(sinks, q, kd, vd, cache, x, mod3, mix, gate, g2, wf, wao, wout, wup, wdn,
               mod_row, rows_per_mod_row, n):
    t = x.shape[0]
    tm = TAIL_TOKEN_TILE
    n_tiles = t // tm
    lat = cache is not None
    tiles_per_mod_row = None if rows_per_mod_row is None else rows_per_mod_row // tm
    front = lambda i: (jnp.minimum(i, n_tiles - 1), 0)
    back = lambda i: (jnp.maximum(i - 1, 0), 0)
    in_specs = [pl.BlockSpec(memory_space=pltpu.SMEM), pl.BlockSpec((tm, ATTN_WIDTH), front)]
    args = [sinks, q, kd, vd]
    if lat:
        past = cache[0].shape[2]
        seq_of = lambda i: jnp.minimum(i, n_tiles - 1) // (n // tm)
        in_specs += [pl.BlockSpec((2 * KV_WIDTH, n), lambda i: (0, seq_of(i))),
                     pl.BlockSpec((n, 2 * KV_WIDTH), lambda i: (seq_of(i), 0))]
        in_specs += [pl.BlockSpec((1, KV_WIDTH, past), lambda i: (seq_of(i), 0, 0))] * 2
        args += list(cache)
        m_rows, keys = 2 * WINDOW, 3 * WINDOW + past
    else:
        in_specs += [pl.BlockSpec((2 * KV_WIDTH, tm), lambda i: (0, front(i)[0])),
                     pl.BlockSpec((tm, 2 * KV_WIDTH), front)]
        m_rows, keys = 2 * n, n
    in_specs += [pl.BlockSpec((tm, D_MODEL), back),
                 _mod_spec(mod_row),
                 pl.BlockSpec((tm, F_WIDTH), back),
                 pl.BlockSpec((tm, 2 * D_MODEL), back),
                 _resident((1, D_MODEL)),
                 _resident(wf.shape), _resident(wao.shape), _resident(wout.shape),
                 _resident(wup.shape), _resident(wdn.shape)]
    args += [x, mod3, mix, gate, g2, wf, wao, wout, wup, wdn]
    scratch = [pltpu.VMEM((tm, ATTN_WIDTH), BF16),
               pltpu.VMEM((2, m_rows, 2 * keys), F32),
               pltpu.VMEM((2, m_rows, 2 * keys), BF16),
               pltpu.VMEM((2, m_rows, LANES), F32)]
    if lat:
        scratch += [pltpu.VMEM((tm // WINDOW, WINDOW, 3 * WINDOW), jnp.int32),
                    pltpu.VMEM((2 * KV_WIDTH, past), BF16),
                    pltpu.VMEM((past, 2 * KV_WIDTH), BF16)]
    scratch += [pltpu.VMEM((tm, wdn.shape[0]), BF16)]
    return pl.pallas_call(
        functools.partial(_attn_tail_kernel, lat=lat, n_tiles=n_tiles, tm=tm, n=n,
                          tiles_per_mod_row=tiles_per_mod_row),
        grid=(n_tiles + 1,),
        in_specs=in_specs,
        out_specs=pl.BlockSpec((tm, D_MODEL), back),
        out_shape=pltpu.HBM((t, D_MODEL), F32),
        scratch_shapes=scratch,
        compiler_params=pltpu.CompilerParams(
            dimension_semantics=("arbitrary",), vmem_limit_bytes=RESIDENT_WEIGHTS_VMEM),
        name="attn_tail_lat" if lat else "attn_tail_ctx",
    )(sinks, *[_hbm(a) for a in args[1:]])


def _rope_tables(n):
    rows = n // GRID_W
    row = np.repeat(np.arange(rows, dtype=np.float64), GRID_W)
    col = np.tile(np.arange(GRID_W, dtype=np.float64), rows)
    inv_freq = ROPE_THETA ** (-np.arange(0, ROPE_AXIS_DIM, 2, dtype=np.float64) / ROPE_AXIS_DIM)

    def axis_tabs(pos):
        ang = pos[:, None] * inv_freq[None, :]
        cos, sin = np.cos(ang), np.sin(ang)
        return np.concatenate([cos, cos], axis=-1), np.concatenate([-sin, sin], axis=-1)

    cr, sr = axis_tabs(row)
    cc, sc = axis_tabs(col)
    cos = np.tile(np.concatenate([cr, cc], axis=-1), (1, LANES // HEAD_DIM))
    sin = np.tile(np.concatenate([sr, sc], axis=-1), (1, LANES // HEAD_DIM))
    return jnp.asarray(cos.astype(np.float32)), jnp.asarray(sin.astype(np.float32))


@jax.jit
def _layer(xp, xs, ck, cv, c, c_ctx, w_ada, b_ada, g_norm1, g_norm2, w_in, g_q, g_k, sink,
           w_f, w_ao, w_out, w_up, w_down):
    bp, sp, _ = xp.shape
    bs, ss, _ = xs.shape
    past = ck.shape[1]

    mod3, w_in_b = _ada(c, c_ctx[None, :], w_ada, b_ada[None, :], w_in)
    g1, g2 = g_norm1[None, :], g_norm2[None, :]
    gq2, gk2 = g_q[None, :], g_k[None, :]
    ctx_mod = (CTX_MOD_ROW, None)
    lat_mod = (0, ss)

    xp2 = xp.reshape(bp * sp, D_MODEL)
    xs2 = xs.reshape(bs * ss, D_MODEL)
    mix, q, kd, vd, gate, k_new, v_new, wup = _proj(
        xp2, mod3, g1, w_in_b, gq2, gk2, sp, None, *ctx_mod, True, [w_up])
    mix_s, q_s, kd_s, vd_s, gate_s, wf, wao, wout, wdn = _proj(
        xs2, mod3, g1, w_in_b, gq2, gk2, ss, _rope_tables(ss), *lat_mod, False,
        [w_f, w_ao, w_out, w_down])

    yp = _attn_tail(sink, q, kd, vd, None, xp2, mod3, mix, gate, g2,
                    wf, wao, wout, wup, wdn, *ctx_mod, sp)

    def cache_t(t):
        return t.transpose(0, 2, 3, 1).reshape(bs, KV_WIDTH, past)

    ys = _attn_tail(sink, q_s, kd_s, vd_s, (cache_t(ck), cache_t(cv)), xs2, mod3,
                    mix_s, gate_s, g2, wf, wao, wout, wup, wdn, *lat_mod, ss)

    def cache_layout(t):
        return t.reshape(bp, N_KV_HEADS, HEAD_DIM, sp).transpose(0, 3, 1, 2)

    return yp.reshape(xp.shape), ys.reshape(xs.shape), cache_layout(k_new), cache_layout(v_new)


def kernel(x_prompt, x_sample, cache_k, cache_v, c, c_ctx, w_ada, b_ada, g_norm1, g_norm2,
           w_in, g_q, g_k, sinks, w_f, w_ao, w_out, w_up, w_down):
    depth = w_in.shape[0]
    xp, xs = x_prompt, x_sample
    new_k, new_v = [], []
    for l in range(depth):
        xp, xs, k_ctx, v_ctx = _layer(
            xp, xs, cache_k[:, l], cache_v[:, l], c, c_ctx, w_ada[l], b_ada[l], g_norm1[l],
            g_norm2[l], w_in[l], g_q[l], g_k[l], sinks[l], w_f[l], w_ao[l], w_out[l], w_up[l],
            w_down[l])
        new_k.append(k_ctx)
        new_v.append(v_ctx)
    return (xp, xs, jnp.stack(new_k, axis=1), jnp.stack(new_v, axis=1))
```

```python
import functools

import numpy as np
import jax
import jax.numpy as jnp
from jax import lax
from jax.experimental import pallas as pl
from jax.experimental.pallas import tpu as pltpu

D_MODEL = 1024
GRID_W = 64
N_HEADS = 8
N_KV_HEADS = 2
GROUP = N_HEADS // N_KV_HEADS
HEAD_DIM = 64
WINDOW = 128
F_GROUPS = 4
F_GROUP_DIM = 128
F_WIDTH = F_GROUPS * F_GROUP_DIM
ATTN_WIDTH = N_HEADS * HEAD_DIM
KV_WIDTH = N_KV_HEADS * HEAD_DIM
ROPE_THETA = 10000.0
ROPE_AXIS_DIM = HEAD_DIM // 2
EPS = 1e-6
NEG_INF = -1e30
LOG2_E = 1.4426950408889634

LANES = 128
MXU_DIM = 256
VMEM_BYTES_V7X = 64 * 1024 * 1024

_Q0 = F_WIDTH
_K0 = _Q0 + ATTN_WIDTH
_V0 = _K0 + KV_WIDTH
_G0 = _V0 + KV_WIDTH
IN_WIDTH = _G0 + 2 * D_MODEL

SUBLANES = 8
N_MOD = 6
MOD_ROWS = 16
CTX_MOD_ROW = 8
ADA_STEPS = 8

PROJ_TOKEN_TILE = 1024
TAIL_TOKEN_TILE = 512
FF_CHUNK = MXU_DIM
GATE_COLUMNS = 2 * D_MODEL
CTX_SOFTMAX_ROWS = 128
LAT_SOFTMAX_ROWS = 64

F32 = jnp.float32
BF16 = jnp.bfloat16


RESIDENT_WEIGHTS_VMEM = VMEM_BYTES_V7X - 6 * 2 ** 20


def _dot(a, b):
    return jnp.dot(a, b, preferred_element_type=F32)


def _sigmoid(x):
    return 1.0 / (1.0 + jnp.exp(-x))


def _resident(shape):
    zeros = (0,) * len(shape)
    return pl.BlockSpec(shape, lambda *_: zeros, pipeline_mode=pl.Buffered(1))


def _hbm(x):
    try:
        return pltpu.with_memory_space_constraint(x, pltpu.HBM)
    except ValueError:
        return x


def _pipelined(units):
    stages = [units[0][0]]
    for k, (_, second) in enumerate(units):
        if k + 1 < len(units):
            stages.append(units[k + 1][0])
        stages.append(second)
    return stages


def _ada_kernel(c_ref, cctx_ref, w_ref, b_ref, win_ref, o_ref, winb_ref):
    lat_rows = c_ref.shape[0]
    pieces = [c_ref[...]]
    if lat_rows < CTX_MOD_ROW:
        pieces.append(jnp.zeros((CTX_MOD_ROW - lat_rows, D_MODEL), F32))
    pieces.append(jnp.broadcast_to(cctx_ref[...], (MOD_ROWS - CTX_MOD_ROW, D_MODEL)))
    c = jnp.concatenate(pieces, axis=0)
    s = c * _sigmoid(c)
    o_ref[...] = _dot(s.astype(BF16), w_ref[...].astype(BF16)) + b_ref[...]
    winb_ref[...] = win_ref[...].astype(BF16)


def _ada(c, c_ctx, w_ada, b_ada, w_in):
    assert c.shape[0] <= CTX_MOD_ROW, "latent conditioning rows must fit below the context row"
    n = w_ada.shape[1]
    bn = n // ADA_STEPS
    slab = pl.BlockSpec((w_in.shape[0] // ADA_STEPS, w_in.shape[1]), lambda j: (j, 0))
    return pl.pallas_call(
        _ada_kernel,
        grid=(ADA_STEPS,),
        in_specs=[pl.BlockSpec(c.shape, lambda j: (0, 0)),
                  pl.BlockSpec((1, D_MODEL), lambda j: (0, 0)),
                  pl.BlockSpec((D_MODEL, bn), lambda j: (0, j)),
                  pl.BlockSpec((1, bn), lambda j: (0, j)),
                  slab],
        out_specs=[pl.BlockSpec((MOD_ROWS, bn), lambda j: (0, j)), slab],
        out_shape=[pltpu.HBM((MOD_ROWS, n), F32), pltpu.HBM(w_in.shape, BF16)],
        name="ada",
    )(c, c_ctx, w_ada, b_ada, w_in)


def _mod_spec(mod_row):
    return pl.BlockSpec((SUBLANES, N_MOD * D_MODEL), lambda i: (mod_row // SUBLANES, 0))


def _mod_vectors(mod_ref, tile, tiles_per_row):
    cols = [slice(j * D_MODEL, (j + 1) * D_MODEL) for j in range(N_MOD)]
    if tiles_per_row is None:
        return [mod_ref[0:1, c] for c in cols]
    r = tile // tiles_per_row
    return [mod_ref[pl.ds(r, 1), c] for c in cols]


def _head_norm(z, g):
    lo = lax.broadcasted_iota(jnp.int32, z.shape, 1) < HEAD_DIM
    s = z * z
    s_lo = jnp.sum(jnp.where(lo, s, 0.0), axis=-1, keepdims=True)
    s_hi = jnp.sum(jnp.where(lo, 0.0, s), axis=-1, keepdims=True)
    ms = jnp.where(lo, s_lo, s_hi) * (1.0 / HEAD_DIM)
    return z * lax.rsqrt(ms + EPS) * g


def _rope(y, cos, sin):
    half = ROPE_AXIS_DIM // 2
    lane = lax.broadcasted_iota(jnp.int32, y.shape, 1)
    first = (lane % ROPE_AXIS_DIM) < half
    partner = jnp.where(first, pltpu.roll(y, LANES - half, 1), pltpu.roll(y, half, 1))
    return y * cos + partner * sin


def _dup_heads(y):
    lo = lax.broadcasted_iota(jnp.int32, y.shape, 1) < HEAD_DIM
    sw = pltpu.roll(y, HEAD_DIM, 1)
    return jnp.concatenate([jnp.where(lo, y, sw), jnp.where(lo, sw, y)], axis=-1)


def _proj_kernel(*refs, rope, seq_len, fold, cache, n_cast, gate_chunk, tiles_per_mod_row):
    x_ref, mod_ref, g1_ref, w_ref, gq_ref, gk_ref = refs[:6]
    refs = refs[6:]
    if rope:
        cos_ref, sin_ref = refs[:2]
        refs = refs[2:]
    n_tabs = 4 if fold else 2
    tabs, refs = refs[:n_tabs], refs[n_tabs:]
    cast_in, refs = refs[:n_cast], refs[n_cast:]
    mix_ref, q_ref, kd_ref, vd_ref, gate_ref = refs[:5]
    refs = refs[5:]
    if cache:
        kc_ref, vc_ref = refs[:2]
        refs = refs[2:]
    cast_out, (zf_ref, ab_ref) = refs[:n_cast], refs[n_cast:]
    for src_ref, dst_ref in zip(cast_in, cast_out):
        dst_ref[...] = src_ref[...].astype(BF16)

    x = x_ref[...]
    sh1, sc1 = _mod_vectors(mod_ref, pl.program_id(0), tiles_per_mod_row)[0:2]
    h = x * lax.rsqrt(jnp.mean(x * x, axis=-1, keepdims=True) + EPS) * g1_ref[...]
    hb = (h * (1.0 + sc1) + sh1).astype(BF16)

    zf_ref[...] = _dot(hb, w_ref[:, 0:_Q0]).astype(BF16)
    mixer = _fourier_stages(zf_ref, tabs, mix_ref, ab_ref, seq_len, fold)
    n_sections = 2 + 2 * D_MODEL // gate_chunk
    issued = [0]

    def run_mixer_share(section):
        while issued[0] < len(mixer) and issued[0] * n_sections < (section + 1) * len(mixer):
            mixer[issued[0]]()
            issued[0] += 1

    gq = jnp.concatenate([gq_ref[...]] * (LANES // HEAD_DIM), axis=-1)
    gk = jnp.concatenate([gk_ref[...]] * (LANES // HEAD_DIM), axis=-1)

    zq = _dot(hb, w_ref[:, _Q0:_K0])
    for j in range(ATTN_WIDTH // LANES):
        y = _head_norm(zq[:, j * LANES:(j + 1) * LANES], gq)
        if rope:
            y = _rope(y, cos_ref[...], sin_ref[...])
        q_ref[:, j * LANES:(j + 1) * LANES] = (y * (HEAD_DIM ** -0.5 * LOG2_E)).astype(BF16)
    run_mixer_share(0)

    zkv = _dot(hb, w_ref[:, _K0:_G0])
    k = _head_norm(zkv[:, 0:KV_WIDTH], gk)
    v = zkv[:, KV_WIDTH:2 * KV_WIDTH]
    if cache:
        for s in range(x.shape[0] // seq_len):
            rows = slice(s * seq_len, (s + 1) * seq_len)
            kc_ref[s] = k[rows, :].T
            vc_ref[s] = v[rows, :].T
    if rope:
        k = _rope(k, cos_ref[...], sin_ref[...])
    kt = k.T
    kd_ref[...] = jnp.concatenate(
        [kt[0:HEAD_DIM], kt[0:HEAD_DIM], kt[HEAD_DIM:], kt[HEAD_DIM:]], axis=0).astype(BF16)
    vd_ref[...] = _dup_heads(v).astype(BF16)
    run_mixer_share(1)

    for j in range(2 * D_MODEL // gate_chunk):
        cols = slice(j * gate_chunk, (j + 1) * gate_chunk)
        zg = _dot(hb, w_ref[:, _G0 + cols.start:_G0 + cols.stop])
        gate_ref[:, cols] = _sigmoid(zg).astype(BF16)
        run_mixer_share(2 + j)


def _proj(x, mod3, g1, w_in_b, gq2, gk2, seq_len, rope_tabs, mod_row, rows_per_mod_row, cache,
          cast_weights):
    t = x.shape[0]
    tm = PROJ_TOKEN_TILE
    assert tm % seq_len == 0, "a projection tile must hold whole sequences for the mixer"
    steps = t // tm
    rope = rope_tabs is not None
    row = lambda i: (i, 0)
    tiles_per_mod_row = None if rows_per_mod_row is None else rows_per_mod_row // tm
    in_specs = [pl.BlockSpec((tm, D_MODEL), row),
                _mod_spec(mod_row),
                _resident((1, D_MODEL)),
                _resident((D_MODEL, IN_WIDTH)),
                _resident((1, HEAD_DIM)),
                _resident((1, HEAD_DIM))]
    args = [x, mod3, g1, w_in_b, gq2, gk2]
    if rope:
        tiles_per_seq = rope_tabs[0].shape[0] // tm
        tab = pl.BlockSpec((tm, LANES), lambda i: (i % tiles_per_seq, 0))
        in_specs += [tab, tab]
        args += list(rope_tabs)
    fold = _fold_dft(seq_len)
    gate_chunk = GATE_COLUMNS // 2 if fold else GATE_COLUMNS
    dft = _dft_tables(seq_len, fold)
    in_specs += [_resident(tab.shape) for tab in dft]
    args += list(dft)
    slabs = [pl.BlockSpec((w.shape[0] // steps, w.shape[1]), row) for w in cast_weights]
    in_specs += slabs
    args += list(cast_weights)
    out_specs = [pl.BlockSpec((tm, F_WIDTH), row),
                 pl.BlockSpec((tm, ATTN_WIDTH), row),
                 pl.BlockSpec((2 * KV_WIDTH, tm), lambda i: (0, i)),
                 pl.BlockSpec((tm, 2 * KV_WIDTH), row),
                 pl.BlockSpec((tm, 2 * D_MODEL), row)]
    out_shape = [pltpu.HBM((t, F_WIDTH), BF16),
                 pltpu.HBM((t, ATTN_WIDTH), BF16),
                 pltpu.HBM((2 * KV_WIDTH, t), BF16),
                 pltpu.HBM((t, 2 * KV_WIDTH), BF16),
                 pltpu.HBM((t, 2 * D_MODEL), BF16)]
    if cache:
        seqs = tm // seq_len
        out_specs += [pl.BlockSpec((seqs, KV_WIDTH, seq_len), lambda i: (i, 0, 0))] * 2
        out_shape += [pltpu.HBM((t // seq_len, KV_WIDTH, seq_len), F32)] * 2
    out_specs += slabs
    out_shape += [pltpu.HBM(w.shape, BF16) for w in cast_weights]
    return pl.pallas_call(
        functools.partial(_proj_kernel, rope=rope, seq_len=seq_len, fold=fold, cache=cache,
                          n_cast=len(cast_weights), gate_chunk=gate_chunk,
                          tiles_per_mod_row=tiles_per_mod_row),
        grid=(steps,),
        in_specs=in_specs,
        out_specs=out_specs,
        out_shape=out_shape,
        scratch_shapes=[pltpu.VMEM((tm, F_WIDTH), BF16),
                        pltpu.VMEM((2, 2 * seq_len, F_WIDTH), BF16)],
        compiler_params=pltpu.CompilerParams(
            dimension_semantics=("arbitrary",), vmem_limit_bytes=RESIDENT_WEIGHTS_VMEM),
        name="proj_lat" if rope else "proj_ctx",
    )(*[_hbm(a) for a in args])


def _dft_tables(n, fold):
    def cs(m):
        idx = np.arange(m, dtype=np.int64)
        ang = 2.0 * np.pi * ((idx[:, None] * idx[None, :]) % m).astype(np.float64) / m
        return np.cos(ang) / np.sqrt(m), np.sin(ang) / np.sqrt(m)
    cd, sd = cs(F_GROUP_DIM)
    cn, sn = cs(n)
    tabs = [np.concatenate([cd, sd], axis=1)]
    if fold:
        half = n // 2
        rev = np.zeros((half, half), np.float64)
        rev[np.arange(1, half), half - np.arange(1, half)] = 1.0
        tabs += [cn[:half], -sn[:half], rev]
    else:
        tabs += [np.concatenate([cn, -sn], axis=1)]
    return tuple(jnp.asarray(t.astype(np.float32)).astype(BF16) for t in tabs)


def _fold_dft(n):
    return n // 2 >= 2 * MXU_DIM


def _fourier_stages(zf_ref, tabs, o_ref, ab_ref, n, fold):
    csd_ref = tabs[0]
    if fold:
        ch_ref, sh_ref, rev_ref = tabs[1:]
    else:
        (csn_ref,) = tabs[1:]
    half = n // 2
    units = []
    for s in range(zf_ref.shape[0] // n):
        buf = s % 2
        r0 = s * n

        def channels(r0=r0, buf=buf):
            for g in range(F_GROUPS):
                cols = slice(g * F_GROUP_DIM, (g + 1) * F_GROUP_DIM)
                ab = _dot(zf_ref[r0:r0 + n, cols], csd_ref[...])
                ab_ref[buf, 0:n, cols] = ab[:, 0:F_GROUP_DIM].astype(BF16)
                ab_ref[buf, n:2 * n, cols] = ab[:, F_GROUP_DIM:].astype(BF16)

        def positions(r0=r0, buf=buf):
            if not fold:
                o_ref[r0:r0 + n, :] = _dot(csn_ref[...], ab_ref[buf]).astype(BF16)
                return
            p = _dot(ch_ref[...], ab_ref[buf, 0:n, :])
            q = _dot(sh_ref[...], ab_ref[buf, n:2 * n, :])
            o_ref[r0:r0 + half, :] = (p + q).astype(BF16)
            upper = _dot(rev_ref[...], (p - q).astype(BF16))
            a = ab_ref[buf, 0:n, :].astype(F32)
            even = lax.broadcasted_iota(jnp.int32, a.shape, 0) % 2 == 0
            mid = jnp.sum(jnp.where(even, a, -a), axis=0, keepdims=True) * (n ** -0.5)
            first = lax.broadcasted_iota(jnp.int32, upper.shape, 0) == 0
            o_ref[r0 + half:r0 + n, :] = jnp.where(first, mid, upper).astype(BF16)

        units.append((channels, positions))
    return _pipelined(units)


def _group_scores(q_stack, keys, s_ref):
    kt = keys[0] if len(keys) == 1 else jnp.concatenate(keys, axis=1)
    top = lax.broadcasted_iota(jnp.int32, kt.shape, 0) < HEAD_DIM
    zero = jnp.zeros_like(kt)
    s_ref[...] = _dot(q_stack, jnp.concatenate(
        [jnp.where(top, kt, zero), jnp.where(top, zero, kt)], axis=1))


def _group_softmax_pv(values, sinks, band, s_ref, p_ref, sk_ref, rows_per_pair, chunk):
    rows = s_ref.shape[0]
    keys = s_ref.shape[1] // 2
    bounds = [0]
    for vd in values:
        bounds.append(bounds[-1] + vd.shape[0])
    lo = lax.broadcasted_iota(jnp.int32, (chunk, LANES), 1) < HEAD_DIM

    for r0 in range(0, rows, chunk):
        rs = slice(r0, r0 + chunk)
        sink_terms = []
        for half in range(2):
            sink = sinks[2 * (r0 // rows_per_pair) + half]
            cols = [slice(half * keys + bounds[i], half * keys + bounds[i + 1])
                    for i in range(len(values))]
            parts = [s_ref[rs, c] for c in cols]
            if band is not None:
                parts[0] = jnp.where(band(r0 % rows_per_pair, chunk), parts[0], NEG_INF)
            m = sink
            for s in parts:
                m = jnp.maximum(m, jnp.max(s, axis=-1, keepdims=True))
            for c, s in zip(cols, parts):
                p_ref[rs, c] = jnp.exp2(s - m).astype(BF16)
            sink_terms.append(jnp.exp2(sink - m))
        sk_ref[rs, :] = jnp.where(lo, sink_terms[0], sink_terms[1])

    vd = values[0] if len(values) == 1 else jnp.concatenate(values, axis=0)
    v_lo = lax.broadcasted_iota(jnp.int32, vd.shape, 1) < HEAD_DIM
    zero = jnp.zeros_like(vd)
    ones_lo = jnp.where(v_lo, 1.0, 0.0).astype(BF16)
    ones_hi = jnp.where(v_lo, 0.0, 1.0).astype(BF16)
    operand = jnp.concatenate(
        [jnp.concatenate([jnp.where(v_lo, vd, zero), ones_lo], axis=1),
         jnp.concatenate([jnp.where(v_lo, zero, vd), ones_hi], axis=1)], axis=0)
    out = _dot(p_ref[...], operand)
    return out[:, 0:LANES] / (out[:, LANES:2 * LANES] + sk_ref[...])


def _pair_cols(g):
    pa, pb = 2 * g, 2 * g + 1
    return slice(pa * LANES, (pa + 1) * LANES), slice(pb * LANES, (pb + 1) * LANES)


def _attn_tail_kernel(*refs, lat, n_tiles, tm, n, tiles_per_mod_row):
    sink_ref, q_ref, kd_ref, vd_ref = refs[:4]
    refs = refs[4:]
    if lat:
        ck_ref, cv_ref = refs[:2]
        refs = refs[2:]
    (x_ref, mod_ref, mix_ref, gate_ref, g2_ref,
     wf_ref, wao_ref, wout_ref, wup_ref, wdn_ref, o_ref, att_ref, s_ref, p_ref, sk_ref) = refs[:15]
    if lat:
        dist_ref, ckt_ref, cvd_ref = refs[15:18]
    act_ref = refs[-1]
    i = pl.program_id(0)

    def attention_stages():
        units = []
        if lat:
            qb = WINDOW
            span = qb + 2 * WINDOW
            t = jnp.minimum(i, n_tiles - 1) % (n // tm)
            ckt = ck_ref[0]
            ckt_ref[...] = jnp.concatenate(
                [ckt[0:HEAD_DIM], ckt[0:HEAD_DIM], ckt[HEAD_DIM:], ckt[HEAD_DIM:]], axis=0).astype(BF16)
            cvd_ref[...] = _dup_heads(cv_ref[0].T).astype(BF16)
            for rr in range(tm // qb):
                q0 = t * tm + rr * qb
                start = pl.multiple_of(jnp.clip(q0 - WINDOW, 0, n - span), WINDOW)
                dist_ref[rr] = jnp.abs((q0 - start) + lax.broadcasted_iota(jnp.int32, (qb, span), 0)
                                       - lax.broadcasted_iota(jnp.int32, (qb, span), 1))
                band = lambda r0, rows, rr=rr: dist_ref[rr, r0:r0 + rows, :] <= WINDOW
                rows = slice(rr * qb, (rr + 1) * qb)
                for g in range(N_KV_HEADS):
                    buf = len(units) % 2
                    cols = slice(g * LANES, (g + 1) * LANES)
                    qa, qbc = _pair_cols(g)

                    def scores(rows=rows, start=start, cols=cols, qa=qa, qbc=qbc, buf=buf):
                        q_stack = jnp.concatenate([q_ref[rows, qa], q_ref[rows, qbc]], axis=0)
                        _group_scores(q_stack, [kd_ref[cols, pl.ds(start, span)], ckt_ref[cols, :]],
                                      s_ref.at[buf])

                    def finish(rows=rows, start=start, g=g, cols=cols, qa=qa, qbc=qbc, buf=buf,
                               band=band):
                        sinks = [sink_ref[GROUP * g + h] * LOG2_E for h in range(GROUP)]
                        o = _group_softmax_pv(
                            [vd_ref[pl.ds(start, span), cols], cvd_ref[:, cols]], sinks, band,
                            s_ref.at[buf], p_ref.at[buf], sk_ref.at[buf], qb, LAT_SOFTMAX_ROWS)
                        att_ref[rows, qa] = o[0:qb].astype(BF16)
                        att_ref[rows, qbc] = o[qb:2 * qb].astype(BF16)

                    units.append((scores, finish))
        else:
            for s in range(tm // n):
                rows = slice(s * n, (s + 1) * n)
                for g in range(N_KV_HEADS):
                    buf = len(units) % 2
                    cols = slice(g * LANES, (g + 1) * LANES)
                    qa, qbc = _pair_cols(g)

                    def scores(rows=rows, cols=cols, qa=qa, qbc=qbc, buf=buf):
                        q_stack = jnp.concatenate([q_ref[rows, qa], q_ref[rows, qbc]], axis=0)
                        _group_scores(q_stack, [kd_ref[cols, rows]], s_ref.at[buf])

                    def finish(rows=rows, g=g, cols=cols, qa=qa, qbc=qbc, buf=buf):
                        sinks = [sink_ref[GROUP * g + h] * LOG2_E for h in range(GROUP)]
                        o = _group_softmax_pv([vd_ref[rows, cols]], sinks, None, s_ref.at[buf],
                                              p_ref.at[buf], sk_ref.at[buf], n, CTX_SOFTMAX_ROWS)
                        att_ref[rows, qa] = o[0:n].astype(BF16)
                        att_ref[rows, qbc] = o[n:2 * n].astype(BF16)

                    units.append((scores, finish))
        return _pipelined(units)

    def tail(stages):
        gt1, sh2, sc2, gt2 = _mod_vectors(mod_ref, i - 1, tiles_per_mod_row)[2:6]
        x1_parts, hb_parts = [], []
        for r0 in range(0, tm, tm // 2):
            rs = slice(r0, r0 + tm // 2)
            yf = _dot(mix_ref[rs, :], wf_ref[...])
            ya = _dot(att_ref[rs, :], wao_ref[...])
            merged = (gate_ref[rs, 0:D_MODEL].astype(F32) * yf
                      + gate_ref[rs, D_MODEL:2 * D_MODEL].astype(F32) * ya)
            x1_h = x_ref[rs, :] + gt1 * _dot(merged.astype(BF16), wout_ref[...])
            h = (x1_h * lax.rsqrt(jnp.mean(x1_h * x1_h, axis=-1, keepdims=True) + EPS)
                 * g2_ref[...])
            x1_parts.append(x1_h)
            hb_parts.append((h * (1.0 + sc2) + sh2).astype(BF16))
        x1 = jnp.concatenate(x1_parts, axis=0)
        hb = jnp.concatenate(hb_parts, axis=0)
        d_ff = wdn_ref.shape[0]
        n_chunks = d_ff // FF_CHUNK
        issued = 0
        for c in range(n_chunks):
            cols = slice(c * FF_CHUNK, (c + 1) * FF_CHUNK)
            a = _dot(hb, wup_ref[:, cols])
            u = _dot(hb, wup_ref[:, d_ff + cols.start:d_ff + cols.stop])
            act_ref[:, cols] = (a * _sigmoid(a) * u).astype(BF16)
            while issued < len(stages) and issued * n_chunks < (c + 1) * len(stages):
                stages[issued]()
                issued += 1
        o_ref[...] = x1 + gt2 * _dot(act_ref[...], wdn_ref[...])

    @pl.when(i == 0)
    def _():
        for stage in attention_stages():
            stage()

    @pl.when(jnp.logical_and(i > 0, i < n_tiles))
    def _():
        tail(attention_stages())

    @pl.when(i == n_tiles)
    def _():
        tail([])


def _attn_tail(sinks, q, kd, vd, cache, x, mod3, mix, gate, g2, wf, wao, wout, wup, wdn,
               mod_row, rows_per_mod_row, n):
    t = x.shape[0]
    tm = TAIL_TOKEN_TILE
    n_tiles = t // tm
    lat = cache is not None
    tiles_per_mod_row = None if rows_per_mod_row is None else rows_per_mod_row // tm
    front = lambda i: (jnp.minimum(i, n_tiles - 1), 0)
    back = lambda i: (jnp.maximum(i - 1, 0), 0)
    in_specs = [pl.BlockSpec(memory_space=pltpu.SMEM), pl.BlockSpec((tm, ATTN_WIDTH), front)]
    args = [sinks, q, kd, vd]
    if lat:
        past = cache[0].shape[2]
        seq_of = lambda i: jnp.minimum(i, n_tiles - 1) // (n // tm)
        in_specs += [pl.BlockSpec((2 * KV_WIDTH, n), lambda i: (0, seq_of(i))),
                     pl.BlockSpec((n, 2 * KV_WIDTH), lambda i: (seq_of(i), 0))]
        in_specs += [pl.BlockSpec((1, KV_WIDTH, past), lambda i: (seq_of(i), 0, 0))] * 2
        args += list(cache)
        m_rows, keys = 2 * WINDOW, 3 * WINDOW + past
    else:
        in_specs += [pl.BlockSpec((2 * KV_WIDTH, tm), lambda i: (0, front(i)[0])),
                     pl.BlockSpec((tm, 2 * KV_WIDTH), front)]
        m_rows, keys = 2 * n, n
    in_specs += [pl.BlockSpec((tm, D_MODEL), back),
                 _mod_spec(mod_row),
                 pl.BlockSpec((tm, F_WIDTH), back),
                 pl.BlockSpec((tm, 2 * D_MODEL), back),
                 _resident((1, D_MODEL)),
                 _resident(wf.shape), _resident(wao.shape), _resident(wout.shape),
                 _resident(wup.shape), _resident(wdn.shape)]
    args += [x, mod3, mix, gate, g2, wf, wao, wout, wup, wdn]
    scratch = [pltpu.VMEM((tm, ATTN_WIDTH), BF16),
               pltpu.VMEM((2, m_rows, 2 * keys), F32),
               pltpu.VMEM((2, m_rows, 2 * keys), BF16),
               pltpu.VMEM((2, m_rows, LANES), F32)]
    if lat:
        scratch += [pltpu.VMEM((tm // WINDOW, WINDOW, 3 * WINDOW), jnp.int32),
                    pltpu.VMEM((2 * KV_WIDTH, past), BF16),
                    pltpu.VMEM((past, 2 * KV_WIDTH), BF16)]
    scratch += [pltpu.VMEM((tm, wdn.shape[0]), BF16)]
    return pl.pallas_call(
        functools.partial(_attn_tail_kernel, lat=lat, n_tiles=n_tiles, tm=tm, n=n,
                          tiles_per_mod_row=tiles_per_mod_row),
        grid=(n_tiles + 1,),
        in_specs=in_specs,
        out_specs=pl.BlockSpec((tm, D_MODEL), back),
        out_shape=pltpu.HBM((t, D_MODEL), F32),
        scratch_shapes=scratch,
        compiler_params=pltpu.CompilerParams(
            dimension_semantics=("arbitrary",), vmem_limit_bytes=RESIDENT_WEIGHTS_VMEM),
        name="attn_tail_lat" if lat else "attn_tail_ctx",
    )(sinks, *[_hbm(a) for a in args[1:]])


def _rope_tables(n):
    rows = n // GRID_W
    row = np.repeat(np.arange(rows, dtype=np.float64), GRID_W)
    col = np.tile(np.arange(GRID_W, dtype=np.float64), rows)
    inv_freq = ROPE_THETA ** (-np.arange(0, ROPE_AXIS_DIM, 2, dtype=np.float64) / ROPE_AXIS_DIM)

    def axis_tabs(pos):
        ang = pos[:, None] * inv_freq[None, :]
        cos, sin = np.cos(ang), np.sin(ang)
        return np.concatenate([cos, cos], axis=-1), np.concatenate([-sin, sin], axis=-1)

    cr, sr = axis_tabs(row)
    cc, sc = axis_tabs(col)
    cos = np.tile(np.concatenate([cr, cc], axis=-1), (1, LANES // HEAD_DIM))
    sin = np.tile(np.concatenate([sr, sc], axis=-1), (1, LANES // HEAD_DIM))
    return jnp.asarray(cos.astype(np.float32)), jnp.asarray(sin.astype(np.float32))


@jax.jit
def _layer(xp, xs, ck, cv, c, c_ctx, w_ada, b_ada, g_norm1, g_norm2, w_in, g_q, g_k, sink,
           w_f, w_ao, w_out, w_up, w_down):
    bp, sp, _ = xp.shape
    bs, ss, _ = xs.shape
    past = ck.shape[1]

    mod3, w_in_b = _ada(c, c_ctx[None, :], w_ada, b_ada[None, :], w_in)
    g1, g2 = g_norm1[None, :], g_norm2[None, :]
    gq2, gk2 = g_q[None, :], g_k[None, :]
    ctx_mod = (CTX_MOD_ROW, None)
    lat_mod = (0, ss)

    xp2 = xp.reshape(bp * sp, D_MODEL)
    xs2 = xs.reshape(bs * ss, D_MODEL)
    mix, q, kd, vd, gate, k_new, v_new, wup = _proj(
        xp2, mod3, g1, w_in_b, gq2, gk2, sp, None, *ctx_mod, True, [w_up])
    mix_s, q_s, kd_s, vd_s, gate_s, wf, wao, wout, wdn = _proj(
        xs2, mod3, g1, w_in_b, gq2, gk2, ss, _rope_tables(ss), *lat_mod, False,
        [w_f, w_ao, w_out, w_down])

    yp = _attn_tail(sink, q, kd, vd, None, xp2, mod3, mix, gate, g2,
                    wf, wao, wout, wup, wdn, *ctx_mod, sp)

    def cache_t(t):
        return t.transpose(0, 2, 3, 1).reshape(bs, KV_WIDTH, past)

    ys = _attn_tail(sink, q_s, kd_s, vd_s, (cache_t(ck), cache_t(cv)), xs2, mod3,
                    mix_s, gate_s, g2, wf, wao, wout, wup, wdn, *lat_mod, ss)

    def cache_layout(t):
        return t.reshape(bp, N_KV_HEADS, HEAD_DIM, sp).transpose(0, 3, 1, 2)

    return yp.reshape(xp.shape), ys.reshape(xs.shape), cache_layout(k_new), cache_layout(v_new)


def kernel(x_prompt, x_sample, cache_k, cache_v, c, c_ctx, w_ada, b_ada, g_norm1, g_norm2,
           w_in, g_q, g_k, sinks, w_f, w_ao, w_out, w_up, w_down):
    depth = w_in.shape[0]
    xp, xs = x_prompt, x_sample
    new_k, new_v = [], []
    for l in range(depth):
        xp, xs, k_ctx, v_ctx = _layer(
            xp, xs, cache_k[:, l], cache_v[:, l], c, c_ctx, w_ada[l], b_ada[l], g_norm1[l],
            g_norm2[l], w_in[l], g_q[l], g_k[l], sinks[l], w_f[l], w_ao[l], w_out[l], w_up[l],
            w_down[l])
        new_k.append(k_ctx)
        new_v.append(v_ctx)
    return (xp, xs, jnp.stack(new_k, axis=1), jnp.stack(new_v, axis=1))
```

```python
import functools

import numpy as np
import jax
import jax.numpy as jnp
from jax import lax
from jax.experimental import pallas as pl
from jax.experimental.pallas import tpu as pltpu

D_MODEL = 1024
GRID_W = 64
N_HEADS = 8
N_KV_HEADS = 2
GROUP = N_HEADS // N_KV_HEADS
HEAD_DIM = 64
WINDOW = 128
F_GROUPS = 4
F_GROUP_DIM = 128
F_WIDTH = F_GROUPS * F_GROUP_DIM
ATTN_WIDTH = N_HEADS * HEAD_DIM
KV_WIDTH = N_KV_HEADS * HEAD_DIM
ROPE_THETA = 10000.0
ROPE_AXIS_DIM = HEAD_DIM // 2
EPS = 1e-6
NEG_INF = -1e30
LOG2_E = 1.4426950408889634

LANES = 128
MXU_DIM = 256
VMEM_BYTES_V7X = 64 * 1024 * 1024

_Q0 = F_WIDTH
_K0 = _Q0 + ATTN_WIDTH
_V0 = _K0 + KV_WIDTH
_G0 = _V0 + KV_WIDTH
IN_WIDTH = _G0 + 2 * D_MODEL

SUBLANES = 8
N_MOD = 6
MOD_ROWS = 16
CTX_MOD_ROW = 8
ADA_STEPS = 8

PROJ_TOKEN_TILE = 1024
TAIL_TOKEN_TILE = 512
FF_CHUNK = MXU_DIM
GATE_COLUMNS = 2 * D_MODEL
CTX_SOFTMAX_ROWS = 128
LAT_SOFTMAX_ROWS = 64

F32 = jnp.float32
BF16 = jnp.bfloat16


RESIDENT_WEIGHTS_VMEM = VMEM_BYTES_V7X - 6 * 2 ** 20


def _dot(a, b):
    return jnp.dot(a, b, preferred_element_type=F32)


def _sigmoid(x):
    return 1.0 / (1.0 + jnp.exp(-x))


def _resident(shape):
    zeros = (0,) * len(shape)
    return pl.BlockSpec(shape, lambda *_: zeros, pipeline_mode=pl.Buffered(1))


def _hbm(x):
    try:
        return pltpu.with_memory_space_constraint(x, pltpu.HBM)
    except ValueError:
        return x


def _pipelined(units):
    stages = [units[0][0]]
    for k, (_, second) in enumerate(units):
        if k + 1 < len(units):
            stages.append(units[k + 1][0])
        stages.append(second)
    return stages


def _ada_kernel(c_ref, cctx_ref, w_ref, b_ref, win_ref, o_ref, winb_ref):
    lat_rows = c_ref.shape[0]
    pieces = [c_ref[...]]
    if lat_rows < CTX_MOD_ROW:
        pieces.append(jnp.zeros((CTX_MOD_ROW - lat_rows, D_MODEL), F32))
    pieces.append(jnp.broadcast_to(cctx_ref[...], (MOD_ROWS - CTX_MOD_ROW, D_MODEL)))
    c = jnp.concatenate(pieces, axis=0)
    s = c * _sigmoid(c)
    o_ref[...] = _dot(s.astype(BF16), w_ref[...].astype(BF16)) + b_ref[...]
    winb_ref[...] = win_ref[...].astype(BF16)


def _ada(c, c_ctx, w_ada, b_ada, w_in):
    assert c.shape[0] <= CTX_MOD_ROW, "latent conditioning rows must fit below the context row"
    n = w_ada.shape[1]
    bn = n // ADA_STEPS
    slab = pl.BlockSpec((w_in.shape[0] // ADA_STEPS, w_in.shape[1]), lambda j: (j, 0))
    return pl.pallas_call(
        _ada_kernel,
        grid=(ADA_STEPS,),
        in_specs=[pl.BlockSpec(c.shape, lambda j: (0, 0)),
                  pl.BlockSpec((1, D_MODEL), lambda j: (0, 0)),
                  pl.BlockSpec((D_MODEL, bn), lambda j: (0, j)),
                  pl.BlockSpec((1, bn), lambda j: (0, j)),
                  slab],
        out_specs=[pl.BlockSpec((MOD_ROWS, bn), lambda j: (0, j)), slab],
        out_shape=[pltpu.HBM((MOD_ROWS, n), F32), pltpu.HBM(w_in.shape, BF16)],
        name="ada",
    )(c, c_ctx, w_ada, b_ada, w_in)


def _mod_spec(mod_row):
    return pl.BlockSpec((SUBLANES, N_MOD * D_MODEL), lambda i: (mod_row // SUBLANES, 0))


def _mod_vectors(mod_ref, tile, tiles_per_row):
    cols = [slice(j * D_MODEL, (j + 1) * D_MODEL) for j in range(N_MOD)]
    if tiles_per_row is None:
        return [mod_ref[0:1, c] for c in cols]
    r = tile // tiles_per_row
    return [mod_ref[pl.ds(r, 1), c] for c in cols]


def _head_norm(z, g):
    lo = lax.broadcasted_iota(jnp.int32, z.shape, 1) < HEAD_DIM
    s = z * z
    s_lo = jnp.sum(jnp.where(lo, s, 0.0), axis=-1, keepdims=True)
    s_hi = jnp.sum(jnp.where(lo, 0.0, s), axis=-1, keepdims=True)
    ms = jnp.where(lo, s_lo, s_hi) * (1.0 / HEAD_DIM)
    return z * lax.rsqrt(ms + EPS) * g


def _rope(y, cos, sin):
    half = ROPE_AXIS_DIM // 2
    lane = lax.broadcasted_iota(jnp.int32, y.shape, 1)
    first = (lane % ROPE_AXIS_DIM) < half
    partner = jnp.where(first, pltpu.roll(y, LANES - half, 1), pltpu.roll(y, half, 1))
    return y * cos + partner * sin


def _dup_heads(y):
    lo = lax.broadcasted_iota(jnp.int32, y.shape, 1) < HEAD_DIM
    sw = pltpu.roll(y, HEAD_DIM, 1)
    return jnp.concatenate([jnp.where(lo, y, sw), jnp.where(lo, sw, y)], axis=-1)


def _proj_kernel(*refs, rope, seq_len, fold, cache, n_cast, gate_chunk, tiles_per_mod_row):
    x_ref, mod_ref, g1_ref, w_ref, gq_ref, gk_ref = refs[:6]
    refs = refs[6:]
    if rope:
        cos_ref, sin_ref = refs[:2]
        refs = refs[2:]
    n_tabs = 4 if fold else 2
    tabs, refs = refs[:n_tabs], refs[n_tabs:]
    cast_in, refs = refs[:n_cast], refs[n_cast:]
    mix_ref, q_ref, kd_ref, vd_ref, gate_ref = refs[:5]
    refs = refs[5:]
    if cache:
        kc_ref, vc_ref = refs[:2]
        refs = refs[2:]
    cast_out, (zf_ref, ab_ref) = refs[:n_cast], refs[n_cast:]
    for src_ref, dst_ref in zip(cast_in, cast_out):
        dst_ref[...] = src_ref[...].astype(BF16)

    x = x_ref[...]
    sh1, sc1 = _mod_vectors(mod_ref, pl.program_id(0), tiles_per_mod_row)[0:2]
    h = x * lax.rsqrt(jnp.mean(x * x, axis=-1, keepdims=True) + EPS) * g1_ref[...]
    hb = (h * (1.0 + sc1) + sh1).astype(BF16)

    zf_ref[...] = _dot(hb, w_ref[:, 0:_Q0]).astype(BF16)
    mixer = _fourier_stages(zf_ref, tabs, mix_ref, ab_ref, seq_len, fold)
    n_sections = 2 + 2 * D_MODEL // gate_chunk
    issued = [0]

    def run_mixer_share(section):
        while issued[0] < len(mixer) and issued[0] * n_sections < (section + 1) * len(mixer):
            mixer[issued[0]]()
            issued[0] += 1

    gq = jnp.concatenate([gq_ref[...]] * (LANES // HEAD_DIM), axis=-1)
    gk = jnp.concatenate([gk_ref[...]] * (LANES // HEAD_DIM), axis=-1)

    zq = _dot(hb, w_ref[:, _Q0:_K0])
    for j in range(ATTN_WIDTH // LANES):
        y = _head_norm(zq[:, j * LANES:(j + 1) * LANES], gq)
        if rope:
            y = _rope(y, cos_ref[...], sin_ref[...])
        q_ref[:, j * LANES:(j + 1) * LANES] = (y * (HEAD_DIM ** -0.5 * LOG2_E)).astype(BF16)
    run_mixer_share(0)

    zkv = _dot(hb, w_ref[:, _K0:_G0])
    k = _head_norm(zkv[:, 0:KV_WIDTH], gk)
    v = zkv[:, KV_WIDTH:2 * KV_WIDTH]
    if cache:
        for s in range(x.shape[0] // seq_len):
            rows = slice(s * seq_len, (s + 1) * seq_len)
            kc_ref[s] = k[rows, :].T
            vc_ref[s] = v[rows, :].T
    if rope:
        k = _rope(k, cos_ref[...], sin_ref[...])
    kt = k.T
    kd_ref[...] = jnp.concatenate(
        [kt[0:HEAD_DIM], kt[0:HEAD_DIM], kt[HEAD_DIM:], kt[HEAD_DIM:]], axis=0).astype(BF16)
    vd_ref[...] = _dup_heads(v).astype(BF16)
    run_mixer_share(1)

    for j in range(2 * D_MODEL // gate_chunk):
        cols = slice(j * gate_chunk, (j + 1) * gate_chunk)
        zg = _dot(hb, w_ref[:, _G0 + cols.start:_G0 + cols.stop])
        gate_ref[:, cols] = _sigmoid(zg).astype(BF16)
        run_mixer_share(2 + j)


def _proj(x, mod3, g1, w_in_b, gq2, gk2, seq_len, rope_tabs, mod_row, rows_per_mod_row, cache,
          cast_weights):
    t = x.shape[0]
    tm = PROJ_TOKEN_TILE
    assert tm % seq_len == 0, "a projection tile must hold whole sequences for the mixer"
    steps = t // tm
    rope = rope_tabs is not None
    row = lambda i: (i, 0)
    tiles_per_mod_row = None if rows_per_mod_row is None else rows_per_mod_row // tm
    in_specs = [pl.BlockSpec((tm, D_MODEL), row),
                _mod_spec(mod_row),
                _resident((1, D_MODEL)),
                _resident((D_MODEL, IN_WIDTH)),
                _resident((1, HEAD_DIM)),
                _resident((1, HEAD_DIM))]
    args = [x, mod3, g1, w_in_b, gq2, gk2]
    if rope:
        tiles_per_seq = rope_tabs[0].shape[0] // tm
        tab = pl.BlockSpec((tm, LANES), lambda i: (i % tiles_per_seq, 0))
        in_specs += [tab, tab]
        args += list(rope_tabs)
    fold = _fold_dft(seq_len)
    gate_chunk = GATE_COLUMNS // 2 if fold else GATE_COLUMNS
    dft = _dft_tables(seq_len, fold)
    in_specs += [_resident(tab.shape) for tab in dft]
    args += list(dft)
    slabs = [pl.BlockSpec((w.shape[0] // steps, w.shape[1]), row) for w in cast_weights]
    in_specs += slabs
    args += list(cast_weights)
    out_specs = [pl.BlockSpec((tm, F_WIDTH), row),
                 pl.BlockSpec((tm, ATTN_WIDTH), row),
                 pl.BlockSpec((2 * KV_WIDTH, tm), lambda i: (0, i)),
                 pl.BlockSpec((tm, 2 * KV_WIDTH), row),
                 pl.BlockSpec((tm, 2 * D_MODEL), row)]
    out_shape = [pltpu.HBM((t, F_WIDTH), BF16),
                 pltpu.HBM((t, ATTN_WIDTH), BF16),
                 pltpu.HBM((2 * KV_WIDTH, t), BF16),
                 pltpu.HBM((t, 2 * KV_WIDTH), BF16),
                 pltpu.HBM((t, 2 * D_MODEL), BF16)]
    if cache:
        seqs = tm // seq_len
        out_specs += [pl.BlockSpec((seqs, KV_WIDTH, seq_len), lambda i: (i, 0, 0))] * 2
        out_shape += [pltpu.HBM((t // seq_len, KV_WIDTH, seq_len), F32)] * 2
    out_specs += slabs
    out_shape += [pltpu.HBM(w.shape, BF16) for w in cast_weights]
    return pl.pallas_call(
        functools.partial(_proj_kernel, rope=rope, seq_len=seq_len, fold=fold, cache=cache,
                          n_cast=len(cast_weights), gate_chunk=gate_chunk,
                          tiles_per_mod_row=tiles_per_mod_row),
        grid=(steps,),
        in_specs=in_specs,
        out_specs=out_specs,
        out_shape=out_shape,
        scratch_shapes=[pltpu.VMEM((tm, F_WIDTH), BF16),
                        pltpu.VMEM((2, 2 * seq_len, F_WIDTH), BF16)],
        compiler_params=pltpu.CompilerParams(
            dimension_semantics=("arbitrary",), vmem_limit_bytes=RESIDENT_WEIGHTS_VMEM),
        name="proj_lat" if rope else "proj_ctx",
    )(*[_hbm(a) for a in args])


def _dft_tables(n, fold):
    def cs(m):
        idx = np.arange(m, dtype=np.int64)
        ang = 2.0 * np.pi * ((idx[:, None] * idx[None, :]) % m).astype(np.float64) / m
        return np.cos(ang) / np.sqrt(m), np.sin(ang) / np.sqrt(m)
    cd, sd = cs(F_GROUP_DIM)
    cn, sn = cs(n)
    tabs = [np.concatenate([cd, sd], axis=1)]
    if fold:
        half = n // 2
        rev = np.zeros((half, half), np.float64)
        rev[np.arange(1, half), half - np.arange(1, half)] = 1.0
        tabs += [cn[:half], -sn[:half], rev]
    else:
        tabs += [np.concatenate([cn, -sn], axis=1)]
    return tuple(jnp.asarray(t.astype(np.float32)).astype(BF16) for t in tabs)


def _fold_dft(n):
    return n // 2 >= 2 * MXU_DIM


def _fourier_stages(zf_ref, tabs, o_ref, ab_ref, n, fold):
    csd_ref = tabs[0]
    if fold:
        ch_ref, sh_ref, rev_ref = tabs[1:]
    else:
        (csn_ref,) = tabs[1:]
    half = n // 2
    units = []
    for s in range(zf_ref.shape[0] // n):
        buf = s % 2
        r0 = s * n

        def channels(r0=r0, buf=buf):
            for g in range(F_GROUPS):
                cols = slice(g * F_GROUP_DIM, (g + 1) * F_GROUP_DIM)
                ab = _dot(zf_ref[r0:r0 + n, cols], csd_ref[...])
                ab_ref[buf, 0:n, cols] = ab[:, 0:F_GROUP_DIM].astype(BF16)
                ab_ref[buf, n:2 * n, cols] = ab[:, F_GROUP_DIM:].astype(BF16)

        def positions(r0=r0, buf=buf):
            if not fold:
                o_ref[r0:r0 + n, :] = _dot(csn_ref[...], ab_ref[buf]).astype(BF16)
                return
            p = _dot(ch_ref[...], ab_ref[buf, 0:n, :])
            q = _dot(sh_ref[...], ab_ref[buf, n:2 * n, :])
            o_ref[r0:r0 + half, :] = (p + q).astype(BF16)
            upper = _dot(rev_ref[...], (p - q).astype(BF16))
            a = ab_ref[buf, 0:n, :].astype(F32)
            even = lax.broadcasted_iota(jnp.int32, a.shape, 0) % 2 == 0
            mid = jnp.sum(jnp.where(even, a, -a), axis=0, keepdims=True) * (n ** -0.5)
            first = lax.broadcasted_iota(jnp.int32, upper.shape, 0) == 0
            o_ref[r0 + half:r0 + n, :] = jnp.where(first, mid, upper).astype(BF16)

        units.append((channels, positions))
    return _pipelined(units)


def _group_scores(q_stack, keys, s_ref):
    kt = keys[0] if len(keys) == 1 else jnp.concatenate(keys, axis=1)
    top = lax.broadcasted_iota(jnp.int32, kt.shape, 0) < HEAD_DIM
    zero = jnp.zeros_like(kt)
    s_ref[...] = _dot(q_stack, jnp.concatenate(
        [jnp.where(top, kt, zero), jnp.where(top, zero, kt)], axis=1))


def _group_softmax_pv(values, sinks, band, s_ref, p_ref, sk_ref, rows_per_pair, chunk):
    rows = s_ref.shape[0]
    keys = s_ref.shape[1] // 2
    bounds = [0]
    for vd in values:
        bounds.append(bounds[-1] + vd.shape[0])
    lo = lax.broadcasted_iota(jnp.int32, (chunk, LANES), 1) < HEAD_DIM

    for r0 in range(0, rows, chunk):
        rs = slice(r0, r0 + chunk)
        sink_terms = []
        for half in range(2):
            sink = sinks[2 * (r0 // rows_per_pair) + half]
            cols = [slice(half * keys + bounds[i], half * keys + bounds[i + 1])
                    for i in range(len(values))]
            parts = [s_ref[rs, c] for c in cols]
            if band is not None:
                parts[0] = jnp.where(band(r0 % rows_per_pair, chunk), parts[0], NEG_INF)
            m = sink
            for s in parts:
                m = jnp.maximum(m, jnp.max(s, axis=-1, keepdims=True))
            for c, s in zip(cols, parts):
                p_ref[rs, c] = jnp.exp2(s - m).astype(BF16)
            sink_terms.append(jnp.exp2(sink - m))
        sk_ref[rs, :] = jnp.where(lo, sink_terms[0], sink_terms[1])

    vd = values[0] if len(values) == 1 else jnp.concatenate(values, axis=0)
    v_lo = lax.broadcasted_iota(jnp.int32, vd.shape, 1) < HEAD_DIM
    zero = jnp.zeros_like(vd)
    ones_lo = jnp.where(v_lo, 1.0, 0.0).astype(BF16)
    ones_hi = jnp.where(v_lo, 0.0, 1.0).astype(BF16)
    operand = jnp.concatenate(
        [jnp.concatenate([jnp.where(v_lo, vd, zero), ones_lo], axis=1),
         jnp.concatenate([jnp.where(v_lo, zero, vd), ones_hi], axis=1)], axis=0)
    out = _dot(p_ref[...], operand)
    return out[:, 0:LANES] / (out[:, LANES:2 * LANES] + sk_ref[...])


def _pair_cols(g):
    pa, pb = 2 * g, 2 * g + 1
    return slice(pa * LANES, (pa + 1) * LANES), slice(pb * LANES, (pb + 1) * LANES)


def _attn_tail_kernel(*refs, lat, n_tiles, tm, n, tiles_per_mod_row):
    sink_ref, q_ref, kd_ref, vd_ref = refs[:4]
    refs = refs[4:]
    if lat:
        ck_ref, cv_ref = refs[:2]
        refs = refs[2:]
    (x_ref, mod_ref, mix_ref, gate_ref, g2_ref,
     wf_ref, wao_ref, wout_ref, wup_ref, wdn_ref, o_ref, att_ref, s_ref, p_ref, sk_ref) = refs[:15]
    if lat:
        dist_ref, ckt_ref, cvd_ref = refs[15:18]
    act_ref = refs[-1]
    i = pl.program_id(0)

    def attention_stages():
        units = []
        if lat:
            qb = WINDOW
            span = qb + 2 * WINDOW
            t = jnp.minimum(i, n_tiles - 1) % (n // tm)
            ckt = ck_ref[0]
            ckt_ref[...] = jnp.concatenate(
                [ckt[0:HEAD_DIM], ckt[0:HEAD_DIM], ckt[HEAD_DIM:], ckt[HEAD_DIM:]], axis=0).astype(BF16)
            cvd_ref[...] = _dup_heads(cv_ref[0].T).astype(BF16)
            for rr in range(tm // qb):
                q0 = t * tm + rr * qb
                start = pl.multiple_of(jnp.clip(q0 - WINDOW, 0, n - span), WINDOW)
                dist_ref[rr] = jnp.abs((q0 - start) + lax.broadcasted_iota(jnp.int32, (qb, span), 0)
                                       - lax.broadcasted_iota(jnp.int32, (qb, span), 1))
                band = lambda r0, rows, rr=rr: dist_ref[rr, r0:r0 + rows, :] <= WINDOW
                rows = slice(rr * qb, (rr + 1) * qb)
                for g in range(N_KV_HEADS):
                    buf = len(units) % 2
                    cols = slice(g * LANES, (g + 1) * LANES)
                    qa, qbc = _pair_cols(g)

                    def scores(rows=rows, start=start, cols=cols, qa=qa, qbc=qbc, buf=buf):
                        q_stack = jnp.concatenate([q_ref[rows, qa], q_ref[rows, qbc]], axis=0)
                        _group_scores(q_stack, [kd_ref[cols, pl.ds(start, span)], ckt_ref[cols, :]],
                                      s_ref.at[buf])

                    def finish(rows=rows, start=start, g=g, cols=cols, qa=qa, qbc=qbc, buf=buf,
                               band=band):
                        sinks = [sink_ref[GROUP * g + h] * LOG2_E for h in range(GROUP)]
                        o = _group_softmax_pv(
                            [vd_ref[pl.ds(start, span), cols], cvd_ref[:, cols]], sinks, band,
                            s_ref.at[buf], p_ref.at[buf], sk_ref.at[buf], qb, LAT_SOFTMAX_ROWS)
                        att_ref[rows, qa] = o[0:qb].astype(BF16)
                        att_ref[rows, qbc] = o[qb:2 * qb].astype(BF16)

                    units.append((scores, finish))
        else:
            for s in range(tm // n):
                rows = slice(s * n, (s + 1) * n)
                for g in range(N_KV_HEADS):
                    buf = len(units) % 2
                    cols = slice(g * LANES, (g + 1) * LANES)
                    qa, qbc = _pair_cols(g)

                    def scores(rows=rows, cols=cols, qa=qa, qbc=qbc, buf=buf):
                        q_stack = jnp.concatenate([q_ref[rows, qa], q_ref[rows, qbc]], axis=0)
                        _group_scores(q_stack, [kd_ref[cols, rows]], s_ref.at[buf])

                    def finish(rows=rows, g=g, cols=cols, qa=qa, qbc=qbc, buf=buf):
                        sinks = [sink_ref[GROUP * g + h] * LOG2_E for h in range(GROUP)]
                        o = _group_softmax_pv([vd_ref[rows, cols]], sinks, None, s_ref.at[buf],
                                              p_ref.at[buf], sk_ref.at[buf], n, CTX_SOFTMAX_ROWS)
                        att_ref[rows, qa] = o[0:n].astype(BF16)
                        att_ref[rows, qbc] = o[n:2 * n].astype(BF16)

                    units.append((scores, finish))
        return _pipelined(units)

    def tail(stages):
        gt1, sh2, sc2, gt2 = _mod_vectors(mod_ref, i - 1, tiles_per_mod_row)[2:6]
        x1_parts, hb_parts = [], []
        for r0 in range(0, tm, tm // 2):
            rs = slice(r0, r0 + tm // 2)
            yf = _dot(mix_ref[rs, :], wf_ref[...])
            ya = _dot(att_ref[rs, :], wao_ref[...])
            merged = (gate_ref[rs, 0:D_MODEL].astype(F32) * yf
                      + gate_ref[rs, D_MODEL:2 * D_MODEL].astype(F32) * ya)
            x1_h = x_ref[rs, :] + gt1 * _dot(merged.astype(BF16), wout_ref[...])
            h = (x1_h * lax.rsqrt(jnp.mean(x1_h * x1_h, axis=-1, keepdims=True) + EPS)
                 * g2_ref[...])
            x1_parts.append(x1_h)
            hb_parts.append((h * (1.0 + sc2) + sh2).astype(BF16))
        x1 = jnp.concatenate(x1_parts, axis=0)
        hb = jnp.concatenate(hb_parts, axis=0)
        d_ff = wdn_ref.shape[0]
        n_chunks = d_ff // FF_CHUNK
        issued = 0
        for c in range(n_chunks):
            cols = slice(c * FF_CHUNK, (c + 1) * FF_CHUNK)
            a = _dot(hb, wup_ref[:, cols])
            u = _dot(hb, wup_ref[:, d_ff + cols.start:d_ff + cols.stop])
            act_ref[:, cols] = (a * _sigmoid(a) * u).astype(BF16)
            while issued < len(stages) and issued * n_chunks < (c + 1) * len(stages):
                stages[issued]()
                issued += 1
        for k, r0 in enumerate(range(0, tm, tm // 2)):
            rs = slice(r0, r0 + tm // 2)
            o_ref[rs, :] = x1_parts[k] + gt2 * _dot(act_ref[rs, :], wdn_ref[...])

    @pl.when(i == 0)
    def _():
        for stage in attention_stages():
            stage()

    @pl.when(jnp.logical_and(i > 0, i < n_tiles))
    def _():
        tail(attention_stages())

    @pl.when(i == n_tiles)
    def _():
        tail([])


def _attn_tail(sinks, q, kd, vd, cache, x, mod3, mix, gate, g2, wf, wao, wout, wup, wdn,
               mod_row, rows_per_mod_row, n):
    t = x.shape[0]
    tm = TAIL_TOKEN_TILE
    n_tiles = t // tm
    lat = cache is not None
    tiles_per_mod_row = None if rows_per_mod_row is None else rows_per_mod_row // tm
    front = lambda i: (jnp.minimum(i, n_tiles - 1), 0)
    back = lambda i: (jnp.maximum(i - 1, 0), 0)
    in_specs = [pl.BlockSpec(memory_space=pltpu.SMEM), pl.BlockSpec((tm, ATTN_WIDTH), front)]
    args = [sinks, q, kd, vd]
    if lat:
        past = cache[0].shape[2]
        seq_of = lambda i: jnp.minimum(i, n_tiles - 1) // (n // tm)
        in_specs += [pl.BlockSpec((2 * KV_WIDTH, n), lambda i: (0, seq_of(i))),
                     pl.BlockSpec((n, 2 * KV_WIDTH), lambda i: (seq_of(i), 0))]
        in_specs += [pl.BlockSpec((1, KV_WIDTH, past), lambda i: (seq_of(i), 0, 0))] * 2
        args += list(cache)
        m_rows, keys = 2 * WINDOW, 3 * WINDOW + past
    else:
        in_specs += [pl.BlockSpec((2 * KV_WIDTH, tm), lambda i: (0, front(i)[0])),
                     pl.BlockSpec((tm, 2 * KV_WIDTH), front)]
        m_rows, keys = 2 * n, n
    in_specs += [pl.BlockSpec((tm, D_MODEL), back),
                 _mod_spec(mod_row),
                 pl.BlockSpec((tm, F_WIDTH), back),
                 pl.BlockSpec((tm, 2 * D_MODEL), back),
                 _resident((1, D_MODEL)),
                 _resident(wf.shape), _resident(wao.shape), _resident(wout.shape),
                 _resident(wup.shape), _resident(wdn.shape)]
    args += [x, mod3, mix, gate, g2, wf, wao, wout, wup, wdn]
    scratch = [pltpu.VMEM((tm, ATTN_WIDTH), BF16),
               pltpu.VMEM((2, m_rows, 2 * keys), F32),
               pltpu.VMEM((2, m_rows, 2 * keys), BF16),
               pltpu.VMEM((2, m_rows, LANES), F32)]
    if lat:
        scratch += [pltpu.VMEM((tm // WINDOW, WINDOW, 3 * WINDOW), jnp.int32),
                    pltpu.VMEM((2 * KV_WIDTH, past), BF16),
                    pltpu.VMEM((past, 2 * KV_WIDTH), BF16)]
    scratch += [pltpu.VMEM((tm, wdn.shape[0]), BF16)]
    return pl.pallas_call(
        functools.partial(_attn_tail_kernel, lat=lat, n_tiles=n_tiles, tm=tm, n=n,
                          tiles_per_mod_row=tiles_per_mod_row),
        grid=(n_tiles + 1,),
        in_specs=in_specs,
        out_specs=pl.BlockSpec((tm, D_MODEL), back),
        out_shape=pltpu.HBM((t, D_MODEL), F32),
        scratch_shapes=scratch,
        compiler_params=pltpu.CompilerParams(
            dimension_semantics=("arbitrary",), vmem_limit_bytes=RESIDENT_WEIGHTS_VMEM),
        name="attn_tail_lat" if lat else "attn_tail_ctx",
    )(sinks, *[_hbm(a) for a in args[1:]])


def _rope_tables(n):
    rows = n // GRID_W
    row = np.repeat(np.arange(rows, dtype=np.float64), GRID_W)
    col = np.tile(np.arange(GRID_W, dtype=np.float64), rows)
    inv_freq = ROPE_THETA ** (-np.arange(0, ROPE_AXIS_DIM, 2, dtype=np.float64) / ROPE_AXIS_DIM)

    def axis_tabs(pos):
        ang = pos[:, None] * inv_freq[None, :]
        cos, sin = np.cos(ang), np.sin(ang)
        return np.concatenate([cos, cos], axis=-1), np.concatenate([-sin, sin], axis=-1)

    cr, sr = axis_tabs(row)
    cc, sc = axis_tabs(col)
    cos = np.tile(np.concatenate([cr, cc], axis=-1), (1, LANES // HEAD_DIM))
    sin = np.tile(np.concatenate([sr, sc], axis=-1), (1, LANES // HEAD_DIM))
    return jnp.asarray(cos.astype(np.float32)), jnp.asarray(sin.astype(np.float32))


@jax.jit
def _layer(xp, xs, ck, cv, c, c_ctx, w_ada, b_ada, g_norm1, g_norm2, w_in, g_q, g_k, sink,
           w_f, w_ao, w_out, w_up, w_down):
    bp, sp, _ = xp.shape
    bs, ss, _ = xs.shape
    past = ck.shape[1]

    mod3, w_in_b = _ada(c, c_ctx[None, :], w_ada, b_ada[None, :], w_in)
    g1, g2 = g_norm1[None, :], g_norm2[None, :]
    gq2, gk2 = g_q[None, :], g_k[None, :]
    ctx_mod = (CTX_MOD_ROW, None)
    lat_mod = (0, ss)

    xp2 = xp.reshape(bp * sp, D_MODEL)
    xs2 = xs.reshape(bs * ss, D_MODEL)
    mix, q, kd, vd, gate, k_new, v_new, wup = _proj(
        xp2, mod3, g1, w_in_b, gq2, gk2, sp, None, *ctx_mod, True, [w_up])
    mix_s, q_s, kd_s, vd_s, gate_s, wf, wao, wout, wdn = _proj(
        xs2, mod3, g1, w_in_b, gq2, gk2, ss, _rope_tables(ss), *lat_mod, False,
        [w_f, w_ao, w_out, w_down])

    yp = _attn_tail(sink, q, kd, vd, None, xp2, mod3, mix, gate, g2,
                    wf, wao, wout, wup, wdn, *ctx_mod, sp)

    def cache_t(t):
        return t.transpose(0, 2, 3, 1).reshape(bs, KV_WIDTH, past)

    ys = _attn_tail(sink, q_s, kd_s, vd_s, (cache_t(ck), cache_t(cv)), xs2, mod3,
                    mix_s, gate_s, g2, wf, wao, wout, wup, wdn, *lat_mod, ss)

    def cache_layout(t):
        return t.reshape(bp, N_KV_HEADS, HEAD_DIM, sp).transpose(0, 3, 1, 2)

    return yp.reshape(xp.shape), ys.reshape(xs.shape), cache_layout(k_new), cache_layout(v_new)


def kernel(x_prompt, x_sample, cache_k, cache_v, c, c_ctx, w_ada, b_ada, g_norm1, g_norm2,
           w_in, g_q, g_k, sinks, w_f, w_ao, w_out, w_up, w_down):
    depth = w_in.shape[0]
    xp, xs = x_prompt, x_sample
    new_k, new_v = [], []
    for l in range(depth):
        xp, xs, k_ctx, v_ctx = _layer(
            xp, xs, cache_k[:, l], cache_v[:, l], c, c_ctx, w_ada[l], b_ada[l], g_norm1[l],
            g_norm2[l], w_in[l], g_q[l], g_k[l], sinks[l], w_f[l], w_ao[l], w_out[l], w_up[l],
            w_down[l])
        new_k.append(k_ctx)
        new_v.append(v_ctx)
    return (xp, xs, jnp.stack(new_k, axis=1), jnp.stack(new_v, axis=1))
```

```python
import functools

import numpy as np
import jax
import jax.numpy as jnp
from jax import lax
from jax.experimental import pallas as pl
from jax.experimental.pallas import tpu as pltpu

D_MODEL = 1024
GRID_W = 64
N_HEADS = 8
N_KV_HEADS = 2
GROUP = N_HEADS // N_KV_HEADS
HEAD_DIM = 64
WINDOW = 128
F_GROUPS = 4
F_GROUP_DIM = 128
F_WIDTH = F_GROUPS * F_GROUP_DIM
ATTN_WIDTH = N_HEADS * HEAD_DIM
KV_WIDTH = N_KV_HEADS * HEAD_DIM
ROPE_THETA = 10000.0
ROPE_AXIS_DIM = HEAD_DIM // 2
EPS = 1e-6
NEG_INF = -1e30
LOG2_E = 1.4426950408889634

LANES = 128
MXU_DIM = 256
VMEM_BYTES_V7X = 64 * 1024 * 1024

_Q0 = F_WIDTH
_K0 = _Q0 + ATTN_WIDTH
_V0 = _K0 + KV_WIDTH
_G0 = _V0 + KV_WIDTH
IN_WIDTH = _G0 + 2 * D_MODEL

SUBLANES = 8
N_MOD = 6
MOD_ROWS = 16
CTX_MOD_ROW = 8
ADA_STEPS = 8

PROJ_TOKEN_TILE = 1024
TAIL_TOKEN_TILE = 512
FF_CHUNK = MXU_DIM
GATE_COLUMNS = 2 * D_MODEL
CTX_SOFTMAX_ROWS = 128
LAT_SOFTMAX_ROWS = 64

F32 = jnp.float32
BF16 = jnp.bfloat16


RESIDENT_WEIGHTS_VMEM = VMEM_BYTES_V7X - 6 * 2 ** 20


def _dot(a, b):
    return jnp.dot(a, b, preferred_element_type=F32)


def _sigmoid(x):
    return 1.0 / (1.0 + jnp.exp(-x))


def _resident(shape):
    zeros = (0,) * len(shape)
    return pl.BlockSpec(shape, lambda *_: zeros, pipeline_mode=pl.Buffered(1))


def _hbm(x):
    try:
        return pltpu.with_memory_space_constraint(x, pltpu.HBM)
    except ValueError:
        return x


def _pipelined(units):
    stages = [units[0][0]]
    for k, (_, second) in enumerate(units):
        if k + 1 < len(units):
            stages.append(units[k + 1][0])
        stages.append(second)
    return stages


def _ada_kernel(c_ref, cctx_ref, w_ref, b_ref, win_ref, o_ref, winb_ref):
    lat_rows = c_ref.shape[0]
    pieces = [c_ref[...]]
    if lat_rows < CTX_MOD_ROW:
        pieces.append(jnp.zeros((CTX_MOD_ROW - lat_rows, D_MODEL), F32))
    pieces.append(jnp.broadcast_to(cctx_ref[...], (MOD_ROWS - CTX_MOD_ROW, D_MODEL)))
    c = jnp.concatenate(pieces, axis=0)
    s = c * _sigmoid(c)
    o_ref[...] = _dot(s.astype(BF16), w_ref[...].astype(BF16)) + b_ref[...]
    winb_ref[...] = win_ref[...].astype(BF16)


def _ada(c, c_ctx, w_ada, b_ada, w_in):
    assert c.shape[0] <= CTX_MOD_ROW, "latent conditioning rows must fit below the context row"
    n = w_ada.shape[1]
    bn = n // ADA_STEPS
    slab = pl.BlockSpec((w_in.shape[0] // ADA_STEPS, w_in.shape[1]), lambda j: (j, 0))
    return pl.pallas_call(
        _ada_kernel,
        grid=(ADA_STEPS,),
        in_specs=[pl.BlockSpec(c.shape, lambda j: (0, 0)),
                  pl.BlockSpec((1, D_MODEL), lambda j: (0, 0)),
                  pl.BlockSpec((D_MODEL, bn), lambda j: (0, j)),
                  pl.BlockSpec((1, bn), lambda j: (0, j)),
                  slab],
        out_specs=[pl.BlockSpec((MOD_ROWS, bn), lambda j: (0, j)), slab],
        out_shape=[pltpu.HBM((MOD_ROWS, n), F32), pltpu.HBM(w_in.shape, BF16)],
        name="ada",
    )(c, c_ctx, w_ada, b_ada, w_in)


def _mod_spec(mod_row):
    return pl.BlockSpec((SUBLANES, N_MOD * D_MODEL), lambda i: (mod_row // SUBLANES, 0))


def _mod_vectors(mod_ref, tile, tiles_per_row):
    cols = [slice(j * D_MODEL, (j + 1) * D_MODEL) for j in range(N_MOD)]
    if tiles_per_row is None:
        return [mod_ref[0:1, c] for c in cols]
    r = tile // tiles_per_row
    return [mod_ref[pl.ds(r, 1), c] for c in cols]


def _head_norm(z, g):
    lo = lax.broadcasted_iota(jnp.int32, z.shape, 1) < HEAD_DIM
    s = z * z
    s_lo = jnp.sum(jnp.where(lo, s, 0.0), axis=-1, keepdims=True)
    s_hi = jnp.sum(jnp.where(lo, 0.0, s), axis=-1, keepdims=True)
    ms = jnp.where(lo, s_lo, s_hi) * (1.0 / HEAD_DIM)
    return z * lax.rsqrt(ms + EPS) * g


def _rope(y, cos, sin):
    half = ROPE_AXIS_DIM // 2
    lane = lax.broadcasted_iota(jnp.int32, y.shape, 1)
    first = (lane % ROPE_AXIS_DIM) < half
    partner = jnp.where(first, pltpu.roll(y, LANES - half, 1), pltpu.roll(y, half, 1))
    return y * cos + partner * sin


def _dup_heads(y):
    lo = lax.broadcasted_iota(jnp.int32, y.shape, 1) < HEAD_DIM
    sw = pltpu.roll(y, HEAD_DIM, 1)
    return jnp.concatenate([jnp.where(lo, y, sw), jnp.where(lo, sw, y)], axis=-1)


def _proj_kernel(*refs, rope, seq_len, fold, cache, n_cast, gate_chunk, tiles_per_mod_row):
    x_ref, mod_ref, g1_ref, w_ref, gq_ref, gk_ref = refs[:6]
    refs = refs[6:]
    if rope:
        cos_ref, sin_ref = refs[:2]
        refs = refs[2:]
    n_tabs = 4 if fold else 2
    tabs, refs = refs[:n_tabs], refs[n_tabs:]
    cast_in, refs = refs[:n_cast], refs[n_cast:]
    mix_ref, q_ref, kd_ref, vd_ref, gate_ref = refs[:5]
    refs = refs[5:]
    if cache:
        kc_ref, vc_ref = refs[:2]
        refs = refs[2:]
    cast_out, (zf_ref, ab_ref) = refs[:n_cast], refs[n_cast:]
    for src_ref, dst_ref in zip(cast_in, cast_out):
        dst_ref[...] = src_ref[...].astype(BF16)

    sh1, sc1 = _mod_vectors(mod_ref, pl.program_id(0), tiles_per_mod_row)[0:2]
    rows_total = x_ref.shape[0]
    hb_parts = []
    for r0 in range(0, rows_total, rows_total // 2):
        rs = slice(r0, r0 + rows_total // 2)
        x = x_ref[rs, :]
        h = x * lax.rsqrt(jnp.mean(x * x, axis=-1, keepdims=True) + EPS) * g1_ref[...]
        hb_parts.append((h * (1.0 + sc1) + sh1).astype(BF16))
        zf_ref[rs, :] = _dot(hb_parts[-1], w_ref[:, 0:_Q0]).astype(BF16)
    hb = jnp.concatenate(hb_parts, axis=0)
    mixer = _fourier_stages(zf_ref, tabs, mix_ref, ab_ref, seq_len, fold)
    n_sections = 2 + 2 * D_MODEL // gate_chunk
    issued = [0]

    def run_mixer_share(section):
        while issued[0] < len(mixer) and issued[0] * n_sections < (section + 1) * len(mixer):
            mixer[issued[0]]()
            issued[0] += 1

    gq = jnp.concatenate([gq_ref[...]] * (LANES // HEAD_DIM), axis=-1)
    gk = jnp.concatenate([gk_ref[...]] * (LANES // HEAD_DIM), axis=-1)

    zq = _dot(hb, w_ref[:, _Q0:_K0])
    for j in range(ATTN_WIDTH // LANES):
        y = _head_norm(zq[:, j * LANES:(j + 1) * LANES], gq)
        if rope:
            y = _rope(y, cos_ref[...], sin_ref[...])
        q_ref[:, j * LANES:(j + 1) * LANES] = (y * (HEAD_DIM ** -0.5 * LOG2_E)).astype(BF16)
    run_mixer_share(0)

    zkv = _dot(hb, w_ref[:, _K0:_G0])
    k = _head_norm(zkv[:, 0:KV_WIDTH], gk)
    v = zkv[:, KV_WIDTH:2 * KV_WIDTH]
    if cache:
        for s in range(rows_total // seq_len):
            rows = slice(s * seq_len, (s + 1) * seq_len)
            kc_ref[s] = k[rows, :].T
            vc_ref[s] = v[rows, :].T
    if rope:
        k = _rope(k, cos_ref[...], sin_ref[...])
    kt = k.T
    kd_ref[...] = jnp.concatenate(
        [kt[0:HEAD_DIM], kt[0:HEAD_DIM], kt[HEAD_DIM:], kt[HEAD_DIM:]], axis=0).astype(BF16)
    vd_ref[...] = _dup_heads(v).astype(BF16)
    run_mixer_share(1)

    for j in range(2 * D_MODEL // gate_chunk):
        cols = slice(j * gate_chunk, (j + 1) * gate_chunk)
        zg = _dot(hb, w_ref[:, _G0 + cols.start:_G0 + cols.stop])
        gate_ref[:, cols] = _sigmoid(zg).astype(BF16)
        run_mixer_share(2 + j)


def _proj(x, mod3, g1, w_in_b, gq2, gk2, seq_len, rope_tabs, mod_row, rows_per_mod_row, cache,
          cast_weights):
    t = x.shape[0]
    tm = PROJ_TOKEN_TILE
    assert tm % seq_len == 0, "a projection tile must hold whole sequences for the mixer"
    steps = t // tm
    rope = rope_tabs is not None
    row = lambda i: (i, 0)
    tiles_per_mod_row = None if rows_per_mod_row is None else rows_per_mod_row // tm
    in_specs = [pl.BlockSpec((tm, D_MODEL), row),
                _mod_spec(mod_row),
                _resident((1, D_MODEL)),
                _resident((D_MODEL, IN_WIDTH)),
                _resident((1, HEAD_DIM)),
                _resident((1, HEAD_DIM))]
    args = [x, mod3, g1, w_in_b, gq2, gk2]
    if rope:
        tiles_per_seq = rope_tabs[0].shape[0] // tm
        tab = pl.BlockSpec((tm, LANES), lambda i: (i % tiles_per_seq, 0))
        in_specs += [tab, tab]
        args += list(rope_tabs)
    fold = _fold_dft(seq_len)
    gate_chunk = GATE_COLUMNS // 2 if fold else GATE_COLUMNS
    dft = _dft_tables(seq_len, fold)
    in_specs += [_resident(tab.shape) for tab in dft]
    args += list(dft)
    slabs = [pl.BlockSpec((w.shape[0] // steps, w.shape[1]), row) for w in cast_weights]
    in_specs += slabs
    args += list(cast_weights)
    out_specs = [pl.BlockSpec((tm, F_WIDTH), row),
                 pl.BlockSpec((tm, ATTN_WIDTH), row),
                 pl.BlockSpec((2 * KV_WIDTH, tm), lambda i: (0, i)),
                 pl.BlockSpec((tm, 2 * KV_WIDTH), row),
                 pl.BlockSpec((tm, 2 * D_MODEL), row)]
    out_shape = [pltpu.HBM((t, F_WIDTH), BF16),
                 pltpu.HBM((t, ATTN_WIDTH), BF16),
                 pltpu.HBM((2 * KV_WIDTH, t), BF16),
                 pltpu.HBM((t, 2 * KV_WIDTH), BF16),
                 pltpu.HBM((t, 2 * D_MODEL), BF16)]
    if cache:
        seqs = tm // seq_len
        out_specs += [pl.BlockSpec((seqs, KV_WIDTH, seq_len), lambda i: (i, 0, 0))] * 2
        out_shape += [pltpu.HBM((t // seq_len, KV_WIDTH, seq_len), F32)] * 2
    out_specs += slabs
    out_shape += [pltpu.HBM(w.shape, BF16) for w in cast_weights]
    return pl.pallas_call(
        functools.partial(_proj_kernel, rope=rope, seq_len=seq_len, fold=fold, cache=cache,
                          n_cast=len(cast_weights), gate_chunk=gate_chunk,
                          tiles_per_mod_row=tiles_per_mod_row),
        grid=(steps,),
        in_specs=in_specs,
        out_specs=out_specs,
        out_shape=out_shape,
        scratch_shapes=[pltpu.VMEM((tm, F_WIDTH), BF16),
                        pltpu.VMEM((2, 2 * seq_len, F_WIDTH), BF16)],
        compiler_params=pltpu.CompilerParams(
            dimension_semantics=("arbitrary",), vmem_limit_bytes=RESIDENT_WEIGHTS_VMEM),
        name="proj_lat" if rope else "proj_ctx",
    )(*[_hbm(a) for a in args])


def _dft_tables(n, fold):
    def cs(m):
        idx = np.arange(m, dtype=np.int64)
        ang = 2.0 * np.pi * ((idx[:, None] * idx[None, :]) % m).astype(np.float64) / m
        return np.cos(ang) / np.sqrt(m), np.sin(ang) / np.sqrt(m)
    cd, sd = cs(F_GROUP_DIM)
    cn, sn = cs(n)
    tabs = [np.concatenate([cd, sd], axis=1)]
    if fold:
        half = n // 2
        rev = np.zeros((half, half), np.float64)
        rev[np.arange(1, half), half - np.arange(1, half)] = 1.0
        tabs += [cn[:half], -sn[:half], rev]
    else:
        tabs += [np.concatenate([cn, -sn], axis=1)]
    return tuple(jnp.asarray(t.astype(np.float32)).astype(BF16) for t in tabs)


def _fold_dft(n):
    return n // 2 >= 2 * MXU_DIM


def _fourier_stages(zf_ref, tabs, o_ref, ab_ref, n, fold):
    csd_ref = tabs[0]
    if fold:
        ch_ref, sh_ref, rev_ref = tabs[1:]
    else:
        (csn_ref,) = tabs[1:]
    half = n // 2
    units = []
    for s in range(zf_ref.shape[0] // n):
        buf = s % 2
        r0 = s * n

        def channels(r0=r0, buf=buf):
            for g in range(F_GROUPS):
                cols = slice(g * F_GROUP_DIM, (g + 1) * F_GROUP_DIM)
                ab = _dot(zf_ref[r0:r0 + n, cols], csd_ref[...])
                ab_ref[buf, 0:n, cols] = ab[:, 0:F_GROUP_DIM].astype(BF16)
                ab_ref[buf, n:2 * n, cols] = ab[:, F_GROUP_DIM:].astype(BF16)

        def positions(r0=r0, buf=buf):
            if not fold:
                o_ref[r0:r0 + n, :] = _dot(csn_ref[...], ab_ref[buf]).astype(BF16)
                return
            p = _dot(ch_ref[...], ab_ref[buf, 0:n, :])
            q = _dot(sh_ref[...], ab_ref[buf, n:2 * n, :])
            o_ref[r0:r0 + half, :] = (p + q).astype(BF16)
            upper = _dot(rev_ref[...], (p - q).astype(BF16))
            a = ab_ref[buf, 0:n, :].astype(F32)
            even = lax.broadcasted_iota(jnp.int32, a.shape, 0) % 2 == 0
            mid = jnp.sum(jnp.where(even, a, -a), axis=0, keepdims=True) * (n ** -0.5)
            first = lax.broadcasted_iota(jnp.int32, upper.shape, 0) == 0
            o_ref[r0 + half:r0 + n, :] = jnp.where(first, mid, upper).astype(BF16)

        units.append((channels, positions))
    return _pipelined(units)


def _group_scores(q_stack, keys, s_ref):
    kt = keys[0] if len(keys) == 1 else jnp.concatenate(keys, axis=1)
    top = lax.broadcasted_iota(jnp.int32, kt.shape, 0) < HEAD_DIM
    zero = jnp.zeros_like(kt)
    s_ref[...] = _dot(q_stack, jnp.concatenate(
        [jnp.where(top, kt, zero), jnp.where(top, zero, kt)], axis=1))


def _group_softmax_pv(values, sinks, band, s_ref, p_ref, sk_ref, rows_per_pair, chunk):
    rows = s_ref.shape[0]
    keys = s_ref.shape[1] // 2
    bounds = [0]
    for vd in values:
        bounds.append(bounds[-1] + vd.shape[0])
    lo = lax.broadcasted_iota(jnp.int32, (chunk, LANES), 1) < HEAD_DIM

    for r0 in range(0, rows, chunk):
        rs = slice(r0, r0 + chunk)
        sink_terms = []
        for half in range(2):
            sink = sinks[2 * (r0 // rows_per_pair) + half]
            cols = [slice(half * keys + bounds[i], half * keys + bounds[i + 1])
                    for i in range(len(values))]
            parts = [s_ref[rs, c] for c in cols]
            if band is not None:
                parts[0] = jnp.where(band(r0 % rows_per_pair, chunk), parts[0], NEG_INF)
            m = sink
            for s in parts:
                m = jnp.maximum(m, jnp.max(s, axis=-1, keepdims=True))
            for c, s in zip(cols, parts):
                p_ref[rs, c] = jnp.exp2(s - m).astype(BF16)
            sink_terms.append(jnp.exp2(sink - m))
        sk_ref[rs, :] = jnp.where(lo, sink_terms[0], sink_terms[1])

    vd = values[0] if len(values) == 1 else jnp.concatenate(values, axis=0)
    v_lo = lax.broadcasted_iota(jnp.int32, vd.shape, 1) < HEAD_DIM
    zero = jnp.zeros_like(vd)
    ones_lo = jnp.where(v_lo, 1.0, 0.0).astype(BF16)
    ones_hi = jnp.where(v_lo, 0.0, 1.0).astype(BF16)
    operand = jnp.concatenate(
        [jnp.concatenate([jnp.where(v_lo, vd, zero), ones_lo], axis=1),
         jnp.concatenate([jnp.where(v_lo, zero, vd), ones_hi], axis=1)], axis=0)
    out = _dot(p_ref[...], operand)
    return out[:, 0:LANES] / (out[:, LANES:2 * LANES] + sk_ref[...])


def _pair_cols(g):
    pa, pb = 2 * g, 2 * g + 1
    return slice(pa * LANES, (pa + 1) * LANES), slice(pb * LANES, (pb + 1) * LANES)


def _attn_tail_kernel(*refs, lat, n_tiles, tm, n, tiles_per_mod_row):
    sink_ref, q_ref, kd_ref, vd_ref = refs[:4]
    refs = refs[4:]
    if lat:
        ck_ref, cv_ref = refs[:2]
        refs = refs[2:]
    (x_ref, mod_ref, mix_ref, gate_ref, g2_ref,
     wf_ref, wao_ref, wout_ref, wup_ref, wdn_ref, o_ref, att_ref, s_ref, p_ref, sk_ref) = refs[:15]
    if lat:
        dist_ref, ckt_ref, cvd_ref = refs[15:18]
    act_ref = refs[-1]
    i = pl.program_id(0)

    def attention_stages():
        units = []
        if lat:
            qb = WINDOW
            span = qb + 2 * WINDOW
            t = jnp.minimum(i, n_tiles - 1) % (n // tm)
            ckt = ck_ref[0]
            ckt_ref[...] = jnp.concatenate(
                [ckt[0:HEAD_DIM], ckt[0:HEAD_DIM], ckt[HEAD_DIM:], ckt[HEAD_DIM:]], axis=0).astype(BF16)
            cvd_ref[...] = _dup_heads(cv_ref[0].T).astype(BF16)
            for rr in range(tm // qb):
                q0 = t * tm + rr * qb
                start = pl.multiple_of(jnp.clip(q0 - WINDOW, 0, n - span), WINDOW)
                dist_ref[rr] = jnp.abs((q0 - start) + lax.broadcasted_iota(jnp.int32, (qb, span), 0)
                                       - lax.broadcasted_iota(jnp.int32, (qb, span), 1))
                band = lambda r0, rows, rr=rr: dist_ref[rr, r0:r0 + rows, :] <= WINDOW
                rows = slice(rr * qb, (rr + 1) * qb)
                for g in range(N_KV_HEADS):
                    buf = len(units) % 2
                    cols = slice(g * LANES, (g + 1) * LANES)
                    qa, qbc = _pair_cols(g)

                    def scores(rows=rows, start=start, cols=cols, qa=qa, qbc=qbc, buf=buf):
                        q_stack = jnp.concatenate([q_ref[rows, qa], q_ref[rows, qbc]], axis=0)
                        _group_scores(q_stack, [kd_ref[cols, pl.ds(start, span)], ckt_ref[cols, :]],
                                      s_ref.at[buf])

                    def finish(rows=rows, start=start, g=g, cols=cols, qa=qa, qbc=qbc, buf=buf,
                               band=band):
                        sinks = [sink_ref[GROUP * g + h] * LOG2_E for h in range(GROUP)]
                        o = _group_softmax_pv(
                            [vd_ref[pl.ds(start, span), cols], cvd_ref[:, cols]], sinks, band,
                            s_ref.at[buf], p_ref.at[buf], sk_ref.at[buf], qb, LAT_SOFTMAX_ROWS)
                        att_ref[rows, qa] = o[0:qb].astype(BF16)
                        att_ref[rows, qbc] = o[qb:2 * qb].astype(BF16)

                    units.append((scores, finish))
        else:
            for s in range(tm // n):
                rows = slice(s * n, (s + 1) * n)
                for g in range(N_KV_HEADS):
                    buf = len(units) % 2
                    cols = slice(g * LANES, (g + 1) * LANES)
                    qa, qbc = _pair_cols(g)

                    def scores(rows=rows, cols=cols, qa=qa, qbc=qbc, buf=buf):
                        q_stack = jnp.concatenate([q_ref[rows, qa], q_ref[rows, qbc]], axis=0)
                        _group_scores(q_stack, [kd_ref[cols, rows]], s_ref.at[buf])

                    def finish(rows=rows, g=g, cols=cols, qa=qa, qbc=qbc, buf=buf):
                        sinks = [sink_ref[GROUP * g + h] * LOG2_E for h in range(GROUP)]
                        o = _group_softmax_pv([vd_ref[rows, cols]], sinks, None, s_ref.at[buf],
                                              p_ref.at[buf], sk_ref.at[buf], n, CTX_SOFTMAX_ROWS)
                        att_ref[rows, qa] = o[0:n].astype(BF16)
                        att_ref[rows, qbc] = o[n:2 * n].astype(BF16)

                    units.append((scores, finish))
        return _pipelined(units)

    def tail(stages):
        gt1, sh2, sc2, gt2 = _mod_vectors(mod_ref, i - 1, tiles_per_mod_row)[2:6]
        x1_parts, hb_parts = [], []
        for r0 in range(0, tm, tm // 2):
            rs = slice(r0, r0 + tm // 2)
            yf = _dot(mix_ref[rs, :], wf_ref[...])
            ya = _dot(att_ref[rs, :], wao_ref[...])
            merged = (gate_ref[rs, 0:D_MODEL].astype(F32) * yf
                      + gate_ref[rs, D_MODEL:2 * D_MODEL].astype(F32) * ya)
            x1_h = x_ref[rs, :] + gt1 * _dot(merged.astype(BF16), wout_ref[...])
            h = (x1_h * lax.rsqrt(jnp.mean(x1_h * x1_h, axis=-1, keepdims=True) + EPS)
                 * g2_ref[...])
            x1_parts.append(x1_h)
            hb_parts.append((h * (1.0 + sc2) + sh2).astype(BF16))
        x1 = jnp.concatenate(x1_parts, axis=0)
        hb = jnp.concatenate(hb_parts, axis=0)
        d_ff = wdn_ref.shape[0]
        n_chunks = d_ff // FF_CHUNK
        issued = 0
        for c in range(n_chunks):
            cols = slice(c * FF_CHUNK, (c + 1) * FF_CHUNK)
            a = _dot(hb, wup_ref[:, cols])
            u = _dot(hb, wup_ref[:, d_ff + cols.start:d_ff + cols.stop])
            act_ref[:, cols] = (a * _sigmoid(a) * u).astype(BF16)
            while issued < len(stages) and issued * n_chunks < (c + 1) * len(stages):
                stages[issued]()
                issued += 1
        o_ref[...] = x1 + gt2 * _dot(act_ref[...], wdn_ref[...])

    @pl.when(i == 0)
    def _():
        for stage in attention_stages():
            stage()

    @pl.when(jnp.logical_and(i > 0, i < n_tiles))
    def _():
        tail(attention_stages())

    @pl.when(i == n_tiles)
    def _():
        tail([])


def _attn_tail(sinks, q, kd, vd, cache, x, mod3, mix, gate, g2, wf, wao, wout, wup, wdn,
               mod_row, rows_per_mod_row, n):
    t = x.shape[0]
    tm = TAIL_TOKEN_TILE
    n_tiles = t // tm
    lat = cache is not None
    tiles_per_mod_row = None if rows_per_mod_row is None else rows_per_mod_row // tm
    front = lambda i: (jnp.minimum(i, n_tiles - 1), 0)
    back = lambda i: (jnp.maximum(i - 1, 0), 0)
    in_specs = [pl.BlockSpec(memory_space=pltpu.SMEM), pl.BlockSpec((tm, ATTN_WIDTH), front)]
    args = [sinks, q, kd, vd]
    if lat:
        past = cache[0].shape[2]
        seq_of = lambda i: jnp.minimum(i, n_tiles - 1) // (n // tm)
        in_specs += [pl.BlockSpec((2 * KV_WIDTH, n), lambda i: (0, seq_of(i))),
                     pl.BlockSpec((n, 2 * KV_WIDTH), lambda i: (seq_of(i), 0))]
        in_specs += [pl.BlockSpec((1, KV_WIDTH, past), lambda i: (seq_of(i), 0, 0))] * 2
        args += list(cache)
        m_rows, keys = 2 * WINDOW, 3 * WINDOW + past
    else:
        in_specs += [pl.BlockSpec((2 * KV_WIDTH, tm), lambda i: (0, front(i)[0])),
                     pl.BlockSpec((tm, 2 * KV_WIDTH), front)]
        m_rows, keys = 2 * n, n
    in_specs += [pl.BlockSpec((tm, D_MODEL), back),
                 _mod_spec(mod_row),
                 pl.BlockSpec((tm, F_WIDTH), back),
                 pl.BlockSpec((tm, 2 * D_MODEL), back),
                 _resident((1, D_MODEL)),
                 _resident(wf.shape), _resident(wao.shape), _resident(wout.shape),
                 _resident(wup.shape), _resident(wdn.shape)]
    args += [x, mod3, mix, gate, g2, wf, wao, wout, wup, wdn]
    scratch = [pltpu.VMEM((tm, ATTN_WIDTH), BF16),
               pltpu.VMEM((2, m_rows, 2 * keys), F32),
               pltpu.VMEM((2, m_rows, 2 * keys), BF16),
               pltpu.VMEM((2, m_rows, LANES), F32)]
    if lat:
        scratch += [pltpu.VMEM((tm // WINDOW, WINDOW, 3 * WINDOW), jnp.int32),
                    pltpu.VMEM((2 * KV_WIDTH, past), BF16),
                    pltpu.VMEM((past, 2 * KV_WIDTH), BF16)]
    scratch += [pltpu.VMEM((tm, wdn.shape[0]), BF16)]
    return pl.pallas_call(
        functools.partial(_attn_tail_kernel, lat=lat, n_tiles=n_tiles, tm=tm, n=n,
                          tiles_per_mod_row=tiles_per_mod_row),
        grid=(n_tiles + 1,),
        in_specs=in_specs,
        out_specs=pl.BlockSpec((tm, D_MODEL), back),
        out_shape=pltpu.HBM((t, D_MODEL), F32),
        scratch_shapes=scratch,
        compiler_params=pltpu.CompilerParams(
            dimension_semantics=("arbitrary",), vmem_limit_bytes=RESIDENT_WEIGHTS_VMEM),
        name="attn_tail_lat" if lat else "attn_tail_ctx",
    )(sinks, *[_hbm(a) for a in args[1:]])


def _rope_tables(n):
    rows = n // GRID_W
    row = np.repeat(np.arange(rows, dtype=np.float64), GRID_W)
    col = np.tile(np.arange(GRID_W, dtype=np.float64), rows)
    inv_freq = ROPE_THETA ** (-np.arange(0, ROPE_AXIS_DIM, 2, dtype=np.float64) / ROPE_AXIS_DIM)

    def axis_tabs(pos):
        ang = pos[:, None] * inv_freq[None, :]
        cos, sin = np.cos(ang), np.sin(ang)
        return np.concatenate([cos, cos], axis=-1), np.concatenate([-sin, sin], axis=-1)

    cr, sr = axis_tabs(row)
    cc, sc = axis_tabs(col)
    cos = np.tile(np.concatenate([cr, cc], axis=-1), (1, LANES // HEAD_DIM))
    sin = np.tile(np.concatenate([sr, sc], axis=-1), (1, LANES // HEAD_DIM))
    return jnp.asarray(cos.astype(np.float32)), jnp.asarray(sin.astype(np.float32))


@jax.jit
def _layer(xp, xs, ck, cv, c, c_ctx, w_ada, b_ada, g_norm1, g_norm2, w_in, g_q, g_k, sink,
           w_f, w_ao, w_out, w_up, w_down):
    bp, sp, _ = xp.shape
    bs, ss, _ = xs.shape
    past = ck.shape[1]

    mod3, w_in_b = _ada(c, c_ctx[None, :], w_ada, b_ada[None, :], w_in)
    g1, g2 = g_norm1[None, :], g_norm2[None, :]
    gq2, gk2 = g_q[None, :], g_k[None, :]
    ctx_mod = (CTX_MOD_ROW, None)
    lat_mod = (0, ss)

    xp2 = xp.reshape(bp * sp, D_MODEL)
    xs2 = xs.reshape(bs * ss, D_MODEL)
    mix, q, kd, vd, gate, k_new, v_new, wup = _proj(
        xp2, mod3, g1, w_in_b, gq2, gk2, sp, None, *ctx_mod, True, [w_up])
    mix_s, q_s, kd_s, vd_s, gate_s, wf, wao, wout, wdn = _proj(
        xs2, mod3, g1, w_in_b, gq2, gk2, ss, _rope_tables(ss), *lat_mod, False,
        [w_f, w_ao, w_out, w_down])

    yp = _attn_tail(sink, q, kd, vd, None, xp2, mod3, mix, gate, g2,
                    wf, wao, wout, wup, wdn, *ctx_mod, sp)

    def cache_t(t):
        return t.transpose(0, 2, 3, 1).reshape(bs, KV_WIDTH, past)

    ys = _attn_tail(sink, q_s, kd_s, vd_s, (cache_t(ck), cache_t(cv)), xs2, mod3,
                    mix_s, gate_s, g2, wf, wao, wout, wup, wdn, *lat_mod, ss)

    def cache_layout(t):
        return t.reshape(bp, N_KV_HEADS, HEAD_DIM, sp).transpose(0, 3, 1, 2)

    return yp.reshape(xp.shape), ys.reshape(xs.shape), cache_layout(k_new), cache_layout(v_new)


def kernel(x_prompt, x_sample, cache_k, cache_v, c, c_ctx, w_ada, b_ada, g_norm1, g_norm2,
           w_in, g_q, g_k, sinks, w_f, w_ao, w_out, w_up, w_down):
    depth = w_in.shape[0]
    xp, xs = x_prompt, x_sample
    new_k, new_v = [], []
    for l in range(depth):
        xp, xs, k_ctx, v_ctx = _layer(
            xp, xs, cache_k[:, l], cache_v[:, l], c, c_ctx, w_ada[l], b_ada[l], g_norm1[l],
            g_norm2[l], w_in[l], g_q[l], g_k[l], sinks[l], w_f[l], w_ao[l], w_out[l], w_up[l],
            w_down[l])
        new_k.append(k_ctx)
        new_v.append(v_ctx)
    return (xp, xs, jnp.stack(new_k, axis=1), jnp.stack(new_v, axis=1))
```
